```python
import math, functools
import jax, jax.numpy as jnp
from jax import lax
import numpy as np

D_MODEL = 2048
BATCH = 4
SEQ = 2048
DEPTH = 1
DEC_BATCH = 32
DEC_SEQ = 1
PAST_LEN = 8192
PAGE_SIZE = 128

HEAD_DIM = 64
C_MIX = D_MODEL
N_ATT_HEADS = C_MIX // (2 * HEAD_DIM)
N_RW_HEADS = C_MIX // (2 * HEAD_DIM)
C_ATT = N_ATT_HEADS * HEAD_DIM
C_RW = N_RW_HEADS * HEAD_DIM
DIL_WINDOWS = (128, 512, 2048)
DIL_RATES = (1, 4, 16)
MAX_WINDOW = max(DIL_WINDOWS)
Q_BLOCK = 128
ATT_SCALE = HEAD_DIM ** -0.5
W_LORA = 64
A_LORA = 64
G_LORA = 160
C_SHIFT = 3 * C_RW + W_LORA + A_LORA + G_LORA
C_IN = 3 * C_ATT + C_SHIFT
D_FF = 256 * ((8 * D_MODEL // 3 + 255) // 256)
CONV_W = 3
NORM_EPS = 1e-6
LNX_EPS = HEAD_DIM * 1e-5

kernel_name = "hybrid_dilated_attn_rwkv7_convffn_step"


def rms_norm(x, g, eps=NORM_EPS):
    xf = x.astype(jnp.float32)
    y = xf * lax.rsqrt(jnp.mean(xf * xf, axis=-1, keepdims=True) + eps)
    return (y * g.astype(jnp.float32)).astype(x.dtype)


def _softmax_av(s, vals, spec):
    m = jnp.max(s, axis=-1, keepdims=True)
    e = jnp.exp(s - m)
    den = jnp.sum(e, axis=-1, keepdims=True)
    o = jnp.einsum(spec, e / den, vals.astype(jnp.float32))
    return o, (m + jnp.log(den))[..., 0]


def _mix_branches(outs, lses):
    wts = jax.nn.softmax(jnp.stack(lses, 0), axis=0)
    return jnp.einsum('nbth,nbthe->bthe', wts, jnp.stack(outs, 0))


def _dilated_branch_prompt(q, k, v, rate, n_back):
    B, S, H, E = q.shape
    L = S // rate
    qb = math.gcd(L, Q_BLOCK)
    nblk = L // qb
    qr = q.reshape(B, nblk, qb, rate, H, E)
    pad = ((0, 0), (n_back, 0), (0, 0), (0, 0), (0, 0))
    kp = jnp.pad(k.reshape(B, L, rate, H, E), pad)
    vp = jnp.pad(v.reshape(B, L, rate, H, E), pad)
    idx = jnp.arange(nblk)[:, None] * qb + jnp.arange(qb + n_back)[None, :]
    kb = kp[:, idx]
    vb = vp[:, idx]
    s = jnp.einsum('bnqrhe,bnkrhe->bnrhqk', qr, kb, preferred_element_type=jnp.float32) * ATT_SCALE
    i = jnp.arange(qb)[:, None]
    c = jnp.arange(qb + n_back)[None, :]
    blk = jnp.arange(nblk)[:, None, None]
    mask = (c >= i) & (c <= i + n_back) & (blk * qb + c >= n_back)
    s = jnp.where(mask[None, :, None, None], s, -jnp.inf)
    o, lse = _softmax_av(s, vb, 'bnrhqk,bnkrhe->bnqrhe')
    o = o.reshape(B, S, H, E)
    lse = lse.transpose(0, 1, 4, 2, 3).reshape(B, S, H)
    return o, lse


def dilated_attention_prompt(q, k, v):
    outs, lses = [], []
    for win, rate in zip(DIL_WINDOWS, DIL_RATES):
        o, lse = _dilated_branch_prompt(q, k, v, rate, win // rate)
        outs.append(o)
        lses.append(lse)
    return _mix_branches(outs, lses).astype(q.dtype)


def dilated_attention_sample(q, k, v, k_cache, v_cache):
    T = q.shape[1]
    W = k_cache.shape[1]
    k_all = jnp.concatenate([k_cache.astype(k.dtype), k], axis=1)
    v_all = jnp.concatenate([v_cache.astype(v.dtype), v], axis=1)
    outs, lses = [], []
    for win, rate in zip(DIL_WINDOWS, DIL_RATES):
        n_back = win // rate
        idx = W + jnp.arange(T)[:, None] - rate * jnp.arange(n_back + 1)[None, :]
        valid = idx >= 0
        idx = jnp.maximum(idx, 0)
        kg = k_all[:, idx]
        vg = v_all[:, idx]
        s = jnp.einsum('bthe,btkhe->bhtk', q, kg, preferred_element_type=jnp.float32) * ATT_SCALE
        s = jnp.where(valid[None, None], s, -jnp.inf)
        o, lse = _softmax_av(s, vg, 'bhtk,btkhe->bthe')
        outs.append(o)
        lses.append(lse.transpose(0, 2, 1))
    return _mix_branches(outs, lses).astype(q.dtype)


def rwkv7_group(cols, prev_cols, wkv0, rw_mu, rw_w0, rw_w_up, rw_a0, rw_a_up, rw_g_up,
                rw_k_k, rw_k_a, rw_r_k, rw_lnx_w, rw_lnx_b):
    f32 = jnp.float32
    B, T, _ = cols.shape
    H, E = N_RW_HEADS, HEAD_DIM
    shifted = jnp.concatenate([prev_cols.astype(cols.dtype), cols[:, :-1]], axis=1)
    xs = cols + rw_mu * (shifted - cols)
    r, k, v, wd, ad, gd = jnp.split(
        xs, [C_RW, 2 * C_RW, 3 * C_RW, 3 * C_RW + W_LORA, 3 * C_RW + W_LORA + A_LORA], axis=-1)
    w = -jax.nn.softplus(-(rw_w0 + jnp.tanh(wd) @ rw_w_up).astype(f32)) - 0.5
    decay = jnp.exp(-jnp.exp(w)).reshape(B, T, H, E)
    a = jax.nn.sigmoid((rw_a0 + ad @ rw_a_up).astype(f32)).reshape(B, T, H, E)
    g = jax.nn.sigmoid(gd) @ rw_g_up
    hs = lambda t: t.astype(f32).reshape(B, T, H, E)
    kh, rh, vh = hs(k), hs(r), hs(v)
    kk = kh * rw_k_k.astype(f32).reshape(H, E)
    kk = kk / jnp.maximum(jnp.linalg.norm(kk, axis=-1, keepdims=True), 1e-12)
    k_eff = kh * (1.0 + (a - 1.0) * rw_k_a.astype(f32).reshape(H, E))
    a_vec = -kk
    b_vec = kk * a

    def step(S, inp):
        rt, wt, kt, vt, at, bt = inp
        sa = jnp.einsum('bhij,bhj->bhi', S, at)
        S = S * wt[:, :, None, :] + sa[..., None] * bt[:, :, None, :] + vt[..., None] * kt[:, :, None, :]
        return S, jnp.einsum('bhij,bhj->bhi', S, rt)

    tm = lambda t: jnp.moveaxis(t, 1, 0)
    S_T, ys = lax.scan(step, wkv0.astype(f32), (tm(rh), tm(decay), tm(k_eff), tm(vh), tm(a_vec), tm(b_vec)))
    y = jnp.moveaxis(ys, 0, 1)
    mean = jnp.mean(y, axis=-1, keepdims=True)
    var = jnp.mean(jnp.square(y - mean), axis=-1, keepdims=True)
    yn = (y - mean) * lax.rsqrt(var + LNX_EPS) * rw_lnx_w.astype(f32).reshape(H, E) + rw_lnx_b.astype(f32).reshape(H, E)
    bonus = jnp.sum(rh * k_eff * rw_r_k.astype(f32).reshape(H, E), axis=-1, keepdims=True) * vh
    out = (yn + bonus).reshape(B, T, C_RW).astype(cols.dtype) * g
    return out, cols[:, -1:], S_T


def decoder_layer(x, attend, rw_prev, wkv0, ffn_prev, norm_mix_g, w_in, att_out_g, rw_mu, rw_w0, rw_w_up,
                  rw_a0, rw_a_up, rw_g_up, rw_k_k, rw_k_a, rw_r_k, rw_lnx_w, rw_lnx_b, w_o,
                  norm_ffn_g, ffn_w_up, ffn_conv_w, ffn_conv_b, ffn_w_down):
    B, T, _ = x.shape
    H, E = N_ATT_HEADS, HEAD_DIM
    h = rms_norm(x, norm_mix_g)
    p = h @ w_in
    q = p[..., :C_ATT].reshape(B, T, H, E)
    k = p[..., C_ATT:2 * C_ATT].reshape(B, T, H, E)
    v = p[..., 2 * C_ATT:3 * C_ATT].reshape(B, T, H, E)
    o_att = attend(q, k, v)
    o_att = rms_norm(o_att, att_out_g.reshape(H, E)).reshape(B, T, C_ATT)
    o_rw, rw_last, wkv_T = rwkv7_group(p[..., 3 * C_ATT:], rw_prev, wkv0, rw_mu, rw_w0, rw_w_up, rw_a0,
                                       rw_a_up, rw_g_up, rw_k_k, rw_k_a, rw_r_k, rw_lnx_w, rw_lnx_b)
    x = x + jnp.concatenate([o_att, o_rw], axis=-1) @ w_o
    h2 = rms_norm(x, norm_ffn_g)
    u = h2 @ ffn_w_up
    up = jnp.concatenate([ffn_prev.astype(u.dtype), u], axis=1)
    c = ffn_conv_b + sum(ffn_conv_w[j] * up[:, j:j + T] for j in range(CONV_W))
    gate, val = jnp.split(c, 2, axis=-1)
    x = x + (jax.nn.silu(gate) * val) @ ffn_w_down
    return x, k, v, rw_last, wkv_T, up[:, -(CONV_W - 1):]


def setup_inputs(seed: int = 0) -> dict:
    key = jax.random.key(seed)
    ks = jax.random.split(key, 28)
    win_buf = min(MAX_WINDOW, PAST_LEN)
    nrm = lambda kk, shape, scale=1.0: scale * jax.random.normal(kk, shape, jnp.float32)
    L = DEPTH
    return {
        "x_prompt": nrm(ks[0], (BATCH, SEQ, D_MODEL)),
        "x_sample": nrm(ks[1], (DEC_BATCH, DEC_SEQ, D_MODEL)),
        "cache_att_k": nrm(ks[2], (L, DEC_BATCH, win_buf, N_ATT_HEADS, HEAD_DIM)),
        "cache_att_v": nrm(ks[3], (L, DEC_BATCH, win_buf, N_ATT_HEADS, HEAD_DIM)),
        "state_rwkv_shift": nrm(ks[4], (L, DEC_BATCH, 1, C_SHIFT)),
        "state_rwkv_wkv": nrm(ks[5], (L, DEC_BATCH, N_RW_HEADS, HEAD_DIM, HEAD_DIM), 0.3),
        "state_ffn_conv": nrm(ks[6], (L, DEC_BATCH, CONV_W - 1, 2 * D_FF)),
        "norm_mix_g": 1.0 + nrm(ks[7], (L, D_MODEL), 0.02),
        "w_in": nrm(ks[8], (L, D_MODEL, C_IN), D_MODEL ** -0.5),
        "att_out_g": 1.0 + nrm(ks[9], (L, C_ATT), 0.02),
        "rw_mu": jax.random.uniform(ks[10], (L, C_SHIFT), jnp.float32),
        "rw_w0": jax.random.uniform(ks[11], (L, C_RW), jnp.float32, minval=-5.0, maxval=0.0),
        "rw_w_up": nrm(ks[12], (L, W_LORA, C_RW), 0.5 * W_LORA ** -0.5),
        "rw_a0": nrm(ks[13], (L, C_RW), 0.5),
        "rw_a_up": nrm(ks[14], (L, A_LORA, C_RW), A_LORA ** -0.5),
        "rw_g_up": nrm(ks[15], (L, G_LORA, C_RW), G_LORA ** -0.5),
        "rw_k_k": 0.85 + nrm(ks[16], (L, C_RW), 0.05),
        "rw_k_a": 1.0 + nrm(ks[17], (L, C_RW), 0.05),
        "rw_r_k": nrm(ks[18], (L, C_RW), 0.1),
        "rw_lnx_w": 1.0 + nrm(ks[19], (L, C_RW), 0.02),
        "rw_lnx_b": nrm(ks[20], (L, C_RW), 0.02),
        "w_o": nrm(ks[21], (L, C_MIX, D_MODEL), C_MIX ** -0.5),
        "norm_ffn_g": 1.0 + nrm(ks[22], (L, D_MODEL), 0.02),
        "ffn_w_up": nrm(ks[23], (L, D_MODEL, 2 * D_FF), D_MODEL ** -0.5),
        "ffn_conv_w": nrm(ks[24], (L, CONV_W, 2 * D_FF), CONV_W ** -0.5),
        "ffn_conv_b": nrm(ks[25], (L, 2 * D_FF), 0.02),
        "ffn_w_down": nrm(ks[26], (L, D_FF, D_MODEL), D_FF ** -0.5),
        "norm_final_g": 1.0 + nrm(ks[27], (D_MODEL,), 0.02),
    }


def reference(x_prompt, x_sample, cache_att_k, cache_att_v, state_rwkv_shift, state_rwkv_wkv, state_ffn_conv,
              norm_mix_g, w_in, att_out_g, rw_mu, rw_w0, rw_w_up, rw_a0, rw_a_up, rw_g_up, rw_k_k, rw_k_a,
              rw_r_k, rw_lnx_w, rw_lnx_b, w_o, norm_ffn_g, ffn_w_up, ffn_conv_w, ffn_conv_b, ffn_w_down,
              norm_final_g):
    B, S, _ = x_prompt.shape
    win_p = min(MAX_WINDOW, S)
    rw_prev_p = jnp.zeros((B, 1, C_SHIFT), x_prompt.dtype)
    wkv_p0 = jnp.zeros((B, N_RW_HEADS, HEAD_DIM, HEAD_DIM), jnp.float32)
    ffn_prev_p = jnp.zeros((B, CONV_W - 1, 2 * D_FF), x_prompt.dtype)
    xp, xs = x_prompt, x_sample
    pk, pv, prw, pwkv, pffn = [], [], [], [], []
    sk, sv, srw, swkv, sffn = [], [], [], [], []
    for l in range(DEPTH):
        lp = dict(norm_mix_g=norm_mix_g[l], w_in=w_in[l], att_out_g=att_out_g[l], rw_mu=rw_mu[l],
                  rw_w0=rw_w0[l], rw_w_up=rw_w_up[l], rw_a0=rw_a0[l], rw_a_up=rw_a_up[l], rw_g_up=rw_g_up[l],
                  rw_k_k=rw_k_k[l], rw_k_a=rw_k_a[l], rw_r_k=rw_r_k[l], rw_lnx_w=rw_lnx_w[l],
                  rw_lnx_b=rw_lnx_b[l], w_o=w_o[l], norm_ffn_g=norm_ffn_g[l], ffn_w_up=ffn_w_up[l],
                  ffn_conv_w=ffn_conv_w[l], ffn_conv_b=ffn_conv_b[l], ffn_w_down=ffn_w_down[l])
        xp, kp_, vp_, rwp, wkvp, ffp = decoder_layer(xp, dilated_attention_prompt, rw_prev_p, wkv_p0,
                                                     ffn_prev_p, **lp)
        attend_s = functools.partial(dilated_attention_sample, k_cache=cache_att_k[l], v_cache=cache_att_v[l])
        xs, ks_, vs_, rws, wkvs, ffs = decoder_layer(xs, attend_s, state_rwkv_shift[l], state_rwkv_wkv[l],
                                                     state_ffn_conv[l], **lp)
        pk.append(kp_[:, -win_p:]); pv.append(vp_[:, -win_p:]); prw.append(rwp); pwkv.append(wkvp); pffn.append(ffp)
        sk.append(ks_); sv.append(vs_); srw.append(rws); swkv.append(wkvs); sffn.append(ffs)
    y_prompt = rms_norm(xp, norm_final_g)
    y_sample = rms_norm(xs, norm_final_g)
    return (y_prompt, y_sample,
            jnp.stack(pk), jnp.stack(pv), jnp.stack(prw), jnp.stack(pwkv), jnp.stack(pffn),
            jnp.stack(sk), jnp.stack(sv), jnp.stack(srw), jnp.stack(swkv), jnp.stack(sffn))
```

```python
import functools

import jax
import jax.numpy as jnp
from jax import lax
from jax.experimental import pallas as pl
from jax.experimental.pallas import tpu as pltpu

F32 = jnp.float32
BF16 = jnp.bfloat16

D_MODEL = 2048
HEAD_DIM = 64
N_HEADS = 16
C_GRP = N_HEADS * HEAD_DIM
W_LORA, A_LORA, G_LORA = 64, 64, 160
C_LORA = W_LORA + A_LORA + G_LORA
C_SHIFT = 3 * C_GRP + C_LORA
C_MAIN = 6 * C_GRP
D_FF = 5632
DIL_RATES = (1, 4, 16)
N_BACK = 128
ATT_SCALE = HEAD_DIM ** -0.5
NORM_EPS = 1e-6
LNX_EPS = HEAD_DIM * 1e-5
NEG_BIG = -1e30

LANES = 128
VMEM_LIMIT = 48 * 1024 * 1024

CHUNK = 64
PAIR = 2 * HEAD_DIM


def _cparams(n_grid):
    return pltpu.CompilerParams(dimension_semantics=("arbitrary",) * n_grid,
                                vmem_limit_bytes=VMEM_LIMIT)


def _bdot(a, b):
    return jnp.dot(a.astype(BF16), b.astype(BF16), preferred_element_type=F32)


def _bdot_nt(a, b):
    return lax.dot_general(a.astype(BF16), b.astype(BF16), (((1,), (1,)), ((), ())),
                           preferred_element_type=F32)


def _bdot_tn(a, b):
    return jnp.dot(a.astype(F32).T.astype(BF16), b.astype(BF16), preferred_element_type=F32)


def _split_dot(mat01, x, terms):
    acc = None
    rem = x
    for _ in range(terms):
        piece = rem.astype(BF16)
        part = jnp.dot(mat01, piece, preferred_element_type=F32)
        acc = part if acc is None else acc + part
        rem = rem - piece.astype(F32)
    return acc


def _split_dot_r(x, mat01, terms):
    acc = None
    rem = x
    for _ in range(terms):
        piece = rem.astype(BF16)
        part = jnp.dot(piece, mat01, preferred_element_type=F32)
        acc = part if acc is None else acc + part
        rem = rem - piece.astype(F32)
    return acc


def _head_ones(n):
    r = lax.broadcasted_iota(jnp.int32, (n, n), 0) // HEAD_DIM
    c = lax.broadcasted_iota(jnp.int32, (n, n), 1) // HEAD_DIM
    return jnp.where(r == c, 1.0, 0.0).astype(BF16)


def _sigmoid(x):
    return 1.0 / (1.0 + jnp.exp(-x))


def _softplus(x):
    return jnp.maximum(x, 0.0) + jnp.log(1.0 + jnp.exp(-jnp.abs(x)))


def _in_proj_kernel(x_ref, g_ref, wm_ref, wl_ref, om_ref, ol_ref, h_scr):
    @pl.when(pl.program_id(1) == 0)
    def _():
        x = x_ref[...]
        ms = jnp.mean(x * x, axis=-1, keepdims=True)
        h = (x * lax.rsqrt(ms + NORM_EPS) * g_ref[...]).astype(BF16)
        h_scr[...] = h
        ol_ref[...] = jnp.dot(h, wl_ref[...], preferred_element_type=F32)

    om_ref[...] = jnp.dot(h_scr[...], wm_ref[...], preferred_element_type=F32)


def _in_proj(x2d, g, w_main, w_lora, tm, tn):
    m = x2d.shape[0]
    return pl.pallas_call(
        _in_proj_kernel,
        grid=(m // tm, C_MAIN // tn),
        in_specs=[
            pl.BlockSpec((tm, D_MODEL), lambda i, j: (i, 0)),
            pl.BlockSpec((1, D_MODEL), lambda i, j: (0, 0)),
            pl.BlockSpec((D_MODEL, tn), lambda i, j: (0, j)),
            pl.BlockSpec((D_MODEL, C_LORA), lambda i, j: (0, 0)),
        ],
        out_specs=[
            pl.BlockSpec((tm, tn), lambda i, j: (i, j)),
            pl.BlockSpec((tm, C_LORA), lambda i, j: (i, 0)),
        ],
        out_shape=[jax.ShapeDtypeStruct((m, C_MAIN), F32), jax.ShapeDtypeStruct((m, C_LORA), F32)],
        scratch_shapes=[pltpu.VMEM((tm, D_MODEL), BF16)],
        compiler_params=_cparams(2),
        name="in_proj",
    )(x2d, g, w_main, w_lora)


def _attn_prompt_kernel(q_ref, k_ref, v_ref, g_ref, o_ref, ob_scr, lb_scr, *, seq):
    blk = N_BACK
    lane = lax.broadcasted_iota(jnp.int32, (blk, PAIR), 1)
    is_a = lane < HEAD_DIM

    def one_head(qm, kk, vm, bias):
        s = _bdot_nt(qm, kk) + bias
        m = jnp.max(s, axis=-1, keepdims=True)
        e = jnp.exp(s - m)
        den = jnp.sum(e, axis=-1, keepdims=True)
        o = _bdot(e / den, vm)
        return o, m + jnp.log(den)

    def both_heads(q, kk, vv, bias):
        qs = q * ATT_SCALE
        oa, la = one_head(jnp.where(is_a, qs, 0.0), kk, jnp.where(is_a[:1], vv, 0.0), bias)
        ob, lb = one_head(jnp.where(is_a, 0.0, qs), kk, jnp.where(is_a[:1], 0.0, vv), bias)
        return oa + ob, jnp.where(is_a, la, lb)

    row2 = lax.broadcasted_iota(jnp.int32, (blk, 2 * blk), 0)
    col2 = lax.broadcasted_iota(jnp.int32, (blk, 2 * blk), 1)
    band_bias = jnp.where((col2 >= row2) & (col2 <= row2 + N_BACK), 0.0, NEG_BIG)
    prev_half = jnp.where(col2 < blk, 1.0, 0.0)
    row1 = lax.broadcasted_iota(jnp.int32, (blk, blk), 0)
    col1 = lax.broadcasted_iota(jnp.int32, (blk, blk), 1)
    causal_bias = jnp.where(col1 <= row1, 0.0, NEG_BIG)

    for br, rate in enumerate(DIL_RATES):
        nblk = seq // rate // blk

        if nblk == 1:
            def body1(c, carry, rate=rate, br=br):
                rows = pl.ds(c, blk, stride=rate)
                o, l = both_heads(q_ref[rows, :], k_ref[rows, :], v_ref[rows, :], causal_bias)
                ob_scr[br, rows, :] = o
                lb_scr[br, rows, :] = l
                return carry
            lax.fori_loop(0, rate, body1, 0)
        else:
            def body2(idx, carry, rate=rate, br=br, nblk=nblk):
                c = idx // nblk
                b = idx % nblk
                rows = pl.ds(c + b * (blk * rate), blk, stride=rate)
                prow = pl.ds(c + jnp.maximum(b - 1, 0) * (blk * rate), blk, stride=rate)
                kk = jnp.concatenate([k_ref[prow, :], k_ref[rows, :]], axis=0)
                vv = jnp.concatenate([v_ref[prow, :], v_ref[rows, :]], axis=0)
                first = jnp.where(b == 0, NEG_BIG, 0.0).astype(F32)
                o, l = both_heads(q_ref[rows, :], kk, vv, band_bias + first * prev_half)
                ob_scr[br, rows, :] = o
                lb_scr[br, rows, :] = l
                return carry
            lax.fori_loop(0, rate * nblk, body2, 0)

    ones_bd = _head_ones(PAIR)
    gain = g_ref[...]
    tile = 256

    def merge(i, carry):
        rows = pl.ds(pl.multiple_of(i * tile, tile), tile)
        l0, l1, l2 = lb_scr[0, rows, :], lb_scr[1, rows, :], lb_scr[2, rows, :]
        m = jnp.maximum(jnp.maximum(l0, l1), l2)
        w0, w1, w2 = jnp.exp(l0 - m), jnp.exp(l1 - m), jnp.exp(l2 - m)
        o = (w0 * ob_scr[0, rows, :] + w1 * ob_scr[1, rows, :] + w2 * ob_scr[2, rows, :]) / (w0 + w1 + w2)
        ms = _split_dot_r(o * o, ones_bd, 2) * (1.0 / HEAD_DIM)
        o_ref[rows, :] = (o * lax.rsqrt(ms + NORM_EPS) * gain).astype(o_ref.dtype)
        return carry
    lax.fori_loop(0, seq // tile, merge, 0)


def _attn_prompt(p_main3, att_g):
    nb, seq, _ = p_main3.shape
    npair = C_GRP // PAIR
    col = lambda off: pl.BlockSpec((None, seq, PAIR), lambda b, p, off=off: (b, 0, off + p))
    return pl.pallas_call(
        functools.partial(_attn_prompt_kernel, seq=seq),
        grid=(nb, npair),
        in_specs=[col(0), col(npair), col(2 * npair), pl.BlockSpec((1, PAIR), lambda b, p: (0, p))],
        out_specs=pl.BlockSpec((None, seq, PAIR), lambda b, p: (b, 0, p)),
        out_shape=jax.ShapeDtypeStruct((nb, seq, C_GRP), BF16),
        scratch_shapes=[pltpu.VMEM((3, seq, PAIR), F32), pltpu.VMEM((3, seq, PAIR), F32)],
        compiler_params=_cparams(2),
        name="attn_prompt",
    )(p_main3, p_main3, p_main3, att_g)


def _attn_sample_kernel(q_ref, kn_ref, vn_ref, k1_ref, k4_ref, k16_ref, v1_ref, v4_ref, v16_ref,
                        g_ref, o_ref):
    q = q_ref[0] * ATT_SCALE
    kn, vn = kn_ref[0], vn_ref[0]
    s_new = jnp.sum(kn * q, axis=-1, keepdims=True)
    outs, lses = [], []
    for kc_ref, vc_ref in ((k1_ref, v1_ref), (k4_ref, v4_ref), (k16_ref, v16_ref)):
        kc, vc = kc_ref[0], vc_ref[0]
        s = jnp.sum(kc * q[None], axis=-1, keepdims=True)
        m = jnp.maximum(jnp.max(s, axis=0), s_new)
        e = jnp.exp(s - m[None])
        e_new = jnp.exp(s_new - m)
        den = jnp.sum(e, axis=0) + e_new
        o = jnp.sum((e / den[None]) * vc, axis=0) + (e_new / den) * vn
        outs.append(o)
        lses.append(m + jnp.log(den))
    m = jnp.maximum(jnp.maximum(lses[0], lses[1]), lses[2])
    ws = [jnp.exp(l - m) for l in lses]
    o = (ws[0] * outs[0] + ws[1] * outs[1] + ws[2] * outs[2]) / (ws[0] + ws[1] + ws[2])
    ms = jnp.mean(o * o, axis=-1, keepdims=True)
    o_ref[0] = o * lax.rsqrt(ms + NORM_EPS) * g_ref[...]


def _attn_sample(q, kn, vn, cache_k, cache_v, att_g2):
    nb, win = cache_k.shape[0], cache_k.shape[1]
    hd = (N_HEADS, HEAD_DIM)
    tok = pl.BlockSpec((1,) + hd, lambda b: (b, 0, 0))
    specs, args = [], []
    for cache in (cache_k, cache_v):
        for rate in DIL_RATES:
            nrow = win // rate
            if rate == 1:
                args.append(cache)
                specs.append(pl.BlockSpec((1, N_BACK) + hd, lambda b, n=nrow: (b, n // N_BACK - 1, 0, 0)))
            else:
                args.append(cache.reshape((nb, nrow, rate) + hd))
                specs.append(pl.BlockSpec((1, N_BACK, None) + hd,
                                          lambda b, n=nrow: (b, n // N_BACK - 1, 0, 0, 0)))
    return pl.pallas_call(
        _attn_sample_kernel,
        grid=(nb,),
        in_specs=[tok, tok, tok] + specs + [pl.BlockSpec(hd, lambda b: (0, 0))],
        out_specs=tok,
        out_shape=jax.ShapeDtypeStruct((nb,) + hd, F32),
        compiler_params=_cparams(1),
        name="attn_sample",
    )(q, kn, vn, *args, att_g2)


def _rwkv_token_math(xr, xk, xv, xl, prm, ones_bd):
    w_raw = prm["w0"] + _bdot(jnp.tanh(xl[:, :2 * W_LORA]), prm["w_up"])
    logw = -jnp.exp(-_softplus(-w_raw) - 0.5)
    a_sig = _sigmoid(prm["a0"] + _bdot(xl[:, :2 * W_LORA], prm["a_up"]))
    gate = _bdot(_sigmoid(xl[:, 2 * W_LORA:]), prm["g_up"])
    kk = xk * prm["k_k"]
    nrm = jnp.sqrt(_split_dot_r(kk * kk, ones_bd, 2))
    kk = kk / jnp.maximum(nrm, 1e-12)
    k_eff = xk * (1.0 + (a_sig - 1.0) * prm["k_a"])
    bonus = _split_dot_r(xr * k_eff * prm["r_k"], ones_bd, 2) * xv
    return dict(r=xr, k=k_eff, v=xv, a=-kk, b=kk * a_sig, logw=logw, gate=gate, bonus=bonus)


def _group_norm_gate(y, bonus, gate, lnx_w, lnx_b, ones_bd):
    mean = _split_dot_r(y, ones_bd, 2) * (1.0 / HEAD_DIM)
    d = y - mean
    var = _split_dot_r(d * d, ones_bd, 2) * (1.0 / HEAD_DIM)
    yn = d * lax.rsqrt(var + LNX_EPS) * lnx_w + lnx_b
    return (yn + bonus) * gate


_RW_VEC_NAMES = ("w0", "a0", "k_k", "k_a", "r_k", "lnx_w", "lnx_b")


def _rwkv_prompt_kernel(r_ref, k_ref, v_ref, l_ref, pm_ref, pl_ref, mu_m_ref, mu_l_ref, vec_ref,
                        wup_ref, aup_ref, gup_ref, o_ref, h_ref,
                        tok_scr, y_scr, rw_scr, y0_scr, g_scr, ha_scr, pc_scr, *, seq):
    n = PAIR
    ones_bd = _head_ones(n)
    row = lax.broadcasted_iota(jnp.int32, (seq, 1), 0)

    def shift(x, prev):
        return jnp.where(row == 0, prev, pltpu.roll(x, 1, axis=0))

    def lerp(x, prev, mu):
        return x + mu * (shift(x, prev) - x)

    pm = pm_ref[...]
    mu_m = mu_m_ref[...]
    prm = {name: vec_ref[i:i + 1, :] for i, name in enumerate(_RW_VEC_NAMES)}
    prm.update(w_up=wup_ref[...], a_up=aup_ref[...], g_up=gup_ref[...])
    tok = _rwkv_token_math(lerp(r_ref[...], pm[0:1], mu_m[0:1]),
                           lerp(k_ref[...], pm[1:2], mu_m[1:2]),
                           lerp(v_ref[...], pm[2:3], mu_m[2:3]),
                           lerp(l_ref[...], pl_ref[...], mu_l_ref[...]), prm, ones_bd)
    for i, name in enumerate(("r", "k", "v", "a", "b", "logw", "gate", "bonus")):
        tok_scr[i] = tok[name]

    nch = seq // CHUNK
    lane = lax.broadcasted_iota(jnp.int32, (CHUNK, n), 1)
    is_a = lane < HEAD_DIM
    ri = lax.broadcasted_iota(jnp.int32, (n, n), 0)
    ci = lax.broadcasted_iota(jnp.int32, (n, n), 1)
    strict = ri > ci
    incl = ri >= ci
    eye = ri == ci
    rc = lax.broadcasted_iota(jnp.int32, (CHUNK, CHUNK), 0)
    cc = lax.broadcasted_iota(jnp.int32, (CHUNK, CHUNK), 1)

    def stack(x):
        return jnp.concatenate([jnp.where(is_a, x, 0.0), jnp.where(is_a, 0.0, x)], axis=0)

    def phase_a(c, carry):
        tril_ones = jnp.where(rc >= cc, 1.0, 0.0).astype(BF16)
        rows = pl.ds(pl.multiple_of(c * CHUNK, CHUNK), CHUNK)
        r_c, k_c, v_c, a_c, b_c, lw_c = [tok_scr[i, rows, :] for i in range(6)]
        lcum = _split_dot(tril_ones, lw_c, 3)
        lend = lcum[CHUNK - 1:CHUNK, :]
        e_pos, e_neg = jnp.exp(lcum), jnp.exp(-lcum)
        e_prev, e_end = jnp.exp(lcum - lw_c), jnp.exp(lend - lcum)
        a_s, r_s = stack(a_c * e_prev).astype(BF16), stack(r_c * e_pos).astype(BF16)
        b_s, k_s = stack(b_c * e_neg).astype(BF16), stack(k_c * e_neg).astype(BF16)
        v_s = stack(v_c).astype(BF16)
        be_s, ke_s = stack(b_c * e_end).astype(BF16), stack(k_c * e_end).astype(BF16)

        s_ab = jnp.where(strict, _bdot_nt(a_s, b_s), 0.0)
        s_ak = jnp.where(strict, _bdot_nt(a_s, k_s), 0.0)
        s_rb = jnp.where(incl, _bdot_nt(r_s, b_s), 0.0)
        s_rk = jnp.where(incl, _bdot_nt(r_s, k_s), 0.0)

        apow = s_ab
        tinv = jnp.where(eye, 1.0, 0.0) + s_ab
        steps = CHUNK.bit_length() - 2
        for _ in range(steps):
            apow = _bdot(apow, apow)
            tinv = tinv + _bdot(tinv, apow)

        w_m = _bdot(tinv, a_s).astype(BF16)
        u0 = _bdot(tinv, _bdot(s_ak, v_s)).astype(BF16)
        rw_scr[c] = _bdot(s_rb, w_m) + r_s.astype(F32)
        y0_scr[c] = _bdot(s_rb, u0) + _bdot(s_rk, v_s)
        g_scr[c] = _bdot_tn(w_m, be_s)
        ha_scr[c] = _bdot_tn(u0, be_s) + _bdot_tn(v_s, ke_s)
        pc_scr[c] = jnp.exp(lend)
        return carry
    lax.fori_loop(0, nch, phase_a, 0)

    h_ref[...] = jnp.zeros((n, n), F32)

    def phase_b(c, carry):
        rows = pl.ds(pl.multiple_of(c * CHUNK, CHUNK), CHUNK)
        ht = h_ref[...]
        hb = ht.astype(BF16)
        ys = _bdot_nt(rw_scr[c], hb) + y0_scr[c]
        y_scr[rows, :] = ys[:CHUNK, :] + ys[CHUNK:, :]
        h_ref[...] = ht * pc_scr[c] + _bdot(hb, g_scr[c]) + ha_scr[c]
        return carry
    lax.fori_loop(0, nch, phase_b, 0)

    o_ref[...] = _group_norm_gate(y_scr[...], tok_scr[7], tok_scr[6], prm["lnx_w"], prm["lnx_b"],
                                  _head_ones(n)).astype(o_ref.dtype)


def _rwkv_prompt(p_main3, p_lora3, prev_main, prev_lora, mu_main, mu_lora, vecs, w_up, a_up, g_up):
    nb, seq, _ = p_main3.shape
    npair = C_GRP // PAIR
    col = lambda off: pl.BlockSpec((None, seq, PAIR), lambda b, p, off=off: (b, 0, off + p))
    pcol = lambda rows: pl.BlockSpec((rows, PAIR), lambda b, p: (0, p))
    return pl.pallas_call(
        functools.partial(_rwkv_prompt_kernel, seq=seq),
        grid=(nb, npair),
        in_specs=[
            col(3 * npair), col(4 * npair), col(5 * npair),
            pl.BlockSpec((None, seq, C_LORA), lambda b, p: (b, 0, 0)),
            pl.BlockSpec((None, 3, PAIR), lambda b, p: (b, 0, p)),
            pl.BlockSpec((None, 1, C_LORA), lambda b, p: (b, 0, 0)),
            pcol(3), pl.BlockSpec((1, C_LORA), lambda b, p: (0, 0)), pcol(len(_RW_VEC_NAMES)),
            pcol(2 * W_LORA), pcol(2 * A_LORA), pcol(G_LORA),
        ],
        out_specs=[
            pl.BlockSpec((None, seq, PAIR), lambda b, p: (b, 0, p)),
            pl.BlockSpec((None, None, PAIR, PAIR), lambda b, p: (b, p, 0, 0)),
        ],
        out_shape=[jax.ShapeDtypeStruct((nb, seq, C_GRP), BF16),
                   jax.ShapeDtypeStruct((nb, npair, PAIR, PAIR), F32)],
        scratch_shapes=[
            pltpu.VMEM((8, seq, PAIR), F32), pltpu.VMEM((seq, PAIR), F32),
            pltpu.VMEM((seq // CHUNK, PAIR, PAIR), F32), pltpu.VMEM((seq // CHUNK, PAIR, PAIR), F32),
            pltpu.VMEM((seq // CHUNK, PAIR, PAIR), F32), pltpu.VMEM((seq // CHUNK, PAIR, PAIR), F32),
            pltpu.VMEM((seq // CHUNK, 1, PAIR), F32),
        ],
        compiler_params=_cparams(2),
        name="rwkv_prompt",
    )(p_main3, p_main3, p_main3, p_lora3, prev_main, prev_lora, mu_main, mu_lora, vecs, w_up, a_up, g_up)


def _rwkv_sample_tok_kernel(pm_ref, l_ref, pvm_ref, pvl_ref, mu_m_ref, mu_l_ref, vec_ref,
                            wup_ref, aup_ref, gup_ref, out_ref):
    ones_bd = _head_ones(C_GRP)
    prm = {name: vec_ref[i:i + 1, :] for i, name in enumerate(_RW_VEC_NAMES)}
    prm.update(w_up=wup_ref[...], a_up=aup_ref[...], g_up=gup_ref[...])
    lerp = lambda x, prev, mu: x + mu * (prev - x)
    xs = [lerp(pm_ref[:, i * C_GRP:(i + 1) * C_GRP], pvm_ref[:, i * C_GRP:(i + 1) * C_GRP],
               mu_m_ref[:, i * C_GRP:(i + 1) * C_GRP]) for i in range(3)]
    tok = _rwkv_token_math(xs[0], xs[1], xs[2], lerp(l_ref[...], pvl_ref[...], mu_l_ref[...]), prm, ones_bd)
    for i, name in enumerate(("r", "k", "v", "a", "b", "logw", "gate", "bonus")):
        out_ref[i] = tok[name]


def _rwkv_sample_step_kernel(s_ref, row_ref, col_ref, s_out_ref, o_ref):
    s = s_ref[0]
    r, k, a, b, logw = [row_ref[0, i] for i in range(5)]
    v, gate, bonus, lnx_w, lnx_b = [col_ref[0, i] for i in range(5)]
    sa = jnp.sum(s * a, axis=-1, keepdims=True)
    s_new = s * jnp.exp(logw) + sa * b + v * k
    s_out_ref[0] = s_new
    y = jnp.sum(s_new * r, axis=-1, keepdims=True)
    mean = jnp.mean(y, axis=1, keepdims=True)
    d = y - mean
    var = jnp.mean(d * d, axis=1, keepdims=True)
    yn = d * lax.rsqrt(var + LNX_EPS) * lnx_w + lnx_b
    o_ref[0] = (yn + bonus) * gate


def _rwkv_sample(pm_s, pl_s, prev_main, prev_lora, mu_main, mu_lora, vecs, w_up, a_up, g_up, wkv0):
    nb = pm_s.shape[0]
    full = lambda shape: pl.BlockSpec(shape, lambda i: (0,) * len(shape))
    tok = pl.pallas_call(
        _rwkv_sample_tok_kernel,
        grid=(1,),
        in_specs=[pl.BlockSpec((nb, 3 * C_GRP), lambda i: (0, 1)), full((nb, C_LORA)),
                  full((nb, 3 * C_GRP)), full((nb, C_LORA)), full((1, 3 * C_GRP)), full((1, C_LORA)),
                  full(vecs.shape), full(w_up.shape), full(a_up.shape), full(g_up.shape)],
        out_specs=full((8, nb, C_GRP)),
        out_shape=jax.ShapeDtypeStruct((8, nb, C_GRP), F32),
        compiler_params=_cparams(1),
        name="rwkv_sample_tok",
    )(pm_s, pl_s, prev_main, prev_lora, mu_main, mu_lora, vecs, w_up, a_up, g_up)
    heads = lambda x: x.reshape(x.shape[0], nb, N_HEADS, HEAD_DIM).transpose(1, 0, 2, 3)
    rows = heads(jnp.concatenate([tok[0:2], tok[3:6]], axis=0))[:, :, :, None, :]
    lnx = jnp.broadcast_to(vecs[5:7, None, :], (2, nb, C_GRP))
    cols = heads(jnp.concatenate([tok[2:3], tok[6:8], lnx], axis=0))[..., None]
    st = (1, N_HEADS, HEAD_DIM, HEAD_DIM)
    s_new, o = pl.pallas_call(
        _rwkv_sample_step_kernel,
        grid=(nb,),
        in_specs=[pl.BlockSpec(st, lambda b: (b, 0, 0, 0)),
                  pl.BlockSpec((1, 5, N_HEADS, 1, HEAD_DIM), lambda b: (b, 0, 0, 0, 0)),
                  pl.BlockSpec((1, 5, N_HEADS, HEAD_DIM, 1), lambda b: (b, 0, 0, 0, 0))],
        out_specs=[pl.BlockSpec(st, lambda b: (b, 0, 0, 0)),
                   pl.BlockSpec((1, N_HEADS, HEAD_DIM, 1), lambda b: (b, 0, 0, 0))],
        out_shape=[jax.ShapeDtypeStruct((nb,) + st[1:], F32),
                   jax.ShapeDtypeStruct((nb, N_HEADS, HEAD_DIM, 1), F32)],
        compiler_params=_cparams(1),
        name="rwkv_sample_step",
    )(wkv0, rows, cols)
    return o.reshape(nb, C_GRP), s_new


def _out_proj_kernel(x_ref, oa_ref, orw_ref, wa_ref, wb_ref, o_ref):
    o_ref[...] = (x_ref[...] + _bdot(oa_ref[...], wa_ref[...]) + _bdot(orw_ref[...], wb_ref[...]))


def _out_proj(x2d, o_att, o_rw, w_o, tm, tn):
    m = x2d.shape[0]
    return pl.pallas_call(
        _out_proj_kernel,
        grid=(m // tm, D_MODEL // tn),
        in_specs=[
            pl.BlockSpec((tm, tn), lambda i, j: (i, j)),
            pl.BlockSpec((tm, C_GRP), lambda i, j: (i, 0)),
            pl.BlockSpec((tm, C_GRP), lambda i, j: (i, 0)),
            pl.BlockSpec((C_GRP, tn), lambda i, j: (0, j)),
            pl.BlockSpec((C_GRP, tn), lambda i, j: (1, j)),
        ],
        out_specs=pl.BlockSpec((tm, tn), lambda i, j: (i, j)),
        out_shape=jax.ShapeDtypeStruct((m, D_MODEL), F32),
        compiler_params=_cparams(2),
        name="out_proj",
    )(x2d, o_att, o_rw, w_o, w_o)


HALO = 16


def _ffn_kernel(*refs, tm, tiles_per_seq, seq_mode):
    if seq_mode:
        (x_ref, xh_ref, g_ref, wg_ref, wv_ref, cwg_ref, cwv_ref, wd_ref, gf_ref,
         y_ref, ug_ref, uv_ref, h_scr, acc_scr) = refs
    else:
        (x_ref, pg_ref, pv_ref, g_ref, wg_ref, wv_ref, cwg_ref, cwv_ref, wd_ref, gf_ref,
         y_ref, ug_ref, uv_ref, h_scr, acc_scr) = refs
    i, j = pl.program_id(0), pl.program_id(1)

    def norm(x):
        ms = jnp.mean(x * x, axis=-1, keepdims=True)
        return x * lax.rsqrt(ms + NORM_EPS) * g_ref[...]

    @pl.when(j == 0)
    def _():
        x = x_ref[...]
        acc_scr[...] = x
        if seq_mode:
            keep = jnp.where(i % tiles_per_seq == 0, 0.0, 1.0).astype(F32)
            h_scr[:HALO, :] = (norm(xh_ref[...]) * keep).astype(BF16)
            h_scr[HALO:, :] = norm(x).astype(BF16)
        else:
            h_scr[...] = norm(x).astype(BF16)

    h = h_scr[...]
    halves = []
    for w_ref, cw_ref, u_ref, p_ref in ((wg_ref, cwg_ref, ug_ref, None if seq_mode else pg_ref),
                                        (wv_ref, cwv_ref, uv_ref, None if seq_mode else pv_ref)):
        u = jnp.dot(h, w_ref[...], preferred_element_type=F32)
        cw = cw_ref[...]
        if seq_mode:
            u_ref[0] = u[HALO + tm - 2:HALO + tm, :]
            c = (cw[3:4] + cw[0:1] * u[HALO - 2:HALO - 2 + tm] + cw[1:2] * u[HALO - 1:HALO - 1 + tm]
                 + cw[2:3] * u[HALO:HALO + tm])
        else:
            u_ref[...] = u
            c = cw[3:4] + cw[0:1] * p_ref[0] + cw[1:2] * p_ref[1] + cw[2:3] * u
        halves.append(c)
    gate, val = halves
    act = gate * _sigmoid(gate) * val
    acc_scr[...] += jnp.dot(act.astype(BF16), wd_ref[...], preferred_element_type=F32)

    @pl.when(j == pl.num_programs(1) - 1)
    def _():
        x2 = acc_scr[...]
        ms = jnp.mean(x2 * x2, axis=-1, keepdims=True)
        y_ref[...] = x2 * lax.rsqrt(ms + NORM_EPS) * gf_ref[...]


def _ffn(x1, norm_g, w_up, conv_wb, w_down, final_g, tm, tf, seq_len=None, prev=None):
    m = x1.shape[0]
    nf = D_FF // tf
    seq_mode = prev is None
    tiles_per_seq = seq_len // tm if seq_mode else 1
    vec = lambda width: pl.BlockSpec((1, width), lambda i, j: (0, 0))
    in_specs = [pl.BlockSpec((tm, D_MODEL), lambda i, j: (i, 0))]
    args = [x1]
    if seq_mode:
        in_specs.append(pl.BlockSpec((HALO, D_MODEL), lambda i, j: (jnp.maximum(i * (tm // HALO) - 1, 0), 0)))
        args.append(x1)
    else:
        in_specs += [pl.BlockSpec((2, tm, tf), lambda i, j: (0, i, j)),
                     pl.BlockSpec((2, tm, tf), lambda i, j: (0, i, nf + j))]
        args += [prev, prev]
    in_specs += [
        vec(D_MODEL),
        pl.BlockSpec((D_MODEL, tf), lambda i, j: (0, j)),
        pl.BlockSpec((D_MODEL, tf), lambda i, j: (0, nf + j)),
        pl.BlockSpec((4, tf), lambda i, j: (0, j)),
        pl.BlockSpec((4, tf), lambda i, j: (0, nf + j)),
        pl.BlockSpec((tf, D_MODEL), lambda i, j: (j, 0)),
        vec(D_MODEL),
    ]
    args += [norm_g, w_up, w_up, conv_wb, conv_wb, w_down, final_g]
    if seq_mode:
        u_spec = pl.BlockSpec((1, 2, tf), lambda i, j: (i, 0, j))
        u_shape = jax.ShapeDtypeStruct((m // tm, 2, D_FF), F32)
    else:
        u_spec = pl.BlockSpec((tm, tf), lambda i, j: (i, j))
        u_shape = jax.ShapeDtypeStruct((m, D_FF), F32)
    rows = tm + HALO if seq_mode else tm
    return pl.pallas_call(
        functools.partial(_ffn_kernel, tm=tm, tiles_per_seq=tiles_per_seq, seq_mode=seq_mode),
        grid=(m // tm, nf),
        in_specs=in_specs,
        out_specs=[pl.BlockSpec((tm, D_MODEL), lambda i, j: (i, 0)), u_spec, u_spec],
        out_shape=[jax.ShapeDtypeStruct((m, D_MODEL), F32), u_shape, u_shape],
        scratch_shapes=[pltpu.VMEM((rows, D_MODEL), BF16), pltpu.VMEM((tm, D_MODEL), F32)],
        compiler_params=_cparams(2),
        name="ffn_seq" if seq_mode else "ffn_state",
    )(*args)


def _layer_params(l, norm_mix_g, w_in, att_out_g, rw_mu, rw_w0, rw_w_up, rw_a0, rw_a_up, rw_g_up, rw_k_k,
                  rw_k_a, rw_r_k, rw_lnx_w, rw_lnx_b, w_o, norm_ffn_g, ffn_w_up, ffn_conv_w, ffn_conv_b,
                  ffn_w_down):
    zeros = jnp.zeros((W_LORA, C_GRP), F32)
    return dict(
        norm_mix_g=norm_mix_g[l][None],
        w_main=w_in[l][:, :C_MAIN].astype(BF16),
        w_lora=w_in[l][:, C_MAIN:].astype(BF16),
        att_g=att_out_g[l][None],
        mu_main=rw_mu[l][:3 * C_GRP].reshape(3, C_GRP),
        mu_lora=rw_mu[l][None, 3 * C_GRP:],
        vecs=jnp.stack([rw_w0[l], rw_a0[l], rw_k_k[l], rw_k_a[l], rw_r_k[l], rw_lnx_w[l], rw_lnx_b[l]]),
        w_up=jnp.concatenate([rw_w_up[l], zeros]).astype(BF16),
        a_up=jnp.concatenate([zeros, rw_a_up[l]]).astype(BF16),
        g_up=rw_g_up[l].astype(BF16),
        w_o=w_o[l].astype(BF16),
        norm_ffn_g=norm_ffn_g[l][None],
        ffn_w_up=ffn_w_up[l].astype(BF16),
        conv_wb=jnp.concatenate([ffn_conv_w[l], ffn_conv_b[l][None]]),
        ffn_w_down=ffn_w_down[l].astype(BF16),
    )


def _prompt_layer(x, lp, final_g):
    nb, seq, _ = x.shape
    x2d = x.reshape(nb * seq, D_MODEL)
    p_main, p_lora = _in_proj(x2d, lp["norm_mix_g"], lp["w_main"], lp["w_lora"], tm=512, tn=512)
    p_main3 = p_main.reshape(nb, seq, C_MAIN)
    p_lora3 = p_lora.reshape(nb, seq, C_LORA)
    o_att = _attn_prompt(p_main3, lp["att_g"])
    o_rw, h_fin = _rwkv_prompt(p_main3, p_lora3, jnp.zeros((nb, 3, C_GRP), F32), jnp.zeros((nb, 1, C_LORA), F32),
                               lp["mu_main"], lp["mu_lora"], lp["vecs"], lp["w_up"], lp["a_up"], lp["g_up"])
    x1 = _out_proj(x2d, o_att.reshape(nb * seq, C_GRP), o_rw.reshape(nb * seq, C_GRP), lp["w_o"], tm=512, tn=1024)
    y, u_g, u_v = _ffn(x1, lp["norm_ffn_g"], lp["ffn_w_up"], lp["conv_wb"], lp["ffn_w_down"], final_g,
                       tm=512, tf=512, seq_len=seq)
    hd = (N_HEADS, HEAD_DIM)
    k_new = p_main3[:, :, C_GRP:2 * C_GRP].reshape((nb, seq) + hd)
    v_new = p_main3[:, :, 2 * C_GRP:3 * C_GRP].reshape((nb, seq) + hd)
    rw_last = jnp.concatenate([p_main3[:, -1:, 3 * C_GRP:], p_lora3[:, -1:, :]], axis=-1)
    wkv = jnp.stack([h_fin[:, :, :HEAD_DIM, :HEAD_DIM], h_fin[:, :, HEAD_DIM:, HEAD_DIM:]], axis=2)
    wkv = wkv.reshape(nb, N_HEADS, HEAD_DIM, HEAD_DIM)
    tiles_per_seq = u_g.shape[0] // nb
    ffn_last = jnp.concatenate([u_g, u_v], axis=-1)[tiles_per_seq - 1::tiles_per_seq]
    return y.reshape(nb, seq, D_MODEL), k_new, v_new, rw_last, wkv, ffn_last


def _sample_layer(x, cache_k, cache_v, rw_prev, wkv0, ffn_prev, lp, final_g):
    nb = x.shape[0]
    x2d = x.reshape(nb, D_MODEL)
    p_main, p_lora = _in_proj(x2d, lp["norm_mix_g"], lp["w_main"], lp["w_lora"], tm=nb, tn=512)
    hd = (N_HEADS, HEAD_DIM)
    q = p_main[:, :C_GRP].reshape((nb,) + hd)
    k_new = p_main[:, C_GRP:2 * C_GRP].reshape((nb,) + hd)
    v_new = p_main[:, 2 * C_GRP:3 * C_GRP].reshape((nb,) + hd)
    o_att = _attn_sample(q, k_new, v_new, cache_k, cache_v, lp["att_g"].reshape(hd)).reshape(nb, C_GRP)
    prev = rw_prev.reshape(nb, C_SHIFT)
    o_rw, wkv = _rwkv_sample(p_main, p_lora, prev[:, :3 * C_GRP], prev[:, 3 * C_GRP:],
                             lp["mu_main"].reshape(1, 3 * C_GRP), lp["mu_lora"], lp["vecs"],
                             lp["w_up"], lp["a_up"], lp["g_up"], wkv0)
    x1 = _out_proj(x2d, o_att, o_rw, lp["w_o"], tm=nb, tn=1024)
    prev_rows = ffn_prev.transpose(1, 0, 2)
    y, u_g, u_v = _ffn(x1, lp["norm_ffn_g"], lp["ffn_w_up"], lp["conv_wb"], lp["ffn_w_down"], final_g,
                       tm=nb, tf=512, prev=prev_rows)
    rw_last = jnp.concatenate([p_main[:, 3 * C_GRP:], p_lora], axis=-1)[:, None, :]
    ffn_last = jnp.stack([ffn_prev[:, 1, :], jnp.concatenate([u_g, u_v], axis=-1)], axis=1)
    return y.reshape(nb, 1, D_MODEL), k_new[:, None], v_new[:, None], rw_last, wkv, ffn_last


def kernel(x_prompt, x_sample, cache_att_k, cache_att_v, state_rwkv_shift, state_rwkv_wkv, state_ffn_conv, norm_mix_g, w_in, att_out_g, rw_mu, rw_w0, rw_w_up, rw_a0, rw_a_up, rw_g_up, rw_k_k, rw_k_a, rw_r_k, rw_lnx_w, rw_lnx_b, w_o, norm_ffn_g, ffn_w_up, ffn_conv_w, ffn_conv_b, ffn_w_down, norm_final_g):
    depth = w_in.shape[0]
    assert depth == 1, "the fused FFN + final-norm kernel assumes a single trunk layer"
    assert x_sample.shape[1] == 1, "the sample path handles one new token per sequence"
    final_g = norm_final_g[None]
    lp = _layer_params(0, norm_mix_g, w_in, att_out_g, rw_mu, rw_w0, rw_w_up, rw_a0, rw_a_up, rw_g_up, rw_k_k,
                       rw_k_a, rw_r_k, rw_lnx_w, rw_lnx_b, w_o, norm_ffn_g, ffn_w_up, ffn_conv_w, ffn_conv_b,
                       ffn_w_down)
    yp, pk, pv, prw, pwkv, pffn = _prompt_layer(x_prompt, lp, final_g)
    ys, sk, sv, srw, swkv, sffn = _sample_layer(x_sample, cache_att_k[0], cache_att_v[0], state_rwkv_shift[0],
                                                state_rwkv_wkv[0], state_ffn_conv[0], lp, final_g)
    lead = lambda t: t[None]
    return (yp, ys, lead(pk), lead(pv), lead(prw), lead(pwkv), lead(pffn),
            lead(sk), lead(sv), lead(srw), lead(swkv), lead(sffn))
```

```python
import functools

import jax
import jax.numpy as jnp
from jax import lax
from jax.experimental import pallas as pl
from jax.experimental.pallas import tpu as pltpu

F32 = jnp.float32
BF16 = jnp.bfloat16

D_MODEL = 2048
HEAD_DIM = 64
N_HEADS = 16
C_GRP = N_HEADS * HEAD_DIM
W_LORA, A_LORA, G_LORA = 64, 64, 160
C_LORA = W_LORA + A_LORA + G_LORA
C_SHIFT = 3 * C_GRP + C_LORA
C_MAIN = 6 * C_GRP
D_FF = 5632
DIL_RATES = (1, 4, 16)
N_BACK = 128
ATT_SCALE = HEAD_DIM ** -0.5
NORM_EPS = 1e-6
LNX_EPS = HEAD_DIM * 1e-5
NEG_BIG = -1e30

LANES = 128
VMEM_LIMIT = 48 * 1024 * 1024

CHUNK = 64
PAIR = 2 * HEAD_DIM


def _cparams(n_grid):
    return pltpu.CompilerParams(dimension_semantics=("arbitrary",) * n_grid,
                                vmem_limit_bytes=VMEM_LIMIT)


def _bdot(a, b):
    return jnp.dot(a.astype(BF16), b.astype(BF16), preferred_element_type=F32)


def _bdot_nt(a, b):
    return lax.dot_general(a.astype(BF16), b.astype(BF16), (((1,), (1,)), ((), ())),
                           preferred_element_type=F32)


def _bdot_tn(a, b):
    return jnp.dot(a.astype(F32).T.astype(BF16), b.astype(BF16), preferred_element_type=F32)


def _split_dot(mat01, x, terms):
    acc = None
    rem = x
    for _ in range(terms):
        piece = rem.astype(BF16)
        part = jnp.dot(mat01, piece, preferred_element_type=F32)
        acc = part if acc is None else acc + part
        rem = rem - piece.astype(F32)
    return acc


def _split_dot_r(x, mat01, terms):
    acc = None
    rem = x
    for _ in range(terms):
        piece = rem.astype(BF16)
        part = jnp.dot(piece, mat01, preferred_element_type=F32)
        acc = part if acc is None else acc + part
        rem = rem - piece.astype(F32)
    return acc


def _head_ones(n):
    r = lax.broadcasted_iota(jnp.int32, (n, n), 0) // HEAD_DIM
    c = lax.broadcasted_iota(jnp.int32, (n, n), 1) // HEAD_DIM
    return jnp.where(r == c, 1.0, 0.0).astype(BF16)


def _sigmoid(x):
    return 1.0 / (1.0 + jnp.exp(-x))


def _softplus(x):
    return jnp.maximum(x, 0.0) + jnp.log(1.0 + jnp.exp(-jnp.abs(x)))


def _in_proj_kernel(x_ref, g_ref, wm_ref, wl_ref, om_ref, ol_ref, h_scr):
    @pl.when(pl.program_id(1) == 0)
    def _():
        x = x_ref[...]
        ms = jnp.mean(x * x, axis=-1, keepdims=True)
        h = (x * lax.rsqrt(ms + NORM_EPS) * g_ref[...]).astype(BF16)
        h_scr[...] = h
        ol_ref[...] = jnp.dot(h, wl_ref[...], preferred_element_type=F32)

    om_ref[...] = jnp.dot(h_scr[...], wm_ref[...], preferred_element_type=F32)


def _in_proj(x2d, g, w_main, w_lora, tm, tn):
    m = x2d.shape[0]
    return pl.pallas_call(
        _in_proj_kernel,
        grid=(m // tm, C_MAIN // tn),
        in_specs=[
            pl.BlockSpec((tm, D_MODEL), lambda i, j: (i, 0)),
            pl.BlockSpec((1, D_MODEL), lambda i, j: (0, 0)),
            pl.BlockSpec((D_MODEL, tn), lambda i, j: (0, j)),
            pl.BlockSpec((D_MODEL, C_LORA), lambda i, j: (0, 0)),
        ],
        out_specs=[
            pl.BlockSpec((tm, tn), lambda i, j: (i, j)),
            pl.BlockSpec((tm, C_LORA), lambda i, j: (i, 0)),
        ],
        out_shape=[jax.ShapeDtypeStruct((m, C_MAIN), F32), jax.ShapeDtypeStruct((m, C_LORA), F32)],
        scratch_shapes=[pltpu.VMEM((tm, D_MODEL), BF16)],
        compiler_params=_cparams(2),
        name="in_proj",
    )(x2d, g, w_main, w_lora)


def _attn_prompt_kernel(q_ref, k_ref, v_ref, g_ref, o_ref, ob_scr, lb_scr, *, seq):
    blk = N_BACK
    lane = lax.broadcasted_iota(jnp.int32, (blk, PAIR), 1)
    is_a = lane < HEAD_DIM

    def one_head(qm, kk, vm, bias):
        s = _bdot_nt(qm, kk) + bias
        m = jnp.max(s, axis=-1, keepdims=True)
        e = jnp.exp(s - m)
        den = jnp.sum(e, axis=-1, keepdims=True)
        o = _bdot(e / den, vm)
        return o, m + jnp.log(den)

    def both_heads(q, kk, vv, bias):
        qs = q * ATT_SCALE
        oa, la = one_head(jnp.where(is_a, qs, 0.0), kk, jnp.where(is_a[:1], vv, 0.0), bias)
        ob, lb = one_head(jnp.where(is_a, 0.0, qs), kk, jnp.where(is_a[:1], 0.0, vv), bias)
        return oa + ob, jnp.where(is_a, la, lb)

    row2 = lax.broadcasted_iota(jnp.int32, (blk, 2 * blk), 0)
    col2 = lax.broadcasted_iota(jnp.int32, (blk, 2 * blk), 1)
    band_bias = jnp.where((col2 >= row2) & (col2 <= row2 + N_BACK), 0.0, NEG_BIG)
    prev_half = jnp.where(col2 < blk, 1.0, 0.0)
    row1 = lax.broadcasted_iota(jnp.int32, (blk, blk), 0)
    col1 = lax.broadcasted_iota(jnp.int32, (blk, blk), 1)
    causal_bias = jnp.where(col1 <= row1, 0.0, NEG_BIG)

    for br, rate in enumerate(DIL_RATES):
        nblk = seq // rate // blk

        if nblk == 1:
            def body1(c, carry, rate=rate, br=br):
                rows = pl.ds(c, blk, stride=rate)
                o, l = both_heads(q_ref[rows, :], k_ref[rows, :], v_ref[rows, :], causal_bias)
                ob_scr[br, rows, :] = o
                lb_scr[br, rows, :] = l
                return carry
            lax.fori_loop(0, rate, body1, 0)
        else:
            def body2(idx, carry, rate=rate, br=br, nblk=nblk):
                c = idx // nblk
                b = idx % nblk
                rows = pl.ds(c + b * (blk * rate), blk, stride=rate)
                prow = pl.ds(c + jnp.maximum(b - 1, 0) * (blk * rate), blk, stride=rate)
                kk = jnp.concatenate([k_ref[prow, :], k_ref[rows, :]], axis=0)
                vv = jnp.concatenate([v_ref[prow, :], v_ref[rows, :]], axis=0)
                first = jnp.where(b == 0, NEG_BIG, 0.0).astype(F32)
                o, l = both_heads(q_ref[rows, :], kk, vv, band_bias + first * prev_half)
                ob_scr[br, rows, :] = o
                lb_scr[br, rows, :] = l
                return carry
            lax.fori_loop(0, rate * nblk, body2, 0)

    ones_bd = _head_ones(PAIR)
    gain = g_ref[...]
    tile = 256

    def merge(i, carry):
        rows = pl.ds(pl.multiple_of(i * tile, tile), tile)
        l0, l1, l2 = lb_scr[0, rows, :], lb_scr[1, rows, :], lb_scr[2, rows, :]
        m = jnp.maximum(jnp.maximum(l0, l1), l2)
        w0, w1, w2 = jnp.exp(l0 - m), jnp.exp(l1 - m), jnp.exp(l2 - m)
        o = (w0 * ob_scr[0, rows, :] + w1 * ob_scr[1, rows, :] + w2 * ob_scr[2, rows, :]) / (w0 + w1 + w2)
        ms = _split_dot_r(o * o, ones_bd, 2) * (1.0 / HEAD_DIM)
        o_ref[rows, :] = (o * lax.rsqrt(ms + NORM_EPS) * gain).astype(o_ref.dtype)
        return carry
    lax.fori_loop(0, seq // tile, merge, 0)


def _attn_prompt(p_main3, att_g):
    nb, seq, _ = p_main3.shape
    npair = C_GRP // PAIR
    col = lambda off: pl.BlockSpec((None, seq, PAIR), lambda b, p, off=off: (b, 0, off + p))
    return pl.pallas_call(
        functools.partial(_attn_prompt_kernel, seq=seq),
        grid=(nb, npair),
        in_specs=[col(0), col(npair), col(2 * npair), pl.BlockSpec((1, PAIR), lambda b, p: (0, p))],
        out_specs=pl.BlockSpec((None, seq, PAIR), lambda b, p: (b, 0, p)),
        out_shape=jax.ShapeDtypeStruct((nb, seq, C_GRP), BF16),
        scratch_shapes=[pltpu.VMEM((3, seq, PAIR), F32), pltpu.VMEM((3, seq, PAIR), F32)],
        compiler_params=_cparams(2),
        name="attn_prompt",
    )(p_main3, p_main3, p_main3, att_g)


SAMPLE_HEADS_PER_STEP = 4


def _attn_sample_kernel(q_ref, kn_ref, vn_ref, kt_ref, vt_ref, g_ref, o_ref, *, win):
    for h in range(SAMPLE_HEADS_PER_STEP):
        qc = q_ref[0, h] * ATT_SCALE
        kn, vn = kn_ref[0, h], vn_ref[0, h]
        s_all = jnp.sum(kt_ref[0, h] * qc, axis=0, keepdims=True)
        s_new = jnp.sum(kn * qc, axis=0, keepdims=True)
        outs, lses = [], []
        for rate in DIL_RATES:
            lo = win - N_BACK * rate
            s = s_all[:, lo:]
            if rate > 1:
                pos = lax.broadcasted_iota(jnp.int32, s.shape, 1)
                s = jnp.where((pos & (rate - 1)) == 0, s, NEG_BIG)
            m = jnp.maximum(jnp.max(s, axis=-1, keepdims=True), s_new)
            e = jnp.exp(s - m)
            e_new = jnp.exp(s_new - m)
            den = jnp.sum(e, axis=-1, keepdims=True) + e_new
            o = jnp.sum(vt_ref[0, h, :, lo:] * (e / den), axis=-1, keepdims=True) + (e_new / den) * vn
            outs.append(o)
            lses.append(m + jnp.log(den))
        m = jnp.maximum(jnp.maximum(lses[0], lses[1]), lses[2])
        ws = [jnp.exp(l - m) for l in lses]
        o = (ws[0] * outs[0] + ws[1] * outs[1] + ws[2] * outs[2]) / (ws[0] + ws[1] + ws[2])
        ms = jnp.mean(o * o, axis=0, keepdims=True)
        o_ref[0, h] = o * lax.rsqrt(ms + NORM_EPS) * g_ref[h]


def _attn_sample(q, kn, vn, cache_kt, cache_vt, att_g):
    nb, win = cache_kt.shape[0], cache_kt.shape[-1]
    assert win % (N_BACK * max(DIL_RATES)) == 0 and win % LANES == 0
    hs = SAMPLE_HEADS_PER_STEP
    tok = pl.BlockSpec((1, hs, HEAD_DIM, 1), lambda b, g: (b, g, 0, 0))
    cache = pl.BlockSpec((1, hs, HEAD_DIM, win), lambda b, g: (b, g, 0, 0))
    return pl.pallas_call(
        functools.partial(_attn_sample_kernel, win=win),
        grid=(nb, N_HEADS // hs),
        in_specs=[tok, tok, tok, cache, cache, pl.BlockSpec((hs, HEAD_DIM, 1), lambda b, g: (g, 0, 0))],
        out_specs=tok,
        out_shape=jax.ShapeDtypeStruct((nb, N_HEADS, HEAD_DIM, 1), F32),
        compiler_params=_cparams(2),
        name="attn_sample",
    )(q, kn, vn, cache_kt, cache_vt, att_g)


def _rwkv_token_math(xr, xk, xv, xl, prm, ones_bd):
    w_raw = prm["w0"] + _bdot(jnp.tanh(xl[:, :2 * W_LORA]), prm["w_up"])
    logw = -jnp.exp(-_softplus(-w_raw) - 0.5)
    a_sig = _sigmoid(prm["a0"] + _bdot(xl[:, :2 * W_LORA], prm["a_up"]))
    gate = _bdot(_sigmoid(xl[:, 2 * W_LORA:]), prm["g_up"])
    kk = xk * prm["k_k"]
    nrm = jnp.sqrt(_split_dot_r(kk * kk, ones_bd, 2))
    kk = kk / jnp.maximum(nrm, 1e-12)
    k_eff = xk * (1.0 + (a_sig - 1.0) * prm["k_a"])
    bonus = _split_dot_r(xr * k_eff * prm["r_k"], ones_bd, 2) * xv
    return dict(r=xr, k=k_eff, v=xv, a=-kk, b=kk * a_sig, logw=logw, gate=gate, bonus=bonus)


def _group_norm_gate(y, bonus, gate, lnx_w, lnx_b, ones_bd):
    mean = _split_dot_r(y, ones_bd, 2) * (1.0 / HEAD_DIM)
    d = y - mean
    var = _split_dot_r(d * d, ones_bd, 2) * (1.0 / HEAD_DIM)
    yn = d * lax.rsqrt(var + LNX_EPS) * lnx_w + lnx_b
    return (yn + bonus) * gate


_RW_VEC_NAMES = ("w0", "a0", "k_k", "k_a", "r_k", "lnx_w", "lnx_b")


def _rwkv_prompt_kernel(r_ref, k_ref, v_ref, l_ref, pm_ref, pl_ref, mu_m_ref, mu_l_ref, vec_ref,
                        wup_ref, aup_ref, gup_ref, o_ref, h_ref,
                        tok_scr, y_scr, rw_scr, y0_scr, g_scr, ha_scr, pc_scr, *, seq):
    n = PAIR
    ones_bd = _head_ones(n)
    row = lax.broadcasted_iota(jnp.int32, (seq, 1), 0)

    def shift(x, prev):
        return jnp.where(row == 0, prev, pltpu.roll(x, 1, axis=0))

    def lerp(x, prev, mu):
        return x + mu * (shift(x, prev) - x)

    pm = pm_ref[...]
    mu_m = mu_m_ref[...]
    prm = {name: vec_ref[i:i + 1, :] for i, name in enumerate(_RW_VEC_NAMES)}
    prm.update(w_up=wup_ref[...], a_up=aup_ref[...], g_up=gup_ref[...])
    tok = _rwkv_token_math(lerp(r_ref[...], pm[0:1], mu_m[0:1]),
                           lerp(k_ref[...], pm[1:2], mu_m[1:2]),
                           lerp(v_ref[...], pm[2:3], mu_m[2:3]),
                           lerp(l_ref[...], pl_ref[...], mu_l_ref[...]), prm, ones_bd)
    for i, name in enumerate(("r", "k", "v", "a", "b", "logw", "gate", "bonus")):
        tok_scr[i] = tok[name]

    nch = seq // CHUNK
    lane = lax.broadcasted_iota(jnp.int32, (CHUNK, n), 1)
    is_a = lane < HEAD_DIM
    ri = lax.broadcasted_iota(jnp.int32, (n, n), 0)
    ci = lax.broadcasted_iota(jnp.int32, (n, n), 1)
    strict = ri > ci
    incl = ri >= ci
    eye = ri == ci
    rc = lax.broadcasted_iota(jnp.int32, (CHUNK, CHUNK), 0)
    cc = lax.broadcasted_iota(jnp.int32, (CHUNK, CHUNK), 1)

    def stack(x):
        return jnp.concatenate([jnp.where(is_a, x, 0.0), jnp.where(is_a, 0.0, x)], axis=0)

    def phase_a(c, carry):
        tril_ones = jnp.where(rc >= cc, 1.0, 0.0).astype(BF16)
        rows = pl.ds(pl.multiple_of(c * CHUNK, CHUNK), CHUNK)
        r_c, k_c, v_c, a_c, b_c, lw_c = [tok_scr[i, rows, :] for i in range(6)]
        lcum = _split_dot(tril_ones, lw_c, 3)
        lend = lcum[CHUNK - 1:CHUNK, :]
        e_pos, e_neg = jnp.exp(lcum), jnp.exp(-lcum)
        e_prev, e_end = jnp.exp(lcum - lw_c), jnp.exp(lend - lcum)
        a_s, r_s = stack(a_c * e_prev).astype(BF16), stack(r_c * e_pos).astype(BF16)
        b_s, k_s = stack(b_c * e_neg).astype(BF16), stack(k_c * e_neg).astype(BF16)
        v_s = stack(v_c).astype(BF16)
        be_s, ke_s = stack(b_c * e_end).astype(BF16), stack(k_c * e_end).astype(BF16)

        s_ab = jnp.where(strict, _bdot_nt(a_s, b_s), 0.0)
        s_ak = jnp.where(strict, _bdot_nt(a_s, k_s), 0.0)
        s_rb = jnp.where(incl, _bdot_nt(r_s, b_s), 0.0)
        s_rk = jnp.where(incl, _bdot_nt(r_s, k_s), 0.0)

        apow = s_ab
        tinv = jnp.where(eye, 1.0, 0.0) + s_ab
        steps = CHUNK.bit_length() - 2
        for _ in range(steps):
            apow = _bdot(apow, apow)
            tinv = tinv + _bdot(tinv, apow)

        w_m = _bdot(tinv, a_s).astype(BF16)
        u0 = _bdot(tinv, _bdot(s_ak, v_s)).astype(BF16)
        rw_scr[c] = _bdot(s_rb, w_m) + r_s.astype(F32)
        y0_scr[c] = _bdot(s_rb, u0) + _bdot(s_rk, v_s)
        g_scr[c] = _bdot_tn(w_m, be_s)
        ha_scr[c] = _bdot_tn(u0, be_s) + _bdot_tn(v_s, ke_s)
        pc_scr[c] = jnp.exp(lend)
        return carry
    lax.fori_loop(0, nch, phase_a, 0)

    h_ref[...] = jnp.zeros((n, n), F32)

    def phase_b(c, carry):
        rows = pl.ds(pl.multiple_of(c * CHUNK, CHUNK), CHUNK)
        ht = h_ref[...]
        hb = ht.astype(BF16)
        ys = _bdot_nt(rw_scr[c], hb) + y0_scr[c]
        y_scr[rows, :] = ys[:CHUNK, :] + ys[CHUNK:, :]
        h_ref[...] = ht * pc_scr[c] + _bdot(hb, g_scr[c]) + ha_scr[c]
        return carry
    lax.fori_loop(0, nch, phase_b, 0)

    o_ref[...] = _group_norm_gate(y_scr[...], tok_scr[7], tok_scr[6], prm["lnx_w"], prm["lnx_b"],
                                  _head_ones(n)).astype(o_ref.dtype)


def _rwkv_prompt(p_main3, p_lora3, prev_main, prev_lora, mu_main, mu_lora, vecs, w_up, a_up, g_up):
    nb, seq, _ = p_main3.shape
    npair = C_GRP // PAIR
    col = lambda off: pl.BlockSpec((None, seq, PAIR), lambda b, p, off=off: (b, 0, off + p))
    pcol = lambda rows: pl.BlockSpec((rows, PAIR), lambda b, p: (0, p))
    return pl.pallas_call(
        functools.partial(_rwkv_prompt_kernel, seq=seq),
        grid=(nb, npair),
        in_specs=[
            col(3 * npair), col(4 * npair), col(5 * npair),
            pl.BlockSpec((None, seq, C_LORA), lambda b, p: (b, 0, 0)),
            pl.BlockSpec((None, 3, PAIR), lambda b, p: (b, 0, p)),
            pl.BlockSpec((None, 1, C_LORA), lambda b, p: (b, 0, 0)),
            pcol(3), pl.BlockSpec((1, C_LORA), lambda b, p: (0, 0)), pcol(len(_RW_VEC_NAMES)),
            pcol(2 * W_LORA), pcol(2 * A_LORA), pcol(G_LORA),
        ],
        out_specs=[
            pl.BlockSpec((None, seq, PAIR), lambda b, p: (b, 0, p)),
            pl.BlockSpec((None, None, PAIR, PAIR), lambda b, p: (b, p, 0, 0)),
        ],
        out_shape=[jax.ShapeDtypeStruct((nb, seq, C_GRP), BF16),
                   jax.ShapeDtypeStruct((nb, npair, PAIR, PAIR), F32)],
        scratch_shapes=[
            pltpu.VMEM((8, seq, PAIR), F32), pltpu.VMEM((seq, PAIR), F32),
            pltpu.VMEM((seq // CHUNK, PAIR, PAIR), F32), pltpu.VMEM((seq // CHUNK, PAIR, PAIR), F32),
            pltpu.VMEM((seq // CHUNK, PAIR, PAIR), F32), pltpu.VMEM((seq // CHUNK, PAIR, PAIR), F32),
            pltpu.VMEM((seq // CHUNK, 1, PAIR), F32),
        ],
        compiler_params=_cparams(2),
        name="rwkv_prompt",
    )(p_main3, p_main3, p_main3, p_lora3, prev_main, prev_lora, mu_main, mu_lora, vecs, w_up, a_up, g_up)


def _rwkv_sample_tok_kernel(pm_ref, l_ref, pvm_ref, pvl_ref, mu_m_ref, mu_l_ref, vec_ref,
                            wup_ref, aup_ref, gup_ref, out_ref):
    ones_bd = _head_ones(C_GRP)
    prm = {name: vec_ref[i:i + 1, :] for i, name in enumerate(_RW_VEC_NAMES)}
    prm.update(w_up=wup_ref[...], a_up=aup_ref[...], g_up=gup_ref[...])
    lerp = lambda x, prev, mu: x + mu * (prev - x)
    xs = [lerp(pm_ref[:, i * C_GRP:(i + 1) * C_GRP], pvm_ref[:, i * C_GRP:(i + 1) * C_GRP],
               mu_m_ref[:, i * C_GRP:(i + 1) * C_GRP]) for i in range(3)]
    tok = _rwkv_token_math(xs[0], xs[1], xs[2], lerp(l_ref[...], pvl_ref[...], mu_l_ref[...]), prm, ones_bd)
    for i, name in enumerate(("r", "k", "v", "a", "b", "logw", "gate", "bonus")):
        out_ref[i] = tok[name]


def _rwkv_sample_step_kernel(s_ref, row_ref, col_ref, s_out_ref, o_ref):
    s = s_ref[0]
    r, k, a, b, logw = [row_ref[0, i] for i in range(5)]
    v, gate, bonus, lnx_w, lnx_b = [col_ref[0, i] for i in range(5)]
    sa = jnp.sum(s * a, axis=-1, keepdims=True)
    s_new = s * jnp.exp(logw) + sa * b + v * k
    s_out_ref[0] = s_new
    y = jnp.sum(s_new * r, axis=-1, keepdims=True)
    mean = jnp.mean(y, axis=1, keepdims=True)
    d = y - mean
    var = jnp.mean(d * d, axis=1, keepdims=True)
    yn = d * lax.rsqrt(var + LNX_EPS) * lnx_w + lnx_b
    o_ref[0] = (yn + bonus) * gate


def _rwkv_sample(pm_s, pl_s, prev_main, prev_lora, mu_main, mu_lora, vecs, w_up, a_up, g_up, wkv0):
    nb = pm_s.shape[0]
    full = lambda shape: pl.BlockSpec(shape, lambda i: (0,) * len(shape))
    tok = pl.pallas_call(
        _rwkv_sample_tok_kernel,
        grid=(1,),
        in_specs=[pl.BlockSpec((nb, 3 * C_GRP), lambda i: (0, 1)), full((nb, C_LORA)),
                  full((nb, 3 * C_GRP)), full((nb, C_LORA)), full((1, 3 * C_GRP)), full((1, C_LORA)),
                  full(vecs.shape), full(w_up.shape), full(a_up.shape), full(g_up.shape)],
        out_specs=full((8, nb, C_GRP)),
        out_shape=jax.ShapeDtypeStruct((8, nb, C_GRP), F32),
        compiler_params=_cparams(1),
        name="rwkv_sample_tok",
    )(pm_s, pl_s, prev_main, prev_lora, mu_main, mu_lora, vecs, w_up, a_up, g_up)
    heads = lambda x: x.reshape(x.shape[0], nb, N_HEADS, HEAD_DIM).transpose(1, 0, 2, 3)
    rows = heads(jnp.concatenate([tok[0:2], tok[3:6]], axis=0))[:, :, :, None, :]
    lnx = jnp.broadcast_to(vecs[5:7, None, :], (2, nb, C_GRP))
    cols = heads(jnp.concatenate([tok[2:3], tok[6:8], lnx], axis=0))[..., None]
    st = (1, N_HEADS, HEAD_DIM, HEAD_DIM)
    s_new, o = pl.pallas_call(
        _rwkv_sample_step_kernel,
        grid=(nb,),
        in_specs=[pl.BlockSpec(st, lambda b: (b, 0, 0, 0)),
                  pl.BlockSpec((1, 5, N_HEADS, 1, HEAD_DIM), lambda b: (b, 0, 0, 0, 0)),
                  pl.BlockSpec((1, 5, N_HEADS, HEAD_DIM, 1), lambda b: (b, 0, 0, 0, 0))],
        out_specs=[pl.BlockSpec(st, lambda b: (b, 0, 0, 0)),
                   pl.BlockSpec((1, N_HEADS, HEAD_DIM, 1), lambda b: (b, 0, 0, 0))],
        out_shape=[jax.ShapeDtypeStruct((nb,) + st[1:], F32),
                   jax.ShapeDtypeStruct((nb, N_HEADS, HEAD_DIM, 1), F32)],
        compiler_params=_cparams(1),
        name="rwkv_sample_step",
    )(wkv0, rows, cols)
    return o.reshape(nb, C_GRP), s_new


def _out_proj_kernel(x_ref, oa_ref, orw_ref, wa_ref, wb_ref, o_ref):
    o_ref[...] = (x_ref[...] + _bdot(oa_ref[...], wa_ref[...]) + _bdot(orw_ref[...], wb_ref[...]))


def _out_proj(x2d, o_att, o_rw, w_o, tm, tn):
    m = x2d.shape[0]
    return pl.pallas_call(
        _out_proj_kernel,
        grid=(m // tm, D_MODEL // tn),
        in_specs=[
            pl.BlockSpec((tm, tn), lambda i, j: (i, j)),
            pl.BlockSpec((tm, C_GRP), lambda i, j: (i, 0)),
            pl.BlockSpec((tm, C_GRP), lambda i, j: (i, 0)),
            pl.BlockSpec((C_GRP, tn), lambda i, j: (0, j)),
            pl.BlockSpec((C_GRP, tn), lambda i, j: (1, j)),
        ],
        out_specs=pl.BlockSpec((tm, tn), lambda i, j: (i, j)),
        out_shape=jax.ShapeDtypeStruct((m, D_MODEL), F32),
        compiler_params=_cparams(2),
        name="out_proj",
    )(x2d, o_att, o_rw, w_o, w_o)


HALO = 16


def _ffn_kernel(*refs, tm, tiles_per_seq, seq_mode):
    if seq_mode:
        (x_ref, xh_ref, g_ref, wg_ref, wv_ref, cwg_ref, cwv_ref, wd_ref, gf_ref,
         y_ref, ug_ref, uv_ref, h_scr, acc_scr) = refs
    else:
        (x_ref, pg_ref, pv_ref, g_ref, wg_ref, wv_ref, cwg_ref, cwv_ref, wd_ref, gf_ref,
         y_ref, ug_ref, uv_ref, h_scr, acc_scr) = refs
    i, j = pl.program_id(0), pl.program_id(1)

    def norm(x):
        ms = jnp.mean(x * x, axis=-1, keepdims=True)
        return x * lax.rsqrt(ms + NORM_EPS) * g_ref[...]

    @pl.when(j == 0)
    def _():
        x = x_ref[...]
        acc_scr[...] = x
        if seq_mode:
            keep = jnp.where(i % tiles_per_seq == 0, 0.0, 1.0).astype(F32)
            h_scr[:HALO, :] = (norm(xh_ref[...]) * keep).astype(BF16)
            h_scr[HALO:, :] = norm(x).astype(BF16)
        else:
            h_scr[...] = norm(x).astype(BF16)

    h = h_scr[...]
    halves = []
    for w_ref, cw_ref, u_ref, p_ref in ((wg_ref, cwg_ref, ug_ref, None if seq_mode else pg_ref),
                                        (wv_ref, cwv_ref, uv_ref, None if seq_mode else pv_ref)):
        u = jnp.dot(h, w_ref[...], preferred_element_type=F32)
        cw = cw_ref[...]
        if seq_mode:
            u_ref[0] = u[HALO + tm - 2:HALO + tm, :]
            c = (cw[3:4] + cw[0:1] * u[HALO - 2:HALO - 2 + tm] + cw[1:2] * u[HALO - 1:HALO - 1 + tm]
                 + cw[2:3] * u[HALO:HALO + tm])
        else:
            u_ref[...] = u
            c = cw[3:4] + cw[0:1] * p_ref[0] + cw[1:2] * p_ref[1] + cw[2:3] * u
        halves.append(c)
    gate, val = halves
    act = gate * _sigmoid(gate) * val
    acc_scr[...] += jnp.dot(act.astype(BF16), wd_ref[...], preferred_element_type=F32)

    @pl.when(j == pl.num_programs(1) - 1)
    def _():
        x2 = acc_scr[...]
        ms = jnp.mean(x2 * x2, axis=-1, keepdims=True)
        y_ref[...] = x2 * lax.rsqrt(ms + NORM_EPS) * gf_ref[...]


def _ffn(x1, norm_g, w_up, conv_wb, w_down, final_g, tm, tf, seq_len=None, prev=None):
    m = x1.shape[0]
    nf = D_FF // tf
    seq_mode = prev is None
    tiles_per_seq = seq_len // tm if seq_mode else 1
    vec = lambda width: pl.BlockSpec((1, width), lambda i, j: (0, 0))
    in_specs = [pl.BlockSpec((tm, D_MODEL), lambda i, j: (i, 0))]
    args = [x1]
    if seq_mode:
        in_specs.append(pl.BlockSpec((HALO, D_MODEL), lambda i, j: (jnp.maximum(i * (tm // HALO) - 1, 0), 0)))
        args.append(x1)
    else:
        in_specs += [pl.BlockSpec((2, tm, tf), lambda i, j: (0, i, j)),
                     pl.BlockSpec((2, tm, tf), lambda i, j: (0, i, nf + j))]
        args += [prev, prev]
    in_specs += [
        vec(D_MODEL),
        pl.BlockSpec((D_MODEL, tf), lambda i, j: (0, j)),
        pl.BlockSpec((D_MODEL, tf), lambda i, j: (0, nf + j)),
        pl.BlockSpec((4, tf), lambda i, j: (0, j)),
        pl.BlockSpec((4, tf), lambda i, j: (0, nf + j)),
        pl.BlockSpec((tf, D_MODEL), lambda i, j: (j, 0)),
        vec(D_MODEL),
    ]
    args += [norm_g, w_up, w_up, conv_wb, conv_wb, w_down, final_g]
    if seq_mode:
        u_spec = pl.BlockSpec((1, 2, tf), lambda i, j: (i, 0, j))
        u_shape = jax.ShapeDtypeStruct((m // tm, 2, D_FF), F32)
    else:
        u_spec = pl.BlockSpec((tm, tf), lambda i, j: (i, j))
        u_shape = jax.ShapeDtypeStruct((m, D_FF), F32)
    rows = tm + HALO if seq_mode else tm
    return pl.pallas_call(
        functools.partial(_ffn_kernel, tm=tm, tiles_per_seq=tiles_per_seq, seq_mode=seq_mode),
        grid=(m // tm, nf),
        in_specs=in_specs,
        out_specs=[pl.BlockSpec((tm, D_MODEL), lambda i, j: (i, 0)), u_spec, u_spec],
        out_shape=[jax.ShapeDtypeStruct((m, D_MODEL), F32), u_shape, u_shape],
        scratch_shapes=[pltpu.VMEM((rows, D_MODEL), BF16), pltpu.VMEM((tm, D_MODEL), F32)],
        compiler_params=_cparams(2),
        name="ffn_seq" if seq_mode else "ffn_state",
    )(*args)


def _layer_params(l, norm_mix_g, w_in, att_out_g, rw_mu, rw_w0, rw_w_up, rw_a0, rw_a_up, rw_g_up, rw_k_k,
                  rw_k_a, rw_r_k, rw_lnx_w, rw_lnx_b, w_o, norm_ffn_g, ffn_w_up, ffn_conv_w, ffn_conv_b,
                  ffn_w_down):
    zeros = jnp.zeros((W_LORA, C_GRP), F32)
    return dict(
        norm_mix_g=norm_mix_g[l][None],
        w_main=w_in[l][:, :C_MAIN].astype(BF16),
        w_lora=w_in[l][:, C_MAIN:].astype(BF16),
        att_g=att_out_g[l][None],
        mu_main=rw_mu[l][:3 * C_GRP].reshape(3, C_GRP),
        mu_lora=rw_mu[l][None, 3 * C_GRP:],
        vecs=jnp.stack([rw_w0[l], rw_a0[l], rw_k_k[l], rw_k_a[l], rw_r_k[l], rw_lnx_w[l], rw_lnx_b[l]]),
        w_up=jnp.concatenate([rw_w_up[l], zeros]).astype(BF16),
        a_up=jnp.concatenate([zeros, rw_a_up[l]]).astype(BF16),
        g_up=rw_g_up[l].astype(BF16),
        w_o=w_o[l].astype(BF16),
        norm_ffn_g=norm_ffn_g[l][None],
        ffn_w_up=ffn_w_up[l].astype(BF16),
        conv_wb=jnp.concatenate([ffn_conv_w[l], ffn_conv_b[l][None]]),
        ffn_w_down=ffn_w_down[l].astype(BF16),
    )


def _prompt_layer(x, lp, final_g):
    nb, seq, _ = x.shape
    x2d = x.reshape(nb * seq, D_MODEL)
    p_main, p_lora = _in_proj(x2d, lp["norm_mix_g"], lp["w_main"], lp["w_lora"], tm=512, tn=512)
    p_main3 = p_main.reshape(nb, seq, C_MAIN)
    p_lora3 = p_lora.reshape(nb, seq, C_LORA)
    o_att = _attn_prompt(p_main3, lp["att_g"])
    o_rw, h_fin = _rwkv_prompt(p_main3, p_lora3, jnp.zeros((nb, 3, C_GRP), F32), jnp.zeros((nb, 1, C_LORA), F32),
                               lp["mu_main"], lp["mu_lora"], lp["vecs"], lp["w_up"], lp["a_up"], lp["g_up"])
    x1 = _out_proj(x2d, o_att.reshape(nb * seq, C_GRP), o_rw.reshape(nb * seq, C_GRP), lp["w_o"], tm=512, tn=1024)
    y, u_g, u_v = _ffn(x1, lp["norm_ffn_g"], lp["ffn_w_up"], lp["conv_wb"], lp["ffn_w_down"], final_g,
                       tm=512, tf=512, seq_len=seq)
    hd = (N_HEADS, HEAD_DIM)
    k_new = p_main3[:, :, C_GRP:2 * C_GRP].reshape((nb, seq) + hd)
    v_new = p_main3[:, :, 2 * C_GRP:3 * C_GRP].reshape((nb, seq) + hd)
    rw_last = jnp.concatenate([p_main3[:, -1:, 3 * C_GRP:], p_lora3[:, -1:, :]], axis=-1)
    wkv = jnp.stack([h_fin[:, :, :HEAD_DIM, :HEAD_DIM], h_fin[:, :, HEAD_DIM:, HEAD_DIM:]], axis=2)
    wkv = wkv.reshape(nb, N_HEADS, HEAD_DIM, HEAD_DIM)
    tiles_per_seq = u_g.shape[0] // nb
    ffn_last = jnp.concatenate([u_g, u_v], axis=-1)[tiles_per_seq - 1::tiles_per_seq]
    return y.reshape(nb, seq, D_MODEL), k_new, v_new, rw_last, wkv, ffn_last


def _sample_layer(x, cache_k, cache_v, rw_prev, wkv0, ffn_prev, lp, final_g):
    nb = x.shape[0]
    x2d = x.reshape(nb, D_MODEL)
    p_main, p_lora = _in_proj(x2d, lp["norm_mix_g"], lp["w_main"], lp["w_lora"], tm=nb, tn=512)
    hd = (N_HEADS, HEAD_DIM)
    q = p_main[:, :C_GRP].reshape((nb,) + hd)
    k_new = p_main[:, C_GRP:2 * C_GRP].reshape((nb,) + hd)
    v_new = p_main[:, 2 * C_GRP:3 * C_GRP].reshape((nb,) + hd)
    cache_kt, cache_vt = cache_k.transpose(0, 2, 3, 1), cache_v.transpose(0, 2, 3, 1)
    o_att = _attn_sample(q[..., None], k_new[..., None], v_new[..., None], cache_kt, cache_vt,
                         lp["att_g"].reshape(hd + (1,))).reshape(nb, C_GRP)
    prev = rw_prev.reshape(nb, C_SHIFT)
    o_rw, wkv = _rwkv_sample(p_main, p_lora, prev[:, :3 * C_GRP], prev[:, 3 * C_GRP:],
                             lp["mu_main"].reshape(1, 3 * C_GRP), lp["mu_lora"], lp["vecs"],
                             lp["w_up"], lp["a_up"], lp["g_up"], wkv0)
    x1 = _out_proj(x2d, o_att, o_rw, lp["w_o"], tm=nb, tn=1024)
    prev_rows = ffn_prev.transpose(1, 0, 2)
    y, u_g, u_v = _ffn(x1, lp["norm_ffn_g"], lp["ffn_w_up"], lp["conv_wb"], lp["ffn_w_down"], final_g,
                       tm=nb, tf=512, prev=prev_rows)
    rw_last = jnp.concatenate([p_main[:, 3 * C_GRP:], p_lora], axis=-1)[:, None, :]
    ffn_last = jnp.stack([ffn_prev[:, 1, :], jnp.concatenate([u_g, u_v], axis=-1)], axis=1)
    return y.reshape(nb, 1, D_MODEL), k_new[:, None], v_new[:, None], rw_last, wkv, ffn_last


def kernel(x_prompt, x_sample, cache_att_k, cache_att_v, state_rwkv_shift, state_rwkv_wkv, state_ffn_conv, norm_mix_g, w_in, att_out_g, rw_mu, rw_w0, rw_w_up, rw_a0, rw_a_up, rw_g_up, rw_k_k, rw_k_a, rw_r_k, rw_lnx_w, rw_lnx_b, w_o, norm_ffn_g, ffn_w_up, ffn_conv_w, ffn_conv_b, ffn_w_down, norm_final_g):
    depth = w_in.shape[0]
    assert depth == 1, "the fused FFN + final-norm kernel assumes a single trunk layer"
    assert x_sample.shape[1] == 1, "the sample path handles one new token per sequence"
    final_g = norm_final_g[None]
    lp = _layer_params(0, norm_mix_g, w_in, att_out_g, rw_mu, rw_w0, rw_w_up, rw_a0, rw_a_up, rw_g_up, rw_k_k,
                       rw_k_a, rw_r_k, rw_lnx_w, rw_lnx_b, w_o, norm_ffn_g, ffn_w_up, ffn_conv_w, ffn_conv_b,
                       ffn_w_down)
    yp, pk, pv, prw, pwkv, pffn = _prompt_layer(x_prompt, lp, final_g)
    ys, sk, sv, srw, swkv, sffn = _sample_layer(x_sample, cache_att_k[0], cache_att_v[0], state_rwkv_shift[0],
                                                state_rwkv_wkv[0], state_ffn_conv[0], lp, final_g)
    lead = lambda t: t[None]
    return (yp, ys, lead(pk), lead(pv), lead(prw), lead(pwkv), lead(pffn),
            lead(sk), lead(sv), lead(srw), lead(swkv), lead(sffn))
```

```python
import functools

import jax
import jax.numpy as jnp
from jax import lax
from jax.experimental import pallas as pl
from jax.experimental.pallas import tpu as pltpu

F32 = jnp.float32
BF16 = jnp.bfloat16

D_MODEL = 2048
HEAD_DIM = 64
N_HEADS = 16
C_GRP = N_HEADS * HEAD_DIM
W_LORA, A_LORA, G_LORA = 64, 64, 160
C_LORA = W_LORA + A_LORA + G_LORA
C_SHIFT = 3 * C_GRP + C_LORA
C_MAIN = 6 * C_GRP
D_FF = 5632
DIL_RATES = (1, 4, 16)
N_BACK = 128
ATT_SCALE = HEAD_DIM ** -0.5
NORM_EPS = 1e-6
LNX_EPS = HEAD_DIM * 1e-5
NEG_BIG = -1e30

LANES = 128
VMEM_LIMIT = 48 * 1024 * 1024

CHUNK = 64
PHASE_A_UNROLL = 8
PAIR = 2 * HEAD_DIM


def _cparams(n_grid):
    return pltpu.CompilerParams(dimension_semantics=("arbitrary",) * n_grid,
                                vmem_limit_bytes=VMEM_LIMIT)


def _bdot(a, b):
    return jnp.dot(a.astype(BF16), b.astype(BF16), preferred_element_type=F32)


def _bdot_nt(a, b):
    return lax.dot_general(a.astype(BF16), b.astype(BF16), (((1,), (1,)), ((), ())),
                           preferred_element_type=F32)


def _bdot_tn(a, b):
    return jnp.dot(a.astype(F32).T.astype(BF16), b.astype(BF16), preferred_element_type=F32)


def _split_dot(mat01, x, terms):
    acc = None
    rem = x
    for _ in range(terms):
        piece = rem.astype(BF16)
        part = jnp.dot(mat01, piece, preferred_element_type=F32)
        acc = part if acc is None else acc + part
        rem = rem - piece.astype(F32)
    return acc


def _split_dot_r(x, mat01, terms):
    acc = None
    rem = x
    for _ in range(terms):
        piece = rem.astype(BF16)
        part = jnp.dot(piece, mat01, preferred_element_type=F32)
        acc = part if acc is None else acc + part
        rem = rem - piece.astype(F32)
    return acc


def _head_ones(n):
    r = lax.broadcasted_iota(jnp.int32, (n, n), 0) // HEAD_DIM
    c = lax.broadcasted_iota(jnp.int32, (n, n), 1) // HEAD_DIM
    return jnp.where(r == c, 1.0, 0.0).astype(BF16)


def _sigmoid(x):
    return 1.0 / (1.0 + jnp.exp(-x))


def _softplus(x):
    return jnp.maximum(x, 0.0) + jnp.log(1.0 + jnp.exp(-jnp.abs(x)))


def _in_proj_kernel(x_ref, g_ref, wm_ref, wl_ref, om_ref, ol_ref, h_scr):
    @pl.when(pl.program_id(1) == 0)
    def _():
        x = x_ref[...]
        ms = jnp.mean(x * x, axis=-1, keepdims=True)
        h = (x * lax.rsqrt(ms + NORM_EPS) * g_ref[...]).astype(BF16)
        h_scr[...] = h
        ol_ref[...] = jnp.dot(h, wl_ref[...], preferred_element_type=F32)

    om_ref[...] = jnp.dot(h_scr[...], wm_ref[...], preferred_element_type=F32)


def _in_proj(x2d, g, w_main, w_lora, tm, tn):
    m = x2d.shape[0]
    return pl.pallas_call(
        _in_proj_kernel,
        grid=(m // tm, C_MAIN // tn),
        in_specs=[
            pl.BlockSpec((tm, D_MODEL), lambda i, j: (i, 0)),
            pl.BlockSpec((1, D_MODEL), lambda i, j: (0, 0)),
            pl.BlockSpec((D_MODEL, tn), lambda i, j: (0, j)),
            pl.BlockSpec((D_MODEL, C_LORA), lambda i, j: (0, 0)),
        ],
        out_specs=[
            pl.BlockSpec((tm, tn), lambda i, j: (i, j)),
            pl.BlockSpec((tm, C_LORA), lambda i, j: (i, 0)),
        ],
        out_shape=[jax.ShapeDtypeStruct((m, C_MAIN), F32), jax.ShapeDtypeStruct((m, C_LORA), F32)],
        scratch_shapes=[pltpu.VMEM((tm, D_MODEL), BF16)],
        compiler_params=_cparams(2),
        name="in_proj",
    )(x2d, g, w_main, w_lora)


def _attn_prompt_kernel(q_ref, k_ref, v_ref, g_ref, o_ref, ob_scr, lb_scr, *, seq):
    blk = N_BACK
    lane = lax.broadcasted_iota(jnp.int32, (blk, PAIR), 1)
    is_a = lane < HEAD_DIM

    def one_head(qm, kk, vm, bias):
        s = _bdot_nt(qm, kk) + bias
        m = jnp.max(s, axis=-1, keepdims=True)
        e = jnp.exp(s - m)
        den = jnp.sum(e, axis=-1, keepdims=True)
        o = _bdot(e / den, vm)
        return o, m + jnp.log(den)

    def both_heads(q, kk, vv, bias):
        qs = q * ATT_SCALE
        oa, la = one_head(jnp.where(is_a, qs, 0.0), kk, jnp.where(is_a[:1], vv, 0.0), bias)
        ob, lb = one_head(jnp.where(is_a, 0.0, qs), kk, jnp.where(is_a[:1], 0.0, vv), bias)
        return oa + ob, jnp.where(is_a, la, lb)

    row2 = lax.broadcasted_iota(jnp.int32, (blk, 2 * blk), 0)
    col2 = lax.broadcasted_iota(jnp.int32, (blk, 2 * blk), 1)
    band_bias = jnp.where((col2 >= row2) & (col2 <= row2 + N_BACK), 0.0, NEG_BIG)
    prev_half = jnp.where(col2 < blk, 1.0, 0.0)
    row1 = lax.broadcasted_iota(jnp.int32, (blk, blk), 0)
    col1 = lax.broadcasted_iota(jnp.int32, (blk, blk), 1)
    causal_bias = jnp.where(col1 <= row1, 0.0, NEG_BIG)

    for br, rate in enumerate(DIL_RATES):
        nblk = seq // rate // blk

        if nblk == 1:
            def body1(c, carry, rate=rate, br=br):
                rows = pl.ds(c, blk, stride=rate)
                o, l = both_heads(q_ref[rows, :], k_ref[rows, :], v_ref[rows, :], causal_bias)
                ob_scr[br, rows, :] = o
                lb_scr[br, rows, :] = l
                return carry
            lax.fori_loop(0, rate, body1, 0)
        else:
            def body2(idx, carry, rate=rate, br=br, nblk=nblk):
                c = idx // nblk
                b = idx % nblk
                rows = pl.ds(c + b * (blk * rate), blk, stride=rate)
                prow = pl.ds(c + jnp.maximum(b - 1, 0) * (blk * rate), blk, stride=rate)
                kk = jnp.concatenate([k_ref[prow, :], k_ref[rows, :]], axis=0)
                vv = jnp.concatenate([v_ref[prow, :], v_ref[rows, :]], axis=0)
                first = jnp.where(b == 0, NEG_BIG, 0.0).astype(F32)
                o, l = both_heads(q_ref[rows, :], kk, vv, band_bias + first * prev_half)
                ob_scr[br, rows, :] = o
                lb_scr[br, rows, :] = l
                return carry
            lax.fori_loop(0, rate * nblk, body2, 0)

    ones_bd = _head_ones(PAIR)
    gain = g_ref[...]
    tile = 256

    def merge(i, carry):
        rows = pl.ds(pl.multiple_of(i * tile, tile), tile)
        l0, l1, l2 = lb_scr[0, rows, :], lb_scr[1, rows, :], lb_scr[2, rows, :]
        m = jnp.maximum(jnp.maximum(l0, l1), l2)
        w0, w1, w2 = jnp.exp(l0 - m), jnp.exp(l1 - m), jnp.exp(l2 - m)
        o = (w0 * ob_scr[0, rows, :] + w1 * ob_scr[1, rows, :] + w2 * ob_scr[2, rows, :]) / (w0 + w1 + w2)
        ms = _split_dot_r(o * o, ones_bd, 2) * (1.0 / HEAD_DIM)
        o_ref[rows, :] = (o * lax.rsqrt(ms + NORM_EPS) * gain).astype(o_ref.dtype)
        return carry
    lax.fori_loop(0, seq // tile, merge, 0)


def _attn_prompt(p_main3, att_g):
    nb, seq, _ = p_main3.shape
    npair = C_GRP // PAIR
    col = lambda off: pl.BlockSpec((None, seq, PAIR), lambda b, p, off=off: (b, 0, off + p))
    return pl.pallas_call(
        functools.partial(_attn_prompt_kernel, seq=seq),
        grid=(nb, npair),
        in_specs=[col(0), col(npair), col(2 * npair), pl.BlockSpec((1, PAIR), lambda b, p: (0, p))],
        out_specs=pl.BlockSpec((None, seq, PAIR), lambda b, p: (b, 0, p)),
        out_shape=jax.ShapeDtypeStruct((nb, seq, C_GRP), BF16),
        scratch_shapes=[pltpu.VMEM((3, seq, PAIR), F32), pltpu.VMEM((3, seq, PAIR), F32)],
        compiler_params=_cparams(2),
        name="attn_prompt",
    )(p_main3, p_main3, p_main3, att_g)


SAMPLE_HEADS_PER_STEP = 4


def _attn_sample_kernel(q_ref, kn_ref, vn_ref, kt_ref, vt_ref, g_ref, o_ref, *, win):
    for h in range(SAMPLE_HEADS_PER_STEP):
        qc = q_ref[0, h] * ATT_SCALE
        kn, vn = kn_ref[0, h], vn_ref[0, h]
        s_all = jnp.sum(kt_ref[0, h] * qc, axis=0, keepdims=True)
        s_new = jnp.sum(kn * qc, axis=0, keepdims=True)
        outs, lses = [], []
        for rate in DIL_RATES:
            lo = win - N_BACK * rate
            s = s_all[:, lo:]
            if rate > 1:
                pos = lax.broadcasted_iota(jnp.int32, s.shape, 1)
                s = jnp.where((pos & (rate - 1)) == 0, s, NEG_BIG)
            m = jnp.maximum(jnp.max(s, axis=-1, keepdims=True), s_new)
            e = jnp.exp(s - m)
            e_new = jnp.exp(s_new - m)
            den = jnp.sum(e, axis=-1, keepdims=True) + e_new
            o = jnp.sum(vt_ref[0, h, :, lo:] * (e / den), axis=-1, keepdims=True) + (e_new / den) * vn
            outs.append(o)
            lses.append(m + jnp.log(den))
        m = jnp.maximum(jnp.maximum(lses[0], lses[1]), lses[2])
        ws = [jnp.exp(l - m) for l in lses]
        o = (ws[0] * outs[0] + ws[1] * outs[1] + ws[2] * outs[2]) / (ws[0] + ws[1] + ws[2])
        ms = jnp.mean(o * o, axis=0, keepdims=True)
        o_ref[0, h] = o * lax.rsqrt(ms + NORM_EPS) * g_ref[h]


def _attn_sample(q, kn, vn, cache_kt, cache_vt, att_g):
    nb, win = cache_kt.shape[0], cache_kt.shape[-1]
    assert win % (N_BACK * max(DIL_RATES)) == 0 and win % LANES == 0
    hs = SAMPLE_HEADS_PER_STEP
    tok = pl.BlockSpec((1, hs, HEAD_DIM, 1), lambda b, g: (b, g, 0, 0))
    cache = pl.BlockSpec((1, hs, HEAD_DIM, win), lambda b, g: (b, g, 0, 0))
    return pl.pallas_call(
        functools.partial(_attn_sample_kernel, win=win),
        grid=(nb, N_HEADS // hs),
        in_specs=[tok, tok, tok, cache, cache, pl.BlockSpec((hs, HEAD_DIM, 1), lambda b, g: (g, 0, 0))],
        out_specs=tok,
        out_shape=jax.ShapeDtypeStruct((nb, N_HEADS, HEAD_DIM, 1), F32),
        compiler_params=_cparams(2),
        name="attn_sample",
    )(q, kn, vn, cache_kt, cache_vt, att_g)


def _rwkv_token_math(xr, xk, xv, xl, prm, ones_bd):
    w_raw = prm["w0"] + _bdot(jnp.tanh(xl[:, :2 * W_LORA]), prm["w_up"])
    logw = -jnp.exp(-_softplus(-w_raw) - 0.5)
    a_sig = _sigmoid(prm["a0"] + _bdot(xl[:, :2 * W_LORA], prm["a_up"]))
    gate = _bdot(_sigmoid(xl[:, 2 * W_LORA:]), prm["g_up"])
    kk = xk * prm["k_k"]
    nrm = jnp.sqrt(_split_dot_r(kk * kk, ones_bd, 2))
    kk = kk / jnp.maximum(nrm, 1e-12)
    k_eff = xk * (1.0 + (a_sig - 1.0) * prm["k_a"])
    bonus = _split_dot_r(xr * k_eff * prm["r_k"], ones_bd, 2) * xv
    return dict(r=xr, k=k_eff, v=xv, a=-kk, b=kk * a_sig, logw=logw, gate=gate, bonus=bonus)


def _group_norm_gate(y, bonus, gate, lnx_w, lnx_b, ones_bd):
    mean = _split_dot_r(y, ones_bd, 2) * (1.0 / HEAD_DIM)
    d = y - mean
    var = _split_dot_r(d * d, ones_bd, 2) * (1.0 / HEAD_DIM)
    yn = d * lax.rsqrt(var + LNX_EPS) * lnx_w + lnx_b
    return (yn + bonus) * gate


_RW_VEC_NAMES = ("w0", "a0", "k_k", "k_a", "r_k", "lnx_w", "lnx_b")


def _rwkv_prompt_kernel(r_ref, k_ref, v_ref, l_ref, pm_ref, pl_ref, mu_m_ref, mu_l_ref, vec_ref,
                        wup_ref, aup_ref, gup_ref, o_ref, h_ref,
                        tok_scr, y_scr, rw_scr, y0_scr, g_scr, ha_scr, pc_scr, *, seq):
    n = PAIR
    ones_bd = _head_ones(n)
    row = lax.broadcasted_iota(jnp.int32, (seq, 1), 0)

    def shift(x, prev):
        return jnp.where(row == 0, prev, pltpu.roll(x, 1, axis=0))

    def lerp(x, prev, mu):
        return x + mu * (shift(x, prev) - x)

    pm = pm_ref[...]
    mu_m = mu_m_ref[...]
    prm = {name: vec_ref[i:i + 1, :] for i, name in enumerate(_RW_VEC_NAMES)}
    prm.update(w_up=wup_ref[...], a_up=aup_ref[...], g_up=gup_ref[...])
    tok = _rwkv_token_math(lerp(r_ref[...], pm[0:1], mu_m[0:1]),
                           lerp(k_ref[...], pm[1:2], mu_m[1:2]),
                           lerp(v_ref[...], pm[2:3], mu_m[2:3]),
                           lerp(l_ref[...], pl_ref[...], mu_l_ref[...]), prm, ones_bd)
    for i, name in enumerate(("r", "k", "v", "a", "b", "logw", "gate", "bonus")):
        tok_scr[i] = tok[name]

    nch = seq // CHUNK
    lane = lax.broadcasted_iota(jnp.int32, (CHUNK, n), 1)
    is_a = lane < HEAD_DIM
    ri = lax.broadcasted_iota(jnp.int32, (n, n), 0)
    ci = lax.broadcasted_iota(jnp.int32, (n, n), 1)
    strict = ri > ci
    incl = ri >= ci
    eye = ri == ci
    rc = lax.broadcasted_iota(jnp.int32, (CHUNK, CHUNK), 0)
    cc = lax.broadcasted_iota(jnp.int32, (CHUNK, CHUNK), 1)

    def stack(x):
        return jnp.concatenate([jnp.where(is_a, x, 0.0), jnp.where(is_a, 0.0, x)], axis=0)

    def phase_a(g, carry):
        cs = [g * PHASE_A_UNROLL + u for u in range(PHASE_A_UNROLL)]
        each = lambda fn, *lists: [fn(*xs) for xs in zip(*lists)]
        tril_ones = jnp.where(rc >= cc, 1.0, 0.0).astype(BF16)
        tok = [[tok_scr[i, pl.ds(pl.multiple_of(c * CHUNK, CHUNK), CHUNK), :] for i in range(6)] for c in cs]
        r_c, k_c, v_c, a_c, b_c, lw_c = [list(x) for x in zip(*tok)]
        lcum = each(lambda lw: _split_dot(tril_ones, lw, 3), lw_c)
        lend = each(lambda l: l[CHUNK - 1:CHUNK, :], lcum)
        sb = lambda x: stack(x).astype(BF16)
        a_s = each(lambda a, l, lw: sb(a * jnp.exp(l - lw)), a_c, lcum, lw_c)
        r_s = each(lambda r, l: sb(r * jnp.exp(l)), r_c, lcum)
        b_s = each(lambda b, l: sb(b * jnp.exp(-l)), b_c, lcum)
        k_s = each(lambda k, l: sb(k * jnp.exp(-l)), k_c, lcum)
        v_s = each(sb, v_c)
        be_s = each(lambda b, l, le: sb(b * jnp.exp(le - l)), b_c, lcum, lend)
        ke_s = each(lambda k, l, le: sb(k * jnp.exp(le - l)), k_c, lcum, lend)

        sc = each(lambda a, r, b, k: _bdot_nt(jnp.concatenate([a, r], axis=0), jnp.concatenate([b, k], axis=0)),
                  a_s, r_s, b_s, k_s)
        s_ab = each(lambda s: jnp.where(strict, s[:n, :n], 0.0), sc)
        s_ak = each(lambda s: jnp.where(strict, s[:n, n:], 0.0).astype(BF16), sc)
        s_rb = each(lambda s: jnp.where(incl, s[n:, :n], 0.0).astype(BF16), sc)
        s_rk = each(lambda s: jnp.where(incl, s[n:, n:], 0.0).astype(BF16), sc)

        apow = each(lambda s: s.astype(BF16), s_ab)
        tinv = each(lambda s: jnp.where(eye, 1.0, 0.0) + s, s_ab)
        for _ in range(CHUNK.bit_length() - 2):
            apow = each(lambda p: _bdot(p, p).astype(BF16), apow)
            tinv = each(lambda t, p: t + _bdot(t, p), tinv, apow)

        x1 = each(_bdot, s_ak, v_s)
        wu = each(lambda t, a, x: _bdot(t, jnp.concatenate([a, x.astype(BF16)], axis=1)), tinv, a_s, x1)
        yk = each(_bdot, s_rk, v_s)
        ry = each(lambda s, w: _bdot(s, w), s_rb, wu)
        gh = each(lambda w, be: _bdot(w.T, be), wu, be_s)
        hk = each(lambda v, ke: _bdot_tn(v, ke), v_s, ke_s)
        for u, c in enumerate(cs):
            rw_scr[c] = ry[u][:, :n] + r_s[u].astype(F32)
            y0_scr[c] = ry[u][:, n:] + yk[u]
            g_scr[c] = gh[u][:n, :]
            ha_scr[c] = gh[u][n:, :] + hk[u]
            pc_scr[c] = jnp.exp(lend[u])
        return carry
    lax.fori_loop(0, nch // PHASE_A_UNROLL, phase_a, 0)

    h_ref[...] = jnp.zeros((n, n), F32)

    def phase_b(c, carry):
        rows = pl.ds(pl.multiple_of(c * CHUNK, CHUNK), CHUNK)
        ht = h_ref[...]
        hb = ht.astype(BF16)
        ys = _bdot_nt(rw_scr[c], hb) + y0_scr[c]
        y_scr[rows, :] = ys[:CHUNK, :] + ys[CHUNK:, :]
        h_ref[...] = ht * pc_scr[c] + _bdot(hb, g_scr[c]) + ha_scr[c]
        return carry
    lax.fori_loop(0, nch, phase_b, 0)

    o_ref[...] = _group_norm_gate(y_scr[...], tok_scr[7], tok_scr[6], prm["lnx_w"], prm["lnx_b"],
                                  _head_ones(n)).astype(o_ref.dtype)


def _rwkv_prompt(p_main3, p_lora3, prev_main, prev_lora, mu_main, mu_lora, vecs, w_up, a_up, g_up):
    nb, seq, _ = p_main3.shape
    npair = C_GRP // PAIR
    col = lambda off: pl.BlockSpec((None, seq, PAIR), lambda b, p, off=off: (b, 0, off + p))
    pcol = lambda rows: pl.BlockSpec((rows, PAIR), lambda b, p: (0, p))
    return pl.pallas_call(
        functools.partial(_rwkv_prompt_kernel, seq=seq),
        grid=(nb, npair),
        in_specs=[
            col(3 * npair), col(4 * npair), col(5 * npair),
            pl.BlockSpec((None, seq, C_LORA), lambda b, p: (b, 0, 0)),
            pl.BlockSpec((None, 3, PAIR), lambda b, p: (b, 0, p)),
            pl.BlockSpec((None, 1, C_LORA), lambda b, p: (b, 0, 0)),
            pcol(3), pl.BlockSpec((1, C_LORA), lambda b, p: (0, 0)), pcol(len(_RW_VEC_NAMES)),
            pcol(2 * W_LORA), pcol(2 * A_LORA), pcol(G_LORA),
        ],
        out_specs=[
            pl.BlockSpec((None, seq, PAIR), lambda b, p: (b, 0, p)),
            pl.BlockSpec((None, None, PAIR, PAIR), lambda b, p: (b, p, 0, 0)),
        ],
        out_shape=[jax.ShapeDtypeStruct((nb, seq, C_GRP), BF16),
                   jax.ShapeDtypeStruct((nb, npair, PAIR, PAIR), F32)],
        scratch_shapes=[
            pltpu.VMEM((8, seq, PAIR), F32), pltpu.VMEM((seq, PAIR), F32),
            pltpu.VMEM((seq // CHUNK, PAIR, PAIR), F32), pltpu.VMEM((seq // CHUNK, PAIR, PAIR), F32),
            pltpu.VMEM((seq // CHUNK, PAIR, PAIR), F32), pltpu.VMEM((seq // CHUNK, PAIR, PAIR), F32),
            pltpu.VMEM((seq // CHUNK, 1, PAIR), F32),
        ],
        compiler_params=_cparams(2),
        name="rwkv_prompt",
    )(p_main3, p_main3, p_main3, p_lora3, prev_main, prev_lora, mu_main, mu_lora, vecs, w_up, a_up, g_up)


def _rwkv_sample_tok_kernel(pm_ref, l_ref, pvm_ref, pvl_ref, mu_m_ref, mu_l_ref, vec_ref,
                            wup_ref, aup_ref, gup_ref, out_ref):
    ones_bd = _head_ones(C_GRP)
    prm = {name: vec_ref[i:i + 1, :] for i, name in enumerate(_RW_VEC_NAMES)}
    prm.update(w_up=wup_ref[...], a_up=aup_ref[...], g_up=gup_ref[...])
    lerp = lambda x, prev, mu: x + mu * (prev - x)
    xs = [lerp(pm_ref[:, i * C_GRP:(i + 1) * C_GRP], pvm_ref[:, i * C_GRP:(i + 1) * C_GRP],
               mu_m_ref[:, i * C_GRP:(i + 1) * C_GRP]) for i in range(3)]
    tok = _rwkv_token_math(xs[0], xs[1], xs[2], lerp(l_ref[...], pvl_ref[...], mu_l_ref[...]), prm, ones_bd)
    for i, name in enumerate(("r", "k", "v", "a", "b", "logw", "gate", "bonus")):
        out_ref[i] = tok[name]


def _rwkv_sample_step_kernel(s_ref, row_ref, col_ref, s_out_ref, o_ref):
    s = s_ref[0]
    r, k, a, b, logw = [row_ref[0, i] for i in range(5)]
    v, gate, bonus, lnx_w, lnx_b = [col_ref[0, i] for i in range(5)]
    sa = jnp.sum(s * a, axis=-1, keepdims=True)
    s_new = s * jnp.exp(logw) + sa * b + v * k
    s_out_ref[0] = s_new
    y = jnp.sum(s_new * r, axis=-1, keepdims=True)
    mean = jnp.mean(y, axis=1, keepdims=True)
    d = y - mean
    var = jnp.mean(d * d, axis=1, keepdims=True)
    yn = d * lax.rsqrt(var + LNX_EPS) * lnx_w + lnx_b
    o_ref[0] = (yn + bonus) * gate


def _rwkv_sample(pm_s, pl_s, prev_main, prev_lora, mu_main, mu_lora, vecs, w_up, a_up, g_up, wkv0):
    nb = pm_s.shape[0]
    full = lambda shape: pl.BlockSpec(shape, lambda i: (0,) * len(shape))
    tok = pl.pallas_call(
        _rwkv_sample_tok_kernel,
        grid=(1,),
        in_specs=[pl.BlockSpec((nb, 3 * C_GRP), lambda i: (0, 1)), full((nb, C_LORA)),
                  full((nb, 3 * C_GRP)), full((nb, C_LORA)), full((1, 3 * C_GRP)), full((1, C_LORA)),
                  full(vecs.shape), full(w_up.shape), full(a_up.shape), full(g_up.shape)],
        out_specs=full((8, nb, C_GRP)),
        out_shape=jax.ShapeDtypeStruct((8, nb, C_GRP), F32),
        compiler_params=_cparams(1),
        name="rwkv_sample_tok",
    )(pm_s, pl_s, prev_main, prev_lora, mu_main, mu_lora, vecs, w_up, a_up, g_up)
    heads = lambda x: x.reshape(x.shape[0], nb, N_HEADS, HEAD_DIM).transpose(1, 0, 2, 3)
    rows = heads(jnp.concatenate([tok[0:2], tok[3:6]], axis=0))[:, :, :, None, :]
    lnx = jnp.broadcast_to(vecs[5:7, None, :], (2, nb, C_GRP))
    cols = heads(jnp.concatenate([tok[2:3], tok[6:8], lnx], axis=0))[..., None]
    st = (1, N_HEADS, HEAD_DIM, HEAD_DIM)
    s_new, o = pl.pallas_call(
        _rwkv_sample_step_kernel,
        grid=(nb,),
        in_specs=[pl.BlockSpec(st, lambda b: (b, 0, 0, 0)),
                  pl.BlockSpec((1, 5, N_HEADS, 1, HEAD_DIM), lambda b: (b, 0, 0, 0, 0)),
                  pl.BlockSpec((1, 5, N_HEADS, HEAD_DIM, 1), lambda b: (b, 0, 0, 0, 0))],
        out_specs=[pl.BlockSpec(st, lambda b: (b, 0, 0, 0)),
                   pl.BlockSpec((1, N_HEADS, HEAD_DIM, 1), lambda b: (b, 0, 0, 0))],
        out_shape=[jax.ShapeDtypeStruct((nb,) + st[1:], F32),
                   jax.ShapeDtypeStruct((nb, N_HEADS, HEAD_DIM, 1), F32)],
        compiler_params=_cparams(1),
        name="rwkv_sample_step",
    )(wkv0, rows, cols)
    return o.reshape(nb, C_GRP), s_new


def _out_proj_kernel(x_ref, oa_ref, orw_ref, wa_ref, wb_ref, o_ref):
    o_ref[...] = (x_ref[...] + _bdot(oa_ref[...], wa_ref[...]) + _bdot(orw_ref[...], wb_ref[...]))


def _out_proj(x2d, o_att, o_rw, w_o, tm, tn):
    m = x2d.shape[0]
    return pl.pallas_call(
        _out_proj_kernel,
        grid=(m // tm, D_MODEL // tn),
        in_specs=[
            pl.BlockSpec((tm, tn), lambda i, j: (i, j)),
            pl.BlockSpec((tm, C_GRP), lambda i, j: (i, 0)),
            pl.BlockSpec((tm, C_GRP), lambda i, j: (i, 0)),
            pl.BlockSpec((C_GRP, tn), lambda i, j: (0, j)),
            pl.BlockSpec((C_GRP, tn), lambda i, j: (1, j)),
        ],
        out_specs=pl.BlockSpec((tm, tn), lambda i, j: (i, j)),
        out_shape=jax.ShapeDtypeStruct((m, D_MODEL), F32),
        compiler_params=_cparams(2),
        name="out_proj",
    )(x2d, o_att, o_rw, w_o, w_o)


HALO = 16


def _ffn_kernel(*refs, tm, tiles_per_seq, seq_mode):
    if seq_mode:
        (x_ref, xh_ref, g_ref, wg_ref, wv_ref, cwg_ref, cwv_ref, wd_ref, gf_ref,
         y_ref, ug_ref, uv_ref, h_scr, acc_scr) = refs
    else:
        (x_ref, pg_ref, pv_ref, g_ref, wg_ref, wv_ref, cwg_ref, cwv_ref, wd_ref, gf_ref,
         y_ref, ug_ref, uv_ref, h_scr, acc_scr) = refs
    i, j = pl.program_id(0), pl.program_id(1)

    def norm(x):
        ms = jnp.mean(x * x, axis=-1, keepdims=True)
        return x * lax.rsqrt(ms + NORM_EPS) * g_ref[...]

    @pl.when(j == 0)
    def _():
        x = x_ref[...]
        acc_scr[...] = x
        if seq_mode:
            keep = jnp.where(i % tiles_per_seq == 0, 0.0, 1.0).astype(F32)
            h_scr[:HALO, :] = (norm(xh_ref[...]) * keep).astype(BF16)
            h_scr[HALO:, :] = norm(x).astype(BF16)
        else:
            h_scr[...] = norm(x).astype(BF16)

    h = h_scr[...]
    halves = []
    for w_ref, cw_ref, u_ref, p_ref in ((wg_ref, cwg_ref, ug_ref, None if seq_mode else pg_ref),
                                        (wv_ref, cwv_ref, uv_ref, None if seq_mode else pv_ref)):
        u = jnp.dot(h, w_ref[...], preferred_element_type=F32)
        cw = cw_ref[...]
        if seq_mode:
            u_ref[0] = u[HALO + tm - 2:HALO + tm, :]
            c = (cw[3:4] + cw[0:1] * u[HALO - 2:HALO - 2 + tm] + cw[1:2] * u[HALO - 1:HALO - 1 + tm]
                 + cw[2:3] * u[HALO:HALO + tm])
        else:
            u_ref[...] = u
            c = cw[3:4] + cw[0:1] * p_ref[0] + cw[1:2] * p_ref[1] + cw[2:3] * u
        halves.append(c)
    gate, val = halves
    act = gate * _sigmoid(gate) * val
    acc_scr[...] += jnp.dot(act.astype(BF16), wd_ref[...], preferred_element_type=F32)

    @pl.when(j == pl.num_programs(1) - 1)
    def _():
        x2 = acc_scr[...]
        ms = jnp.mean(x2 * x2, axis=-1, keepdims=True)
        y_ref[...] = x2 * lax.rsqrt(ms + NORM_EPS) * gf_ref[...]


def _ffn(x1, norm_g, w_up, conv_wb, w_down, final_g, tm, tf, seq_len=None, prev=None):
    m = x1.shape[0]
    nf = D_FF // tf
    seq_mode = prev is None
    tiles_per_seq = seq_len // tm if seq_mode else 1
    vec = lambda width: pl.BlockSpec((1, width), lambda i, j: (0, 0))
    in_specs = [pl.BlockSpec((tm, D_MODEL), lambda i, j: (i, 0))]
    args = [x1]
    if seq_mode:
        in_specs.append(pl.BlockSpec((HALO, D_MODEL), lambda i, j: (jnp.maximum(i * (tm // HALO) - 1, 0), 0)))
        args.append(x1)
    else:
        in_specs += [pl.BlockSpec((2, tm, tf), lambda i, j: (0, i, j)),
                     pl.BlockSpec((2, tm, tf), lambda i, j: (0, i, nf + j))]
        args += [prev, prev]
    in_specs += [
        vec(D_MODEL),
        pl.BlockSpec((D_MODEL, tf), lambda i, j: (0, j)),
        pl.BlockSpec((D_MODEL, tf), lambda i, j: (0, nf + j)),
        pl.BlockSpec((4, tf), lambda i, j: (0, j)),
        pl.BlockSpec((4, tf), lambda i, j: (0, nf + j)),
        pl.BlockSpec((tf, D_MODEL), lambda i, j: (j, 0)),
        vec(D_MODEL),
    ]
    args += [norm_g, w_up, w_up, conv_wb, conv_wb, w_down, final_g]
    if seq_mode:
        u_spec = pl.BlockSpec((1, 2, tf), lambda i, j: (i, 0, j))
        u_shape = jax.ShapeDtypeStruct((m // tm, 2, D_FF), F32)
    else:
        u_spec = pl.BlockSpec((tm, tf), lambda i, j: (i, j))
        u_shape = jax.ShapeDtypeStruct((m, D_FF), F32)
    rows = tm + HALO if seq_mode else tm
    return pl.pallas_call(
        functools.partial(_ffn_kernel, tm=tm, tiles_per_seq=tiles_per_seq, seq_mode=seq_mode),
        grid=(m // tm, nf),
        in_specs=in_specs,
        out_specs=[pl.BlockSpec((tm, D_MODEL), lambda i, j: (i, 0)), u_spec, u_spec],
        out_shape=[jax.ShapeDtypeStruct((m, D_MODEL), F32), u_shape, u_shape],
        scratch_shapes=[pltpu.VMEM((rows, D_MODEL), BF16), pltpu.VMEM((tm, D_MODEL), F32)],
        compiler_params=_cparams(2),
        name="ffn_seq" if seq_mode else "ffn_state",
    )(*args)


def _layer_params(l, norm_mix_g, w_in, att_out_g, rw_mu, rw_w0, rw_w_up, rw_a0, rw_a_up, rw_g_up, rw_k_k,
                  rw_k_a, rw_r_k, rw_lnx_w, rw_lnx_b, w_o, norm_ffn_g, ffn_w_up, ffn_conv_w, ffn_conv_b,
                  ffn_w_down):
    zeros = jnp.zeros((W_LORA, C_GRP), F32)
    return dict(
        norm_mix_g=norm_mix_g[l][None],
        w_main=w_in[l][:, :C_MAIN].astype(BF16),
        w_lora=w_in[l][:, C_MAIN:].astype(BF16),
        att_g=att_out_g[l][None],
        mu_main=rw_mu[l][:3 * C_GRP].reshape(3, C_GRP),
        mu_lora=rw_mu[l][None, 3 * C_GRP:],
        vecs=jnp.stack([rw_w0[l], rw_a0[l], rw_k_k[l], rw_k_a[l], rw_r_k[l], rw_lnx_w[l], rw_lnx_b[l]]),
        w_up=jnp.concatenate([rw_w_up[l], zeros]).astype(BF16),
        a_up=jnp.concatenate([zeros, rw_a_up[l]]).astype(BF16),
        g_up=rw_g_up[l].astype(BF16),
        w_o=w_o[l].astype(BF16),
        norm_ffn_g=norm_ffn_g[l][None],
        ffn_w_up=ffn_w_up[l].astype(BF16),
        conv_wb=jnp.concatenate([ffn_conv_w[l], ffn_conv_b[l][None]]),
        ffn_w_down=ffn_w_down[l].astype(BF16),
    )


def _prompt_layer(x, lp, final_g):
    nb, seq, _ = x.shape
    x2d = x.reshape(nb * seq, D_MODEL)
    p_main, p_lora = _in_proj(x2d, lp["norm_mix_g"], lp["w_main"], lp["w_lora"], tm=512, tn=512)
    p_main3 = p_main.reshape(nb, seq, C_MAIN)
    p_lora3 = p_lora.reshape(nb, seq, C_LORA)
    o_att = _attn_prompt(p_main3, lp["att_g"])
    o_rw, h_fin = _rwkv_prompt(p_main3, p_lora3, jnp.zeros((nb, 3, C_GRP), F32), jnp.zeros((nb, 1, C_LORA), F32),
                               lp["mu_main"], lp["mu_lora"], lp["vecs"], lp["w_up"], lp["a_up"], lp["g_up"])
    x1 = _out_proj(x2d, o_att.reshape(nb * seq, C_GRP), o_rw.reshape(nb * seq, C_GRP), lp["w_o"], tm=512, tn=1024)
    y, u_g, u_v = _ffn(x1, lp["norm_ffn_g"], lp["ffn_w_up"], lp["conv_wb"], lp["ffn_w_down"], final_g,
                       tm=512, tf=512, seq_len=seq)
    hd = (N_HEADS, HEAD_DIM)
    k_new = p_main3[:, :, C_GRP:2 * C_GRP].reshape((nb, seq) + hd)
    v_new = p_main3[:, :, 2 * C_GRP:3 * C_GRP].reshape((nb, seq) + hd)
    rw_last = jnp.concatenate([p_main3[:, -1:, 3 * C_GRP:], p_lora3[:, -1:, :]], axis=-1)
    wkv = jnp.stack([h_fin[:, :, :HEAD_DIM, :HEAD_DIM], h_fin[:, :, HEAD_DIM:, HEAD_DIM:]], axis=2)
    wkv = wkv.reshape(nb, N_HEADS, HEAD_DIM, HEAD_DIM)
    tiles_per_seq = u_g.shape[0] // nb
    ffn_last = jnp.concatenate([u_g, u_v], axis=-1)[tiles_per_seq - 1::tiles_per_seq]
    return y.reshape(nb, seq, D_MODEL), k_new, v_new, rw_last, wkv, ffn_last


def _sample_layer(x, cache_k, cache_v, rw_prev, wkv0, ffn_prev, lp, final_g):
    nb = x.shape[0]
    x2d = x.reshape(nb, D_MODEL)
    p_main, p_lora = _in_proj(x2d, lp["norm_mix_g"], lp["w_main"], lp["w_lora"], tm=nb, tn=512)
    hd = (N_HEADS, HEAD_DIM)
    q = p_main[:, :C_GRP].reshape((nb,) + hd)
    k_new = p_main[:, C_GRP:2 * C_GRP].reshape((nb,) + hd)
    v_new = p_main[:, 2 * C_GRP:3 * C_GRP].reshape((nb,) + hd)
    cache_kt, cache_vt = cache_k.transpose(0, 2, 3, 1), cache_v.transpose(0, 2, 3, 1)
    o_att = _attn_sample(q[..., None], k_new[..., None], v_new[..., None], cache_kt, cache_vt,
                         lp["att_g"].reshape(hd + (1,))).reshape(nb, C_GRP)
    prev = rw_prev.reshape(nb, C_SHIFT)
    o_rw, wkv = _rwkv_sample(p_main, p_lora, prev[:, :3 * C_GRP], prev[:, 3 * C_GRP:],
                             lp["mu_main"].reshape(1, 3 * C_GRP), lp["mu_lora"], lp["vecs"],
                             lp["w_up"], lp["a_up"], lp["g_up"], wkv0)
    x1 = _out_proj(x2d, o_att, o_rw, lp["w_o"], tm=nb, tn=1024)
    prev_rows = ffn_prev.transpose(1, 0, 2)
    y, u_g, u_v = _ffn(x1, lp["norm_ffn_g"], lp["ffn_w_up"], lp["conv_wb"], lp["ffn_w_down"], final_g,
                       tm=nb, tf=512, prev=prev_rows)
    rw_last = jnp.concatenate([p_main[:, 3 * C_GRP:], p_lora], axis=-1)[:, None, :]
    ffn_last = jnp.stack([ffn_prev[:, 1, :], jnp.concatenate([u_g, u_v], axis=-1)], axis=1)
    return y.reshape(nb, 1, D_MODEL), k_new[:, None], v_new[:, None], rw_last, wkv, ffn_last


def kernel(x_prompt, x_sample, cache_att_k, cache_att_v, state_rwkv_shift, state_rwkv_wkv, state_ffn_conv, norm_mix_g, w_in, att_out_g, rw_mu, rw_w0, rw_w_up, rw_a0, rw_a_up, rw_g_up, rw_k_k, rw_k_a, rw_r_k, rw_lnx_w, rw_lnx_b, w_o, norm_ffn_g, ffn_w_up, ffn_conv_w, ffn_conv_b, ffn_w_down, norm_final_g):
    depth = w_in.shape[0]
    assert depth == 1, "the fused FFN + final-norm kernel assumes a single trunk layer"
    assert x_sample.shape[1] == 1, "the sample path handles one new token per sequence"
    final_g = norm_final_g[None]
    lp = _layer_params(0, norm_mix_g, w_in, att_out_g, rw_mu, rw_w0, rw_w_up, rw_a0, rw_a_up, rw_g_up, rw_k_k,
                       rw_k_a, rw_r_k, rw_lnx_w, rw_lnx_b, w_o, norm_ffn_g, ffn_w_up, ffn_conv_w, ffn_conv_b,
                       ffn_w_down)
    yp, pk, pv, prw, pwkv, pffn = _prompt_layer(x_prompt, lp, final_g)
    ys, sk, sv, srw, swkv, sffn = _sample_layer(x_sample, cache_att_k[0], cache_att_v[0], state_rwkv_shift[0],
                                                state_rwkv_wkv[0], state_ffn_conv[0], lp, final_g)
    lead = lambda t: t[None]
    return (yp, ys, lead(pk), lead(pv), lead(prw), lead(pwkv), lead(pffn),
            lead(sk), lead(sv), lead(srw), lead(swkv), lead(sffn))
```

```python
import functools

import jax
import jax.numpy as jnp
from jax import lax
from jax.experimental import pallas as pl
from jax.experimental.pallas import tpu as pltpu

F32 = jnp.float32
BF16 = jnp.bfloat16

D_MODEL = 2048
HEAD_DIM = 64
N_HEADS = 16
C_GRP = N_HEADS * HEAD_DIM
W_LORA, A_LORA, G_LORA = 64, 64, 160
C_LORA = W_LORA + A_LORA + G_LORA
C_SHIFT = 3 * C_GRP + C_LORA
C_MAIN = 6 * C_GRP
D_FF = 5632
DIL_RATES = (1, 4, 16)
N_BACK = 128
ATT_SCALE = HEAD_DIM ** -0.5
NORM_EPS = 1e-6
LNX_EPS = HEAD_DIM * 1e-5
NEG_BIG = -1e30

LANES = 128
VMEM_LIMIT = 48 * 1024 * 1024

CHUNK = 64
PHASE_A_UNROLL = 8
ATTN_ITEMS_PER_ITER = 4
PAIR = 2 * HEAD_DIM


def _cparams(n_grid):
    return pltpu.CompilerParams(dimension_semantics=("arbitrary",) * n_grid,
                                vmem_limit_bytes=VMEM_LIMIT)


def _bdot(a, b):
    return jnp.dot(a.astype(BF16), b.astype(BF16), preferred_element_type=F32)


def _bdot_nt(a, b):
    return lax.dot_general(a.astype(BF16), b.astype(BF16), (((1,), (1,)), ((), ())),
                           preferred_element_type=F32)


def _bdot_tn(a, b):
    return jnp.dot(a.astype(F32).T.astype(BF16), b.astype(BF16), preferred_element_type=F32)


def _split_dot(mat01, x, terms):
    acc = None
    rem = x
    for _ in range(terms):
        piece = rem.astype(BF16)
        part = jnp.dot(mat01, piece, preferred_element_type=F32)
        acc = part if acc is None else acc + part
        rem = rem - piece.astype(F32)
    return acc


def _split_dot_r(x, mat01, terms):
    acc = None
    rem = x
    for _ in range(terms):
        piece = rem.astype(BF16)
        part = jnp.dot(piece, mat01, preferred_element_type=F32)
        acc = part if acc is None else acc + part
        rem = rem - piece.astype(F32)
    return acc


def _head_ones(n):
    r = lax.broadcasted_iota(jnp.int32, (n, n), 0) // HEAD_DIM
    c = lax.broadcasted_iota(jnp.int32, (n, n), 1) // HEAD_DIM
    return jnp.where(r == c, 1.0, 0.0).astype(BF16)


def _sigmoid(x):
    return 1.0 / (1.0 + jnp.exp(-x))


def _softplus(x):
    return jnp.maximum(x, 0.0) + jnp.log(1.0 + jnp.exp(-jnp.abs(x)))


def _in_proj_kernel(x_ref, g_ref, wm_ref, wl_ref, om_ref, ol_ref, h_scr):
    @pl.when(pl.program_id(1) == 0)
    def _():
        x = x_ref[...]
        ms = jnp.mean(x * x, axis=-1, keepdims=True)
        h = (x * lax.rsqrt(ms + NORM_EPS) * g_ref[...]).astype(BF16)
        h_scr[...] = h
        ol_ref[...] = jnp.dot(h, wl_ref[...], preferred_element_type=F32)

    om_ref[...] = jnp.dot(h_scr[...], wm_ref[...], preferred_element_type=F32)


def _in_proj(x2d, g, w_main, w_lora, tm, tn):
    m = x2d.shape[0]
    return pl.pallas_call(
        _in_proj_kernel,
        grid=(m // tm, C_MAIN // tn),
        in_specs=[
            pl.BlockSpec((tm, D_MODEL), lambda i, j: (i, 0)),
            pl.BlockSpec((1, D_MODEL), lambda i, j: (0, 0)),
            pl.BlockSpec((D_MODEL, tn), lambda i, j: (0, j)),
            pl.BlockSpec((D_MODEL, C_LORA), lambda i, j: (0, 0)),
        ],
        out_specs=[
            pl.BlockSpec((tm, tn), lambda i, j: (i, j)),
            pl.BlockSpec((tm, C_LORA), lambda i, j: (i, 0)),
        ],
        out_shape=[jax.ShapeDtypeStruct((m, C_MAIN), F32), jax.ShapeDtypeStruct((m, C_LORA), F32)],
        scratch_shapes=[pltpu.VMEM((tm, D_MODEL), BF16)],
        compiler_params=_cparams(2),
        name="in_proj",
    )(x2d, g, w_main, w_lora)


def _attn_prompt_kernel(q_ref, k_ref, v_ref, g_ref, o_ref, ob_scr, lb_scr, qa_scr, qb_scr, kt_scr,
                        va_scr, vb_scr, *, seq):
    blk = N_BACK
    nitem = seq // blk
    lane = lax.broadcasted_iota(jnp.int32, (blk, PAIR), 1)
    is_a = lane < HEAD_DIM

    row2 = lax.broadcasted_iota(jnp.int32, (blk, 2 * blk), 0)
    col2 = lax.broadcasted_iota(jnp.int32, (blk, 2 * blk), 1)
    band_bias = jnp.where((col2 >= row2) & (col2 <= row2 + N_BACK), 0.0, NEG_BIG)
    causal_bias = band_bias[:, blk:]

    for br, rate in enumerate(DIL_RATES):
        nblk = seq // rate // blk

        def item_rows(j, rate=rate, nblk=nblk):
            return pl.ds((j // nblk) + (j % nblk) * (blk * rate), blk, stride=rate)

        def prep(j, carry, item_rows=item_rows):
            rows = item_rows(j)
            qs = q_ref[rows, :] * ATT_SCALE
            v = v_ref[rows, :]
            qa_scr[j] = jnp.where(is_a, qs, 0.0).astype(BF16)
            qb_scr[j] = jnp.where(is_a, 0.0, qs).astype(BF16)
            kt_scr[j] = k_ref[rows, :].T.astype(BF16)
            va_scr[j] = jnp.where(is_a, v, 0.0).astype(BF16)
            vb_scr[j] = jnp.where(is_a, 0.0, v).astype(BF16)
            return carry
        lax.fori_loop(0, nitem, prep, 0, unroll=2)

        def attend(i, carry, item_rows=item_rows, nblk=nblk, br=br):
            js = [i * ATTN_ITEMS_PER_ITER + u for u in range(ATTN_ITEMS_PER_ITER)]
            nk = blk if nblk == 1 else 2 * blk
            ones = jnp.ones((nk, PAIR), BF16)
            kts, vms, biases = [], [], []
            for j in js:
                if nblk == 1:
                    biases.append(causal_bias)
                    kts.append(kt_scr[j])
                    vs = (va_scr[j], vb_scr[j])
                else:
                    jp = jnp.maximum(j - 1, 0)
                    lo = jnp.where(j % nblk == 0, blk, 0)
                    biases.append(jnp.where(col2 >= lo, band_bias, NEG_BIG))
                    kts.append(jnp.concatenate([kt_scr[jp], kt_scr[j]], axis=1))
                    vs = (jnp.concatenate([va_scr[jp], va_scr[j]], axis=0),
                          jnp.concatenate([vb_scr[jp], vb_scr[j]], axis=0))
                vms.append([jnp.concatenate([v, ones], axis=1) for v in vs])
            qs = [(qa_scr[j], qb_scr[j]) for j in js]
            s = [[jnp.dot(q, kt, preferred_element_type=F32) + bias for q in qp]
                 for qp, kt, bias in zip(qs, kts, biases)]
            m = [[jnp.max(x, axis=-1, keepdims=True) for x in xs] for xs in s]
            e = [[jnp.exp(x - mx).astype(BF16) for x, mx in zip(xs, ms)] for xs, ms in zip(s, m)]
            of = [[jnp.dot(x, vm, preferred_element_type=F32) for x, vm in zip(xs, vp)] for xs, vp in zip(e, vms)]
            for j, (ofa, ofb), (ma, mb) in zip(js, of, m):
                den = jnp.where(is_a, ofa[:, PAIR:], ofb[:, PAIR:])
                rows = item_rows(j)
                ob_scr[br, rows, :] = (ofa[:, :PAIR] + ofb[:, :PAIR]) * (1.0 / den)
                lb_scr[br, rows, :] = jnp.where(is_a, ma, mb) + jnp.log(den)
            return carry
        lax.fori_loop(0, nitem // ATTN_ITEMS_PER_ITER, attend, 0)

    ones_bd = _head_ones(PAIR)
    gain = g_ref[...]
    tile = 256

    def merge(i, carry):
        rows = pl.ds(pl.multiple_of(i * tile, tile), tile)
        l0, l1, l2 = lb_scr[0, rows, :], lb_scr[1, rows, :], lb_scr[2, rows, :]
        m = jnp.maximum(jnp.maximum(l0, l1), l2)
        w0, w1, w2 = jnp.exp(l0 - m), jnp.exp(l1 - m), jnp.exp(l2 - m)
        o = (w0 * ob_scr[0, rows, :] + w1 * ob_scr[1, rows, :] + w2 * ob_scr[2, rows, :]) / (w0 + w1 + w2)
        ms = _split_dot_r(o * o, ones_bd, 2) * (1.0 / HEAD_DIM)
        o_ref[rows, :] = (o * lax.rsqrt(ms + NORM_EPS) * gain).astype(o_ref.dtype)
        return carry
    lax.fori_loop(0, seq // tile, merge, 0)


def _attn_prompt(p_main3, att_g):
    nb, seq, _ = p_main3.shape
    npair = C_GRP // PAIR
    col = lambda off: pl.BlockSpec((None, seq, PAIR), lambda b, p, off=off: (b, 0, off + p))
    return pl.pallas_call(
        functools.partial(_attn_prompt_kernel, seq=seq),
        grid=(nb, npair),
        in_specs=[col(0), col(npair), col(2 * npair), pl.BlockSpec((1, PAIR), lambda b, p: (0, p))],
        out_specs=pl.BlockSpec((None, seq, PAIR), lambda b, p: (b, 0, p)),
        out_shape=jax.ShapeDtypeStruct((nb, seq, C_GRP), BF16),
        scratch_shapes=[pltpu.VMEM((3, seq, PAIR), F32), pltpu.VMEM((3, seq, PAIR), F32)]
        + [pltpu.VMEM((seq // N_BACK, N_BACK, PAIR), BF16)] * 5,
        compiler_params=_cparams(2),
        name="attn_prompt",
    )(p_main3, p_main3, p_main3, att_g)


SAMPLE_HEADS_PER_STEP = 4


def _attn_sample_kernel(q_ref, kn_ref, vn_ref, kt_ref, vt_ref, g_ref, o_ref, *, win):
    for h in range(SAMPLE_HEADS_PER_STEP):
        qc = q_ref[0, h] * ATT_SCALE
        kn, vn = kn_ref[0, h], vn_ref[0, h]
        s_all = jnp.sum(kt_ref[0, h] * qc, axis=0, keepdims=True)
        s_new = jnp.sum(kn * qc, axis=0, keepdims=True)
        outs, lses = [], []
        for rate in DIL_RATES:
            lo = win - N_BACK * rate
            s = s_all[:, lo:]
            if rate > 1:
                pos = lax.broadcasted_iota(jnp.int32, s.shape, 1)
                s = jnp.where((pos & (rate - 1)) == 0, s, NEG_BIG)
            m = jnp.maximum(jnp.max(s, axis=-1, keepdims=True), s_new)
            e = jnp.exp(s - m)
            e_new = jnp.exp(s_new - m)
            den = jnp.sum(e, axis=-1, keepdims=True) + e_new
            o = jnp.sum(vt_ref[0, h, :, lo:] * (e / den), axis=-1, keepdims=True) + (e_new / den) * vn
            outs.append(o)
            lses.append(m + jnp.log(den))
        m = jnp.maximum(jnp.maximum(lses[0], lses[1]), lses[2])
        ws = [jnp.exp(l - m) for l in lses]
        o = (ws[0] * outs[0] + ws[1] * outs[1] + ws[2] * outs[2]) / (ws[0] + ws[1] + ws[2])
        ms = jnp.mean(o * o, axis=0, keepdims=True)
        o_ref[0, h] = o * lax.rsqrt(ms + NORM_EPS) * g_ref[h]


def _attn_sample(q, kn, vn, cache_kt, cache_vt, att_g):
    nb, win = cache_kt.shape[0], cache_kt.shape[-1]
    assert win % (N_BACK * max(DIL_RATES)) == 0 and win % LANES == 0
    hs = SAMPLE_HEADS_PER_STEP
    tok = pl.BlockSpec((1, hs, HEAD_DIM, 1), lambda b, g: (b, g, 0, 0))
    cache = pl.BlockSpec((1, hs, HEAD_DIM, win), lambda b, g: (b, g, 0, 0))
    return pl.pallas_call(
        functools.partial(_attn_sample_kernel, win=win),
        grid=(nb, N_HEADS // hs),
        in_specs=[tok, tok, tok, cache, cache, pl.BlockSpec((hs, HEAD_DIM, 1), lambda b, g: (g, 0, 0))],
        out_specs=tok,
        out_shape=jax.ShapeDtypeStruct((nb, N_HEADS, HEAD_DIM, 1), F32),
        compiler_params=_cparams(2),
        name="attn_sample",
    )(q, kn, vn, cache_kt, cache_vt, att_g)


def _rwkv_token_math(xr, xk, xv, xl, prm, ones_bd):
    w_raw = prm["w0"] + _bdot(jnp.tanh(xl[:, :2 * W_LORA]), prm["w_up"])
    logw = -jnp.exp(-_softplus(-w_raw) - 0.5)
    a_sig = _sigmoid(prm["a0"] + _bdot(xl[:, :2 * W_LORA], prm["a_up"]))
    gate = _bdot(_sigmoid(xl[:, 2 * W_LORA:]), prm["g_up"])
    kk = xk * prm["k_k"]
    nrm = jnp.sqrt(_split_dot_r(kk * kk, ones_bd, 2))
    kk = kk / jnp.maximum(nrm, 1e-12)
    k_eff = xk * (1.0 + (a_sig - 1.0) * prm["k_a"])
    bonus = _split_dot_r(xr * k_eff * prm["r_k"], ones_bd, 2) * xv
    return dict(r=xr, k=k_eff, v=xv, a=-kk, b=kk * a_sig, logw=logw, gate=gate, bonus=bonus)


def _group_norm_gate(y, bonus, gate, lnx_w, lnx_b, ones_bd):
    mean = _split_dot_r(y, ones_bd, 2) * (1.0 / HEAD_DIM)
    d = y - mean
    var = _split_dot_r(d * d, ones_bd, 2) * (1.0 / HEAD_DIM)
    yn = d * lax.rsqrt(var + LNX_EPS) * lnx_w + lnx_b
    return (yn + bonus) * gate


_RW_VEC_NAMES = ("w0", "a0", "k_k", "k_a", "r_k", "lnx_w", "lnx_b")


def _rwkv_prompt_kernel(r_ref, k_ref, v_ref, l_ref, pm_ref, pl_ref, mu_m_ref, mu_l_ref, vec_ref,
                        wup_ref, aup_ref, gup_ref, o_ref, h_ref,
                        tok_scr, y_scr, rw_scr, y0_scr, g_scr, ha_scr, pc_scr, *, seq):
    n = PAIR
    ones_bd = _head_ones(n)
    row = lax.broadcasted_iota(jnp.int32, (seq, 1), 0)

    def shift(x, prev):
        return jnp.where(row == 0, prev, pltpu.roll(x, 1, axis=0))

    def lerp(x, prev, mu):
        return x + mu * (shift(x, prev) - x)

    pm = pm_ref[...]
    mu_m = mu_m_ref[...]
    prm = {name: vec_ref[i:i + 1, :] for i, name in enumerate(_RW_VEC_NAMES)}
    prm.update(w_up=wup_ref[...], a_up=aup_ref[...], g_up=gup_ref[...])
    tok = _rwkv_token_math(lerp(r_ref[...], pm[0:1], mu_m[0:1]),
                           lerp(k_ref[...], pm[1:2], mu_m[1:2]),
                           lerp(v_ref[...], pm[2:3], mu_m[2:3]),
                           lerp(l_ref[...], pl_ref[...], mu_l_ref[...]), prm, ones_bd)
    for i, name in enumerate(("r", "k", "v", "a", "b", "logw", "gate", "bonus")):
        tok_scr[i] = tok[name]

    nch = seq // CHUNK
    lane = lax.broadcasted_iota(jnp.int32, (CHUNK, n), 1)
    is_a = lane < HEAD_DIM
    ri = lax.broadcasted_iota(jnp.int32, (n, n), 0)
    ci = lax.broadcasted_iota(jnp.int32, (n, n), 1)
    strict = ri > ci
    incl = ri >= ci
    eye = ri == ci
    rc = lax.broadcasted_iota(jnp.int32, (CHUNK, CHUNK), 0)
    cc = lax.broadcasted_iota(jnp.int32, (CHUNK, CHUNK), 1)

    def stack(x):
        return jnp.concatenate([jnp.where(is_a, x, 0.0), jnp.where(is_a, 0.0, x)], axis=0)

    def phase_a(g, carry):
        cs = [g * PHASE_A_UNROLL + u for u in range(PHASE_A_UNROLL)]
        each = lambda fn, *lists: [fn(*xs) for xs in zip(*lists)]
        tril_ones = jnp.where(rc >= cc, 1.0, 0.0).astype(BF16)
        tok = [[tok_scr[i, pl.ds(pl.multiple_of(c * CHUNK, CHUNK), CHUNK), :] for i in range(6)] for c in cs]
        r_c, k_c, v_c, a_c, b_c, lw_c = [list(x) for x in zip(*tok)]
        lcum = each(lambda lw: _split_dot(tril_ones, lw, 3), lw_c)
        lend = each(lambda l: l[CHUNK - 1:CHUNK, :], lcum)
        sb = lambda x: stack(x).astype(BF16)
        a_s = each(lambda a, l, lw: sb(a * jnp.exp(l - lw)), a_c, lcum, lw_c)
        r_s = each(lambda r, l: sb(r * jnp.exp(l)), r_c, lcum)
        b_s = each(lambda b, l: sb(b * jnp.exp(-l)), b_c, lcum)
        k_s = each(lambda k, l: sb(k * jnp.exp(-l)), k_c, lcum)
        v_s = each(sb, v_c)
        be_s = each(lambda b, l, le: sb(b * jnp.exp(le - l)), b_c, lcum, lend)
        ke_s = each(lambda k, l, le: sb(k * jnp.exp(le - l)), k_c, lcum, lend)

        sc = each(lambda a, r, b, k: _bdot_nt(jnp.concatenate([a, r], axis=0), jnp.concatenate([b, k], axis=0)),
                  a_s, r_s, b_s, k_s)
        s_ab = each(lambda s: jnp.where(strict, s[:n, :n], 0.0), sc)
        s_ak = each(lambda s: jnp.where(strict, s[:n, n:], 0.0).astype(BF16), sc)
        s_rb = each(lambda s: jnp.where(incl, s[n:, :n], 0.0).astype(BF16), sc)
        s_rk = each(lambda s: jnp.where(incl, s[n:, n:], 0.0).astype(BF16), sc)

        apow = each(lambda s: s.astype(BF16), s_ab)
        tinv = each(lambda s: jnp.where(eye, 1.0, 0.0) + s, s_ab)
        for _ in range(CHUNK.bit_length() - 2):
            apow = each(lambda p: _bdot(p, p).astype(BF16), apow)
            tinv = each(lambda t, p: t + _bdot(t, p), tinv, apow)

        x1 = each(_bdot, s_ak, v_s)
        wu = each(lambda t, a, x: _bdot(t, jnp.concatenate([a, x.astype(BF16)], axis=1)), tinv, a_s, x1)
        yk = each(_bdot, s_rk, v_s)
        ry = each(lambda s, w: _bdot(s, w), s_rb, wu)
        gh = each(lambda w, be: _bdot(w.T, be), wu, be_s)
        hk = each(lambda v, ke: _bdot_tn(v, ke), v_s, ke_s)
        for u, c in enumerate(cs):
            rw_scr[c] = ry[u][:, :n] + r_s[u].astype(F32)
            y0_scr[c] = ry[u][:, n:] + yk[u]
            g_scr[c] = gh[u][:n, :]
            ha_scr[c] = gh[u][n:, :] + hk[u]
            pc_scr[c] = jnp.exp(lend[u])
        return carry
    lax.fori_loop(0, nch // PHASE_A_UNROLL, phase_a, 0)

    h_ref[...] = jnp.zeros((n, n), F32)

    def phase_b(c, carry):
        rows = pl.ds(pl.multiple_of(c * CHUNK, CHUNK), CHUNK)
        ht = h_ref[...]
        hb = ht.astype(BF16)
        ys = _bdot_nt(rw_scr[c], hb) + y0_scr[c]
        y_scr[rows, :] = ys[:CHUNK, :] + ys[CHUNK:, :]
        h_ref[...] = ht * pc_scr[c] + _bdot(hb, g_scr[c]) + ha_scr[c]
        return carry
    lax.fori_loop(0, nch, phase_b, 0)

    o_ref[...] = _group_norm_gate(y_scr[...], tok_scr[7], tok_scr[6], prm["lnx_w"], prm["lnx_b"],
                                  _head_ones(n)).astype(o_ref.dtype)


def _rwkv_prompt(p_main3, p_lora3, prev_main, prev_lora, mu_main, mu_lora, vecs, w_up, a_up, g_up):
    nb, seq, _ = p_main3.shape
    npair = C_GRP // PAIR
    col = lambda off: pl.BlockSpec((None, seq, PAIR), lambda b, p, off=off: (b, 0, off + p))
    pcol = lambda rows: pl.BlockSpec((rows, PAIR), lambda b, p: (0, p))
    return pl.pallas_call(
        functools.partial(_rwkv_prompt_kernel, seq=seq),
        grid=(nb, npair),
        in_specs=[
            col(3 * npair), col(4 * npair), col(5 * npair),
            pl.BlockSpec((None, seq, C_LORA), lambda b, p: (b, 0, 0)),
            pl.BlockSpec((None, 3, PAIR), lambda b, p: (b, 0, p)),
            pl.BlockSpec((None, 1, C_LORA), lambda b, p: (b, 0, 0)),
            pcol(3), pl.BlockSpec((1, C_LORA), lambda b, p: (0, 0)), pcol(len(_RW_VEC_NAMES)),
            pcol(2 * W_LORA), pcol(2 * A_LORA), pcol(G_LORA),
        ],
        out_specs=[
            pl.BlockSpec((None, seq, PAIR), lambda b, p: (b, 0, p)),
            pl.BlockSpec((None, None, PAIR, PAIR), lambda b, p: (b, p, 0, 0)),
        ],
        out_shape=[jax.ShapeDtypeStruct((nb, seq, C_GRP), BF16),
                   jax.ShapeDtypeStruct((nb, npair, PAIR, PAIR), F32)],
        scratch_shapes=[
            pltpu.VMEM((8, seq, PAIR), F32), pltpu.VMEM((seq, PAIR), F32),
            pltpu.VMEM((seq // CHUNK, PAIR, PAIR), F32), pltpu.VMEM((seq // CHUNK, PAIR, PAIR), F32),
            pltpu.VMEM((seq // CHUNK, PAIR, PAIR), F32), pltpu.VMEM((seq // CHUNK, PAIR, PAIR), F32),
            pltpu.VMEM((seq // CHUNK, 1, PAIR), F32),
        ],
        compiler_params=_cparams(2),
        name="rwkv_prompt",
    )(p_main3, p_main3, p_main3, p_lora3, prev_main, prev_lora, mu_main, mu_lora, vecs, w_up, a_up, g_up)


def _rwkv_sample_tok_kernel(pm_ref, l_ref, pvm_ref, pvl_ref, mu_m_ref, mu_l_ref, vec_ref,
                            wup_ref, aup_ref, gup_ref, out_ref):
    ones_bd = _head_ones(C_GRP)
    prm = {name: vec_ref[i:i + 1, :] for i, name in enumerate(_RW_VEC_NAMES)}
    prm.update(w_up=wup_ref[...], a_up=aup_ref[...], g_up=gup_ref[...])
    lerp = lambda x, prev, mu: x + mu * (prev - x)
    xs = [lerp(pm_ref[:, i * C_GRP:(i + 1) * C_GRP], pvm_ref[:, i * C_GRP:(i + 1) * C_GRP],
               mu_m_ref[:, i * C_GRP:(i + 1) * C_GRP]) for i in range(3)]
    tok = _rwkv_token_math(xs[0], xs[1], xs[2], lerp(l_ref[...], pvl_ref[...], mu_l_ref[...]), prm, ones_bd)
    for i, name in enumerate(("r", "k", "v", "a", "b", "logw", "gate", "bonus")):
        out_ref[i] = tok[name]


def _rwkv_sample_step_kernel(s_ref, row_ref, col_ref, s_out_ref, o_ref):
    s = s_ref[0]
    r, k, a, b, logw = [row_ref[0, i] for i in range(5)]
    v, gate, bonus, lnx_w, lnx_b = [col_ref[0, i] for i in range(5)]
    sa = jnp.sum(s * a, axis=-1, keepdims=True)
    s_new = s * jnp.exp(logw) + sa * b + v * k
    s_out_ref[0] = s_new
    y = jnp.sum(s_new * r, axis=-1, keepdims=True)
    mean = jnp.mean(y, axis=1, keepdims=True)
    d = y - mean
    var = jnp.mean(d * d, axis=1, keepdims=True)
    yn = d * lax.rsqrt(var + LNX_EPS) * lnx_w + lnx_b
    o_ref[0] = (yn + bonus) * gate


def _rwkv_sample(pm_s, pl_s, prev_main, prev_lora, mu_main, mu_lora, vecs, w_up, a_up, g_up, wkv0):
    nb = pm_s.shape[0]
    full = lambda shape: pl.BlockSpec(shape, lambda i: (0,) * len(shape))
    tok = pl.pallas_call(
        _rwkv_sample_tok_kernel,
        grid=(1,),
        in_specs=[pl.BlockSpec((nb, 3 * C_GRP), lambda i: (0, 1)), full((nb, C_LORA)),
                  full((nb, 3 * C_GRP)), full((nb, C_LORA)), full((1, 3 * C_GRP)), full((1, C_LORA)),
                  full(vecs.shape), full(w_up.shape), full(a_up.shape), full(g_up.shape)],
        out_specs=full((8, nb, C_GRP)),
        out_shape=jax.ShapeDtypeStruct((8, nb, C_GRP), F32),
        compiler_params=_cparams(1),
        name="rwkv_sample_tok",
    )(pm_s, pl_s, prev_main, prev_lora, mu_main, mu_lora, vecs, w_up, a_up, g_up)
    heads = lambda x: x.reshape(x.shape[0], nb, N_HEADS, HEAD_DIM).transpose(1, 0, 2, 3)
    rows = heads(jnp.concatenate([tok[0:2], tok[3:6]], axis=0))[:, :, :, None, :]
    lnx = jnp.broadcast_to(vecs[5:7, None, :], (2, nb, C_GRP))
    cols = heads(jnp.concatenate([tok[2:3], tok[6:8], lnx], axis=0))[..., None]
    st = (1, N_HEADS, HEAD_DIM, HEAD_DIM)
    s_new, o = pl.pallas_call(
        _rwkv_sample_step_kernel,
        grid=(nb,),
        in_specs=[pl.BlockSpec(st, lambda b: (b, 0, 0, 0)),
                  pl.BlockSpec((1, 5, N_HEADS, 1, HEAD_DIM), lambda b: (b, 0, 0, 0, 0)),
                  pl.BlockSpec((1, 5, N_HEADS, HEAD_DIM, 1), lambda b: (b, 0, 0, 0, 0))],
        out_specs=[pl.BlockSpec(st, lambda b: (b, 0, 0, 0)),
                   pl.BlockSpec((1, N_HEADS, HEAD_DIM, 1), lambda b: (b, 0, 0, 0))],
        out_shape=[jax.ShapeDtypeStruct((nb,) + st[1:], F32),
                   jax.ShapeDtypeStruct((nb, N_HEADS, HEAD_DIM, 1), F32)],
        compiler_params=_cparams(1),
        name="rwkv_sample_step",
    )(wkv0, rows, cols)
    return o.reshape(nb, C_GRP), s_new


def _out_proj_kernel(x_ref, oa_ref, orw_ref, wa_ref, wb_ref, o_ref):
    o_ref[...] = (x_ref[...] + _bdot(oa_ref[...], wa_ref[...]) + _bdot(orw_ref[...], wb_ref[...]))


def _out_proj(x2d, o_att, o_rw, w_o, tm, tn):
    m = x2d.shape[0]
    return pl.pallas_call(
        _out_proj_kernel,
        grid=(m // tm, D_MODEL // tn),
        in_specs=[
            pl.BlockSpec((tm, tn), lambda i, j: (i, j)),
            pl.BlockSpec((tm, C_GRP), lambda i, j: (i, 0)),
            pl.BlockSpec((tm, C_GRP), lambda i, j: (i, 0)),
            pl.BlockSpec((C_GRP, tn), lambda i, j: (0, j)),
            pl.BlockSpec((C_GRP, tn), lambda i, j: (1, j)),
        ],
        out_specs=pl.BlockSpec((tm, tn), lambda i, j: (i, j)),
        out_shape=jax.ShapeDtypeStruct((m, D_MODEL), F32),
        compiler_params=_cparams(2),
        name="out_proj",
    )(x2d, o_att, o_rw, w_o, w_o)


HALO = 16


def _ffn_kernel(*refs, tm, tiles_per_seq, seq_mode):
    if seq_mode:
        (x_ref, xh_ref, g_ref, wg_ref, wv_ref, cwg_ref, cwv_ref, wd_ref, gf_ref,
         y_ref, ug_ref, uv_ref, h_scr, acc_scr) = refs
    else:
        (x_ref, pg_ref, pv_ref, g_ref, wg_ref, wv_ref, cwg_ref, cwv_ref, wd_ref, gf_ref,
         y_ref, ug_ref, uv_ref, h_scr, acc_scr) = refs
    i, j = pl.program_id(0), pl.program_id(1)

    def norm(x):
        ms = jnp.mean(x * x, axis=-1, keepdims=True)
        return x * lax.rsqrt(ms + NORM_EPS) * g_ref[...]

    @pl.when(j == 0)
    def _():
        x = x_ref[...]
        acc_scr[...] = x
        if seq_mode:
            keep = jnp.where(i % tiles_per_seq == 0, 0.0, 1.0).astype(F32)
            h_scr[:HALO, :] = (norm(xh_ref[...]) * keep).astype(BF16)
            h_scr[HALO:, :] = norm(x).astype(BF16)
        else:
            h_scr[...] = norm(x).astype(BF16)

    h = h_scr[...]
    halves = []
    for w_ref, cw_ref, u_ref, p_ref in ((wg_ref, cwg_ref, ug_ref, None if seq_mode else pg_ref),
                                        (wv_ref, cwv_ref, uv_ref, None if seq_mode else pv_ref)):
        u = jnp.dot(h, w_ref[...], preferred_element_type=F32)
        cw = cw_ref[...]
        if seq_mode:
            u_ref[0] = u[HALO + tm - 2:HALO + tm, :]
            c = (cw[3:4] + cw[0:1] * u[HALO - 2:HALO - 2 + tm] + cw[1:2] * u[HALO - 1:HALO - 1 + tm]
                 + cw[2:3] * u[HALO:HALO + tm])
        else:
            u_ref[...] = u
            c = cw[3:4] + cw[0:1] * p_ref[0] + cw[1:2] * p_ref[1] + cw[2:3] * u
        halves.append(c)
    gate, val = halves
    act = gate * _sigmoid(gate) * val
    acc_scr[...] += jnp.dot(act.astype(BF16), wd_ref[...], preferred_element_type=F32)

    @pl.when(j == pl.num_programs(1) - 1)
    def _():
        x2 = acc_scr[...]
        ms = jnp.mean(x2 * x2, axis=-1, keepdims=True)
        y_ref[...] = x2 * lax.rsqrt(ms + NORM_EPS) * gf_ref[...]


def _ffn(x1, norm_g, w_up, conv_wb, w_down, final_g, tm, tf, seq_len=None, prev=None):
    m = x1.shape[0]
    nf = D_FF // tf
    seq_mode = prev is None
    tiles_per_seq = seq_len // tm if seq_mode else 1
    vec = lambda width: pl.BlockSpec((1, width), lambda i, j: (0, 0))
    in_specs = [pl.BlockSpec((tm, D_MODEL), lambda i, j: (i, 0))]
    args = [x1]
    if seq_mode:
        in_specs.append(pl.BlockSpec((HALO, D_MODEL), lambda i, j: (jnp.maximum(i * (tm // HALO) - 1, 0), 0)))
        args.append(x1)
    else:
        in_specs += [pl.BlockSpec((2, tm, tf), lambda i, j: (0, i, j)),
                     pl.BlockSpec((2, tm, tf), lambda i, j: (0, i, nf + j))]
        args += [prev, prev]
    in_specs += [
        vec(D_MODEL),
        pl.BlockSpec((D_MODEL, tf), lambda i, j: (0, j)),
        pl.BlockSpec((D_MODEL, tf), lambda i, j: (0, nf + j)),
        pl.BlockSpec((4, tf), lambda i, j: (0, j)),
        pl.BlockSpec((4, tf), lambda i, j: (0, nf + j)),
        pl.BlockSpec((tf, D_MODEL), lambda i, j: (j, 0)),
        vec(D_MODEL),
    ]
    args += [norm_g, w_up, w_up, conv_wb, conv_wb, w_down, final_g]
    if seq_mode:
        u_spec = pl.BlockSpec((1, 2, tf), lambda i, j: (i, 0, j))
        u_shape = jax.ShapeDtypeStruct((m // tm, 2, D_FF), F32)
    else:
        u_spec = pl.BlockSpec((tm, tf), lambda i, j: (i, j))
        u_shape = jax.ShapeDtypeStruct((m, D_FF), F32)
    rows = tm + HALO if seq_mode else tm
    return pl.pallas_call(
        functools.partial(_ffn_kernel, tm=tm, tiles_per_seq=tiles_per_seq, seq_mode=seq_mode),
        grid=(m // tm, nf),
        in_specs=in_specs,
        out_specs=[pl.BlockSpec((tm, D_MODEL), lambda i, j: (i, 0)), u_spec, u_spec],
        out_shape=[jax.ShapeDtypeStruct((m, D_MODEL), F32), u_shape, u_shape],
        scratch_shapes=[pltpu.VMEM((rows, D_MODEL), BF16), pltpu.VMEM((tm, D_MODEL), F32)],
        compiler_params=_cparams(2),
        name="ffn_seq" if seq_mode else "ffn_state",
    )(*args)


def _layer_params(l, norm_mix_g, w_in, att_out_g, rw_mu, rw_w0, rw_w_up, rw_a0, rw_a_up, rw_g_up, rw_k_k,
                  rw_k_a, rw_r_k, rw_lnx_w, rw_lnx_b, w_o, norm_ffn_g, ffn_w_up, ffn_conv_w, ffn_conv_b,
                  ffn_w_down):
    zeros = jnp.zeros((W_LORA, C_GRP), F32)
    return dict(
        norm_mix_g=norm_mix_g[l][None],
        w_main=w_in[l][:, :C_MAIN].astype(BF16),
        w_lora=w_in[l][:, C_MAIN:].astype(BF16),
        att_g=att_out_g[l][None],
        mu_main=rw_mu[l][:3 * C_GRP].reshape(3, C_GRP),
        mu_lora=rw_mu[l][None, 3 * C_GRP:],
        vecs=jnp.stack([rw_w0[l], rw_a0[l], rw_k_k[l], rw_k_a[l], rw_r_k[l], rw_lnx_w[l], rw_lnx_b[l]]),
        w_up=jnp.concatenate([rw_w_up[l], zeros]).astype(BF16),
        a_up=jnp.concatenate([zeros, rw_a_up[l]]).astype(BF16),
        g_up=rw_g_up[l].astype(BF16),
        w_o=w_o[l].astype(BF16),
        norm_ffn_g=norm_ffn_g[l][None],
        ffn_w_up=ffn_w_up[l].astype(BF16),
        conv_wb=jnp.concatenate([ffn_conv_w[l], ffn_conv_b[l][None]]),
        ffn_w_down=ffn_w_down[l].astype(BF16),
    )


def _prompt_layer(x, lp, final_g):
    nb, seq, _ = x.shape
    x2d = x.reshape(nb * seq, D_MODEL)
    p_main, p_lora = _in_proj(x2d, lp["norm_mix_g"], lp["w_main"], lp["w_lora"], tm=512, tn=512)
    p_main3 = p_main.reshape(nb, seq, C_MAIN)
    p_lora3 = p_lora.reshape(nb, seq, C_LORA)
    o_att = _attn_prompt(p_main3, lp["att_g"])
    o_rw, h_fin = _rwkv_prompt(p_main3, p_lora3, jnp.zeros((nb, 3, C_GRP), F32), jnp.zeros((nb, 1, C_LORA), F32),
                               lp["mu_main"], lp["mu_lora"], lp["vecs"], lp["w_up"], lp["a_up"], lp["g_up"])
    x1 = _out_proj(x2d, o_att.reshape(nb * seq, C_GRP), o_rw.reshape(nb * seq, C_GRP), lp["w_o"], tm=512, tn=1024)
    y, u_g, u_v = _ffn(x1, lp["norm_ffn_g"], lp["ffn_w_up"], lp["conv_wb"], lp["ffn_w_down"], final_g,
                       tm=512, tf=512, seq_len=seq)
    hd = (N_HEADS, HEAD_DIM)
    k_new = p_main3[:, :, C_GRP:2 * C_GRP].reshape((nb, seq) + hd)
    v_new = p_main3[:, :, 2 * C_GRP:3 * C_GRP].reshape((nb, seq) + hd)
    rw_last = jnp.concatenate([p_main3[:, -1:, 3 * C_GRP:], p_lora3[:, -1:, :]], axis=-1)
    wkv = jnp.stack([h_fin[:, :, :HEAD_DIM, :HEAD_DIM], h_fin[:, :, HEAD_DIM:, HEAD_DIM:]], axis=2)
    wkv = wkv.reshape(nb, N_HEADS, HEAD_DIM, HEAD_DIM)
    tiles_per_seq = u_g.shape[0] // nb
    ffn_last = jnp.concatenate([u_g, u_v], axis=-1)[tiles_per_seq - 1::tiles_per_seq]
    return y.reshape(nb, seq, D_MODEL), k_new, v_new, rw_last, wkv, ffn_last


def _sample_layer(x, cache_k, cache_v, rw_prev, wkv0, ffn_prev, lp, final_g):
    nb = x.shape[0]
    x2d = x.reshape(nb, D_MODEL)
    p_main, p_lora = _in_proj(x2d, lp["norm_mix_g"], lp["w_main"], lp["w_lora"], tm=nb, tn=512)
    hd = (N_HEADS, HEAD_DIM)
    q = p_main[:, :C_GRP].reshape((nb,) + hd)
    k_new = p_main[:, C_GRP:2 * C_GRP].reshape((nb,) + hd)
    v_new = p_main[:, 2 * C_GRP:3 * C_GRP].reshape((nb,) + hd)
    cache_kt, cache_vt = cache_k.transpose(0, 2, 3, 1), cache_v.transpose(0, 2, 3, 1)
    o_att = _attn_sample(q[..., None], k_new[..., None], v_new[..., None], cache_kt, cache_vt,
                         lp["att_g"].reshape(hd + (1,))).reshape(nb, C_GRP)
    prev = rw_prev.reshape(nb, C_SHIFT)
    o_rw, wkv = _rwkv_sample(p_main, p_lora, prev[:, :3 * C_GRP], prev[:, 3 * C_GRP:],
                             lp["mu_main"].reshape(1, 3 * C_GRP), lp["mu_lora"], lp["vecs"],
                             lp["w_up"], lp["a_up"], lp["g_up"], wkv0)
    x1 = _out_proj(x2d, o_att, o_rw, lp["w_o"], tm=nb, tn=1024)
    prev_rows = ffn_prev.transpose(1, 0, 2)
    y, u_g, u_v = _ffn(x1, lp["norm_ffn_g"], lp["ffn_w_up"], lp["conv_wb"], lp["ffn_w_down"], final_g,
                       tm=nb, tf=512, prev=prev_rows)
    rw_last = jnp.concatenate([p_main[:, 3 * C_GRP:], p_lora], axis=-1)[:, None, :]
    ffn_last = jnp.stack([ffn_prev[:, 1, :], jnp.concatenate([u_g, u_v], axis=-1)], axis=1)
    return y.reshape(nb, 1, D_MODEL), k_new[:, None], v_new[:, None], rw_last, wkv, ffn_last


def kernel(x_prompt, x_sample, cache_att_k, cache_att_v, state_rwkv_shift, state_rwkv_wkv, state_ffn_conv, norm_mix_g, w_in, att_out_g, rw_mu, rw_w0, rw_w_up, rw_a0, rw_a_up, rw_g_up, rw_k_k, rw_k_a, rw_r_k, rw_lnx_w, rw_lnx_b, w_o, norm_ffn_g, ffn_w_up, ffn_conv_w, ffn_conv_b, ffn_w_down, norm_final_g):
    depth = w_in.shape[0]
    assert depth == 1, "the fused FFN + final-norm kernel assumes a single trunk layer"
    assert x_sample.shape[1] == 1, "the sample path handles one new token per sequence"
    final_g = norm_final_g[None]
    lp = _layer_params(0, norm_mix_g, w_in, att_out_g, rw_mu, rw_w0, rw_w_up, rw_a0, rw_a_up, rw_g_up, rw_k_k,
                       rw_k_a, rw_r_k, rw_lnx_w, rw_lnx_b, w_o, norm_ffn_g, ffn_w_up, ffn_conv_w, ffn_conv_b,
                       ffn_w_down)
    yp, pk, pv, prw, pwkv, pffn = _prompt_layer(x_prompt, lp, final_g)
    ys, sk, sv, srw, swkv, sffn = _sample_layer(x_sample, cache_att_k[0], cache_att_v[0], state_rwkv_shift[0],
                                                state_rwkv_wkv[0], state_ffn_conv[0], lp, final_g)
    lead = lambda t: t[None]
    return (yp, ys, lead(pk), lead(pv), lead(prw), lead(pwkv), lead(pffn),
            lead(sk), lead(sv), lead(srw), lead(swkv), lead(sffn))
```

```python
import functools

import jax
import jax.numpy as jnp
from jax import lax
from jax.experimental import pallas as pl
from jax.experimental.pallas import tpu as pltpu

F32 = jnp.float32
BF16 = jnp.bfloat16

D_MODEL = 2048
HEAD_DIM = 64
N_HEADS = 16
C_GRP = N_HEADS * HEAD_DIM
W_LORA, A_LORA, G_LORA = 64, 64, 160
C_LORA = W_LORA + A_LORA + G_LORA
C_SHIFT = 3 * C_GRP + C_LORA
C_MAIN = 6 * C_GRP
D_FF = 5632
DIL_RATES = (1, 4, 16)
N_BACK = 128
ATT_SCALE = HEAD_DIM ** -0.5
NORM_EPS = 1e-6
LNX_EPS = HEAD_DIM * 1e-5
NEG_BIG = -1e30

LANES = 128
VMEM_LIMIT = 48 * 1024 * 1024

CHUNK = 64
PHASE_A_UNROLL = 8
ATTN_ITEMS_PER_ITER = 4
PAIR = 2 * HEAD_DIM


def _cparams(n_grid):
    return pltpu.CompilerParams(dimension_semantics=("arbitrary",) * n_grid,
                                vmem_limit_bytes=VMEM_LIMIT)


def _bdot(a, b):
    return jnp.dot(a.astype(BF16), b.astype(BF16), preferred_element_type=F32)


def _bdot_nt(a, b):
    return lax.dot_general(a.astype(BF16), b.astype(BF16), (((1,), (1,)), ((), ())),
                           preferred_element_type=F32)


def _bdot_tn(a, b):
    return jnp.dot(a.astype(F32).T.astype(BF16), b.astype(BF16), preferred_element_type=F32)


def _split_dot(mat01, x, terms):
    acc = None
    rem = x
    for _ in range(terms):
        piece = rem.astype(BF16)
        part = jnp.dot(mat01, piece, preferred_element_type=F32)
        acc = part if acc is None else acc + part
        rem = rem - piece.astype(F32)
    return acc


def _split_dot_r(x, mat01, terms):
    acc = None
    rem = x
    for _ in range(terms):
        piece = rem.astype(BF16)
        part = jnp.dot(piece, mat01, preferred_element_type=F32)
        acc = part if acc is None else acc + part
        rem = rem - piece.astype(F32)
    return acc


def _head_ones(n):
    r = lax.broadcasted_iota(jnp.int32, (n, n), 0) // HEAD_DIM
    c = lax.broadcasted_iota(jnp.int32, (n, n), 1) // HEAD_DIM
    return jnp.where(r == c, 1.0, 0.0).astype(BF16)


def _sigmoid(x):
    return 1.0 / (1.0 + jnp.exp(-x))


def _softplus(x):
    return jnp.maximum(x, 0.0) + jnp.log(1.0 + jnp.exp(-jnp.abs(x)))


def _in_proj_kernel(x_ref, g_ref, wm_ref, wl_ref, om_ref, ol_ref, h_scr):
    @pl.when(pl.program_id(1) == 0)
    def _():
        x = x_ref[...]
        ms = jnp.mean(x * x, axis=-1, keepdims=True)
        h = (x * lax.rsqrt(ms + NORM_EPS) * g_ref[...]).astype(BF16)
        h_scr[...] = h
        ol_ref[...] = jnp.dot(h, wl_ref[...], preferred_element_type=F32)

    om_ref[...] = jnp.dot(h_scr[...], wm_ref[...], preferred_element_type=F32)


def _in_proj(x2d, g, w_main, w_lora, tm, tn):
    m = x2d.shape[0]
    return pl.pallas_call(
        _in_proj_kernel,
        grid=(m // tm, C_MAIN // tn),
        in_specs=[
            pl.BlockSpec((tm, D_MODEL), lambda i, j: (i, 0)),
            pl.BlockSpec((1, D_MODEL), lambda i, j: (0, 0)),
            pl.BlockSpec((D_MODEL, tn), lambda i, j: (0, j)),
            pl.BlockSpec((D_MODEL, C_LORA), lambda i, j: (0, 0)),
        ],
        out_specs=[
            pl.BlockSpec((tm, tn), lambda i, j: (i, j)),
            pl.BlockSpec((tm, C_LORA), lambda i, j: (i, 0)),
        ],
        out_shape=[jax.ShapeDtypeStruct((m, C_MAIN), F32), jax.ShapeDtypeStruct((m, C_LORA), F32)],
        scratch_shapes=[pltpu.VMEM((tm, D_MODEL), BF16)],
        compiler_params=_cparams(2),
        name="in_proj",
    )(x2d, g, w_main, w_lora)


def _attn_prompt_kernel(q_ref, k_ref, v_ref, g_ref, o_ref, ob_scr, lb_scr, qa_scr, qb_scr, kt_scr,
                        va_scr, vb_scr, *, seq):
    blk = N_BACK
    nitem = seq // blk
    lane = lax.broadcasted_iota(jnp.int32, (blk, PAIR), 1)
    is_a = lane < HEAD_DIM

    row2 = lax.broadcasted_iota(jnp.int32, (blk, 2 * blk), 0)
    col2 = lax.broadcasted_iota(jnp.int32, (blk, 2 * blk), 1)
    band_bias = jnp.where((col2 >= row2) & (col2 <= row2 + N_BACK), 0.0, NEG_BIG)
    causal_bias = band_bias[:, blk:]

    for br, rate in enumerate(DIL_RATES):
        nblk = seq // rate // blk

        def item_rows(j, rate=rate, nblk=nblk):
            return pl.ds((j // nblk) + (j % nblk) * (blk * rate), blk, stride=rate)

        def prep(j, carry, item_rows=item_rows):
            rows = item_rows(j)
            qs = q_ref[rows, :] * ATT_SCALE
            v = v_ref[rows, :]
            qa_scr[j] = jnp.where(is_a, qs, 0.0).astype(BF16)
            qb_scr[j] = jnp.where(is_a, 0.0, qs).astype(BF16)
            kt_scr[j] = k_ref[rows, :].T.astype(BF16)
            va_scr[j] = jnp.where(is_a, v, 0.0).astype(BF16)
            vb_scr[j] = jnp.where(is_a, 0.0, v).astype(BF16)
            return carry
        lax.fori_loop(0, nitem, prep, 0, unroll=2)

        def attend(i, carry, item_rows=item_rows, nblk=nblk, br=br):
            js = [i * ATTN_ITEMS_PER_ITER + u for u in range(ATTN_ITEMS_PER_ITER)]
            nk = blk if nblk == 1 else 2 * blk
            ones = jnp.ones((nk, PAIR), BF16)
            kts, vms, biases = [], [], []
            for j in js:
                if nblk == 1:
                    biases.append(causal_bias)
                    kts.append(kt_scr[j])
                    vs = (va_scr[j], vb_scr[j])
                else:
                    jp = jnp.maximum(j - 1, 0)
                    lo = jnp.where(j % nblk == 0, blk, 0)
                    biases.append(jnp.where(col2 >= lo, band_bias, NEG_BIG))
                    kts.append(jnp.concatenate([kt_scr[jp], kt_scr[j]], axis=1))
                    vs = (jnp.concatenate([va_scr[jp], va_scr[j]], axis=0),
                          jnp.concatenate([vb_scr[jp], vb_scr[j]], axis=0))
                vms.append([jnp.concatenate([v, ones], axis=1) for v in vs])
            qs = [(qa_scr[j], qb_scr[j]) for j in js]
            s = [[jnp.dot(q, kt, preferred_element_type=F32) + bias for q in qp]
                 for qp, kt, bias in zip(qs, kts, biases)]
            m = [[jnp.max(x, axis=-1, keepdims=True) for x in xs] for xs in s]
            e = [[jnp.exp(x - mx).astype(BF16) for x, mx in zip(xs, ms)] for xs, ms in zip(s, m)]
            of = [[jnp.dot(x, vm, preferred_element_type=F32) for x, vm in zip(xs, vp)] for xs, vp in zip(e, vms)]
            for j, (ofa, ofb), (ma, mb) in zip(js, of, m):
                den = jnp.where(is_a, ofa[:, PAIR:], ofb[:, PAIR:])
                rows = item_rows(j)
                ob_scr[br, rows, :] = (ofa[:, :PAIR] + ofb[:, :PAIR]) * (1.0 / den)
                lb_scr[br, rows, :] = jnp.where(is_a, ma, mb) + jnp.log(den)
            return carry
        lax.fori_loop(0, nitem // ATTN_ITEMS_PER_ITER, attend, 0)

    ones_bd = _head_ones(PAIR)
    gain = g_ref[...]
    tile = 256

    def merge(i, carry):
        rows = pl.ds(pl.multiple_of(i * tile, tile), tile)
        l0, l1, l2 = lb_scr[0, rows, :], lb_scr[1, rows, :], lb_scr[2, rows, :]
        m = jnp.maximum(jnp.maximum(l0, l1), l2)
        w0, w1, w2 = jnp.exp(l0 - m), jnp.exp(l1 - m), jnp.exp(l2 - m)
        o = (w0 * ob_scr[0, rows, :] + w1 * ob_scr[1, rows, :] + w2 * ob_scr[2, rows, :]) / (w0 + w1 + w2)
        ms = _split_dot_r(o * o, ones_bd, 2) * (1.0 / HEAD_DIM)
        o_ref[rows, :] = (o * lax.rsqrt(ms + NORM_EPS) * gain).astype(o_ref.dtype)
        return carry
    lax.fori_loop(0, seq // tile, merge, 0)


def _attn_prompt(p_main3, att_g):
    nb, seq, _ = p_main3.shape
    npair = C_GRP // PAIR
    col = lambda off: pl.BlockSpec((None, seq, PAIR), lambda b, p, off=off: (b, 0, off + p))
    return pl.pallas_call(
        functools.partial(_attn_prompt_kernel, seq=seq),
        grid=(nb, npair),
        in_specs=[col(0), col(npair), col(2 * npair), pl.BlockSpec((1, PAIR), lambda b, p: (0, p))],
        out_specs=pl.BlockSpec((None, seq, PAIR), lambda b, p: (b, 0, p)),
        out_shape=jax.ShapeDtypeStruct((nb, seq, C_GRP), BF16),
        scratch_shapes=[pltpu.VMEM((3, seq, PAIR), F32), pltpu.VMEM((3, seq, PAIR), F32)]
        + [pltpu.VMEM((seq // N_BACK, N_BACK, PAIR), BF16)] * 5,
        compiler_params=_cparams(2),
        name="attn_prompt",
    )(p_main3, p_main3, p_main3, att_g)


SAMPLE_HEADS_PER_STEP = 8


def _attn_sample_kernel(q_ref, kn_ref, vn_ref, kt_ref, vt_ref, g_ref, o_ref, *, win):
    for h in range(SAMPLE_HEADS_PER_STEP):
        qc = q_ref[0, h] * ATT_SCALE
        kn, vn = kn_ref[0, h], vn_ref[0, h]
        s_all = jnp.sum(kt_ref[0, h] * qc, axis=0, keepdims=True)
        s_new = jnp.sum(kn * qc, axis=0, keepdims=True)
        outs, lses = [], []
        for rate in DIL_RATES:
            lo = win - N_BACK * rate
            s = s_all[:, lo:]
            if rate > 1:
                pos = lax.broadcasted_iota(jnp.int32, s.shape, 1)
                s = jnp.where((pos & (rate - 1)) == 0, s, NEG_BIG)
            m = jnp.maximum(jnp.max(s, axis=-1, keepdims=True), s_new)
            e = jnp.exp(s - m)
            e_new = jnp.exp(s_new - m)
            den = jnp.sum(e, axis=-1, keepdims=True) + e_new
            o = jnp.sum(vt_ref[0, h, :, lo:] * (e / den), axis=-1, keepdims=True) + (e_new / den) * vn
            outs.append(o)
            lses.append(m + jnp.log(den))
        m = jnp.maximum(jnp.maximum(lses[0], lses[1]), lses[2])
        ws = [jnp.exp(l - m) for l in lses]
        o = (ws[0] * outs[0] + ws[1] * outs[1] + ws[2] * outs[2]) / (ws[0] + ws[1] + ws[2])
        ms = jnp.mean(o * o, axis=0, keepdims=True)
        o_ref[0, h] = o * lax.rsqrt(ms + NORM_EPS) * g_ref[h]


def _attn_sample(q, kn, vn, cache_kt, cache_vt, att_g):
    nb, win = cache_kt.shape[0], cache_kt.shape[-1]
    assert win % (N_BACK * max(DIL_RATES)) == 0 and win % LANES == 0
    hs = SAMPLE_HEADS_PER_STEP
    tok = pl.BlockSpec((1, hs, HEAD_DIM, 1), lambda b, g: (b, g, 0, 0))
    cache = pl.BlockSpec((1, hs, HEAD_DIM, win), lambda b, g: (b, g, 0, 0))
    return pl.pallas_call(
        functools.partial(_attn_sample_kernel, win=win),
        grid=(nb, N_HEADS // hs),
        in_specs=[tok, tok, tok, cache, cache, pl.BlockSpec((hs, HEAD_DIM, 1), lambda b, g: (g, 0, 0))],
        out_specs=tok,
        out_shape=jax.ShapeDtypeStruct((nb, N_HEADS, HEAD_DIM, 1), F32),
        compiler_params=_cparams(2),
        name="attn_sample",
    )(q, kn, vn, cache_kt, cache_vt, att_g)


def _rwkv_token_math(xr, xk, xv, xl, prm, ones_bd):
    w_raw = prm["w0"] + _bdot(jnp.tanh(xl[:, :2 * W_LORA]), prm["w_up"])
    logw = -jnp.exp(-_softplus(-w_raw) - 0.5)
    a_sig = _sigmoid(prm["a0"] + _bdot(xl[:, :2 * W_LORA], prm["a_up"]))
    gate = _bdot(_sigmoid(xl[:, 2 * W_LORA:]), prm["g_up"])
    kk = xk * prm["k_k"]
    nrm = jnp.sqrt(_split_dot_r(kk * kk, ones_bd, 2))
    kk = kk / jnp.maximum(nrm, 1e-12)
    k_eff = xk * (1.0 + (a_sig - 1.0) * prm["k_a"])
    bonus = _split_dot_r(xr * k_eff * prm["r_k"], ones_bd, 2) * xv
    return dict(r=xr, k=k_eff, v=xv, a=-kk, b=kk * a_sig, logw=logw, gate=gate, bonus=bonus)


def _group_norm_gate(y, bonus, gate, lnx_w, lnx_b, ones_bd):
    mean = _split_dot_r(y, ones_bd, 2) * (1.0 / HEAD_DIM)
    d = y - mean
    var = _split_dot_r(d * d, ones_bd, 2) * (1.0 / HEAD_DIM)
    yn = d * lax.rsqrt(var + LNX_EPS) * lnx_w + lnx_b
    return (yn + bonus) * gate


_RW_VEC_NAMES = ("w0", "a0", "k_k", "k_a", "r_k", "lnx_w", "lnx_b")


def _rwkv_prompt_kernel(r_ref, k_ref, v_ref, l_ref, pm_ref, pl_ref, mu_m_ref, mu_l_ref, vec_ref,
                        wup_ref, aup_ref, gup_ref, o_ref, h_ref,
                        tok_scr, y_scr, rw_scr, y0_scr, g_scr, ha_scr, pc_scr, *, seq):
    n = PAIR
    ones_bd = _head_ones(n)
    row = lax.broadcasted_iota(jnp.int32, (seq, 1), 0)

    def shift(x, prev):
        return jnp.where(row == 0, prev, pltpu.roll(x, 1, axis=0))

    def lerp(x, prev, mu):
        return x + mu * (shift(x, prev) - x)

    pm = pm_ref[...]
    mu_m = mu_m_ref[...]
    prm = {name: vec_ref[i:i + 1, :] for i, name in enumerate(_RW_VEC_NAMES)}
    prm.update(w_up=wup_ref[...], a_up=aup_ref[...], g_up=gup_ref[...])
    tok = _rwkv_token_math(lerp(r_ref[...], pm[0:1], mu_m[0:1]),
                           lerp(k_ref[...], pm[1:2], mu_m[1:2]),
                           lerp(v_ref[...], pm[2:3], mu_m[2:3]),
                           lerp(l_ref[...], pl_ref[...], mu_l_ref[...]), prm, ones_bd)
    for i, name in enumerate(("r", "k", "v", "a", "b", "logw", "gate", "bonus")):
        tok_scr[i] = tok[name]

    nch = seq // CHUNK
    lane = lax.broadcasted_iota(jnp.int32, (CHUNK, n), 1)
    is_a = lane < HEAD_DIM
    ri = lax.broadcasted_iota(jnp.int32, (n, n), 0)
    ci = lax.broadcasted_iota(jnp.int32, (n, n), 1)
    strict = ri > ci
    incl = ri >= ci
    eye = ri == ci
    rc = lax.broadcasted_iota(jnp.int32, (CHUNK, CHUNK), 0)
    cc = lax.broadcasted_iota(jnp.int32, (CHUNK, CHUNK), 1)

    def stack(x):
        return jnp.concatenate([jnp.where(is_a, x, 0.0), jnp.where(is_a, 0.0, x)], axis=0)

    def phase_a(g, carry):
        cs = [g * PHASE_A_UNROLL + u for u in range(PHASE_A_UNROLL)]
        each = lambda fn, *lists: [fn(*xs) for xs in zip(*lists)]
        tril_ones = jnp.where(rc >= cc, 1.0, 0.0).astype(BF16)
        tok = [[tok_scr[i, pl.ds(pl.multiple_of(c * CHUNK, CHUNK), CHUNK), :] for i in range(6)] for c in cs]
        r_c, k_c, v_c, a_c, b_c, lw_c = [list(x) for x in zip(*tok)]
        lcum = each(lambda lw: _split_dot(tril_ones, lw, 3), lw_c)
        lend = each(lambda l: l[CHUNK - 1:CHUNK, :], lcum)
        sb = lambda x: stack(x).astype(BF16)
        a_s = each(lambda a, l, lw: sb(a * jnp.exp(l - lw)), a_c, lcum, lw_c)
        r_s = each(lambda r, l: sb(r * jnp.exp(l)), r_c, lcum)
        b_s = each(lambda b, l: sb(b * jnp.exp(-l)), b_c, lcum)
        k_s = each(lambda k, l: sb(k * jnp.exp(-l)), k_c, lcum)
        v_s = each(sb, v_c)
        be_s = each(lambda b, l, le: sb(b * jnp.exp(le - l)), b_c, lcum, lend)
        ke_s = each(lambda k, l, le: sb(k * jnp.exp(le - l)), k_c, lcum, lend)

        sc = each(lambda a, r, b, k: _bdot_nt(jnp.concatenate([a, r], axis=0), jnp.concatenate([b, k], axis=0)),
                  a_s, r_s, b_s, k_s)
        s_ab = each(lambda s: jnp.where(strict, s[:n, :n], 0.0), sc)
        s_ak = each(lambda s: jnp.where(strict, s[:n, n:], 0.0).astype(BF16), sc)
        s_rb = each(lambda s: jnp.where(incl, s[n:, :n], 0.0).astype(BF16), sc)
        s_rk = each(lambda s: jnp.where(incl, s[n:, n:], 0.0).astype(BF16), sc)

        apow = each(lambda s: s.astype(BF16), s_ab)
        tinv = each(lambda s: jnp.where(eye, 1.0, 0.0) + s, s_ab)
        for _ in range(CHUNK.bit_length() - 2):
            apow = each(lambda p: _bdot(p, p).astype(BF16), apow)
            tinv = each(lambda t, p: t + _bdot(t, p), tinv, apow)

        x1 = each(_bdot, s_ak, v_s)
        wu = each(lambda t, a, x: _bdot(t, jnp.concatenate([a, x.astype(BF16)], axis=1)), tinv, a_s, x1)
        yk = each(_bdot, s_rk, v_s)
        ry = each(lambda s, w: _bdot(s, w), s_rb, wu)
        gh = each(lambda w, be: _bdot(w.T, be), wu, be_s)
        hk = each(lambda v, ke: _bdot_tn(v, ke), v_s, ke_s)
        for u, c in enumerate(cs):
            rw_scr[c] = ry[u][:, :n] + r_s[u].astype(F32)
            y0_scr[c] = ry[u][:, n:] + yk[u]
            g_scr[c] = gh[u][:n, :]
            ha_scr[c] = gh[u][n:, :] + hk[u]
            pc_scr[c] = jnp.exp(lend[u])
        return carry
    lax.fori_loop(0, nch // PHASE_A_UNROLL, phase_a, 0)

    h_ref[...] = jnp.zeros((n, n), F32)

    def phase_b(c, carry):
        rows = pl.ds(pl.multiple_of(c * CHUNK, CHUNK), CHUNK)
        ht = h_ref[...]
        hb = ht.astype(BF16)
        ys = _bdot_nt(rw_scr[c], hb) + y0_scr[c]
        y_scr[rows, :] = ys[:CHUNK, :] + ys[CHUNK:, :]
        h_ref[...] = ht * pc_scr[c] + _bdot(hb, g_scr[c]) + ha_scr[c]
        return carry
    lax.fori_loop(0, nch, phase_b, 0)

    o_ref[...] = _group_norm_gate(y_scr[...], tok_scr[7], tok_scr[6], prm["lnx_w"], prm["lnx_b"],
                                  _head_ones(n)).astype(o_ref.dtype)


def _rwkv_prompt(p_main3, p_lora3, prev_main, prev_lora, mu_main, mu_lora, vecs, w_up, a_up, g_up):
    nb, seq, _ = p_main3.shape
    npair = C_GRP // PAIR
    col = lambda off: pl.BlockSpec((None, seq, PAIR), lambda b, p, off=off: (b, 0, off + p))
    pcol = lambda rows: pl.BlockSpec((rows, PAIR), lambda b, p: (0, p))
    return pl.pallas_call(
        functools.partial(_rwkv_prompt_kernel, seq=seq),
        grid=(nb, npair),
        in_specs=[
            col(3 * npair), col(4 * npair), col(5 * npair),
            pl.BlockSpec((None, seq, C_LORA), lambda b, p: (b, 0, 0)),
            pl.BlockSpec((None, 3, PAIR), lambda b, p: (b, 0, p)),
            pl.BlockSpec((None, 1, C_LORA), lambda b, p: (b, 0, 0)),
            pcol(3), pl.BlockSpec((1, C_LORA), lambda b, p: (0, 0)), pcol(len(_RW_VEC_NAMES)),
            pcol(2 * W_LORA), pcol(2 * A_LORA), pcol(G_LORA),
        ],
        out_specs=[
            pl.BlockSpec((None, seq, PAIR), lambda b, p: (b, 0, p)),
            pl.BlockSpec((None, None, PAIR, PAIR), lambda b, p: (b, p, 0, 0)),
        ],
        out_shape=[jax.ShapeDtypeStruct((nb, seq, C_GRP), BF16),
                   jax.ShapeDtypeStruct((nb, npair, PAIR, PAIR), F32)],
        scratch_shapes=[
            pltpu.VMEM((8, seq, PAIR), F32), pltpu.VMEM((seq, PAIR), F32),
            pltpu.VMEM((seq // CHUNK, PAIR, PAIR), F32), pltpu.VMEM((seq // CHUNK, PAIR, PAIR), F32),
            pltpu.VMEM((seq // CHUNK, PAIR, PAIR), F32), pltpu.VMEM((seq // CHUNK, PAIR, PAIR), F32),
            pltpu.VMEM((seq // CHUNK, 1, PAIR), F32),
        ],
        compiler_params=_cparams(2),
        name="rwkv_prompt",
    )(p_main3, p_main3, p_main3, p_lora3, prev_main, prev_lora, mu_main, mu_lora, vecs, w_up, a_up, g_up)


def _rwkv_sample_tok_kernel(pm_ref, l_ref, pvm_ref, pvl_ref, mu_m_ref, mu_l_ref, vec_ref,
                            wup_ref, aup_ref, gup_ref, out_ref):
    ones_bd = _head_ones(C_GRP)
    prm = {name: vec_ref[i:i + 1, :] for i, name in enumerate(_RW_VEC_NAMES)}
    prm.update(w_up=wup_ref[...], a_up=aup_ref[...], g_up=gup_ref[...])
    lerp = lambda x, prev, mu: x + mu * (prev - x)
    xs = [lerp(pm_ref[:, i * C_GRP:(i + 1) * C_GRP], pvm_ref[:, i * C_GRP:(i + 1) * C_GRP],
               mu_m_ref[:, i * C_GRP:(i + 1) * C_GRP]) for i in range(3)]
    tok = _rwkv_token_math(xs[0], xs[1], xs[2], lerp(l_ref[...], pvl_ref[...], mu_l_ref[...]), prm, ones_bd)
    for i, name in enumerate(("r", "k", "v", "a", "b", "logw", "gate", "bonus")):
        out_ref[i] = tok[name]


def _rwkv_sample_step_kernel(s_ref, row_ref, col_ref, s_out_ref, o_ref):
    s = s_ref[0]
    r, k, a, b, logw = [row_ref[0, i] for i in range(5)]
    v, gate, bonus, lnx_w, lnx_b = [col_ref[0, i] for i in range(5)]
    sa = jnp.sum(s * a, axis=-1, keepdims=True)
    s_new = s * jnp.exp(logw) + sa * b + v * k
    s_out_ref[0] = s_new
    y = jnp.sum(s_new * r, axis=-1, keepdims=True)
    mean = jnp.mean(y, axis=1, keepdims=True)
    d = y - mean
    var = jnp.mean(d * d, axis=1, keepdims=True)
    yn = d * lax.rsqrt(var + LNX_EPS) * lnx_w + lnx_b
    o_ref[0] = (yn + bonus) * gate


def _rwkv_sample(pm_s, pl_s, prev_main, prev_lora, mu_main, mu_lora, vecs, w_up, a_up, g_up, wkv0):
    nb = pm_s.shape[0]
    full = lambda shape: pl.BlockSpec(shape, lambda i: (0,) * len(shape))
    tok = pl.pallas_call(
        _rwkv_sample_tok_kernel,
        grid=(1,),
        in_specs=[pl.BlockSpec((nb, 3 * C_GRP), lambda i: (0, 1)), full((nb, C_LORA)),
                  full((nb, 3 * C_GRP)), full((nb, C_LORA)), full((1, 3 * C_GRP)), full((1, C_LORA)),
                  full(vecs.shape), full(w_up.shape), full(a_up.shape), full(g_up.shape)],
        out_specs=full((8, nb, C_GRP)),
        out_shape=jax.ShapeDtypeStruct((8, nb, C_GRP), F32),
        compiler_params=_cparams(1),
        name="rwkv_sample_tok",
    )(pm_s, pl_s, prev_main, prev_lora, mu_main, mu_lora, vecs, w_up, a_up, g_up)
    heads = lambda x: x.reshape(x.shape[0], nb, N_HEADS, HEAD_DIM).transpose(1, 0, 2, 3)
    rows = heads(jnp.concatenate([tok[0:2], tok[3:6]], axis=0))[:, :, :, None, :]
    lnx = jnp.broadcast_to(vecs[5:7, None, :], (2, nb, C_GRP))
    cols = heads(jnp.concatenate([tok[2:3], tok[6:8], lnx], axis=0))[..., None]
    st = (1, N_HEADS, HEAD_DIM, HEAD_DIM)
    s_new, o = pl.pallas_call(
        _rwkv_sample_step_kernel,
        grid=(nb,),
        in_specs=[pl.BlockSpec(st, lambda b: (b, 0, 0, 0)),
                  pl.BlockSpec((1, 5, N_HEADS, 1, HEAD_DIM), lambda b: (b, 0, 0, 0, 0)),
                  pl.BlockSpec((1, 5, N_HEADS, HEAD_DIM, 1), lambda b: (b, 0, 0, 0, 0))],
        out_specs=[pl.BlockSpec(st, lambda b: (b, 0, 0, 0)),
                   pl.BlockSpec((1, N_HEADS, HEAD_DIM, 1), lambda b: (b, 0, 0, 0))],
        out_shape=[jax.ShapeDtypeStruct((nb,) + st[1:], F32),
                   jax.ShapeDtypeStruct((nb, N_HEADS, HEAD_DIM, 1), F32)],
        compiler_params=_cparams(1),
        name="rwkv_sample_step",
    )(wkv0, rows, cols)
    return o.reshape(nb, C_GRP), s_new


def _out_proj_kernel(x_ref, oa_ref, orw_ref, wa_ref, wb_ref, o_ref):
    o_ref[...] = (x_ref[...] + _bdot(oa_ref[...], wa_ref[...]) + _bdot(orw_ref[...], wb_ref[...]))


def _out_proj(x2d, o_att, o_rw, w_o, tm, tn):
    m = x2d.shape[0]
    return pl.pallas_call(
        _out_proj_kernel,
        grid=(m // tm, D_MODEL // tn),
        in_specs=[
            pl.BlockSpec((tm, tn), lambda i, j: (i, j)),
            pl.BlockSpec((tm, C_GRP), lambda i, j: (i, 0)),
            pl.BlockSpec((tm, C_GRP), lambda i, j: (i, 0)),
            pl.BlockSpec((C_GRP, tn), lambda i, j: (0, j)),
            pl.BlockSpec((C_GRP, tn), lambda i, j: (1, j)),
        ],
        out_specs=pl.BlockSpec((tm, tn), lambda i, j: (i, j)),
        out_shape=jax.ShapeDtypeStruct((m, D_MODEL), F32),
        compiler_params=_cparams(2),
        name="out_proj",
    )(x2d, o_att, o_rw, w_o, w_o)


HALO = 16


def _ffn_kernel(*refs, tm, tiles_per_seq, seq_mode):
    if seq_mode:
        (x_ref, xh_ref, g_ref, wg_ref, wv_ref, cwg_ref, cwv_ref, wd_ref, gf_ref,
         y_ref, ug_ref, uv_ref, h_scr, acc_scr) = refs
    else:
        (x_ref, pg_ref, pv_ref, g_ref, wg_ref, wv_ref, cwg_ref, cwv_ref, wd_ref, gf_ref,
         y_ref, ug_ref, uv_ref, h_scr, acc_scr) = refs
    i, j = pl.program_id(0), pl.program_id(1)

    def norm(x):
        ms = jnp.mean(x * x, axis=-1, keepdims=True)
        return x * lax.rsqrt(ms + NORM_EPS) * g_ref[...]

    @pl.when(j == 0)
    def _():
        x = x_ref[...]
        acc_scr[...] = x
        if seq_mode:
            keep = jnp.where(i % tiles_per_seq == 0, 0.0, 1.0).astype(F32)
            h_scr[:HALO, :] = (norm(xh_ref[...]) * keep).astype(BF16)
            h_scr[HALO:, :] = norm(x).astype(BF16)
        else:
            h_scr[...] = norm(x).astype(BF16)

    h = h_scr[...]
    halves = []
    for w_ref, cw_ref, u_ref, p_ref in ((wg_ref, cwg_ref, ug_ref, None if seq_mode else pg_ref),
                                        (wv_ref, cwv_ref, uv_ref, None if seq_mode else pv_ref)):
        u = jnp.dot(h, w_ref[...], preferred_element_type=F32)
        cw = cw_ref[...]
        if seq_mode:
            u_ref[0] = u[HALO + tm - 2:HALO + tm, :]
            c = (cw[3:4] + cw[0:1] * u[HALO - 2:HALO - 2 + tm] + cw[1:2] * u[HALO - 1:HALO - 1 + tm]
                 + cw[2:3] * u[HALO:HALO + tm])
        else:
            u_ref[...] = u
            c = cw[3:4] + cw[0:1] * p_ref[0] + cw[1:2] * p_ref[1] + cw[2:3] * u
        halves.append(c)
    gate, val = halves
    act = gate * _sigmoid(gate) * val
    acc_scr[...] += jnp.dot(act.astype(BF16), wd_ref[...], preferred_element_type=F32)

    @pl.when(j == pl.num_programs(1) - 1)
    def _():
        x2 = acc_scr[...]
        ms = jnp.mean(x2 * x2, axis=-1, keepdims=True)
        y_ref[...] = x2 * lax.rsqrt(ms + NORM_EPS) * gf_ref[...]


def _ffn(x1, norm_g, w_up, conv_wb, w_down, final_g, tm, tf, seq_len=None, prev=None):
    m = x1.shape[0]
    nf = D_FF // tf
    seq_mode = prev is None
    tiles_per_seq = seq_len // tm if seq_mode else 1
    vec = lambda width: pl.BlockSpec((1, width), lambda i, j: (0, 0))
    in_specs = [pl.BlockSpec((tm, D_MODEL), lambda i, j: (i, 0))]
    args = [x1]
    if seq_mode:
        in_specs.append(pl.BlockSpec((HALO, D_MODEL), lambda i, j: (jnp.maximum(i * (tm // HALO) - 1, 0), 0)))
        args.append(x1)
    else:
        in_specs += [pl.BlockSpec((2, tm, tf), lambda i, j: (0, i, j)),
                     pl.BlockSpec((2, tm, tf), lambda i, j: (0, i, nf + j))]
        args += [prev, prev]
    in_specs += [
        vec(D_MODEL),
        pl.BlockSpec((D_MODEL, tf), lambda i, j: (0, j)),
        pl.BlockSpec((D_MODEL, tf), lambda i, j: (0, nf + j)),
        pl.BlockSpec((4, tf), lambda i, j: (0, j)),
        pl.BlockSpec((4, tf), lambda i, j: (0, nf + j)),
        pl.BlockSpec((tf, D_MODEL), lambda i, j: (j, 0)),
        vec(D_MODEL),
    ]
    args += [norm_g, w_up, w_up, conv_wb, conv_wb, w_down, final_g]
    if seq_mode:
        u_spec = pl.BlockSpec((1, 2, tf), lambda i, j: (i, 0, j))
        u_shape = jax.ShapeDtypeStruct((m // tm, 2, D_FF), F32)
    else:
        u_spec = pl.BlockSpec((tm, tf), lambda i, j: (i, j))
        u_shape = jax.ShapeDtypeStruct((m, D_FF), F32)
    rows = tm + HALO if seq_mode else tm
    return pl.pallas_call(
        functools.partial(_ffn_kernel, tm=tm, tiles_per_seq=tiles_per_seq, seq_mode=seq_mode),
        grid=(m // tm, nf),
        in_specs=in_specs,
        out_specs=[pl.BlockSpec((tm, D_MODEL), lambda i, j: (i, 0)), u_spec, u_spec],
        out_shape=[jax.ShapeDtypeStruct((m, D_MODEL), F32), u_shape, u_shape],
        scratch_shapes=[pltpu.VMEM((rows, D_MODEL), BF16), pltpu.VMEM((tm, D_MODEL), F32)],
        compiler_params=_cparams(2),
        name="ffn_seq" if seq_mode else "ffn_state",
    )(*args)


def _layer_params(l, norm_mix_g, w_in, att_out_g, rw_mu, rw_w0, rw_w_up, rw_a0, rw_a_up, rw_g_up, rw_k_k,
                  rw_k_a, rw_r_k, rw_lnx_w, rw_lnx_b, w_o, norm_ffn_g, ffn_w_up, ffn_conv_w, ffn_conv_b,
                  ffn_w_down):
    zeros = jnp.zeros((W_LORA, C_GRP), F32)
    return dict(
        norm_mix_g=norm_mix_g[l][None],
        w_main=w_in[l][:, :C_MAIN].astype(BF16),
        w_lora=w_in[l][:, C_MAIN:].astype(BF16),
        att_g=att_out_g[l][None],
        mu_main=rw_mu[l][:3 * C_GRP].reshape(3, C_GRP),
        mu_lora=rw_mu[l][None, 3 * C_GRP:],
        vecs=jnp.stack([rw_w0[l], rw_a0[l], rw_k_k[l], rw_k_a[l], rw_r_k[l], rw_lnx_w[l], rw_lnx_b[l]]),
        w_up=jnp.concatenate([rw_w_up[l], zeros]).astype(BF16),
        a_up=jnp.concatenate([zeros, rw_a_up[l]]).astype(BF16),
        g_up=rw_g_up[l].astype(BF16),
        w_o=w_o[l].astype(BF16),
        norm_ffn_g=norm_ffn_g[l][None],
        ffn_w_up=ffn_w_up[l].astype(BF16),
        conv_wb=jnp.concatenate([ffn_conv_w[l], ffn_conv_b[l][None]]),
        ffn_w_down=ffn_w_down[l].astype(BF16),
    )


def _prompt_layer(x, lp, final_g):
    nb, seq, _ = x.shape
    x2d = x.reshape(nb * seq, D_MODEL)
    p_main, p_lora = _in_proj(x2d, lp["norm_mix_g"], lp["w_main"], lp["w_lora"], tm=1024, tn=1024)
    p_main3 = p_main.reshape(nb, seq, C_MAIN)
    p_lora3 = p_lora.reshape(nb, seq, C_LORA)
    o_att = _attn_prompt(p_main3, lp["att_g"])
    o_rw, h_fin = _rwkv_prompt(p_main3, p_lora3, jnp.zeros((nb, 3, C_GRP), F32), jnp.zeros((nb, 1, C_LORA), F32),
                               lp["mu_main"], lp["mu_lora"], lp["vecs"], lp["w_up"], lp["a_up"], lp["g_up"])
    x1 = _out_proj(x2d, o_att.reshape(nb * seq, C_GRP), o_rw.reshape(nb * seq, C_GRP), lp["w_o"], tm=512, tn=D_MODEL)
    y, u_g, u_v = _ffn(x1, lp["norm_ffn_g"], lp["ffn_w_up"], lp["conv_wb"], lp["ffn_w_down"], final_g,
                       tm=512, tf=512, seq_len=seq)
    hd = (N_HEADS, HEAD_DIM)
    k_new = p_main3[:, :, C_GRP:2 * C_GRP].reshape((nb, seq) + hd)
    v_new = p_main3[:, :, 2 * C_GRP:3 * C_GRP].reshape((nb, seq) + hd)
    rw_last = jnp.concatenate([p_main3[:, -1:, 3 * C_GRP:], p_lora3[:, -1:, :]], axis=-1)
    wkv = jnp.stack([h_fin[:, :, :HEAD_DIM, :HEAD_DIM], h_fin[:, :, HEAD_DIM:, HEAD_DIM:]], axis=2)
    wkv = wkv.reshape(nb, N_HEADS, HEAD_DIM, HEAD_DIM)
    tiles_per_seq = u_g.shape[0] // nb
    ffn_last = jnp.concatenate([u_g, u_v], axis=-1)[tiles_per_seq - 1::tiles_per_seq]
    return y.reshape(nb, seq, D_MODEL), k_new, v_new, rw_last, wkv, ffn_last


def _sample_layer(x, cache_k, cache_v, rw_prev, wkv0, ffn_prev, lp, final_g):
    nb = x.shape[0]
    x2d = x.reshape(nb, D_MODEL)
    p_main, p_lora = _in_proj(x2d, lp["norm_mix_g"], lp["w_main"], lp["w_lora"], tm=nb, tn=1024)
    hd = (N_HEADS, HEAD_DIM)
    q = p_main[:, :C_GRP].reshape((nb,) + hd)
    k_new = p_main[:, C_GRP:2 * C_GRP].reshape((nb,) + hd)
    v_new = p_main[:, 2 * C_GRP:3 * C_GRP].reshape((nb,) + hd)
    cache_kt, cache_vt = cache_k.transpose(0, 2, 3, 1), cache_v.transpose(0, 2, 3, 1)
    o_att = _attn_sample(q[..., None], k_new[..., None], v_new[..., None], cache_kt, cache_vt,
                         lp["att_g"].reshape(hd + (1,))).reshape(nb, C_GRP)
    prev = rw_prev.reshape(nb, C_SHIFT)
    o_rw, wkv = _rwkv_sample(p_main, p_lora, prev[:, :3 * C_GRP], prev[:, 3 * C_GRP:],
                             lp["mu_main"].reshape(1, 3 * C_GRP), lp["mu_lora"], lp["vecs"],
                             lp["w_up"], lp["a_up"], lp["g_up"], wkv0)
    x1 = _out_proj(x2d, o_att, o_rw, lp["w_o"], tm=nb, tn=1024)
    prev_rows = ffn_prev.transpose(1, 0, 2)
    y, u_g, u_v = _ffn(x1, lp["norm_ffn_g"], lp["ffn_w_up"], lp["conv_wb"], lp["ffn_w_down"], final_g,
                       tm=nb, tf=512, prev=prev_rows)
    rw_last = jnp.concatenate([p_main[:, 3 * C_GRP:], p_lora], axis=-1)[:, None, :]
    ffn_last = jnp.stack([ffn_prev[:, 1, :], jnp.concatenate([u_g, u_v], axis=-1)], axis=1)
    return y.reshape(nb, 1, D_MODEL), k_new[:, None], v_new[:, None], rw_last, wkv, ffn_last


def kernel(x_prompt, x_sample, cache_att_k, cache_att_v, state_rwkv_shift, state_rwkv_wkv, state_ffn_conv, norm_mix_g, w_in, att_out_g, rw_mu, rw_w0, rw_w_up, rw_a0, rw_a_up, rw_g_up, rw_k_k, rw_k_a, rw_r_k, rw_lnx_w, rw_lnx_b, w_o, norm_ffn_g, ffn_w_up, ffn_conv_w, ffn_conv_b, ffn_w_down, norm_final_g):
    depth = w_in.shape[0]
    assert depth == 1, "the fused FFN + final-norm kernel assumes a single trunk layer"
    assert x_sample.shape[1] == 1, "the sample path handles one new token per sequence"
    final_g = norm_final_g[None]
    lp = _layer_params(0, norm_mix_g, w_in, att_out_g, rw_mu, rw_w0, rw_w_up, rw_a0, rw_a_up, rw_g_up, rw_k_k,
                       rw_k_a, rw_r_k, rw_lnx_w, rw_lnx_b, w_o, norm_ffn_g, ffn_w_up, ffn_conv_w, ffn_conv_b,
                       ffn_w_down)
    yp, pk, pv, prw, pwkv, pffn = _prompt_layer(x_prompt, lp, final_g)
    ys, sk, sv, srw, swkv, sffn = _sample_layer(x_sample, cache_att_k[0], cache_att_v[0], state_rwkv_shift[0],
                                                state_rwkv_wkv[0], state_ffn_conv[0], lp, final_g)
    lead = lambda t: t[None]
    return (yp, ys, lead(pk), lead(pv), lead(prw), lead(pwkv), lead(pffn),
            lead(sk), lead(sv), lead(srw), lead(swkv), lead(sffn))
```

```python
import functools

import jax
import jax.numpy as jnp
from jax import lax
from jax.experimental import pallas as pl
from jax.experimental.pallas import tpu as pltpu

F32 = jnp.float32
BF16 = jnp.bfloat16

D_MODEL = 2048
HEAD_DIM = 64
N_HEADS = 16
C_GRP = N_HEADS * HEAD_DIM
W_LORA, A_LORA, G_LORA = 64, 64, 160
C_LORA = W_LORA + A_LORA + G_LORA
C_SHIFT = 3 * C_GRP + C_LORA
C_MAIN = 6 * C_GRP
D_FF = 5632
DIL_RATES = (1, 4, 16)
N_BACK = 128
ATT_SCALE = HEAD_DIM ** -0.5
NORM_EPS = 1e-6
LNX_EPS = HEAD_DIM * 1e-5
NEG_BIG = -1e30

LANES = 128
VMEM_LIMIT = 48 * 1024 * 1024

CHUNK = 64
PHASE_A_UNROLL = 8
ATTN_ITEMS_PER_ITER = 4
PAIR = 2 * HEAD_DIM


def _cparams(n_grid):
    return pltpu.CompilerParams(dimension_semantics=("arbitrary",) * n_grid,
                                vmem_limit_bytes=VMEM_LIMIT)


def _bdot(a, b):
    return jnp.dot(a.astype(BF16), b.astype(BF16), preferred_element_type=F32)


def _bdot_nt(a, b):
    return lax.dot_general(a.astype(BF16), b.astype(BF16), (((1,), (1,)), ((), ())),
                           preferred_element_type=F32)


def _bdot_tn(a, b):
    return jnp.dot(a.astype(F32).T.astype(BF16), b.astype(BF16), preferred_element_type=F32)


def _split_dot(mat01, x, terms):
    acc = None
    rem = x
    for _ in range(terms):
        piece = rem.astype(BF16)
        part = jnp.dot(mat01, piece, preferred_element_type=F32)
        acc = part if acc is None else acc + part
        rem = rem - piece.astype(F32)
    return acc


def _split_dot_r(x, mat01, terms):
    acc = None
    rem = x
    for _ in range(terms):
        piece = rem.astype(BF16)
        part = jnp.dot(piece, mat01, preferred_element_type=F32)
        acc = part if acc is None else acc + part
        rem = rem - piece.astype(F32)
    return acc


def _head_ones(n):
    r = lax.broadcasted_iota(jnp.int32, (n, n), 0) // HEAD_DIM
    c = lax.broadcasted_iota(jnp.int32, (n, n), 1) // HEAD_DIM
    return jnp.where(r == c, 1.0, 0.0).astype(BF16)


def _sigmoid(x):
    return 1.0 / (1.0 + jnp.exp(-x))


def _softplus(x):
    return jnp.maximum(x, 0.0) + jnp.log(1.0 + jnp.exp(-jnp.abs(x)))


def _in_proj_kernel(x_ref, g_ref, wm_ref, wl_ref, om_ref, ol_ref, h_scr):
    @pl.when(pl.program_id(1) == 0)
    def _():
        x = x_ref[...]
        ms = jnp.mean(x * x, axis=-1, keepdims=True)
        h = (x * lax.rsqrt(ms + NORM_EPS) * g_ref[...]).astype(BF16)
        h_scr[...] = h
        ol_ref[...] = jnp.dot(h, wl_ref[...], preferred_element_type=F32)

    om_ref[...] = jnp.dot(h_scr[...], wm_ref[...], preferred_element_type=F32)


def _in_proj(x2d, g, w_main, w_lora, tm, tn):
    m = x2d.shape[0]
    return pl.pallas_call(
        _in_proj_kernel,
        grid=(m // tm, C_MAIN // tn),
        in_specs=[
            pl.BlockSpec((tm, D_MODEL), lambda i, j: (i, 0)),
            pl.BlockSpec((1, D_MODEL), lambda i, j: (0, 0)),
            pl.BlockSpec((D_MODEL, tn), lambda i, j: (0, j)),
            pl.BlockSpec((D_MODEL, C_LORA), lambda i, j: (0, 0)),
        ],
        out_specs=[
            pl.BlockSpec((tm, tn), lambda i, j: (i, j)),
            pl.BlockSpec((tm, C_LORA), lambda i, j: (i, 0)),
        ],
        out_shape=[jax.ShapeDtypeStruct((m, C_MAIN), F32), jax.ShapeDtypeStruct((m, C_LORA), F32)],
        scratch_shapes=[pltpu.VMEM((tm, D_MODEL), BF16)],
        compiler_params=_cparams(2),
        name="in_proj",
    )(x2d, g, w_main, w_lora)


def _attn_prompt_kernel(q_ref, k_ref, v_ref, g_ref, o_ref, kt_out_ref, vt_out_ref, ob_scr, lb_scr,
                        qa_scr, qb_scr, kt_scr, va_scr, vb_scr, *, seq):
    blk = N_BACK
    nitem = seq // blk
    lane = lax.broadcasted_iota(jnp.int32, (blk, PAIR), 1)
    is_a = lane < HEAD_DIM

    row2 = lax.broadcasted_iota(jnp.int32, (blk, 2 * blk), 0)
    col2 = lax.broadcasted_iota(jnp.int32, (blk, 2 * blk), 1)
    band_bias = jnp.where((col2 >= row2) & (col2 <= row2 + N_BACK), 0.0, NEG_BIG)
    causal_bias = band_bias[:, blk:]

    for br, rate in enumerate(DIL_RATES):
        nblk = seq // rate // blk

        def item_rows(j, rate=rate, nblk=nblk):
            return pl.ds((j // nblk) + (j % nblk) * (blk * rate), blk, stride=rate)

        def prep(j, carry, item_rows=item_rows, rate=rate):
            rows = item_rows(j)
            qs = q_ref[rows, :] * ATT_SCALE
            v = v_ref[rows, :]
            kt = k_ref[rows, :].T
            qa_scr[j] = jnp.where(is_a, qs, 0.0).astype(BF16)
            qb_scr[j] = jnp.where(is_a, 0.0, qs).astype(BF16)
            kt_scr[j] = kt.astype(BF16)
            va_scr[j] = jnp.where(is_a, v, 0.0).astype(BF16)
            vb_scr[j] = jnp.where(is_a, 0.0, v).astype(BF16)
            if rate == 1:
                kt_out_ref[:, j * blk:(j + 1) * blk] = kt
                vt_out_ref[:, j * blk:(j + 1) * blk] = v.T
            return carry
        if rate == 1:
            for j in range(nitem):
                prep(j, 0)
        else:
            lax.fori_loop(0, nitem, prep, 0, unroll=2)

        def attend(i, carry, item_rows=item_rows, nblk=nblk, br=br):
            js = [i * ATTN_ITEMS_PER_ITER + u for u in range(ATTN_ITEMS_PER_ITER)]
            nk = blk if nblk == 1 else 2 * blk
            ones = jnp.ones((nk, PAIR), BF16)
            kts, vms, biases = [], [], []
            for j in js:
                if nblk == 1:
                    biases.append(causal_bias)
                    kts.append(kt_scr[j])
                    vs = (va_scr[j], vb_scr[j])
                else:
                    jp = jnp.maximum(j - 1, 0)
                    lo = jnp.where(j % nblk == 0, blk, 0)
                    biases.append(jnp.where(col2 >= lo, band_bias, NEG_BIG))
                    kts.append(jnp.concatenate([kt_scr[jp], kt_scr[j]], axis=1))
                    vs = (jnp.concatenate([va_scr[jp], va_scr[j]], axis=0),
                          jnp.concatenate([vb_scr[jp], vb_scr[j]], axis=0))
                vms.append([jnp.concatenate([v, ones], axis=1) for v in vs])
            qs = [(qa_scr[j], qb_scr[j]) for j in js]
            s = [[jnp.dot(q, kt, preferred_element_type=F32) + bias for q in qp]
                 for qp, kt, bias in zip(qs, kts, biases)]
            m = [[jnp.max(x, axis=-1, keepdims=True) for x in xs] for xs in s]
            e = [[jnp.exp(x - mx).astype(BF16) for x, mx in zip(xs, ms)] for xs, ms in zip(s, m)]
            of = [[jnp.dot(x, vm, preferred_element_type=F32) for x, vm in zip(xs, vp)] for xs, vp in zip(e, vms)]
            for j, (ofa, ofb), (ma, mb) in zip(js, of, m):
                den = jnp.where(is_a, ofa[:, PAIR:], ofb[:, PAIR:])
                rows = item_rows(j)
                ob_scr[br, rows, :] = (ofa[:, :PAIR] + ofb[:, :PAIR]) * (1.0 / den)
                lb_scr[br, rows, :] = jnp.where(is_a, ma, mb) + jnp.log(den)
            return carry
        lax.fori_loop(0, nitem // ATTN_ITEMS_PER_ITER, attend, 0)

    ones_bd = _head_ones(PAIR)
    gain = g_ref[...]
    tile = 256

    def merge(i, carry):
        rows = pl.ds(pl.multiple_of(i * tile, tile), tile)
        l0, l1, l2 = lb_scr[0, rows, :], lb_scr[1, rows, :], lb_scr[2, rows, :]
        m = jnp.maximum(jnp.maximum(l0, l1), l2)
        w0, w1, w2 = jnp.exp(l0 - m), jnp.exp(l1 - m), jnp.exp(l2 - m)
        o = (w0 * ob_scr[0, rows, :] + w1 * ob_scr[1, rows, :] + w2 * ob_scr[2, rows, :]) / (w0 + w1 + w2)
        ms = _split_dot_r(o * o, ones_bd, 2) * (1.0 / HEAD_DIM)
        o_ref[rows, :] = (o * lax.rsqrt(ms + NORM_EPS) * gain).astype(o_ref.dtype)
        return carry
    lax.fori_loop(0, seq // tile, merge, 0)


def _attn_prompt(p_main3, att_g):
    nb, seq, _ = p_main3.shape
    npair = C_GRP // PAIR
    col = lambda off: pl.BlockSpec((None, seq, PAIR), lambda b, p, off=off: (b, 0, off + p))
    return pl.pallas_call(
        functools.partial(_attn_prompt_kernel, seq=seq),
        grid=(nb, npair),
        in_specs=[col(0), col(npair), col(2 * npair), pl.BlockSpec((1, PAIR), lambda b, p: (0, p))],
        out_specs=[pl.BlockSpec((None, seq, PAIR), lambda b, p: (b, 0, p)),
                   pl.BlockSpec((None, PAIR, seq), lambda b, p: (b, p, 0)),
                   pl.BlockSpec((None, PAIR, seq), lambda b, p: (b, p, 0))],
        out_shape=[jax.ShapeDtypeStruct((nb, seq, C_GRP), BF16),
                   jax.ShapeDtypeStruct((nb, C_GRP, seq), F32), jax.ShapeDtypeStruct((nb, C_GRP, seq), F32)],
        scratch_shapes=[pltpu.VMEM((3, seq, PAIR), F32), pltpu.VMEM((3, seq, PAIR), F32)]
        + [pltpu.VMEM((seq // N_BACK, N_BACK, PAIR), BF16)] * 5,
        compiler_params=_cparams(2),
        name="attn_prompt",
    )(p_main3, p_main3, p_main3, att_g)


SAMPLE_HEADS_PER_STEP = 4


def _attn_sample_kernel(q_ref, kn_ref, vn_ref, kt_ref, vt_ref, g_ref, o_ref, *, win):
    base = pl.program_id(1) * SAMPLE_HEADS_PER_STEP
    head_lane = lax.broadcasted_iota(jnp.int32, (1, N_HEADS), 1)
    pick = lambda x, hl: jnp.sum(jnp.where(head_lane == hl, x, 0.0), axis=-1, keepdims=True)
    q_all, kn_all, vn_all, g_all = q_ref[0], kn_ref[0], vn_ref[0], g_ref[...]
    out = jnp.zeros((HEAD_DIM, N_HEADS), F32)
    for h in range(SAMPLE_HEADS_PER_STEP):
        qc = pick(q_all, base + h) * ATT_SCALE
        kn, vn = pick(kn_all, base + h), pick(vn_all, base + h)
        s_all = jnp.sum(kt_ref[0, h] * qc, axis=0, keepdims=True)
        s_new = jnp.sum(kn * qc, axis=0, keepdims=True)
        outs, lses = [], []
        for rate in DIL_RATES:
            lo = win - N_BACK * rate
            s = s_all[:, lo:]
            if rate > 1:
                pos = lax.broadcasted_iota(jnp.int32, s.shape, 1)
                s = jnp.where((pos & (rate - 1)) == 0, s, NEG_BIG)
            m = jnp.maximum(jnp.max(s, axis=-1, keepdims=True), s_new)
            e = jnp.exp(s - m)
            e_new = jnp.exp(s_new - m)
            den = jnp.sum(e, axis=-1, keepdims=True) + e_new
            o = jnp.sum(vt_ref[0, h, :, lo:] * (e / den), axis=-1, keepdims=True) + (e_new / den) * vn
            outs.append(o)
            lses.append(m + jnp.log(den))
        m = jnp.maximum(jnp.maximum(lses[0], lses[1]), lses[2])
        ws = [jnp.exp(l - m) for l in lses]
        o = (ws[0] * outs[0] + ws[1] * outs[1] + ws[2] * outs[2]) / (ws[0] + ws[1] + ws[2])
        ms = jnp.mean(o * o, axis=0, keepdims=True)
        o = o * lax.rsqrt(ms + NORM_EPS) * pick(g_all, base + h)
        out = out + jnp.where(head_lane == base + h, o, 0.0)

    @pl.when(pl.program_id(1) == 0)
    def _():
        o_ref[0] = out

    @pl.when(pl.program_id(1) != 0)
    def _():
        o_ref[0] += out


def _attn_sample(q, kn, vn, cache_kt, cache_vt, att_g):
    nb, win = cache_kt.shape[0], cache_kt.shape[-1]
    assert win % (N_BACK * max(DIL_RATES)) == 0 and win % LANES == 0
    hs = SAMPLE_HEADS_PER_STEP
    tok = pl.BlockSpec((1, HEAD_DIM, N_HEADS), lambda b, g: (b, 0, 0))
    cache = pl.BlockSpec((1, hs, HEAD_DIM, win), lambda b, g: (b, g, 0, 0))
    return pl.pallas_call(
        functools.partial(_attn_sample_kernel, win=win),
        grid=(nb, N_HEADS // hs),
        in_specs=[tok, tok, tok, cache, cache, pl.BlockSpec((HEAD_DIM, N_HEADS), lambda b, g: (0, 0))],
        out_specs=tok,
        out_shape=jax.ShapeDtypeStruct((nb, HEAD_DIM, N_HEADS), F32),
        compiler_params=_cparams(2),
        name="attn_sample",
    )(q, kn, vn, cache_kt, cache_vt, att_g)


def _rwkv_token_math(xr, xk, xv, xl, prm, ones_bd):
    w_raw = prm["w0"] + _bdot(jnp.tanh(xl[:, :2 * W_LORA]), prm["w_up"])
    logw = -jnp.exp(-_softplus(-w_raw) - 0.5)
    a_sig = _sigmoid(prm["a0"] + _bdot(xl[:, :2 * W_LORA], prm["a_up"]))
    gate = _bdot(_sigmoid(xl[:, 2 * W_LORA:]), prm["g_up"])
    kk = xk * prm["k_k"]
    nrm = jnp.sqrt(_split_dot_r(kk * kk, ones_bd, 2))
    kk = kk / jnp.maximum(nrm, 1e-12)
    k_eff = xk * (1.0 + (a_sig - 1.0) * prm["k_a"])
    bonus = _split_dot_r(xr * k_eff * prm["r_k"], ones_bd, 2) * xv
    return dict(r=xr, k=k_eff, v=xv, a=-kk, b=kk * a_sig, logw=logw, gate=gate, bonus=bonus)


def _group_norm_gate(y, bonus, gate, lnx_w, lnx_b, ones_bd):
    mean = _split_dot_r(y, ones_bd, 2) * (1.0 / HEAD_DIM)
    d = y - mean
    var = _split_dot_r(d * d, ones_bd, 2) * (1.0 / HEAD_DIM)
    yn = d * lax.rsqrt(var + LNX_EPS) * lnx_w + lnx_b
    return (yn + bonus) * gate


_RW_VEC_NAMES = ("w0", "a0", "k_k", "k_a", "r_k", "lnx_w", "lnx_b")


def _rwkv_prompt_kernel(r_ref, k_ref, v_ref, l_ref, pm_ref, pl_ref, mu_m_ref, mu_l_ref, vec_ref,
                        wup_ref, aup_ref, gup_ref, o_ref, h_ref,
                        tok_scr, y_scr, rw_scr, y0_scr, g_scr, ha_scr, pc_scr, *, seq):
    n = PAIR
    ones_bd = _head_ones(n)
    row = lax.broadcasted_iota(jnp.int32, (seq, 1), 0)

    def shift(x, prev):
        return jnp.where(row == 0, prev, pltpu.roll(x, 1, axis=0))

    def lerp(x, prev, mu):
        return x + mu * (shift(x, prev) - x)

    pm = pm_ref[...]
    mu_m = mu_m_ref[...]
    prm = {name: vec_ref[i:i + 1, :] for i, name in enumerate(_RW_VEC_NAMES)}
    prm.update(w_up=wup_ref[...], a_up=aup_ref[...], g_up=gup_ref[...])
    tok = _rwkv_token_math(lerp(r_ref[...], pm[0:1], mu_m[0:1]),
                           lerp(k_ref[...], pm[1:2], mu_m[1:2]),
                           lerp(v_ref[...], pm[2:3], mu_m[2:3]),
                           lerp(l_ref[...], pl_ref[...], mu_l_ref[...]), prm, ones_bd)
    for i, name in enumerate(("r", "k", "v", "a", "b", "logw", "gate", "bonus")):
        tok_scr[i] = tok[name]

    nch = seq // CHUNK
    lane = lax.broadcasted_iota(jnp.int32, (CHUNK, n), 1)
    is_a = lane < HEAD_DIM
    ri = lax.broadcasted_iota(jnp.int32, (n, n), 0)
    ci = lax.broadcasted_iota(jnp.int32, (n, n), 1)
    strict = ri > ci
    incl = ri >= ci
    eye = ri == ci
    rc = lax.broadcasted_iota(jnp.int32, (CHUNK, CHUNK), 0)
    cc = lax.broadcasted_iota(jnp.int32, (CHUNK, CHUNK), 1)

    def stack(x):
        return jnp.concatenate([jnp.where(is_a, x, 0.0), jnp.where(is_a, 0.0, x)], axis=0)

    def phase_b_step(c):
        rows = pl.ds(pl.multiple_of(c * CHUNK, CHUNK), CHUNK)
        ht = h_ref[...]
        hb = ht.astype(BF16)
        ys = _bdot_nt(rw_scr[c], hb) + y0_scr[c]
        y_scr[rows, :] = ys[:CHUNK, :] + ys[CHUNK:, :]
        h_ref[...] = ht * pc_scr[c] + _bdot(hb, g_scr[c]) + ha_scr[c]

    def group(ga, gb):
        b_todo = [] if gb is None else [gb * PHASE_A_UNROLL + u for u in range(PHASE_A_UNROLL)]

        def fill():
            if b_todo:
                phase_b_step(b_todo.pop(0))

        if ga is None:
            while b_todo:
                fill()
            return
        cs = [ga * PHASE_A_UNROLL + u for u in range(PHASE_A_UNROLL)]
        each = lambda fn, *lists: [fn(*xs) for xs in zip(*lists)]
        tril_ones = jnp.where(rc >= cc, 1.0, 0.0).astype(BF16)
        tok = [[tok_scr[i, pl.ds(pl.multiple_of(c * CHUNK, CHUNK), CHUNK), :] for i in range(6)] for c in cs]
        r_c, k_c, v_c, a_c, b_c, lw_c = [list(x) for x in zip(*tok)]
        lcum = each(lambda lw: _split_dot(tril_ones, lw, 3), lw_c)
        lend = each(lambda l: l[CHUNK - 1:CHUNK, :], lcum)
        fill()
        sb = lambda x: stack(x).astype(BF16)
        a_s = each(lambda a, l, lw: sb(a * jnp.exp(l - lw)), a_c, lcum, lw_c)
        r_s = each(lambda r, l: sb(r * jnp.exp(l)), r_c, lcum)
        b_s = each(lambda b, l: sb(b * jnp.exp(-l)), b_c, lcum)
        k_s = each(lambda k, l: sb(k * jnp.exp(-l)), k_c, lcum)
        v_s = each(sb, v_c)
        be_s = each(lambda b, l, le: sb(b * jnp.exp(le - l)), b_c, lcum, lend)
        ke_s = each(lambda k, l, le: sb(k * jnp.exp(le - l)), k_c, lcum, lend)

        sc = each(lambda a, r, b, k: _bdot_nt(jnp.concatenate([a, r], axis=0), jnp.concatenate([b, k], axis=0)),
                  a_s, r_s, b_s, k_s)
        fill()
        s_ab = each(lambda s: jnp.where(strict, s[:n, :n], 0.0), sc)
        s_ak = each(lambda s: jnp.where(strict, s[:n, n:], 0.0).astype(BF16), sc)
        s_rb = each(lambda s: jnp.where(incl, s[n:, :n], 0.0).astype(BF16), sc)
        s_rk = each(lambda s: jnp.where(incl, s[n:, n:], 0.0).astype(BF16), sc)

        apow = each(lambda s: s.astype(BF16), s_ab)
        tinv = each(lambda s: jnp.where(eye, 1.0, 0.0) + s, s_ab)
        for _ in range(CHUNK.bit_length() - 2):
            apow = each(lambda p: _bdot(p, p).astype(BF16), apow)
            tinv = each(lambda t, p: t + _bdot(t, p), tinv, apow)
            fill()

        x1 = each(_bdot, s_ak, v_s)
        wu = each(lambda t, a, x: _bdot(t, jnp.concatenate([a, x.astype(BF16)], axis=1)), tinv, a_s, x1)
        fill()
        yk = each(_bdot, s_rk, v_s)
        ry = each(lambda s, w: _bdot(s, w), s_rb, wu)
        gh = each(lambda w, be: _bdot(w.T, be), wu, be_s)
        hk = each(lambda v, ke: _bdot_tn(v, ke), v_s, ke_s)
        while b_todo:
            fill()
        for u, c in enumerate(cs):
            rw_scr[c] = ry[u][:, :n] + r_s[u].astype(F32)
            y0_scr[c] = ry[u][:, n:] + yk[u]
            g_scr[c] = gh[u][:n, :]
            ha_scr[c] = gh[u][n:, :] + hk[u]
            pc_scr[c] = jnp.exp(lend[u])

    h_ref[...] = jnp.zeros((n, n), F32)
    ngroup = nch // PHASE_A_UNROLL
    group(0, None)

    def pipelined(g, carry):
        group(g, g - 1)
        return carry
    lax.fori_loop(1, ngroup, pipelined, 0)
    group(None, ngroup - 1)

    o_ref[...] = _group_norm_gate(y_scr[...], tok_scr[7], tok_scr[6], prm["lnx_w"], prm["lnx_b"],
                                  _head_ones(n)).astype(o_ref.dtype)


def _rwkv_prompt(p_main3, p_lora3, prev_main, prev_lora, mu_main, mu_lora, vecs, w_up, a_up, g_up):
    nb, seq, _ = p_main3.shape
    npair = C_GRP // PAIR
    col = lambda off: pl.BlockSpec((None, seq, PAIR), lambda b, p, off=off: (b, 0, off + p))
    pcol = lambda rows: pl.BlockSpec((rows, PAIR), lambda b, p: (0, p))
    return pl.pallas_call(
        functools.partial(_rwkv_prompt_kernel, seq=seq),
        grid=(nb, npair),
        in_specs=[
            col(3 * npair), col(4 * npair), col(5 * npair),
            pl.BlockSpec((None, seq, C_LORA), lambda b, p: (b, 0, 0)),
            pl.BlockSpec((None, 3, PAIR), lambda b, p: (b, 0, p)),
            pl.BlockSpec((None, 1, C_LORA), lambda b, p: (b, 0, 0)),
            pcol(3), pl.BlockSpec((1, C_LORA), lambda b, p: (0, 0)), pcol(len(_RW_VEC_NAMES)),
            pcol(2 * W_LORA), pcol(2 * A_LORA), pcol(G_LORA),
        ],
        out_specs=[
            pl.BlockSpec((None, seq, PAIR), lambda b, p: (b, 0, p)),
            pl.BlockSpec((None, None, PAIR, PAIR), lambda b, p: (b, p, 0, 0)),
        ],
        out_shape=[jax.ShapeDtypeStruct((nb, seq, C_GRP), BF16),
                   jax.ShapeDtypeStruct((nb, npair, PAIR, PAIR), F32)],
        scratch_shapes=[
            pltpu.VMEM((8, seq, PAIR), F32), pltpu.VMEM((seq, PAIR), F32),
            pltpu.VMEM((seq // CHUNK, PAIR, PAIR), F32), pltpu.VMEM((seq // CHUNK, PAIR, PAIR), F32),
            pltpu.VMEM((seq // CHUNK, PAIR, PAIR), F32), pltpu.VMEM((seq // CHUNK, PAIR, PAIR), F32),
            pltpu.VMEM((seq // CHUNK, 1, PAIR), F32),
        ],
        compiler_params=_cparams(2),
        name="rwkv_prompt",
    )(p_main3, p_main3, p_main3, p_lora3, prev_main, prev_lora, mu_main, mu_lora, vecs, w_up, a_up, g_up)


def _rwkv_sample_tok_kernel(pm_ref, l_ref, pvm_ref, pvl_ref, mu_m_ref, mu_l_ref, vec_ref,
                            wup_ref, aup_ref, gup_ref, out_ref):
    ones_bd = _head_ones(C_GRP)
    prm = {name: vec_ref[i:i + 1, :] for i, name in enumerate(_RW_VEC_NAMES)}
    prm.update(w_up=wup_ref[...], a_up=aup_ref[...], g_up=gup_ref[...])
    lerp = lambda x, prev, mu: x + mu * (prev - x)
    xs = [lerp(pm_ref[:, i * C_GRP:(i + 1) * C_GRP], pvm_ref[:, i * C_GRP:(i + 1) * C_GRP],
               mu_m_ref[:, i * C_GRP:(i + 1) * C_GRP]) for i in range(3)]
    tok = _rwkv_token_math(xs[0], xs[1], xs[2], lerp(l_ref[...], pvl_ref[...], mu_l_ref[...]), prm, ones_bd)
    for i, name in enumerate(("r", "k", "v", "a", "b", "logw", "gate", "bonus")):
        out_ref[i] = tok[name]


def _rwkv_sample_step_kernel(s_ref, row_ref, col_ref, s_out_ref, o_ref):
    head_lane = lax.broadcasted_iota(jnp.int32, (1, N_HEADS), 1)
    out = jnp.zeros((HEAD_DIM, N_HEADS), F32)
    for h in range(N_HEADS):
        s = s_ref[0, h]
        r, k, a, b, logw = [row_ref[0, i, h:h + 1, :] for i in range(5)]
        v, gate, bonus, lnx_w, lnx_b = [col_ref[0, i, :, h:h + 1] for i in range(5)]
        sa = jnp.sum(s * a, axis=-1, keepdims=True)
        s_new = s * jnp.exp(logw) + sa * b + v * k
        s_out_ref[0, h] = s_new
        y = jnp.sum(s_new * r, axis=-1, keepdims=True)
        mean = jnp.mean(y, axis=0, keepdims=True)
        d = y - mean
        var = jnp.mean(d * d, axis=0, keepdims=True)
        yn = d * lax.rsqrt(var + LNX_EPS) * lnx_w + lnx_b
        out = out + jnp.where(head_lane == h, (yn + bonus) * gate, 0.0)
    o_ref[0] = out


def _rwkv_sample(pm_s, pl_s, prev_main, prev_lora, mu_main, mu_lora, vecs, w_up, a_up, g_up, wkv0):
    nb = pm_s.shape[0]
    full = lambda shape: pl.BlockSpec(shape, lambda i: (0,) * len(shape))
    tok = pl.pallas_call(
        _rwkv_sample_tok_kernel,
        grid=(1,),
        in_specs=[pl.BlockSpec((nb, 3 * C_GRP), lambda i: (0, 1)), full((nb, C_LORA)),
                  full((nb, 3 * C_GRP)), full((nb, C_LORA)), full((1, 3 * C_GRP)), full((1, C_LORA)),
                  full(vecs.shape), full(w_up.shape), full(a_up.shape), full(g_up.shape)],
        out_specs=full((8, nb, C_GRP)),
        out_shape=jax.ShapeDtypeStruct((8, nb, C_GRP), F32),
        compiler_params=_cparams(1),
        name="rwkv_sample_tok",
    )(pm_s, pl_s, prev_main, prev_lora, mu_main, mu_lora, vecs, w_up, a_up, g_up)
    heads = lambda x: x.reshape(x.shape[0], nb, N_HEADS, HEAD_DIM).transpose(1, 0, 2, 3)
    rows = heads(jnp.concatenate([tok[0:2], tok[3:6]], axis=0))
    lnx = jnp.broadcast_to(vecs[5:7, None, :], (2, nb, C_GRP))
    cols = jnp.swapaxes(heads(jnp.concatenate([tok[2:3], tok[6:8], lnx], axis=0)), -1, -2)
    st = (1, N_HEADS, HEAD_DIM, HEAD_DIM)
    s_new, o = pl.pallas_call(
        _rwkv_sample_step_kernel,
        grid=(nb,),
        in_specs=[pl.BlockSpec(st, lambda b: (b, 0, 0, 0)),
                  pl.BlockSpec((1, 5, N_HEADS, HEAD_DIM), lambda b: (b, 0, 0, 0)),
                  pl.BlockSpec((1, 5, HEAD_DIM, N_HEADS), lambda b: (b, 0, 0, 0))],
        out_specs=[pl.BlockSpec(st, lambda b: (b, 0, 0, 0)),
                   pl.BlockSpec((1, HEAD_DIM, N_HEADS), lambda b: (b, 0, 0))],
        out_shape=[jax.ShapeDtypeStruct((nb,) + st[1:], F32),
                   jax.ShapeDtypeStruct((nb, HEAD_DIM, N_HEADS), F32)],
        compiler_params=_cparams(1),
        name="rwkv_sample_step",
    )(wkv0, rows, cols)
    return jnp.swapaxes(o, -1, -2).reshape(nb, C_GRP), s_new


def _out_proj_kernel(x_ref, oa_ref, orw_ref, wa_ref, wb_ref, o_ref):
    o_ref[...] = (x_ref[...] + _bdot(oa_ref[...], wa_ref[...]) + _bdot(orw_ref[...], wb_ref[...]))


def _out_proj(x2d, o_att, o_rw, w_o, tm, tn):
    m = x2d.shape[0]
    return pl.pallas_call(
        _out_proj_kernel,
        grid=(m // tm, D_MODEL // tn),
        in_specs=[
            pl.BlockSpec((tm, tn), lambda i, j: (i, j)),
            pl.BlockSpec((tm, C_GRP), lambda i, j: (i, 0)),
            pl.BlockSpec((tm, C_GRP), lambda i, j: (i, 0)),
            pl.BlockSpec((C_GRP, tn), lambda i, j: (0, j)),
            pl.BlockSpec((C_GRP, tn), lambda i, j: (1, j)),
        ],
        out_specs=pl.BlockSpec((tm, tn), lambda i, j: (i, j)),
        out_shape=jax.ShapeDtypeStruct((m, D_MODEL), F32),
        compiler_params=_cparams(2),
        name="out_proj",
    )(x2d, o_att, o_rw, w_o, w_o)


HALO = 16


def _ffn_kernel(*refs, tm, tiles_per_seq, seq_mode):
    if seq_mode:
        (x_ref, xh_ref, g_ref, wg_ref, wv_ref, cwg_ref, cwv_ref, wd_ref, gf_ref,
         y_ref, ug_ref, uv_ref, h_scr, acc_scr) = refs
    else:
        (x_ref, pg_ref, pv_ref, g_ref, wg_ref, wv_ref, cwg_ref, cwv_ref, wd_ref, gf_ref,
         y_ref, ug_ref, uv_ref, h_scr, acc_scr) = refs
    i, j = pl.program_id(0), pl.program_id(1)

    def norm(x):
        ms = jnp.mean(x * x, axis=-1, keepdims=True)
        return x * lax.rsqrt(ms + NORM_EPS) * g_ref[...]

    @pl.when(j == 0)
    def _():
        x = x_ref[...]
        acc_scr[...] = x
        if seq_mode:
            keep = jnp.where(i % tiles_per_seq == 0, 0.0, 1.0).astype(F32)
            h_scr[:HALO, :] = (norm(xh_ref[...]) * keep).astype(BF16)
            h_scr[HALO:, :] = norm(x).astype(BF16)
        else:
            h_scr[...] = norm(x).astype(BF16)

    h = h_scr[...]
    halves = []
    for w_ref, cw_ref, u_ref, p_ref in ((wg_ref, cwg_ref, ug_ref, None if seq_mode else pg_ref),
                                        (wv_ref, cwv_ref, uv_ref, None if seq_mode else pv_ref)):
        u = jnp.dot(h, w_ref[...], preferred_element_type=F32)
        cw = cw_ref[...]
        if seq_mode:
            u_ref[0] = u[HALO + tm - 2:HALO + tm, :]
            c = (cw[3:4] + cw[0:1] * u[HALO - 2:HALO - 2 + tm] + cw[1:2] * u[HALO - 1:HALO - 1 + tm]
                 + cw[2:3] * u[HALO:HALO + tm])
        else:
            u_ref[...] = u
            c = cw[3:4] + cw[0:1] * p_ref[0] + cw[1:2] * p_ref[1] + cw[2:3] * u
        halves.append(c)
    gate, val = halves
    act = gate * _sigmoid(gate) * val
    acc_scr[...] += jnp.dot(act.astype(BF16), wd_ref[...], preferred_element_type=F32)

    @pl.when(j == pl.num_programs(1) - 1)
    def _():
        x2 = acc_scr[...]
        ms = jnp.mean(x2 * x2, axis=-1, keepdims=True)
        y_ref[...] = x2 * lax.rsqrt(ms + NORM_EPS) * gf_ref[...]


def _ffn(x1, norm_g, w_up, conv_wb, w_down, final_g, tm, tf, seq_len=None, prev=None):
    m = x1.shape[0]
    nf = D_FF // tf
    seq_mode = prev is None
    tiles_per_seq = seq_len // tm if seq_mode else 1
    vec = lambda width: pl.BlockSpec((1, width), lambda i, j: (0, 0))
    in_specs = [pl.BlockSpec((tm, D_MODEL), lambda i, j: (i, 0))]
    args = [x1]
    if seq_mode:
        in_specs.append(pl.BlockSpec((HALO, D_MODEL), lambda i, j: (jnp.maximum(i * (tm // HALO) - 1, 0), 0)))
        args.append(x1)
    else:
        in_specs += [pl.BlockSpec((2, tm, tf), lambda i, j: (0, i, j)),
                     pl.BlockSpec((2, tm, tf), lambda i, j: (0, i, nf + j))]
        args += [prev, prev]
    in_specs += [
        vec(D_MODEL),
        pl.BlockSpec((D_MODEL, tf), lambda i, j: (0, j)),
        pl.BlockSpec((D_MODEL, tf), lambda i, j: (0, nf + j)),
        pl.BlockSpec((4, tf), lambda i, j: (0, j)),
        pl.BlockSpec((4, tf), lambda i, j: (0, nf + j)),
        pl.BlockSpec((tf, D_MODEL), lambda i, j: (j, 0)),
        vec(D_MODEL),
    ]
    args += [norm_g, w_up, w_up, conv_wb, conv_wb, w_down, final_g]
    if seq_mode:
        u_spec = pl.BlockSpec((1, 2, tf), lambda i, j: (i, 0, j))
        u_shape = jax.ShapeDtypeStruct((m // tm, 2, D_FF), F32)
    else:
        u_spec = pl.BlockSpec((tm, tf), lambda i, j: (i, j))
        u_shape = jax.ShapeDtypeStruct((m, D_FF), F32)
    rows = tm + HALO if seq_mode else tm
    return pl.pallas_call(
        functools.partial(_ffn_kernel, tm=tm, tiles_per_seq=tiles_per_seq, seq_mode=seq_mode),
        grid=(m // tm, nf),
        in_specs=in_specs,
        out_specs=[pl.BlockSpec((tm, D_MODEL), lambda i, j: (i, 0)), u_spec, u_spec],
        out_shape=[jax.ShapeDtypeStruct((m, D_MODEL), F32), u_shape, u_shape],
        scratch_shapes=[pltpu.VMEM((rows, D_MODEL), BF16), pltpu.VMEM((tm, D_MODEL), F32)],
        compiler_params=_cparams(2),
        name="ffn_seq" if seq_mode else "ffn_state",
    )(*args)


def _layer_params(l, norm_mix_g, w_in, att_out_g, rw_mu, rw_w0, rw_w_up, rw_a0, rw_a_up, rw_g_up, rw_k_k,
                  rw_k_a, rw_r_k, rw_lnx_w, rw_lnx_b, w_o, norm_ffn_g, ffn_w_up, ffn_conv_w, ffn_conv_b,
                  ffn_w_down):
    zeros = jnp.zeros((W_LORA, C_GRP), F32)
    return dict(
        norm_mix_g=norm_mix_g[l][None],
        w_main=w_in[l].astype(BF16),
        w_lora=w_in[l][:, C_MAIN:].astype(BF16),
        att_g=att_out_g[l][None],
        mu_main=rw_mu[l][:3 * C_GRP].reshape(3, C_GRP),
        mu_lora=rw_mu[l][None, 3 * C_GRP:],
        vecs=jnp.stack([rw_w0[l], rw_a0[l], rw_k_k[l], rw_k_a[l], rw_r_k[l], rw_lnx_w[l], rw_lnx_b[l]]),
        w_up=jnp.concatenate([rw_w_up[l], zeros]).astype(BF16),
        a_up=jnp.concatenate([zeros, rw_a_up[l]]).astype(BF16),
        g_up=rw_g_up[l].astype(BF16),
        w_o=w_o[l].astype(BF16),
        norm_ffn_g=norm_ffn_g[l][None],
        ffn_w_up=ffn_w_up[l].astype(BF16),
        conv_wb=jnp.concatenate([ffn_conv_w[l], ffn_conv_b[l][None]]),
        ffn_w_down=ffn_w_down[l].astype(BF16),
    )


def _prompt_layer(x, lp, final_g):
    nb, seq, _ = x.shape
    x2d = x.reshape(nb * seq, D_MODEL)
    p_main, p_lora = _in_proj(x2d, lp["norm_mix_g"], lp["w_main"], lp["w_lora"], tm=1024, tn=1024)
    p_main3 = p_main.reshape(nb, seq, C_MAIN)
    p_lora3 = p_lora.reshape(nb, seq, C_LORA)
    o_att, kt_new, vt_new = _attn_prompt(p_main3, lp["att_g"])
    o_rw, h_fin = _rwkv_prompt(p_main3, p_lora3, jnp.zeros((nb, 3, C_GRP), F32), jnp.zeros((nb, 1, C_LORA), F32),
                               lp["mu_main"], lp["mu_lora"], lp["vecs"], lp["w_up"], lp["a_up"], lp["g_up"])
    x1 = _out_proj(x2d, o_att.reshape(nb * seq, C_GRP), o_rw.reshape(nb * seq, C_GRP), lp["w_o"], tm=512, tn=D_MODEL)
    y, u_g, u_v = _ffn(x1, lp["norm_ffn_g"], lp["ffn_w_up"], lp["conv_wb"], lp["ffn_w_down"], final_g,
                       tm=512, tf=512, seq_len=seq)
    hd = (N_HEADS, HEAD_DIM)
    k_new = kt_new.reshape((nb,) + hd + (seq,)).transpose(0, 3, 1, 2)
    v_new = vt_new.reshape((nb,) + hd + (seq,)).transpose(0, 3, 1, 2)
    rw_last = jnp.concatenate([p_main3[:, -1:, 3 * C_GRP:], p_lora3[:, -1:, :]], axis=-1)
    wkv = jnp.stack([h_fin[:, :, :HEAD_DIM, :HEAD_DIM], h_fin[:, :, HEAD_DIM:, HEAD_DIM:]], axis=2)
    wkv = wkv.reshape(nb, N_HEADS, HEAD_DIM, HEAD_DIM)
    tiles_per_seq = u_g.shape[0] // nb
    ffn_last = jnp.concatenate([u_g, u_v], axis=-1)[tiles_per_seq - 1::tiles_per_seq]
    return y.reshape(nb, seq, D_MODEL), k_new, v_new, rw_last, wkv, ffn_last


def _sample_layer(x, cache_k, cache_v, rw_prev, wkv0, ffn_prev, lp, final_g):
    nb = x.shape[0]
    x2d = x.reshape(nb, D_MODEL)
    p_main, p_lora = _in_proj(x2d, lp["norm_mix_g"], lp["w_main"], lp["w_lora"], tm=nb, tn=1024)
    hd = (N_HEADS, HEAD_DIM)
    q = p_main[:, :C_GRP].reshape((nb,) + hd)
    k_new = p_main[:, C_GRP:2 * C_GRP].reshape((nb,) + hd)
    v_new = p_main[:, 2 * C_GRP:3 * C_GRP].reshape((nb,) + hd)
    cache_kt, cache_vt = cache_k.transpose(0, 2, 3, 1), cache_v.transpose(0, 2, 3, 1)
    head_minor = lambda t: jnp.swapaxes(t, -1, -2)
    o_att = head_minor(_attn_sample(head_minor(q), head_minor(k_new), head_minor(v_new), cache_kt, cache_vt,
                                    head_minor(lp["att_g"].reshape(hd)))).reshape(nb, C_GRP)
    prev = rw_prev.reshape(nb, C_SHIFT)
    o_rw, wkv = _rwkv_sample(p_main, p_lora, prev[:, :3 * C_GRP], prev[:, 3 * C_GRP:],
                             lp["mu_main"].reshape(1, 3 * C_GRP), lp["mu_lora"], lp["vecs"],
                             lp["w_up"], lp["a_up"], lp["g_up"], wkv0)
    x1 = _out_proj(x2d, o_att, o_rw, lp["w_o"], tm=nb, tn=1024)
    prev_rows = ffn_prev.transpose(1, 0, 2)
    y, u_g, u_v = _ffn(x1, lp["norm_ffn_g"], lp["ffn_w_up"], lp["conv_wb"], lp["ffn_w_down"], final_g,
                       tm=nb, tf=512, prev=prev_rows)
    rw_last = jnp.concatenate([p_main[:, 3 * C_GRP:], p_lora], axis=-1)[:, None, :]
    ffn_last = jnp.stack([ffn_prev[:, 1, :], jnp.concatenate([u_g, u_v], axis=-1)], axis=1)
    return y.reshape(nb, 1, D_MODEL), k_new[:, None], v_new[:, None], rw_last, wkv, ffn_last


def kernel(x_prompt, x_sample, cache_att_k, cache_att_v, state_rwkv_shift, state_rwkv_wkv, state_ffn_conv, norm_mix_g, w_in, att_out_g, rw_mu, rw_w0, rw_w_up, rw_a0, rw_a_up, rw_g_up, rw_k_k, rw_k_a, rw_r_k, rw_lnx_w, rw_lnx_b, w_o, norm_ffn_g, ffn_w_up, ffn_conv_w, ffn_conv_b, ffn_w_down, norm_final_g):
    depth = w_in.shape[0]
    assert depth == 1, "the fused FFN + final-norm kernel assumes a single trunk layer"
    assert x_sample.shape[1] == 1, "the sample path handles one new token per sequence"
    final_g = norm_final_g[None]
    lp = _layer_params(0, norm_mix_g, w_in, att_out_g, rw_mu, rw_w0, rw_w_up, rw_a0, rw_a_up, rw_g_up, rw_k_k,
                       rw_k_a, rw_r_k, rw_lnx_w, rw_lnx_b, w_o, norm_ffn_g, ffn_w_up, ffn_conv_w, ffn_conv_b,
                       ffn_w_down)
    yp, pk, pv, prw, pwkv, pffn = _prompt_layer(x_prompt, lp, final_g)
    ys, sk, sv, srw, swkv, sffn = _sample_layer(x_sample, cache_att_k[0], cache_att_v[0], state_rwkv_shift[0],
                                                state_rwkv_wkv[0], state_ffn_conv[0], lp, final_g)
    lead = lambda t: t[None]
    return (yp, ys, lead(pk), lead(pv), lead(prw), lead(pwkv), lead(pffn),
            lead(sk), lead(sv), lead(srw), lead(swkv), lead(sffn))
```

```python
import functools

import jax
import jax.numpy as jnp
from jax import lax
from jax.experimental import pallas as pl
from jax.experimental.pallas import tpu as pltpu

F32 = jnp.float32
BF16 = jnp.bfloat16

D_MODEL = 2048
HEAD_DIM = 64
N_HEADS = 16
C_GRP = N_HEADS * HEAD_DIM
W_LORA, A_LORA, G_LORA = 64, 64, 160
C_LORA = W_LORA + A_LORA + G_LORA
C_SHIFT = 3 * C_GRP + C_LORA
C_MAIN = 6 * C_GRP
D_FF = 5632
DIL_RATES = (1, 4, 16)
N_BACK = 128
ATT_SCALE = HEAD_DIM ** -0.5
NORM_EPS = 1e-6
LNX_EPS = HEAD_DIM * 1e-5
NEG_BIG = -1e30

LANES = 128
VMEM_LIMIT = 48 * 1024 * 1024

CHUNK = 64
PHASE_A_UNROLL = 8
ATTN_ITEMS_PER_ITER = 4
PAIR = 2 * HEAD_DIM


def _cparams(n_grid):
    return pltpu.CompilerParams(dimension_semantics=("arbitrary",) * n_grid,
                                vmem_limit_bytes=VMEM_LIMIT)


def _bdot(a, b):
    return jnp.dot(a.astype(BF16), b.astype(BF16), preferred_element_type=F32)


def _bdot_nt(a, b):
    return lax.dot_general(a.astype(BF16), b.astype(BF16), (((1,), (1,)), ((), ())),
                           preferred_element_type=F32)


def _bdot_tn(a, b):
    return jnp.dot(a.astype(F32).T.astype(BF16), b.astype(BF16), preferred_element_type=F32)


def _split_dot(mat01, x, terms):
    acc = None
    rem = x
    for _ in range(terms):
        piece = rem.astype(BF16)
        part = jnp.dot(mat01, piece, preferred_element_type=F32)
        acc = part if acc is None else acc + part
        rem = rem - piece.astype(F32)
    return acc


def _split_dot_r(x, mat01, terms):
    acc = None
    rem = x
    for _ in range(terms):
        piece = rem.astype(BF16)
        part = jnp.dot(piece, mat01, preferred_element_type=F32)
        acc = part if acc is None else acc + part
        rem = rem - piece.astype(F32)
    return acc


def _head_ones(n):
    r = lax.broadcasted_iota(jnp.int32, (n, n), 0) // HEAD_DIM
    c = lax.broadcasted_iota(jnp.int32, (n, n), 1) // HEAD_DIM
    return jnp.where(r == c, 1.0, 0.0).astype(BF16)


def _sigmoid(x):
    return 1.0 / (1.0 + jnp.exp(-x))


def _softplus(x):
    return jnp.maximum(x, 0.0) + jnp.log(1.0 + jnp.exp(-jnp.abs(x)))


def _in_proj_kernel(x_ref, g_ref, wm_ref, wl_ref, om_ref, ol_ref, h_scr):
    @pl.when(pl.program_id(1) == 0)
    def _():
        x = x_ref[...]
        ms = jnp.mean(x * x, axis=-1, keepdims=True)
        h = (x * lax.rsqrt(ms + NORM_EPS) * g_ref[...]).astype(BF16)
        h_scr[...] = h
        ol_ref[...] = jnp.dot(h, wl_ref[...], preferred_element_type=F32)

    om_ref[...] = jnp.dot(h_scr[...], wm_ref[...], preferred_element_type=F32)


def _in_proj(x2d, g, w_main, w_lora, tm, tn):
    m = x2d.shape[0]
    return pl.pallas_call(
        _in_proj_kernel,
        grid=(m // tm, C_MAIN // tn),
        in_specs=[
            pl.BlockSpec((tm, D_MODEL), lambda i, j: (i, 0)),
            pl.BlockSpec((1, D_MODEL), lambda i, j: (0, 0)),
            pl.BlockSpec((D_MODEL, tn), lambda i, j: (0, j)),
            pl.BlockSpec((D_MODEL, C_LORA), lambda i, j: (0, 0)),
        ],
        out_specs=[
            pl.BlockSpec((tm, tn), lambda i, j: (i, j)),
            pl.BlockSpec((tm, C_LORA), lambda i, j: (i, 0)),
        ],
        out_shape=[jax.ShapeDtypeStruct((m, C_MAIN), F32), jax.ShapeDtypeStruct((m, C_LORA), F32)],
        scratch_shapes=[pltpu.VMEM((tm, D_MODEL), BF16)],
        compiler_params=_cparams(2),
        name="in_proj",
    )(x2d, g, w_main, w_lora)


def _attn_prompt_kernel(q_ref, k_ref, v_ref, g_ref, o_ref, kt_out_ref, vt_out_ref, ob_scr, lb_scr,
                        qa_scr, qb_scr, kt_scr, va_scr, vb_scr, *, seq):
    blk = N_BACK
    nitem = seq // blk
    lane = lax.broadcasted_iota(jnp.int32, (blk, PAIR), 1)
    is_a = lane < HEAD_DIM

    row2 = lax.broadcasted_iota(jnp.int32, (blk, 2 * blk), 0)
    col2 = lax.broadcasted_iota(jnp.int32, (blk, 2 * blk), 1)
    band_bias = jnp.where((col2 >= row2) & (col2 <= row2 + N_BACK), 0.0, NEG_BIG)
    causal_bias = band_bias[:, blk:]

    for br, rate in enumerate(DIL_RATES):
        nblk = seq // rate // blk

        def item_rows(j, rate=rate, nblk=nblk):
            return pl.ds((j // nblk) + (j % nblk) * (blk * rate), blk, stride=rate)

        def prep(j, carry, item_rows=item_rows, rate=rate):
            rows = item_rows(j)
            qs = q_ref[rows, :] * ATT_SCALE
            v = v_ref[rows, :]
            kt = k_ref[rows, :].T
            qa_scr[j] = jnp.where(is_a, qs, 0.0).astype(BF16)
            qb_scr[j] = jnp.where(is_a, 0.0, qs).astype(BF16)
            kt_scr[j] = kt.astype(BF16)
            va_scr[j] = jnp.where(is_a, v, 0.0).astype(BF16)
            vb_scr[j] = jnp.where(is_a, 0.0, v).astype(BF16)
            if rate == 1:
                kt_out_ref[:, j * blk:(j + 1) * blk] = kt
                vt_out_ref[:, j * blk:(j + 1) * blk] = v.T
            return carry
        if rate == 1:
            for j in range(nitem):
                prep(j, 0)
        else:
            lax.fori_loop(0, nitem, prep, 0, unroll=2)

        def attend(i, carry, item_rows=item_rows, nblk=nblk, br=br):
            js = [i * ATTN_ITEMS_PER_ITER + u for u in range(ATTN_ITEMS_PER_ITER)]
            nk = blk if nblk == 1 else 2 * blk
            ones = jnp.ones((nk, PAIR), BF16)
            kts, vms, biases = [], [], []
            for j in js:
                if nblk == 1:
                    biases.append(causal_bias)
                    kts.append(kt_scr[j])
                    vs = (va_scr[j], vb_scr[j])
                else:
                    jp = jnp.maximum(j - 1, 0)
                    lo = jnp.where(j % nblk == 0, blk, 0)
                    biases.append(jnp.where(col2 >= lo, band_bias, NEG_BIG))
                    kts.append(jnp.concatenate([kt_scr[jp], kt_scr[j]], axis=1))
                    vs = (jnp.concatenate([va_scr[jp], va_scr[j]], axis=0),
                          jnp.concatenate([vb_scr[jp], vb_scr[j]], axis=0))
                vms.append([jnp.concatenate([v, ones], axis=1) for v in vs])
            qs = [(qa_scr[j], qb_scr[j]) for j in js]
            s = [[jnp.dot(q, kt, preferred_element_type=F32) + bias for q in qp]
                 for qp, kt, bias in zip(qs, kts, biases)]
            m = [[jnp.max(x, axis=-1, keepdims=True) for x in xs] for xs in s]
            e = [[jnp.exp(x - mx).astype(BF16) for x, mx in zip(xs, ms)] for xs, ms in zip(s, m)]
            of = [[jnp.dot(x, vm, preferred_element_type=F32) for x, vm in zip(xs, vp)] for xs, vp in zip(e, vms)]
            for j, (ofa, ofb), (ma, mb) in zip(js, of, m):
                den = jnp.where(is_a, ofa[:, PAIR:], ofb[:, PAIR:])
                rows = item_rows(j)
                ob_scr[br, rows, :] = (ofa[:, :PAIR] + ofb[:, :PAIR]) * (1.0 / den)
                lb_scr[br, rows, :] = jnp.where(is_a, ma, mb) + jnp.log(den)
            return carry
        lax.fori_loop(0, nitem // ATTN_ITEMS_PER_ITER, attend, 0)

    ones_bd = _head_ones(PAIR)
    gain = g_ref[...]
    tile = 256

    def merge(i, carry):
        rows = pl.ds(pl.multiple_of(i * tile, tile), tile)
        l0, l1, l2 = lb_scr[0, rows, :], lb_scr[1, rows, :], lb_scr[2, rows, :]
        m = jnp.maximum(jnp.maximum(l0, l1), l2)
        w0, w1, w2 = jnp.exp(l0 - m), jnp.exp(l1 - m), jnp.exp(l2 - m)
        o = (w0 * ob_scr[0, rows, :] + w1 * ob_scr[1, rows, :] + w2 * ob_scr[2, rows, :]) / (w0 + w1 + w2)
        ms = _split_dot_r(o * o, ones_bd, 2) * (1.0 / HEAD_DIM)
        o_ref[rows, :] = (o * lax.rsqrt(ms + NORM_EPS) * gain).astype(o_ref.dtype)
        return carry
    lax.fori_loop(0, seq // tile, merge, 0)


def _attn_prompt(p_main3, att_g):
    nb, seq, _ = p_main3.shape
    npair = C_GRP // PAIR
    col = lambda off: pl.BlockSpec((None, seq, PAIR), lambda b, p, off=off: (b, 0, off + p))
    return pl.pallas_call(
        functools.partial(_attn_prompt_kernel, seq=seq),
        grid=(nb, npair),
        in_specs=[col(0), col(npair), col(2 * npair), pl.BlockSpec((1, PAIR), lambda b, p: (0, p))],
        out_specs=[pl.BlockSpec((None, seq, PAIR), lambda b, p: (b, 0, p)),
                   pl.BlockSpec((None, PAIR, seq), lambda b, p: (b, p, 0)),
                   pl.BlockSpec((None, PAIR, seq), lambda b, p: (b, p, 0))],
        out_shape=[jax.ShapeDtypeStruct((nb, seq, C_GRP), BF16),
                   jax.ShapeDtypeStruct((nb, C_GRP, seq), F32), jax.ShapeDtypeStruct((nb, C_GRP, seq), F32)],
        scratch_shapes=[pltpu.VMEM((3, seq, PAIR), F32), pltpu.VMEM((3, seq, PAIR), F32)]
        + [pltpu.VMEM((seq // N_BACK, N_BACK, PAIR), BF16)] * 5,
        compiler_params=_cparams(2),
        name="attn_prompt",
    )(p_main3, p_main3, p_main3, att_g)


SAMPLE_HEADS_PER_STEP = 8


def _attn_sample_kernel(q_ref, kn_ref, vn_ref, kt_ref, vt_ref, g_ref, o_ref, *, win):
    base = pl.program_id(1) * SAMPLE_HEADS_PER_STEP
    head_lane = lax.broadcasted_iota(jnp.int32, (1, N_HEADS), 1)
    pick = lambda x, hl: jnp.sum(jnp.where(head_lane == hl, x, 0.0), axis=-1, keepdims=True)
    q_all, kn_all, vn_all = q_ref[0], kn_ref[0], vn_ref[0]
    hs = range(SAMPLE_HEADS_PER_STEP)
    qc = [pick(q_all, base + h) * ATT_SCALE for h in hs]
    kn = [pick(kn_all, base + h) for h in hs]
    vn = [pick(vn_all, base + h) for h in hs]
    s_all = [jnp.sum(kt_ref[0, h] * qc[h], axis=0, keepdims=True) for h in hs]
    s_new = [jnp.sum(kn[h] * qc[h], axis=0, keepdims=True) for h in hs]
    outs, lses = [], []
    for rate in DIL_RATES:
        lo = win - N_BACK * rate
        s = [x[:, lo:] for x in s_all]
        if rate > 1:
            pos = lax.broadcasted_iota(jnp.int32, s[0].shape, 1)
            s = [jnp.where((pos & (rate - 1)) == 0, x, NEG_BIG) for x in s]
        m = [jnp.maximum(jnp.max(s[h], axis=-1, keepdims=True), s_new[h]) for h in hs]
        e = [jnp.exp(s[h] - m[h]) for h in hs]
        e_new = [jnp.exp(s_new[h] - m[h]) for h in hs]
        den = [jnp.sum(e[h], axis=-1, keepdims=True) + e_new[h] for h in hs]
        o = [(jnp.sum(vt_ref[0, h, :, lo:] * e[h], axis=-1, keepdims=True) + e_new[h] * vn[h]) / den[h] for h in hs]
        o_b, l_b = jnp.zeros((HEAD_DIM, N_HEADS), F32), jnp.zeros((1, N_HEADS), F32)
        for h in hs:
            o_b = o_b + jnp.where(head_lane == base + h, o[h], 0.0)
            l_b = l_b + jnp.where(head_lane == base + h, m[h] + jnp.log(den[h]), 0.0)
        outs.append(o_b)
        lses.append(l_b)
    m = jnp.maximum(jnp.maximum(lses[0], lses[1]), lses[2])
    ws = [jnp.exp(l - m) for l in lses]
    o = (ws[0] * outs[0] + ws[1] * outs[1] + ws[2] * outs[2]) / (ws[0] + ws[1] + ws[2])
    ms = jnp.mean(o * o, axis=0, keepdims=True)
    mine = (head_lane >= base) & (head_lane < base + SAMPLE_HEADS_PER_STEP)
    out = jnp.where(mine, o * lax.rsqrt(ms + NORM_EPS) * g_ref[...], 0.0)

    @pl.when(pl.program_id(1) == 0)
    def _():
        o_ref[0] = out

    @pl.when(pl.program_id(1) != 0)
    def _():
        o_ref[0] += out


def _attn_sample(q, kn, vn, cache_kt, cache_vt, att_g):
    nb, win = cache_kt.shape[0], cache_kt.shape[-1]
    assert win % (N_BACK * max(DIL_RATES)) == 0 and win % LANES == 0
    hs = SAMPLE_HEADS_PER_STEP
    tok = pl.BlockSpec((1, HEAD_DIM, N_HEADS), lambda b, g: (b, 0, 0))
    cache = pl.BlockSpec((1, hs, HEAD_DIM, win), lambda b, g: (b, g, 0, 0))
    return pl.pallas_call(
        functools.partial(_attn_sample_kernel, win=win),
        grid=(nb, N_HEADS // hs),
        in_specs=[tok, tok, tok, cache, cache, pl.BlockSpec((HEAD_DIM, N_HEADS), lambda b, g: (0, 0))],
        out_specs=tok,
        out_shape=jax.ShapeDtypeStruct((nb, HEAD_DIM, N_HEADS), F32),
        compiler_params=_cparams(2),
        name="attn_sample",
    )(q, kn, vn, cache_kt, cache_vt, att_g)


def _rwkv_token_math(xr, xk, xv, xl, prm, ones_bd):
    w_raw = prm["w0"] + _bdot(jnp.tanh(xl[:, :2 * W_LORA]), prm["w_up"])
    logw = -jnp.exp(-_softplus(-w_raw) - 0.5)
    a_sig = _sigmoid(prm["a0"] + _bdot(xl[:, :2 * W_LORA], prm["a_up"]))
    gate = _bdot(_sigmoid(xl[:, 2 * W_LORA:]), prm["g_up"])
    kk = xk * prm["k_k"]
    nrm = jnp.sqrt(_split_dot_r(kk * kk, ones_bd, 2))
    kk = kk / jnp.maximum(nrm, 1e-12)
    k_eff = xk * (1.0 + (a_sig - 1.0) * prm["k_a"])
    bonus = _split_dot_r(xr * k_eff * prm["r_k"], ones_bd, 2) * xv
    return dict(r=xr, k=k_eff, v=xv, a=-kk, b=kk * a_sig, logw=logw, gate=gate, bonus=bonus)


def _group_norm_gate(y, bonus, gate, lnx_w, lnx_b, ones_bd):
    mean = _split_dot_r(y, ones_bd, 2) * (1.0 / HEAD_DIM)
    d = y - mean
    var = _split_dot_r(d * d, ones_bd, 2) * (1.0 / HEAD_DIM)
    yn = d * lax.rsqrt(var + LNX_EPS) * lnx_w + lnx_b
    return (yn + bonus) * gate


_RW_VEC_NAMES = ("w0", "a0", "k_k", "k_a", "r_k", "lnx_w", "lnx_b")


def _rwkv_prompt_kernel(r_ref, k_ref, v_ref, l_ref, pm_ref, pl_ref, mu_m_ref, mu_l_ref, vec_ref,
                        wup_ref, aup_ref, gup_ref, o_ref, h_ref,
                        tok_scr, y_scr, rw_scr, y0_scr, g_scr, ha_scr, pc_scr, *, seq):
    n = PAIR
    ones_bd = _head_ones(n)
    row = lax.broadcasted_iota(jnp.int32, (seq, 1), 0)

    def shift(x, prev):
        return jnp.where(row == 0, prev, pltpu.roll(x, 1, axis=0))

    def lerp(x, prev, mu):
        return x + mu * (shift(x, prev) - x)

    pm = pm_ref[...]
    mu_m = mu_m_ref[...]
    prm = {name: vec_ref[i:i + 1, :] for i, name in enumerate(_RW_VEC_NAMES)}
    prm.update(w_up=wup_ref[...], a_up=aup_ref[...], g_up=gup_ref[...])
    tok = _rwkv_token_math(lerp(r_ref[...], pm[0:1], mu_m[0:1]),
                           lerp(k_ref[...], pm[1:2], mu_m[1:2]),
                           lerp(v_ref[...], pm[2:3], mu_m[2:3]),
                           lerp(l_ref[...], pl_ref[...], mu_l_ref[...]), prm, ones_bd)
    for i, name in enumerate(("r", "k", "v", "a", "b", "logw", "gate", "bonus")):
        tok_scr[i] = tok[name]

    nch = seq // CHUNK
    lane = lax.broadcasted_iota(jnp.int32, (CHUNK, n), 1)
    is_a = lane < HEAD_DIM
    ri = lax.broadcasted_iota(jnp.int32, (n, n), 0)
    ci = lax.broadcasted_iota(jnp.int32, (n, n), 1)
    strict = ri > ci
    incl = ri >= ci
    eye = ri == ci
    rc = lax.broadcasted_iota(jnp.int32, (CHUNK, CHUNK), 0)
    cc = lax.broadcasted_iota(jnp.int32, (CHUNK, CHUNK), 1)

    def stack(x):
        return jnp.concatenate([jnp.where(is_a, x, 0.0), jnp.where(is_a, 0.0, x)], axis=0)

    def phase_b_step(c):
        rows = pl.ds(pl.multiple_of(c * CHUNK, CHUNK), CHUNK)
        ht = h_ref[...]
        hb = ht.astype(BF16)
        ys = _bdot_nt(rw_scr[c], hb) + y0_scr[c]
        y_scr[rows, :] = ys[:CHUNK, :] + ys[CHUNK:, :]
        h_ref[...] = ht * pc_scr[c] + _bdot(hb, g_scr[c]) + ha_scr[c]

    def group(ga, gb):
        b_todo = [] if gb is None else [gb * PHASE_A_UNROLL + u for u in range(PHASE_A_UNROLL)]

        def fill():
            if b_todo:
                phase_b_step(b_todo.pop(0))

        if ga is None:
            while b_todo:
                fill()
            return
        cs = [ga * PHASE_A_UNROLL + u for u in range(PHASE_A_UNROLL)]
        each = lambda fn, *lists: [fn(*xs) for xs in zip(*lists)]
        tril_ones = jnp.where(rc >= cc, 1.0, 0.0).astype(BF16)
        tok = [[tok_scr[i, pl.ds(pl.multiple_of(c * CHUNK, CHUNK), CHUNK), :] for i in range(6)] for c in cs]
        r_c, k_c, v_c, a_c, b_c, lw_c = [list(x) for x in zip(*tok)]
        lcum = each(lambda lw: _split_dot(tril_ones, lw, 3), lw_c)
        lend = each(lambda l: l[CHUNK - 1:CHUNK, :], lcum)
        fill()
        sb = lambda x: stack(x).astype(BF16)
        a_s = each(lambda a, l, lw: sb(a * jnp.exp(l - lw)), a_c, lcum, lw_c)
        r_s = each(lambda r, l: sb(r * jnp.exp(l)), r_c, lcum)
        b_s = each(lambda b, l: sb(b * jnp.exp(-l)), b_c, lcum)
        k_s = each(lambda k, l: sb(k * jnp.exp(-l)), k_c, lcum)
        v_s = each(sb, v_c)
        be_s = each(lambda b, l, le: sb(b * jnp.exp(le - l)), b_c, lcum, lend)
        ke_s = each(lambda k, l, le: sb(k * jnp.exp(le - l)), k_c, lcum, lend)

        sc = each(lambda a, r, b, k: _bdot_nt(jnp.concatenate([a, r], axis=0), jnp.concatenate([b, k], axis=0)),
                  a_s, r_s, b_s, k_s)
        fill()
        s_ab = each(lambda s: jnp.where(strict, s[:n, :n], 0.0), sc)
        s_ak = each(lambda s: jnp.where(strict, s[:n, n:], 0.0).astype(BF16), sc)
        s_rb = each(lambda s: jnp.where(incl, s[n:, :n], 0.0).astype(BF16), sc)
        s_rk = each(lambda s: jnp.where(incl, s[n:, n:], 0.0).astype(BF16), sc)

        apow = each(lambda s: s.astype(BF16), s_ab)
        tinv = each(lambda s: jnp.where(eye, 1.0, 0.0) + s, s_ab)
        for _ in range(CHUNK.bit_length() - 2):
            apow = each(lambda p: _bdot(p, p).astype(BF16), apow)
            tinv = each(lambda t, p: t + _bdot(t, p), tinv, apow)
            fill()

        x1 = each(_bdot, s_ak, v_s)
        wu = each(lambda t, a, x: _bdot(t, jnp.concatenate([a, x.astype(BF16)], axis=1)), tinv, a_s, x1)
        fill()
        yk = each(_bdot, s_rk, v_s)
        ry = each(lambda s, w: _bdot(s, w), s_rb, wu)
        gh = each(lambda w, be: _bdot(w.T, be), wu, be_s)
        hk = each(lambda v, ke: _bdot_tn(v, ke), v_s, ke_s)
        while b_todo:
            fill()
        for u, c in enumerate(cs):
            rw_scr[c] = ry[u][:, :n] + r_s[u].astype(F32)
            y0_scr[c] = ry[u][:, n:] + yk[u]
            g_scr[c] = gh[u][:n, :]
            ha_scr[c] = gh[u][n:, :] + hk[u]
            pc_scr[c] = jnp.exp(lend[u])

    h_ref[...] = jnp.zeros((n, n), F32)
    ngroup = nch // PHASE_A_UNROLL
    group(0, None)

    def pipelined(g, carry):
        group(g, g - 1)
        return carry
    lax.fori_loop(1, ngroup, pipelined, 0)
    group(None, ngroup - 1)

    o_ref[...] = _group_norm_gate(y_scr[...], tok_scr[7], tok_scr[6], prm["lnx_w"], prm["lnx_b"],
                                  _head_ones(n)).astype(o_ref.dtype)


def _rwkv_prompt(p_main3, p_lora3, prev_main, prev_lora, mu_main, mu_lora, vecs, w_up, a_up, g_up):
    nb, seq, _ = p_main3.shape
    npair = C_GRP // PAIR
    col = lambda off: pl.BlockSpec((None, seq, PAIR), lambda b, p, off=off: (b, 0, off + p))
    pcol = lambda rows: pl.BlockSpec((rows, PAIR), lambda b, p: (0, p))
    return pl.pallas_call(
        functools.partial(_rwkv_prompt_kernel, seq=seq),
        grid=(nb, npair),
        in_specs=[
            col(3 * npair), col(4 * npair), col(5 * npair),
            pl.BlockSpec((None, seq, C_LORA), lambda b, p: (b, 0, 0)),
            pl.BlockSpec((None, 3, PAIR), lambda b, p: (b, 0, p)),
            pl.BlockSpec((None, 1, C_LORA), lambda b, p: (b, 0, 0)),
            pcol(3), pl.BlockSpec((1, C_LORA), lambda b, p: (0, 0)), pcol(len(_RW_VEC_NAMES)),
            pcol(2 * W_LORA), pcol(2 * A_LORA), pcol(G_LORA),
        ],
        out_specs=[
            pl.BlockSpec((None, seq, PAIR), lambda b, p: (b, 0, p)),
            pl.BlockSpec((None, None, PAIR, PAIR), lambda b, p: (b, p, 0, 0)),
        ],
        out_shape=[jax.ShapeDtypeStruct((nb, seq, C_GRP), BF16),
                   jax.ShapeDtypeStruct((nb, npair, PAIR, PAIR), F32)],
        scratch_shapes=[
            pltpu.VMEM((8, seq, PAIR), F32), pltpu.VMEM((seq, PAIR), F32),
            pltpu.VMEM((seq // CHUNK, PAIR, PAIR), F32), pltpu.VMEM((seq // CHUNK, PAIR, PAIR), F32),
            pltpu.VMEM((seq // CHUNK, PAIR, PAIR), F32), pltpu.VMEM((seq // CHUNK, PAIR, PAIR), F32),
            pltpu.VMEM((seq // CHUNK, 1, PAIR), F32),
        ],
        compiler_params=_cparams(2),
        name="rwkv_prompt",
    )(p_main3, p_main3, p_main3, p_lora3, prev_main, prev_lora, mu_main, mu_lora, vecs, w_up, a_up, g_up)


def _rwkv_sample_tok_kernel(pm_ref, l_ref, pvm_ref, pvl_ref, mu_m_ref, mu_l_ref, vec_ref,
                            wup_ref, aup_ref, gup_ref, out_ref):
    ones_bd = _head_ones(C_GRP)
    prm = {name: vec_ref[i:i + 1, :] for i, name in enumerate(_RW_VEC_NAMES)}
    prm.update(w_up=wup_ref[...], a_up=aup_ref[...], g_up=gup_ref[...])
    lerp = lambda x, prev, mu: x + mu * (prev - x)
    xs = [lerp(pm_ref[:, i * C_GRP:(i + 1) * C_GRP], pvm_ref[:, i * C_GRP:(i + 1) * C_GRP],
               mu_m_ref[:, i * C_GRP:(i + 1) * C_GRP]) for i in range(3)]
    tok = _rwkv_token_math(xs[0], xs[1], xs[2], lerp(l_ref[...], pvl_ref[...], mu_l_ref[...]), prm, ones_bd)
    for i, name in enumerate(("r", "k", "v", "a", "b", "logw", "gate", "bonus")):
        out_ref[i] = tok[name]


def _rwkv_sample_step_kernel(s_ref, row_ref, col_ref, s_out_ref, o_ref):
    head_lane = lax.broadcasted_iota(jnp.int32, (1, N_HEADS), 1)
    heads = range(N_HEADS)
    row = lambda i: [row_ref[0, i, h:h + 1, :] for h in heads]
    r, k, a, b, logw = [row(i) for i in range(5)]
    v = [col_ref[0, 0, :, h:h + 1] for h in heads]
    s = [s_ref[0, h] for h in heads]
    sa = [jnp.sum(s[h] * a[h], axis=-1, keepdims=True) for h in heads]
    s_new = [s[h] * jnp.exp(logw[h]) + sa[h] * b[h] + v[h] * k[h] for h in heads]
    for h in heads:
        s_out_ref[0, h] = s_new[h]
    y = jnp.zeros((HEAD_DIM, N_HEADS), F32)
    for h in heads:
        y = y + jnp.where(head_lane == h, jnp.sum(s_new[h] * r[h], axis=-1, keepdims=True), 0.0)
    gate, bonus, lnx_w, lnx_b = [col_ref[0, i] for i in range(1, 5)]
    mean = jnp.mean(y, axis=0, keepdims=True)
    d = y - mean
    var = jnp.mean(d * d, axis=0, keepdims=True)
    o_ref[0] = (d * lax.rsqrt(var + LNX_EPS) * lnx_w + lnx_b + bonus) * gate


def _rwkv_sample(pm_s, pl_s, prev_main, prev_lora, mu_main, mu_lora, vecs, w_up, a_up, g_up, wkv0):
    nb = pm_s.shape[0]
    full = lambda shape: pl.BlockSpec(shape, lambda i: (0,) * len(shape))
    tok = pl.pallas_call(
        _rwkv_sample_tok_kernel,
        grid=(1,),
        in_specs=[pl.BlockSpec((nb, 3 * C_GRP), lambda i: (0, 1)), full((nb, C_LORA)),
                  full((nb, 3 * C_GRP)), full((nb, C_LORA)), full((1, 3 * C_GRP)), full((1, C_LORA)),
                  full(vecs.shape), full(w_up.shape), full(a_up.shape), full(g_up.shape)],
        out_specs=full((8, nb, C_GRP)),
        out_shape=jax.ShapeDtypeStruct((8, nb, C_GRP), F32),
        compiler_params=_cparams(1),
        name="rwkv_sample_tok",
    )(pm_s, pl_s, prev_main, prev_lora, mu_main, mu_lora, vecs, w_up, a_up, g_up)
    heads = lambda x: x.reshape(x.shape[0], nb, N_HEADS, HEAD_DIM).transpose(1, 0, 2, 3)
    rows = heads(jnp.concatenate([tok[0:2], tok[3:6]], axis=0))
    lnx = jnp.broadcast_to(vecs[5:7, None, :], (2, nb, C_GRP))
    cols = jnp.swapaxes(heads(jnp.concatenate([tok[2:3], tok[6:8], lnx], axis=0)), -1, -2)
    st = (1, N_HEADS, HEAD_DIM, HEAD_DIM)
    s_new, o = pl.pallas_call(
        _rwkv_sample_step_kernel,
        grid=(nb,),
        in_specs=[pl.BlockSpec(st, lambda b: (b, 0, 0, 0)),
                  pl.BlockSpec((1, 5, N_HEADS, HEAD_DIM), lambda b: (b, 0, 0, 0)),
                  pl.BlockSpec((1, 5, HEAD_DIM, N_HEADS), lambda b: (b, 0, 0, 0))],
        out_specs=[pl.BlockSpec(st, lambda b: (b, 0, 0, 0)),
                   pl.BlockSpec((1, HEAD_DIM, N_HEADS), lambda b: (b, 0, 0))],
        out_shape=[jax.ShapeDtypeStruct((nb,) + st[1:], F32),
                   jax.ShapeDtypeStruct((nb, HEAD_DIM, N_HEADS), F32)],
        compiler_params=_cparams(1),
        name="rwkv_sample_step",
    )(wkv0, rows, cols)
    return jnp.swapaxes(o, -1, -2).reshape(nb, C_GRP), s_new


def _out_proj_kernel(x_ref, oa_ref, orw_ref, wa_ref, wb_ref, o_ref):
    o_ref[...] = (x_ref[...] + _bdot(oa_ref[...], wa_ref[...]) + _bdot(orw_ref[...], wb_ref[...]))


def _out_proj(x2d, o_att, o_rw, w_o, tm, tn):
    m = x2d.shape[0]
    return pl.pallas_call(
        _out_proj_kernel,
        grid=(m // tm, D_MODEL // tn),
        in_specs=[
            pl.BlockSpec((tm, tn), lambda i, j: (i, j)),
            pl.BlockSpec((tm, C_GRP), lambda i, j: (i, 0)),
            pl.BlockSpec((tm, C_GRP), lambda i, j: (i, 0)),
            pl.BlockSpec((C_GRP, tn), lambda i, j: (0, j)),
            pl.BlockSpec((C_GRP, tn), lambda i, j: (1, j)),
        ],
        out_specs=pl.BlockSpec((tm, tn), lambda i, j: (i, j)),
        out_shape=jax.ShapeDtypeStruct((m, D_MODEL), F32),
        compiler_params=_cparams(2),
        name="out_proj",
    )(x2d, o_att, o_rw, w_o, w_o)


HALO = 16


def _ffn_kernel(*refs, tm, tiles_per_seq, seq_mode):
    if seq_mode:
        (x_ref, xh_ref, g_ref, wg_ref, wv_ref, cwg_ref, cwv_ref, wd_ref, gf_ref,
         y_ref, ug_ref, uv_ref, h_scr, acc_scr) = refs
    else:
        (x_ref, pg_ref, pv_ref, g_ref, wg_ref, wv_ref, cwg_ref, cwv_ref, wd_ref, gf_ref,
         y_ref, ug_ref, uv_ref, wgq_ref, wvq_ref, wdq_ref, h_scr, acc_scr) = refs
    i, j = pl.program_id(0), pl.program_id(1)

    def weight(w_ref, wq_ref):
        if seq_mode:
            return w_ref[...]
        w = w_ref[...].astype(BF16)
        wq_ref[...] = w
        return w

    def norm(x):
        ms = jnp.mean(x * x, axis=-1, keepdims=True)
        return x * lax.rsqrt(ms + NORM_EPS) * g_ref[...]

    @pl.when(j == 0)
    def _():
        x = x_ref[...]
        acc_scr[...] = x
        if seq_mode:
            keep = jnp.where(i % tiles_per_seq == 0, 0.0, 1.0).astype(F32)
            h_scr[:HALO, :] = (norm(xh_ref[...]) * keep).astype(BF16)
            h_scr[HALO:, :] = norm(x).astype(BF16)
        else:
            h_scr[...] = norm(x).astype(BF16)

    h = h_scr[...]
    halves = []
    for w_ref, wq_ref, cw_ref, u_ref, p_ref in (
            (wg_ref, None if seq_mode else wgq_ref, cwg_ref, ug_ref, None if seq_mode else pg_ref),
            (wv_ref, None if seq_mode else wvq_ref, cwv_ref, uv_ref, None if seq_mode else pv_ref)):
        u = jnp.dot(h, weight(w_ref, wq_ref), preferred_element_type=F32)
        cw = cw_ref[...]
        if seq_mode:
            u_ref[0] = u[HALO + tm - 2:HALO + tm, :]
            c = (cw[3:4] + cw[0:1] * u[HALO - 2:HALO - 2 + tm] + cw[1:2] * u[HALO - 1:HALO - 1 + tm]
                 + cw[2:3] * u[HALO:HALO + tm])
        else:
            u_ref[...] = u
            c = cw[3:4] + cw[0:1] * p_ref[0] + cw[1:2] * p_ref[1] + cw[2:3] * u
        halves.append(c)
    gate, val = halves
    act = gate * _sigmoid(gate) * val
    acc_scr[...] += jnp.dot(act.astype(BF16), weight(wd_ref, None if seq_mode else wdq_ref),
                            preferred_element_type=F32)

    @pl.when(j == pl.num_programs(1) - 1)
    def _():
        x2 = acc_scr[...]
        ms = jnp.mean(x2 * x2, axis=-1, keepdims=True)
        y_ref[...] = x2 * lax.rsqrt(ms + NORM_EPS) * gf_ref[...]


def _ffn(x1, norm_g, w_gate, w_val, conv_wb, w_down, final_g, tm, tf, seq_len=None, prev=None):
    m = x1.shape[0]
    nf = D_FF // tf
    seq_mode = prev is None
    assert seq_mode or m == tm, "state mode emits each bf16 weight tile once: it needs a single row tile"
    val_off = 0 if seq_mode else nf
    tiles_per_seq = seq_len // tm if seq_mode else 1
    vec = lambda width: pl.BlockSpec((1, width), lambda i, j: (0, 0))
    in_specs = [pl.BlockSpec((tm, D_MODEL), lambda i, j: (i, 0))]
    args = [x1]
    if seq_mode:
        in_specs.append(pl.BlockSpec((HALO, D_MODEL), lambda i, j: (jnp.maximum(i * (tm // HALO) - 1, 0), 0)))
        args.append(x1)
    else:
        in_specs += [pl.BlockSpec((2, tm, tf), lambda i, j: (0, i, j)),
                     pl.BlockSpec((2, tm, tf), lambda i, j: (0, i, nf + j))]
        args += [prev, prev]
    in_specs += [
        vec(D_MODEL),
        pl.BlockSpec((D_MODEL, tf), lambda i, j: (0, j)),
        pl.BlockSpec((D_MODEL, tf), lambda i, j: (0, val_off + j)),
        pl.BlockSpec((4, tf), lambda i, j: (0, j)),
        pl.BlockSpec((4, tf), lambda i, j: (0, nf + j)),
        pl.BlockSpec((tf, D_MODEL), lambda i, j: (j, 0)),
        vec(D_MODEL),
    ]
    args += [norm_g, w_gate, w_val, conv_wb, conv_wb, w_down, final_g]
    out_specs = [pl.BlockSpec((tm, D_MODEL), lambda i, j: (i, 0))]
    out_shape = [jax.ShapeDtypeStruct((m, D_MODEL), F32)]
    if seq_mode:
        out_specs += [pl.BlockSpec((1, 2, tf), lambda i, j: (i, 0, j))] * 2
        out_shape += [jax.ShapeDtypeStruct((m // tm, 2, D_FF), F32)] * 2
    else:
        out_specs += [pl.BlockSpec((tm, tf), lambda i, j: (i, j))] * 2
        out_shape += [jax.ShapeDtypeStruct((m, D_FF), F32)] * 2
        out_specs += [pl.BlockSpec((D_MODEL, tf), lambda i, j: (0, j))] * 2 + [pl.BlockSpec((tf, D_MODEL), lambda i, j: (j, 0))]
        out_shape += [jax.ShapeDtypeStruct((D_MODEL, D_FF), BF16)] * 2 + [jax.ShapeDtypeStruct((D_FF, D_MODEL), BF16)]
    rows = tm + HALO if seq_mode else tm
    return pl.pallas_call(
        functools.partial(_ffn_kernel, tm=tm, tiles_per_seq=tiles_per_seq, seq_mode=seq_mode),
        grid=(m // tm, nf),
        in_specs=in_specs,
        out_specs=out_specs,
        out_shape=out_shape,
        scratch_shapes=[pltpu.VMEM((rows, D_MODEL), BF16), pltpu.VMEM((tm, D_MODEL), F32)],
        compiler_params=_cparams(2),
        name="ffn_seq" if seq_mode else "ffn_state",
    )(*args)


def _layer_params(l, norm_mix_g, w_in, att_out_g, rw_mu, rw_w0, rw_w_up, rw_a0, rw_a_up, rw_g_up, rw_k_k,
                  rw_k_a, rw_r_k, rw_lnx_w, rw_lnx_b, w_o, norm_ffn_g, ffn_w_up, ffn_conv_w, ffn_conv_b,
                  ffn_w_down):
    zeros = jnp.zeros((W_LORA, C_GRP), F32)
    return dict(
        norm_mix_g=norm_mix_g[l][None],
        w_main=w_in[l].astype(BF16),
        w_lora=w_in[l][:, C_MAIN:].astype(BF16),
        att_g=att_out_g[l][None],
        mu_main=rw_mu[l][:3 * C_GRP].reshape(3, C_GRP),
        mu_lora=rw_mu[l][None, 3 * C_GRP:],
        vecs=jnp.stack([rw_w0[l], rw_a0[l], rw_k_k[l], rw_k_a[l], rw_r_k[l], rw_lnx_w[l], rw_lnx_b[l]]),
        w_up=jnp.concatenate([rw_w_up[l], zeros]).astype(BF16),
        a_up=jnp.concatenate([zeros, rw_a_up[l]]).astype(BF16),
        g_up=rw_g_up[l].astype(BF16),
        w_o=w_o[l].astype(BF16),
        norm_ffn_g=norm_ffn_g[l][None],
        ffn_w_up=ffn_w_up[l],
        conv_wb=jnp.concatenate([ffn_conv_w[l], ffn_conv_b[l][None]]),
        ffn_w_down=ffn_w_down[l],
    )


def _prompt_layer(x, lp, ffn_w_bf16, final_g):
    nb, seq, _ = x.shape
    x2d = x.reshape(nb * seq, D_MODEL)
    p_main, p_lora = _in_proj(x2d, lp["norm_mix_g"], lp["w_main"], lp["w_lora"], tm=1024, tn=1024)
    p_main3 = p_main.reshape(nb, seq, C_MAIN)
    p_lora3 = p_lora.reshape(nb, seq, C_LORA)
    o_att, kt_new, vt_new = _attn_prompt(p_main3, lp["att_g"])
    o_rw, h_fin = _rwkv_prompt(p_main3, p_lora3, jnp.zeros((nb, 3, C_GRP), F32), jnp.zeros((nb, 1, C_LORA), F32),
                               lp["mu_main"], lp["mu_lora"], lp["vecs"], lp["w_up"], lp["a_up"], lp["g_up"])
    x1 = _out_proj(x2d, o_att.reshape(nb * seq, C_GRP), o_rw.reshape(nb * seq, C_GRP), lp["w_o"], tm=512, tn=D_MODEL)
    w_gate, w_val, w_down = ffn_w_bf16
    y, u_g, u_v = _ffn(x1, lp["norm_ffn_g"], w_gate, w_val, lp["conv_wb"], w_down, final_g,
                       tm=512, tf=512, seq_len=seq)
    hd = (N_HEADS, HEAD_DIM)
    k_new = kt_new.reshape((nb,) + hd + (seq,)).transpose(0, 3, 1, 2)
    v_new = vt_new.reshape((nb,) + hd + (seq,)).transpose(0, 3, 1, 2)
    rw_last = jnp.concatenate([p_main3[:, -1:, 3 * C_GRP:], p_lora3[:, -1:, :]], axis=-1)
    wkv = jnp.stack([h_fin[:, :, :HEAD_DIM, :HEAD_DIM], h_fin[:, :, HEAD_DIM:, HEAD_DIM:]], axis=2)
    wkv = wkv.reshape(nb, N_HEADS, HEAD_DIM, HEAD_DIM)
    tiles_per_seq = u_g.shape[0] // nb
    ffn_last = jnp.concatenate([u_g, u_v], axis=-1)[tiles_per_seq - 1::tiles_per_seq]
    return y.reshape(nb, seq, D_MODEL), k_new, v_new, rw_last, wkv, ffn_last


def _sample_layer(x, cache_k, cache_v, rw_prev, wkv0, ffn_prev, lp, final_g):
    nb = x.shape[0]
    x2d = x.reshape(nb, D_MODEL)
    p_main, p_lora = _in_proj(x2d, lp["norm_mix_g"], lp["w_main"], lp["w_lora"], tm=nb, tn=1024)
    hd = (N_HEADS, HEAD_DIM)
    q = p_main[:, :C_GRP].reshape((nb,) + hd)
    k_new = p_main[:, C_GRP:2 * C_GRP].reshape((nb,) + hd)
    v_new = p_main[:, 2 * C_GRP:3 * C_GRP].reshape((nb,) + hd)
    cache_kt, cache_vt = cache_k.transpose(0, 2, 3, 1), cache_v.transpose(0, 2, 3, 1)
    head_minor = lambda t: jnp.swapaxes(t, -1, -2)
    o_att = head_minor(_attn_sample(head_minor(q), head_minor(k_new), head_minor(v_new), cache_kt, cache_vt,
                                    head_minor(lp["att_g"].reshape(hd)))).reshape(nb, C_GRP)
    prev = rw_prev.reshape(nb, C_SHIFT)
    o_rw, wkv = _rwkv_sample(p_main, p_lora, prev[:, :3 * C_GRP], prev[:, 3 * C_GRP:],
                             lp["mu_main"].reshape(1, 3 * C_GRP), lp["mu_lora"], lp["vecs"],
                             lp["w_up"], lp["a_up"], lp["g_up"], wkv0)
    x1 = _out_proj(x2d, o_att, o_rw, lp["w_o"], tm=nb, tn=1024)
    prev_rows = ffn_prev.transpose(1, 0, 2)
    y, u_g, u_v, *ffn_w_bf16 = _ffn(x1, lp["norm_ffn_g"], lp["ffn_w_up"], lp["ffn_w_up"], lp["conv_wb"],
                                    lp["ffn_w_down"], final_g, tm=nb, tf=512, prev=prev_rows)
    rw_last = jnp.concatenate([p_main[:, 3 * C_GRP:], p_lora], axis=-1)[:, None, :]
    ffn_last = jnp.stack([ffn_prev[:, 1, :], jnp.concatenate([u_g, u_v], axis=-1)], axis=1)
    return (y.reshape(nb, 1, D_MODEL), k_new[:, None], v_new[:, None], rw_last, wkv, ffn_last), ffn_w_bf16


def kernel(x_prompt, x_sample, cache_att_k, cache_att_v, state_rwkv_shift, state_rwkv_wkv, state_ffn_conv, norm_mix_g, w_in, att_out_g, rw_mu, rw_w0, rw_w_up, rw_a0, rw_a_up, rw_g_up, rw_k_k, rw_k_a, rw_r_k, rw_lnx_w, rw_lnx_b, w_o, norm_ffn_g, ffn_w_up, ffn_conv_w, ffn_conv_b, ffn_w_down, norm_final_g):
    depth = w_in.shape[0]
    assert depth == 1, "the fused FFN + final-norm kernel assumes a single trunk layer"
    assert x_sample.shape[1] == 1, "the sample path handles one new token per sequence"
    final_g = norm_final_g[None]
    lp = _layer_params(0, norm_mix_g, w_in, att_out_g, rw_mu, rw_w0, rw_w_up, rw_a0, rw_a_up, rw_g_up, rw_k_k,
                       rw_k_a, rw_r_k, rw_lnx_w, rw_lnx_b, w_o, norm_ffn_g, ffn_w_up, ffn_conv_w, ffn_conv_b,
                       ffn_w_down)
    (ys, sk, sv, srw, swkv, sffn), ffn_w_bf16 = _sample_layer(
        x_sample, cache_att_k[0], cache_att_v[0], state_rwkv_shift[0], state_rwkv_wkv[0], state_ffn_conv[0],
        lp, final_g)
    yp, pk, pv, prw, pwkv, pffn = _prompt_layer(x_prompt, lp, ffn_w_bf16, final_g)
    lead = lambda t: t[None]
    return (yp, ys, lead(pk), lead(pv), lead(prw), lead(pwkv), lead(pffn),
            lead(sk), lead(sv), lead(srw), lead(swkv), lead(sffn))
```

```python
import functools

import jax
import jax.numpy as jnp
from jax import lax
from jax.experimental import pallas as pl
from jax.experimental.pallas import tpu as pltpu

F32 = jnp.float32
BF16 = jnp.bfloat16

D_MODEL = 2048
HEAD_DIM = 64
N_HEADS = 16
C_GRP = N_HEADS * HEAD_DIM
W_LORA, A_LORA, G_LORA = 64, 64, 160
C_LORA = W_LORA + A_LORA + G_LORA
C_SHIFT = 3 * C_GRP + C_LORA
C_MAIN = 6 * C_GRP
D_FF = 5632
DIL_RATES = (1, 4, 16)
N_BACK = 128
ATT_SCALE = HEAD_DIM ** -0.5
NORM_EPS = 1e-6
LNX_EPS = HEAD_DIM * 1e-5
NEG_BIG = -1e30

LANES = 128
VMEM_LIMIT = 48 * 1024 * 1024

CHUNK = 64
PHASE_A_UNROLL = 8
ATTN_ITEMS_PER_ITER = 4
PAIR = 2 * HEAD_DIM


def _cparams(n_grid):
    return pltpu.CompilerParams(dimension_semantics=("arbitrary",) * n_grid,
                                vmem_limit_bytes=VMEM_LIMIT)


def _bdot(a, b):
    return jnp.dot(a.astype(BF16), b.astype(BF16), preferred_element_type=F32)


def _bdot_nt(a, b):
    return lax.dot_general(a.astype(BF16), b.astype(BF16), (((1,), (1,)), ((), ())),
                           preferred_element_type=F32)


def _bdot_tn(a, b):
    return jnp.dot(a.astype(F32).T.astype(BF16), b.astype(BF16), preferred_element_type=F32)


def _split_dot(mat01, x, terms):
    acc = None
    rem = x
    for _ in range(terms):
        piece = rem.astype(BF16)
        part = jnp.dot(mat01, piece, preferred_element_type=F32)
        acc = part if acc is None else acc + part
        rem = rem - piece.astype(F32)
    return acc


def _split_dot_r(x, mat01, terms):
    return _dot_pieces(_split_pieces(x, terms), mat01)


def _head_ones(n):
    r = lax.broadcasted_iota(jnp.int32, (n, n), 0) // HEAD_DIM
    c = lax.broadcasted_iota(jnp.int32, (n, n), 1) // HEAD_DIM
    return jnp.where(r == c, 1.0, 0.0).astype(BF16)


def _sigmoid(x):
    return 1.0 / (1.0 + jnp.exp(-x))


def _softplus(x):
    return jnp.maximum(x, 0.0) + jnp.log(1.0 + jnp.exp(-jnp.abs(x)))


def _in_proj_kernel(x_ref, g_ref, wm_ref, wl_ref, om_ref, ol_ref, h_scr):
    @pl.when(pl.program_id(1) == 0)
    def _():
        x = x_ref[...]
        ms = jnp.mean(x * x, axis=-1, keepdims=True)
        h = (x * lax.rsqrt(ms + NORM_EPS) * g_ref[...]).astype(BF16)
        h_scr[...] = h
        ol_ref[...] = jnp.dot(h, wl_ref[...], preferred_element_type=F32)

    om_ref[...] = jnp.dot(h_scr[...], wm_ref[...], preferred_element_type=F32)


def _in_proj(x2d, g, w_main, w_lora, tm, tn):
    m = x2d.shape[0]
    return pl.pallas_call(
        _in_proj_kernel,
        grid=(m // tm, C_MAIN // tn),
        in_specs=[
            pl.BlockSpec((tm, D_MODEL), lambda i, j: (i, 0)),
            pl.BlockSpec((1, D_MODEL), lambda i, j: (0, 0)),
            pl.BlockSpec((D_MODEL, tn), lambda i, j: (0, j)),
            pl.BlockSpec((D_MODEL, C_LORA), lambda i, j: (0, 0)),
        ],
        out_specs=[
            pl.BlockSpec((tm, tn), lambda i, j: (i, j)),
            pl.BlockSpec((tm, C_LORA), lambda i, j: (i, 0)),
        ],
        out_shape=[jax.ShapeDtypeStruct((m, C_MAIN), F32), jax.ShapeDtypeStruct((m, C_LORA), F32)],
        scratch_shapes=[pltpu.VMEM((tm, D_MODEL), BF16)],
        compiler_params=_cparams(2),
        name="in_proj",
    )(x2d, g, w_main, w_lora)


def _attn_prompt_kernel(q_ref, k_ref, v_ref, g_ref, o_ref, kt_out_ref, vt_out_ref, ob_scr, lb_scr,
                        *item_scr, seq):
    blk = N_BACK
    nitem = seq // blk
    lane = lax.broadcasted_iota(jnp.int32, (blk, PAIR), 1)
    is_a = lane < HEAD_DIM

    row2 = lax.broadcasted_iota(jnp.int32, (blk, 2 * blk), 0)
    col2 = lax.broadcasted_iota(jnp.int32, (blk, 2 * blk), 1)
    band_bias = jnp.where((col2 >= row2) & (col2 <= row2 + N_BACK), 0.0, NEG_BIG)
    causal_bias = band_bias[:, blk:]

    def item_rows(j, rate):
        nblk = seq // rate // blk
        return pl.ds((j // nblk) + (j % nblk) * (blk * rate), blk, stride=rate)

    def prep(j, rate, dst):
        qa_scr, qb_scr, kt_scr, va_scr, vb_scr = dst
        rows = item_rows(j, rate)
        qs = q_ref[rows, :] * ATT_SCALE
        v = v_ref[rows, :]
        kt = k_ref[rows, :].T
        qa_scr[j] = jnp.where(is_a, qs, 0.0).astype(BF16)
        qb_scr[j] = jnp.where(is_a, 0.0, qs).astype(BF16)
        kt_scr[j] = kt.astype(BF16)
        va_scr[j] = jnp.where(is_a, v, 0.0).astype(BF16)
        vb_scr[j] = jnp.where(is_a, 0.0, v).astype(BF16)
        if rate == 1:
            kt_out_ref[:, j * blk:(j + 1) * blk] = kt
            vt_out_ref[:, j * blk:(j + 1) * blk] = v.T

    def attend(i, br, src, next_rate, dst):
        qa_scr, qb_scr, kt_scr, va_scr, vb_scr = src
        rate = DIL_RATES[br]
        nblk = seq // rate // blk
        js = [i * ATTN_ITEMS_PER_ITER + u for u in range(ATTN_ITEMS_PER_ITER)]
        nk = blk if nblk == 1 else 2 * blk
        ones = jnp.ones((nk, PAIR), BF16)
        kts, vms, biases = [], [], []
        for j in js:
            if nblk == 1:
                biases.append(causal_bias)
                kts.append(kt_scr[j])
                vs = (va_scr[j], vb_scr[j])
            else:
                jp = jnp.maximum(j - 1, 0)
                lo = jnp.where(j % nblk == 0, blk, 0)
                biases.append(jnp.where(col2 >= lo, band_bias, NEG_BIG))
                kts.append(jnp.concatenate([kt_scr[jp], kt_scr[j]], axis=1))
                vs = (jnp.concatenate([va_scr[jp], va_scr[j]], axis=0),
                      jnp.concatenate([vb_scr[jp], vb_scr[j]], axis=0))
            vms.append([jnp.concatenate([v, ones], axis=1) for v in vs])
        qs = [(qa_scr[j], qb_scr[j]) for j in js]
        s = [[jnp.dot(q, kt, preferred_element_type=F32) + bias for q in qp]
             for qp, kt, bias in zip(qs, kts, biases)]
        if next_rate is not None:
            for j in js:
                prep(j, next_rate, dst)
        m = [[jnp.max(x, axis=-1, keepdims=True) for x in xs] for xs in s]
        e = [[jnp.exp(x - mx).astype(BF16) for x, mx in zip(xs, ms)] for xs, ms in zip(s, m)]
        of = [[jnp.dot(x, vm, preferred_element_type=F32) for x, vm in zip(xs, vp)] for xs, vp in zip(e, vms)]
        for j, (ofa, ofb), (ma, mb) in zip(js, of, m):
            den = jnp.where(is_a, ofa[:, PAIR:], ofb[:, PAIR:])
            rows = item_rows(j, rate)
            ob_scr[br, rows, :] = (ofa[:, :PAIR] + ofb[:, :PAIR]) * (1.0 / den)
            lb_scr[br, rows, :] = jnp.where(is_a, ma, mb) + jnp.log(den)

    sets = (item_scr[:5], item_scr[5:])
    for j in range(nitem):
        prep(j, DIL_RATES[0], sets[0])
    for br in range(len(DIL_RATES)):
        next_rate = DIL_RATES[br + 1] if br + 1 < len(DIL_RATES) else None

        def body(i, carry, br=br, next_rate=next_rate):
            attend(i, br, sets[br % 2], next_rate, sets[(br + 1) % 2])
            return carry
        lax.fori_loop(0, nitem // ATTN_ITEMS_PER_ITER, body, 0)

    ones_bd = _head_ones(PAIR)
    gain = g_ref[...]
    tile = 256

    def merge(i, carry):
        rows = pl.ds(pl.multiple_of(i * tile, tile), tile)
        l0, l1, l2 = lb_scr[0, rows, :], lb_scr[1, rows, :], lb_scr[2, rows, :]
        m = jnp.maximum(jnp.maximum(l0, l1), l2)
        w0, w1, w2 = jnp.exp(l0 - m), jnp.exp(l1 - m), jnp.exp(l2 - m)
        o = (w0 * ob_scr[0, rows, :] + w1 * ob_scr[1, rows, :] + w2 * ob_scr[2, rows, :]) / (w0 + w1 + w2)
        ms = _split_dot_r(o * o, ones_bd, 2) * (1.0 / HEAD_DIM)
        o_ref[rows, :] = (o * lax.rsqrt(ms + NORM_EPS) * gain).astype(o_ref.dtype)
        return carry
    lax.fori_loop(0, seq // tile, merge, 0)


def _attn_prompt(p_main3, att_g):
    nb, seq, _ = p_main3.shape
    npair = C_GRP // PAIR
    col = lambda off: pl.BlockSpec((None, seq, PAIR), lambda b, p, off=off: (b, 0, off + p))
    return pl.pallas_call(
        functools.partial(_attn_prompt_kernel, seq=seq),
        grid=(nb, npair),
        in_specs=[col(0), col(npair), col(2 * npair), pl.BlockSpec((1, PAIR), lambda b, p: (0, p))],
        out_specs=[pl.BlockSpec((None, seq, PAIR), lambda b, p: (b, 0, p)),
                   pl.BlockSpec((None, PAIR, seq), lambda b, p: (b, p, 0)),
                   pl.BlockSpec((None, PAIR, seq), lambda b, p: (b, p, 0))],
        out_shape=[jax.ShapeDtypeStruct((nb, seq, C_GRP), BF16),
                   jax.ShapeDtypeStruct((nb, C_GRP, seq), F32), jax.ShapeDtypeStruct((nb, C_GRP, seq), F32)],
        scratch_shapes=[pltpu.VMEM((3, seq, PAIR), F32), pltpu.VMEM((3, seq, PAIR), F32)]
        + [pltpu.VMEM((seq // N_BACK, N_BACK, PAIR), BF16)] * 10,
        compiler_params=_cparams(2),
        name="attn_prompt",
    )(p_main3, p_main3, p_main3, att_g)


SAMPLE_HEADS_PER_STEP = 8


def _attn_sample_kernel(q_ref, kn_ref, vn_ref, kt_ref, vt_ref, g_ref, o_ref, *, win):
    base = pl.program_id(1) * SAMPLE_HEADS_PER_STEP
    head_lane = lax.broadcasted_iota(jnp.int32, (1, N_HEADS), 1)
    pick = lambda x, hl: jnp.sum(jnp.where(head_lane == hl, x, 0.0), axis=-1, keepdims=True)
    q_all, kn_all, vn_all = q_ref[0], kn_ref[0], vn_ref[0]
    hs = range(SAMPLE_HEADS_PER_STEP)
    qc = [pick(q_all, base + h) * ATT_SCALE for h in hs]
    kn = [pick(kn_all, base + h) for h in hs]
    vn = [pick(vn_all, base + h) for h in hs]
    s_all = [jnp.sum(kt_ref[0, h] * qc[h], axis=0, keepdims=True) for h in hs]
    s_new = [jnp.sum(kn[h] * qc[h], axis=0, keepdims=True) for h in hs]
    outs, lses = [], []
    for rate in DIL_RATES:
        lo = win - N_BACK * rate
        s = [x[:, lo:] for x in s_all]
        if rate > 1:
            pos = lax.broadcasted_iota(jnp.int32, s[0].shape, 1)
            s = [jnp.where((pos & (rate - 1)) == 0, x, NEG_BIG) for x in s]
        m = [jnp.maximum(jnp.max(s[h], axis=-1, keepdims=True), s_new[h]) for h in hs]
        e = [jnp.exp(s[h] - m[h]) for h in hs]
        e_new = [jnp.exp(s_new[h] - m[h]) for h in hs]
        den = [jnp.sum(e[h], axis=-1, keepdims=True) + e_new[h] for h in hs]
        o = [(jnp.sum(vt_ref[0, h, :, lo:] * e[h], axis=-1, keepdims=True) + e_new[h] * vn[h]) / den[h] for h in hs]
        o_b, l_b = jnp.zeros((HEAD_DIM, N_HEADS), F32), jnp.zeros((1, N_HEADS), F32)
        for h in hs:
            o_b = o_b + jnp.where(head_lane == base + h, o[h], 0.0)
            l_b = l_b + jnp.where(head_lane == base + h, m[h] + jnp.log(den[h]), 0.0)
        outs.append(o_b)
        lses.append(l_b)
    m = jnp.maximum(jnp.maximum(lses[0], lses[1]), lses[2])
    ws = [jnp.exp(l - m) for l in lses]
    o = (ws[0] * outs[0] + ws[1] * outs[1] + ws[2] * outs[2]) / (ws[0] + ws[1] + ws[2])
    ms = jnp.mean(o * o, axis=0, keepdims=True)
    mine = (head_lane >= base) & (head_lane < base + SAMPLE_HEADS_PER_STEP)
    out = jnp.where(mine, o * lax.rsqrt(ms + NORM_EPS) * g_ref[...], 0.0)

    @pl.when(pl.program_id(1) == 0)
    def _():
        o_ref[0] = out

    @pl.when(pl.program_id(1) != 0)
    def _():
        o_ref[0] += out


def _attn_sample(q, kn, vn, cache_kt, cache_vt, att_g):
    nb, win = cache_kt.shape[0], cache_kt.shape[-1]
    assert win % (N_BACK * max(DIL_RATES)) == 0 and win % LANES == 0
    hs = SAMPLE_HEADS_PER_STEP
    tok = pl.BlockSpec((1, HEAD_DIM, N_HEADS), lambda b, g: (b, 0, 0))
    cache = pl.BlockSpec((1, hs, HEAD_DIM, win), lambda b, g: (b, g, 0, 0))
    return pl.pallas_call(
        functools.partial(_attn_sample_kernel, win=win),
        grid=(nb, N_HEADS // hs),
        in_specs=[tok, tok, tok, cache, cache, pl.BlockSpec((HEAD_DIM, N_HEADS), lambda b, g: (0, 0))],
        out_specs=tok,
        out_shape=jax.ShapeDtypeStruct((nb, HEAD_DIM, N_HEADS), F32),
        compiler_params=_cparams(2),
        name="attn_sample",
    )(q, kn, vn, cache_kt, cache_vt, att_g)


def _split_pieces(x, terms):
    pieces, rem = [], x
    for _ in range(terms):
        piece = rem.astype(BF16)
        pieces.append(piece)
        rem = rem - piece.astype(F32)
    return pieces


def _dot_pieces(pieces, mat01):
    return jnp.dot(jnp.concatenate(pieces, axis=1), jnp.concatenate([mat01] * len(pieces), axis=0),
                   preferred_element_type=F32)


def _rwkv_token_stages(inputs, prm, ones_bd):
    xr, xk, xv, xl = inputs()
    act_w, act_a = jnp.tanh(xl[:, :2 * W_LORA]).astype(BF16), xl[:, :2 * W_LORA].astype(BF16)
    act_g = _sigmoid(xl[:, 2 * W_LORA:]).astype(BF16)
    kk = xk * prm["k_k"]
    kk_sq = _split_pieces(kk * kk, 2)
    yield None
    w_raw = prm["w0"] + jnp.dot(act_w, prm["w_up"], preferred_element_type=F32)
    a_pre = prm["a0"] + jnp.dot(act_a, prm["a_up"], preferred_element_type=F32)
    gate = jnp.dot(act_g, prm["g_up"], preferred_element_type=F32)
    nrm_sq = _dot_pieces(kk_sq, ones_bd)
    yield None
    logw = -jnp.exp(-_softplus(-w_raw) - 0.5)
    a_sig = _sigmoid(a_pre)
    kk = kk / jnp.maximum(jnp.sqrt(nrm_sq), 1e-12)
    k_eff = xk * (1.0 + (a_sig - 1.0) * prm["k_a"])
    rk = _split_pieces(xr * k_eff * prm["r_k"], 2)
    yield None
    bonus = _dot_pieces(rk, ones_bd) * xv
    yield dict(r=xr, k=k_eff, v=xv, a=-kk, b=kk * a_sig, logw=logw, gate=gate, bonus=bonus)


def _rwkv_token_math(xr, xk, xv, xl, prm, ones_bd):
    *_, tok = _rwkv_token_stages(lambda: (xr, xk, xv, xl), prm, ones_bd)
    return tok


def _group_norm_gate(y, bonus, gate, lnx_w, lnx_b, ones_bd):
    mean = _split_dot_r(y, ones_bd, 2) * (1.0 / HEAD_DIM)
    d = y - mean
    var = _split_dot_r(d * d, ones_bd, 2) * (1.0 / HEAD_DIM)
    yn = d * lax.rsqrt(var + LNX_EPS) * lnx_w + lnx_b
    return (yn + bonus) * gate


_RW_VEC_NAMES = ("w0", "a0", "k_k", "k_a", "r_k", "lnx_w", "lnx_b")


def _rwkv_prompt_kernel(r_ref, k_ref, v_ref, l_ref, pm_ref, pl_ref, mu_m_ref, mu_l_ref, vec_ref,
                        wup_ref, aup_ref, gup_ref, o_ref, h_ref,
                        tok_scr, y_scr, rw_scr, y0_scr, g_scr, ha_scr, pc_scr, *, seq):
    n = PAIR
    grows = PHASE_A_UNROLL * CHUNK
    pad = 8

    def token_group(g):
        first = isinstance(g, int) and g == 0
        prm = {name: vec_ref[i:i + 1, :] for i, name in enumerate(_RW_VEC_NAMES)}
        prm.update(w_up=wup_ref[...], a_up=aup_ref[...], g_up=gup_ref[...])
        if first:
            rows = pl.ds(0, grows)
            row0 = lax.broadcasted_iota(jnp.int32, (grows, 1), 0) == 0

            def lerp(ref, prev, mu):
                x = ref[rows, :]
                return x + mu * (jnp.where(row0, prev, pltpu.roll(x, 1, axis=0)) - x)
        else:
            rows = pl.ds(pl.multiple_of(g * grows, grows), grows)
            ext_rows = pl.ds(pl.multiple_of(g * grows - pad, pad), grows + pad)

            def lerp(ref, prev, mu):
                xe = ref[ext_rows, :]
                return xe[pad:] + mu * (pltpu.roll(xe, 1, axis=0)[pad:] - xe[pad:])
        def inputs():
            pm, mu_m = pm_ref[...], mu_m_ref[...]
            return (lerp(r_ref, pm[0:1], mu_m[0:1]), lerp(k_ref, pm[1:2], mu_m[1:2]),
                    lerp(v_ref, pm[2:3], mu_m[2:3]), lerp(l_ref, pl_ref[...], mu_l_ref[...]))

        tok = None
        for tok in _rwkv_token_stages(inputs, prm, _head_ones(n)):
            if tok is None:
                yield
        for i, name in enumerate(("r", "k", "v", "a", "b", "logw", "gate", "bonus")):
            tok_scr[i, rows, :] = tok[name]

    nch = seq // CHUNK
    lane = lax.broadcasted_iota(jnp.int32, (CHUNK, n), 1)
    is_a = lane < HEAD_DIM
    ri = lax.broadcasted_iota(jnp.int32, (n, n), 0)
    ci = lax.broadcasted_iota(jnp.int32, (n, n), 1)
    strict = ri > ci
    incl = ri >= ci
    eye = ri == ci
    rc = lax.broadcasted_iota(jnp.int32, (CHUNK, CHUNK), 0)
    cc = lax.broadcasted_iota(jnp.int32, (CHUNK, CHUNK), 1)

    def stack(x):
        return jnp.concatenate([jnp.where(is_a, x, 0.0), jnp.where(is_a, 0.0, x)], axis=0)

    def phase_b_step(c):
        rows = pl.ds(pl.multiple_of(c * CHUNK, CHUNK), CHUNK)
        ht = h_ref[...]
        hb = ht.astype(BF16)
        ys = _bdot_nt(rw_scr[c], hb) + y0_scr[c]
        y_scr[rows, :] = ys[:CHUNK, :] + ys[CHUNK:, :]
        h_ref[...] = ht * pc_scr[c] + _bdot(hb, g_scr[c]) + ha_scr[c]

    def group(ga, gb, gt=None):
        b_todo = [] if gb is None else [gb * PHASE_A_UNROLL + u for u in range(PHASE_A_UNROLL)]

        def fill():
            if b_todo:
                phase_b_step(b_todo.pop(0))

        if ga is None:
            while b_todo:
                fill()
            return
        cs = [ga * PHASE_A_UNROLL + u for u in range(PHASE_A_UNROLL)]
        each = lambda fn, *lists: [fn(*xs) for xs in zip(*lists)]
        tril_ones = jnp.where(rc >= cc, 1.0, 0.0).astype(BF16)
        tok = [[tok_scr[i, pl.ds(pl.multiple_of(c * CHUNK, CHUNK), CHUNK), :] for i in range(6)] for c in cs]
        r_c, k_c, v_c, a_c, b_c, lw_c = [list(x) for x in zip(*tok)]
        token_gen = iter(()) if gt is None else token_group(gt)
        token_step = lambda: next(token_gen, None)
        token_step()
        pairs = [_split_dot(tril_ones, jnp.concatenate(lw_c[u:u + 2], axis=1), 3)
                 for u in range(0, PHASE_A_UNROLL, 2)]
        lcum = [p[:, h * n:(h + 1) * n] for p in pairs for h in range(2)]
        lend = each(lambda l: l[CHUNK - 1:CHUNK, :], lcum)
        fill()
        sb = lambda x: stack(x).astype(BF16)
        a_s = each(lambda a, l, lw: sb(a * jnp.exp(l - lw)), a_c, lcum, lw_c)
        r_s = each(lambda r, l: sb(r * jnp.exp(l)), r_c, lcum)
        b_s = each(lambda b, l: sb(b * jnp.exp(-l)), b_c, lcum)
        k_s = each(lambda k, l: sb(k * jnp.exp(-l)), k_c, lcum)
        v_s = each(sb, v_c)
        be_s = each(lambda b, l, le: sb(b * jnp.exp(le - l)), b_c, lcum, lend)
        ke_s = each(lambda k, l, le: sb(k * jnp.exp(le - l)), k_c, lcum, lend)

        sc = each(lambda a, r, b, k: _bdot_nt(jnp.concatenate([a, r], axis=0), jnp.concatenate([b, k], axis=0)),
                  a_s, r_s, b_s, k_s)
        fill()
        token_step()
        s_ab = each(lambda s: jnp.where(strict, s[:n, :n], 0.0), sc)
        s_ak = each(lambda s: jnp.where(strict, s[:n, n:], 0.0).astype(BF16), sc)
        s_rb = each(lambda s: jnp.where(incl, s[n:, :n], 0.0).astype(BF16), sc)
        s_rk = each(lambda s: jnp.where(incl, s[n:, n:], 0.0).astype(BF16), sc)

        tinv = each(lambda s: jnp.where(eye, 1.0, 0.0) + s, s_ab)
        apow = each(lambda s: _bdot(s, s).astype(BF16), s_ab)
        fill()
        m = 2
        while 2 * m < CHUNK:
            res = each(lambda p, t: _bdot(p, jnp.concatenate([t.astype(BF16), p], axis=1)), apow, tinv)
            tinv = each(lambda t, x: t + x[:, :n], tinv, res)
            apow = each(lambda x: x[:, n:].astype(BF16), res)
            m *= 2
            fill()
            if m in (4, 16):
                token_step()
        tinv = each(lambda t, p: t + _bdot(p, t), tinv, apow)

        x1 = each(_bdot, s_ak, v_s)
        wu = each(lambda t, a, x: _bdot(t, jnp.concatenate([a, x.astype(BF16)], axis=1)), tinv, a_s, x1)
        fill()
        yk = each(_bdot, s_rk, v_s)
        ry = each(lambda s, w: _bdot(s, w), s_rb, wu)
        gh = each(lambda w, be: _bdot(w.T, be), wu, be_s)
        hk = each(lambda v, ke: _bdot_tn(v, ke), v_s, ke_s)
        while b_todo:
            fill()
        for _ in token_gen:
            pass
        for u, c in enumerate(cs):
            rw_scr[c] = ry[u][:, :n] + r_s[u].astype(F32)
            y0_scr[c] = ry[u][:, n:] + yk[u]
            g_scr[c] = gh[u][:n, :]
            ha_scr[c] = gh[u][n:, :] + hk[u]
            pc_scr[c] = jnp.exp(lend[u])

    h_ref[...] = jnp.zeros((n, n), F32)
    ngroup = nch // PHASE_A_UNROLL
    assert ngroup >= 3
    for _ in token_group(0):
        pass
    group(0, None, 1)

    def pipelined(g, carry):
        group(g, g - 1, g + 1)
        return carry
    lax.fori_loop(1, ngroup - 1, pipelined, 0)
    group(ngroup - 1, ngroup - 2)
    group(None, ngroup - 1)

    iw, ib = _RW_VEC_NAMES.index("lnx_w"), _RW_VEC_NAMES.index("lnx_b")
    o_ref[...] = _group_norm_gate(y_scr[...], tok_scr[7], tok_scr[6], vec_ref[iw:iw + 1, :], vec_ref[ib:ib + 1, :],
                                  _head_ones(n)).astype(o_ref.dtype)


def _rwkv_prompt(p_main3, p_lora3, prev_main, prev_lora, mu_main, mu_lora, vecs, w_up, a_up, g_up):
    nb, seq, _ = p_main3.shape
    npair = C_GRP // PAIR
    col = lambda off: pl.BlockSpec((None, seq, PAIR), lambda b, p, off=off: (b, 0, off + p))
    pcol = lambda rows: pl.BlockSpec((rows, PAIR), lambda b, p: (0, p))
    return pl.pallas_call(
        functools.partial(_rwkv_prompt_kernel, seq=seq),
        grid=(nb, npair),
        in_specs=[
            col(3 * npair), col(4 * npair), col(5 * npair),
            pl.BlockSpec((None, seq, C_LORA), lambda b, p: (b, 0, 0)),
            pl.BlockSpec((None, 3, PAIR), lambda b, p: (b, 0, p)),
            pl.BlockSpec((None, 1, C_LORA), lambda b, p: (b, 0, 0)),
            pcol(3), pl.BlockSpec((1, C_LORA), lambda b, p: (0, 0)), pcol(len(_RW_VEC_NAMES)),
            pcol(2 * W_LORA), pcol(2 * A_LORA), pcol(G_LORA),
        ],
        out_specs=[
            pl.BlockSpec((None, seq, PAIR), lambda b, p: (b, 0, p)),
            pl.BlockSpec((None, None, PAIR, PAIR), lambda b, p: (b, p, 0, 0)),
        ],
        out_shape=[jax.ShapeDtypeStruct((nb, seq, C_GRP), BF16),
                   jax.ShapeDtypeStruct((nb, npair, PAIR, PAIR), F32)],
        scratch_shapes=[
            pltpu.VMEM((8, seq, PAIR), F32), pltpu.VMEM((seq, PAIR), F32),
            pltpu.VMEM((seq // CHUNK, PAIR, PAIR), F32), pltpu.VMEM((seq // CHUNK, PAIR, PAIR), F32),
            pltpu.VMEM((seq // CHUNK, PAIR, PAIR), F32), pltpu.VMEM((seq // CHUNK, PAIR, PAIR), F32),
            pltpu.VMEM((seq // CHUNK, 1, PAIR), F32),
        ],
        compiler_params=_cparams(2),
        name="rwkv_prompt",
    )(p_main3, p_main3, p_main3, p_lora3, prev_main, prev_lora, mu_main, mu_lora, vecs, w_up, a_up, g_up)


def _rwkv_sample_tok_kernel(pm_ref, l_ref, pvm_ref, pvl_ref, mu_m_ref, mu_l_ref, vec_ref,
                            wup_ref, aup_ref, gup_ref, out_ref):
    ones_bd = _head_ones(C_GRP)
    prm = {name: vec_ref[i:i + 1, :] for i, name in enumerate(_RW_VEC_NAMES)}
    prm.update(w_up=wup_ref[...], a_up=aup_ref[...], g_up=gup_ref[...])
    lerp = lambda x, prev, mu: x + mu * (prev - x)
    xs = [lerp(pm_ref[:, i * C_GRP:(i + 1) * C_GRP], pvm_ref[:, i * C_GRP:(i + 1) * C_GRP],
               mu_m_ref[:, i * C_GRP:(i + 1) * C_GRP]) for i in range(3)]
    tok = _rwkv_token_math(xs[0], xs[1], xs[2], lerp(l_ref[...], pvl_ref[...], mu_l_ref[...]), prm, ones_bd)
    for i, name in enumerate(("r", "k", "v", "a", "b", "logw", "gate", "bonus")):
        out_ref[i] = tok[name]


def _rwkv_sample_step_kernel(s_ref, row_ref, col_ref, s_out_ref, o_ref):
    head_lane = lax.broadcasted_iota(jnp.int32, (1, N_HEADS), 1)
    heads = range(N_HEADS)
    row = lambda i: [row_ref[0, i, h:h + 1, :] for h in heads]
    r, k, a, b, logw = [row(i) for i in range(5)]
    v = [col_ref[0, 0, :, h:h + 1] for h in heads]
    s = [s_ref[0, h] for h in heads]
    sa = [jnp.sum(s[h] * a[h], axis=-1, keepdims=True) for h in heads]
    s_new = [s[h] * jnp.exp(logw[h]) + sa[h] * b[h] + v[h] * k[h] for h in heads]
    for h in heads:
        s_out_ref[0, h] = s_new[h]
    y = jnp.zeros((HEAD_DIM, N_HEADS), F32)
    for h in heads:
        y = y + jnp.where(head_lane == h, jnp.sum(s_new[h] * r[h], axis=-1, keepdims=True), 0.0)
    gate, bonus, lnx_w, lnx_b = [col_ref[0, i] for i in range(1, 5)]
    mean = jnp.mean(y, axis=0, keepdims=True)
    d = y - mean
    var = jnp.mean(d * d, axis=0, keepdims=True)
    o_ref[0] = (d * lax.rsqrt(var + LNX_EPS) * lnx_w + lnx_b + bonus) * gate


def _rwkv_sample(pm_s, pl_s, prev_main, prev_lora, mu_main, mu_lora, vecs, w_up, a_up, g_up, wkv0):
    nb = pm_s.shape[0]
    full = lambda shape: pl.BlockSpec(shape, lambda i: (0,) * len(shape))
    tok = pl.pallas_call(
        _rwkv_sample_tok_kernel,
        grid=(1,),
        in_specs=[pl.BlockSpec((nb, 3 * C_GRP), lambda i: (0, 1)), full((nb, C_LORA)),
                  full((nb, 3 * C_GRP)), full((nb, C_LORA)), full((1, 3 * C_GRP)), full((1, C_LORA)),
                  full(vecs.shape), full(w_up.shape), full(a_up.shape), full(g_up.shape)],
        out_specs=full((8, nb, C_GRP)),
        out_shape=jax.ShapeDtypeStruct((8, nb, C_GRP), F32),
        compiler_params=_cparams(1),
        name="rwkv_sample_tok",
    )(pm_s, pl_s, prev_main, prev_lora, mu_main, mu_lora, vecs, w_up, a_up, g_up)
    heads = lambda x: x.reshape(x.shape[0], nb, N_HEADS, HEAD_DIM).transpose(1, 0, 2, 3)
    rows = heads(jnp.concatenate([tok[0:2], tok[3:6]], axis=0))
    lnx = jnp.broadcast_to(vecs[5:7, None, :], (2, nb, C_GRP))
    cols = jnp.swapaxes(heads(jnp.concatenate([tok[2:3], tok[6:8], lnx], axis=0)), -1, -2)
    st = (1, N_HEADS, HEAD_DIM, HEAD_DIM)
    s_new, o = pl.pallas_call(
        _rwkv_sample_step_kernel,
        grid=(nb,),
        in_specs=[pl.BlockSpec(st, lambda b: (b, 0, 0, 0)),
                  pl.BlockSpec((1, 5, N_HEADS, HEAD_DIM), lambda b: (b, 0, 0, 0)),
                  pl.BlockSpec((1, 5, HEAD_DIM, N_HEADS), lambda b: (b, 0, 0, 0))],
        out_specs=[pl.BlockSpec(st, lambda b: (b, 0, 0, 0)),
                   pl.BlockSpec((1, HEAD_DIM, N_HEADS), lambda b: (b, 0, 0))],
        out_shape=[jax.ShapeDtypeStruct((nb,) + st[1:], F32),
                   jax.ShapeDtypeStruct((nb, HEAD_DIM, N_HEADS), F32)],
        compiler_params=_cparams(1),
        name="rwkv_sample_step",
    )(wkv0, rows, cols)
    return jnp.swapaxes(o, -1, -2).reshape(nb, C_GRP), s_new


def _out_proj_kernel(x_ref, oa_ref, orw_ref, wa_ref, wb_ref, o_ref):
    o_ref[...] = (x_ref[...] + _bdot(oa_ref[...], wa_ref[...]) + _bdot(orw_ref[...], wb_ref[...]))


def _out_proj(x2d, o_att, o_rw, w_o, tm, tn):
    m = x2d.shape[0]
    return pl.pallas_call(
        _out_proj_kernel,
        grid=(m // tm, D_MODEL // tn),
        in_specs=[
            pl.BlockSpec((tm, tn), lambda i, j: (i, j)),
            pl.BlockSpec((tm, C_GRP), lambda i, j: (i, 0)),
            pl.BlockSpec((tm, C_GRP), lambda i, j: (i, 0)),
            pl.BlockSpec((C_GRP, tn), lambda i, j: (0, j)),
            pl.BlockSpec((C_GRP, tn), lambda i, j: (1, j)),
        ],
        out_specs=pl.BlockSpec((tm, tn), lambda i, j: (i, j)),
        out_shape=jax.ShapeDtypeStruct((m, D_MODEL), F32),
        compiler_params=_cparams(2),
        name="out_proj",
    )(x2d, o_att, o_rw, w_o, w_o)


HALO = 16


def _ffn_kernel(*refs, tm, tiles_per_seq, seq_mode):
    if seq_mode:
        (x_ref, xh_ref, g_ref, wg_ref, wv_ref, cwg_ref, cwv_ref, wd_ref, gf_ref,
         y_ref, ug_ref, uv_ref, h_scr, acc_scr) = refs
    else:
        (x_ref, pg_ref, pv_ref, g_ref, wg_ref, wv_ref, cwg_ref, cwv_ref, wd_ref, gf_ref,
         y_ref, ug_ref, uv_ref, wgq_ref, wvq_ref, wdq_ref, h_scr, acc_scr) = refs
    i, j = pl.program_id(0), pl.program_id(1)

    def weight(w_ref, wq_ref):
        if seq_mode:
            return w_ref[...]
        w = w_ref[...].astype(BF16)
        wq_ref[...] = w
        return w

    def norm(x):
        ms = jnp.mean(x * x, axis=-1, keepdims=True)
        return x * lax.rsqrt(ms + NORM_EPS) * g_ref[...]

    @pl.when(j == 0)
    def _():
        x = x_ref[...]
        acc_scr[...] = x
        if seq_mode:
            keep = jnp.where(i % tiles_per_seq == 0, 0.0, 1.0).astype(F32)
            h_scr[:HALO, :] = (norm(xh_ref[...]) * keep).astype(BF16)
            h_scr[HALO:, :] = norm(x).astype(BF16)
        else:
            h_scr[...] = norm(x).astype(BF16)

    h = h_scr[...]
    halves = []
    for w_ref, wq_ref, cw_ref, u_ref, p_ref in (
            (wg_ref, None if seq_mode else wgq_ref, cwg_ref, ug_ref, None if seq_mode else pg_ref),
            (wv_ref, None if seq_mode else wvq_ref, cwv_ref, uv_ref, None if seq_mode else pv_ref)):
        u = jnp.dot(h, weight(w_ref, wq_ref), preferred_element_type=F32)
        cw = cw_ref[...]
        if seq_mode:
            u_ref[0] = u[HALO + tm - 2:HALO + tm, :]
            c = (cw[3:4] + cw[0:1] * u[HALO - 2:HALO - 2 + tm] + cw[1:2] * u[HALO - 1:HALO - 1 + tm]
                 + cw[2:3] * u[HALO:HALO + tm])
        else:
            u_ref[...] = u
            c = cw[3:4] + cw[0:1] * p_ref[0] + cw[1:2] * p_ref[1] + cw[2:3] * u
        halves.append(c)
    gate, val = halves
    act = gate * _sigmoid(gate) * val
    acc_scr[...] += jnp.dot(act.astype(BF16), weight(wd_ref, None if seq_mode else wdq_ref),
                            preferred_element_type=F32)

    @pl.when(j == pl.num_programs(1) - 1)
    def _():
        x2 = acc_scr[...]
        ms = jnp.mean(x2 * x2, axis=-1, keepdims=True)
        y_ref[...] = x2 * lax.rsqrt(ms + NORM_EPS) * gf_ref[...]


def _ffn(x1, norm_g, w_gate, w_val, conv_wb, w_down, final_g, tm, tf, seq_len=None, prev=None):
    m = x1.shape[0]
    nf = D_FF // tf
    seq_mode = prev is None
    assert seq_mode or m == tm, "state mode emits each bf16 weight tile once: it needs a single row tile"
    val_off = 0 if seq_mode else nf
    tiles_per_seq = seq_len // tm if seq_mode else 1
    vec = lambda width: pl.BlockSpec((1, width), lambda i, j: (0, 0))
    in_specs = [pl.BlockSpec((tm, D_MODEL), lambda i, j: (i, 0))]
    args = [x1]
    if seq_mode:
        in_specs.append(pl.BlockSpec((HALO, D_MODEL), lambda i, j: (jnp.maximum(i * (tm // HALO) - 1, 0), 0)))
        args.append(x1)
    else:
        in_specs += [pl.BlockSpec((2, tm, tf), lambda i, j: (0, i, j)),
                     pl.BlockSpec((2, tm, tf), lambda i, j: (0, i, nf + j))]
        args += [prev, prev]
    in_specs += [
        vec(D_MODEL),
        pl.BlockSpec((D_MODEL, tf), lambda i, j: (0, j)),
        pl.BlockSpec((D_MODEL, tf), lambda i, j: (0, val_off + j)),
        pl.BlockSpec((4, tf), lambda i, j: (0, j)),
        pl.BlockSpec((4, tf), lambda i, j: (0, nf + j)),
        pl.BlockSpec((tf, D_MODEL), lambda i, j: (j, 0)),
        vec(D_MODEL),
    ]
    args += [norm_g, w_gate, w_val, conv_wb, conv_wb, w_down, final_g]
    out_specs = [pl.BlockSpec((tm, D_MODEL), lambda i, j: (i, 0))]
    out_shape = [jax.ShapeDtypeStruct((m, D_MODEL), F32)]
    if seq_mode:
        out_specs += [pl.BlockSpec((1, 2, tf), lambda i, j: (i, 0, j))] * 2
        out_shape += [jax.ShapeDtypeStruct((m // tm, 2, D_FF), F32)] * 2
    else:
        out_specs += [pl.BlockSpec((tm, tf), lambda i, j: (i, j))] * 2
        out_shape += [jax.ShapeDtypeStruct((m, D_FF), F32)] * 2
        out_specs += [pl.BlockSpec((D_MODEL, tf), lambda i, j: (0, j))] * 2 + [pl.BlockSpec((tf, D_MODEL), lambda i, j: (j, 0))]
        out_shape += [jax.ShapeDtypeStruct((D_MODEL, D_FF), BF16)] * 2 + [jax.ShapeDtypeStruct((D_FF, D_MODEL), BF16)]
    rows = tm + HALO if seq_mode else tm
    return pl.pallas_call(
        functools.partial(_ffn_kernel, tm=tm, tiles_per_seq=tiles_per_seq, seq_mode=seq_mode),
        grid=(m // tm, nf),
        in_specs=in_specs,
        out_specs=out_specs,
        out_shape=out_shape,
        scratch_shapes=[pltpu.VMEM((rows, D_MODEL), BF16), pltpu.VMEM((tm, D_MODEL), F32)],
        compiler_params=_cparams(2),
        name="ffn_seq" if seq_mode else "ffn_state",
    )(*args)


def _layer_params(l, norm_mix_g, w_in, att_out_g, rw_mu, rw_w0, rw_w_up, rw_a0, rw_a_up, rw_g_up, rw_k_k,
                  rw_k_a, rw_r_k, rw_lnx_w, rw_lnx_b, w_o, norm_ffn_g, ffn_w_up, ffn_conv_w, ffn_conv_b,
                  ffn_w_down):
    zeros = jnp.zeros((W_LORA, C_GRP), F32)
    return dict(
        norm_mix_g=norm_mix_g[l][None],
        w_main=w_in[l].astype(BF16),
        w_lora=w_in[l][:, C_MAIN:].astype(BF16),
        att_g=att_out_g[l][None],
        mu_main=rw_mu[l][:3 * C_GRP].reshape(3, C_GRP),
        mu_lora=rw_mu[l][None, 3 * C_GRP:],
        vecs=jnp.stack([rw_w0[l], rw_a0[l], rw_k_k[l], rw_k_a[l], rw_r_k[l], rw_lnx_w[l], rw_lnx_b[l]]),
        w_up=jnp.concatenate([rw_w_up[l], zeros]).astype(BF16),
        a_up=jnp.concatenate([zeros, rw_a_up[l]]).astype(BF16),
        g_up=rw_g_up[l].astype(BF16),
        w_o=w_o[l].astype(BF16),
        norm_ffn_g=norm_ffn_g[l][None],
        ffn_w_up=ffn_w_up[l],
        conv_wb=jnp.concatenate([ffn_conv_w[l], ffn_conv_b[l][None]]),
        ffn_w_down=ffn_w_down[l],
    )


def _prompt_layer(x, lp, ffn_w_bf16, final_g):
    nb, seq, _ = x.shape
    x2d = x.reshape(nb * seq, D_MODEL)
    p_main, p_lora = _in_proj(x2d, lp["norm_mix_g"], lp["w_main"], lp["w_lora"], tm=1024, tn=1024)
    p_main3 = p_main.reshape(nb, seq, C_MAIN)
    p_lora3 = p_lora.reshape(nb, seq, C_LORA)
    o_att, kt_new, vt_new = _attn_prompt(p_main3, lp["att_g"])
    o_rw, h_fin = _rwkv_prompt(p_main3, p_lora3, jnp.zeros((nb, 3, C_GRP), F32), jnp.zeros((nb, 1, C_LORA), F32),
                               lp["mu_main"], lp["mu_lora"], lp["vecs"], lp["w_up"], lp["a_up"], lp["g_up"])
    x1 = _out_proj(x2d, o_att.reshape(nb * seq, C_GRP), o_rw.reshape(nb * seq, C_GRP), lp["w_o"], tm=512, tn=D_MODEL)
    w_gate, w_val, w_down = ffn_w_bf16
    y, u_g, u_v = _ffn(x1, lp["norm_ffn_g"], w_gate, w_val, lp["conv_wb"], w_down, final_g,
                       tm=512, tf=512, seq_len=seq)
    hd = (N_HEADS, HEAD_DIM)
    k_new = kt_new.reshape((nb,) + hd + (seq,)).transpose(0, 3, 1, 2)
    v_new = vt_new.reshape((nb,) + hd + (seq,)).transpose(0, 3, 1, 2)
    rw_last = jnp.concatenate([p_main3[:, -1:, 3 * C_GRP:], p_lora3[:, -1:, :]], axis=-1)
    wkv = jnp.stack([h_fin[:, :, :HEAD_DIM, :HEAD_DIM], h_fin[:, :, HEAD_DIM:, HEAD_DIM:]], axis=2)
    wkv = wkv.reshape(nb, N_HEADS, HEAD_DIM, HEAD_DIM)
    tiles_per_seq = u_g.shape[0] // nb
    ffn_last = jnp.concatenate([u_g, u_v], axis=-1)[tiles_per_seq - 1::tiles_per_seq]
    return y.reshape(nb, seq, D_MODEL), k_new, v_new, rw_last, wkv, ffn_last


def _sample_layer(x, cache_k, cache_v, rw_prev, wkv0, ffn_prev, lp, final_g):
    nb = x.shape[0]
    x2d = x.reshape(nb, D_MODEL)
    p_main, p_lora = _in_proj(x2d, lp["norm_mix_g"], lp["w_main"], lp["w_lora"], tm=nb, tn=1024)
    hd = (N_HEADS, HEAD_DIM)
    q = p_main[:, :C_GRP].reshape((nb,) + hd)
    k_new = p_main[:, C_GRP:2 * C_GRP].reshape((nb,) + hd)
    v_new = p_main[:, 2 * C_GRP:3 * C_GRP].reshape((nb,) + hd)
    cache_kt, cache_vt = cache_k.transpose(0, 2, 3, 1), cache_v.transpose(0, 2, 3, 1)
    head_minor = lambda t: jnp.swapaxes(t, -1, -2)
    o_att = head_minor(_attn_sample(head_minor(q), head_minor(k_new), head_minor(v_new), cache_kt, cache_vt,
                                    head_minor(lp["att_g"].reshape(hd)))).reshape(nb, C_GRP)
    prev = rw_prev.reshape(nb, C_SHIFT)
    o_rw, wkv = _rwkv_sample(p_main, p_lora, prev[:, :3 * C_GRP], prev[:, 3 * C_GRP:],
                             lp["mu_main"].reshape(1, 3 * C_GRP), lp["mu_lora"], lp["vecs"],
                             lp["w_up"], lp["a_up"], lp["g_up"], wkv0)
    x1 = _out_proj(x2d, o_att, o_rw, lp["w_o"], tm=nb, tn=1024)
    prev_rows = ffn_prev.transpose(1, 0, 2)
    y, u_g, u_v, *ffn_w_bf16 = _ffn(x1, lp["norm_ffn_g"], lp["ffn_w_up"], lp["ffn_w_up"], lp["conv_wb"],
                                    lp["ffn_w_down"], final_g, tm=nb, tf=512, prev=prev_rows)
    rw_last = jnp.concatenate([p_main[:, 3 * C_GRP:], p_lora], axis=-1)[:, None, :]
    ffn_last = jnp.stack([ffn_prev[:, 1, :], jnp.concatenate([u_g, u_v], axis=-1)], axis=1)
    return (y.reshape(nb, 1, D_MODEL), k_new[:, None], v_new[:, None], rw_last, wkv, ffn_last), ffn_w_bf16


def kernel(x_prompt, x_sample, cache_att_k, cache_att_v, state_rwkv_shift, state_rwkv_wkv, state_ffn_conv, norm_mix_g, w_in, att_out_g, rw_mu, rw_w0, rw_w_up, rw_a0, rw_a_up, rw_g_up, rw_k_k, rw_k_a, rw_r_k, rw_lnx_w, rw_lnx_b, w_o, norm_ffn_g, ffn_w_up, ffn_conv_w, ffn_conv_b, ffn_w_down, norm_final_g):
    depth = w_in.shape[0]
    assert depth == 1, "the fused FFN + final-norm kernel assumes a single trunk layer"
    assert x_sample.shape[1] == 1, "the sample path handles one new token per sequence"
    final_g = norm_final_g[None]
    lp = _layer_params(0, norm_mix_g, w_in, att_out_g, rw_mu, rw_w0, rw_w_up, rw_a0, rw_a_up, rw_g_up, rw_k_k,
                       rw_k_a, rw_r_k, rw_lnx_w, rw_lnx_b, w_o, norm_ffn_g, ffn_w_up, ffn_conv_w, ffn_conv_b,
                       ffn_w_down)
    (ys, sk, sv, srw, swkv, sffn), ffn_w_bf16 = _sample_layer(
        x_sample, cache_att_k[0], cache_att_v[0], state_rwkv_shift[0], state_rwkv_wkv[0], state_ffn_conv[0],
        lp, final_g)
    yp, pk, pv, prw, pwkv, pffn = _prompt_layer(x_prompt, lp, ffn_w_bf16, final_g)
    lead = lambda t: t[None]
    return (yp, ys, lead(pk), lead(pv), lead(prw), lead(pwkv), lead(pffn),
            lead(sk), lead(sv), lead(srw), lead(swkv), lead(sffn))
```

```python
import functools

import jax
import jax.numpy as jnp
from jax import lax
from jax.experimental import pallas as pl
from jax.experimental.pallas import tpu as pltpu

F32 = jnp.float32
BF16 = jnp.bfloat16

D_MODEL = 2048
HEAD_DIM = 64
N_HEADS = 16
C_GRP = N_HEADS * HEAD_DIM
W_LORA, A_LORA, G_LORA = 64, 64, 160
C_LORA = W_LORA + A_LORA + G_LORA
C_SHIFT = 3 * C_GRP + C_LORA
C_MAIN = 6 * C_GRP
D_FF = 5632
DIL_RATES = (1, 4, 16)
N_BACK = 128
ATT_SCALE = HEAD_DIM ** -0.5
NORM_EPS = 1e-6
LNX_EPS = HEAD_DIM * 1e-5
NEG_BIG = -1e30

LANES = 128
VMEM_LIMIT = 48 * 1024 * 1024
IN_PROJ_VMEM_LIMIT = 52 * 1024 * 1024

CHUNK = 64
PHASE_A_UNROLL = 8
ATTN_ITEMS_PER_ITER = 4
PAIR = 2 * HEAD_DIM


def _cparams(n_grid):
    return pltpu.CompilerParams(dimension_semantics=("arbitrary",) * n_grid,
                                vmem_limit_bytes=VMEM_LIMIT)


def _bdot(a, b):
    return jnp.dot(a.astype(BF16), b.astype(BF16), preferred_element_type=F32)


def _bdot_nt(a, b):
    return lax.dot_general(a.astype(BF16), b.astype(BF16), (((1,), (1,)), ((), ())),
                           preferred_element_type=F32)


def _bdot_tn(a, b):
    return jnp.dot(a.astype(F32).T.astype(BF16), b.astype(BF16), preferred_element_type=F32)


def _split_dot(mat01, x, terms):
    acc = None
    rem = x
    for _ in range(terms):
        piece = rem.astype(BF16)
        part = jnp.dot(mat01, piece, preferred_element_type=F32)
        acc = part if acc is None else acc + part
        rem = rem - piece.astype(F32)
    return acc


def _split_dot_r(x, mat01, terms):
    return _dot_pieces(_split_pieces(x, terms), mat01)


def _head_ones(n):
    r = lax.broadcasted_iota(jnp.int32, (n, n), 0) // HEAD_DIM
    c = lax.broadcasted_iota(jnp.int32, (n, n), 1) // HEAD_DIM
    return jnp.where(r == c, 1.0, 0.0).astype(BF16)


def _sigmoid(x):
    return 1.0 / (1.0 + jnp.exp(-x))


def _softplus(x):
    return jnp.maximum(x, 0.0) + jnp.log(1.0 + jnp.exp(-jnp.abs(x)))


def _in_proj_kernel(*refs, with_sample_attn, win):
    if with_sample_attn:
        x_ref, g_ref, wm_ref, wl_ref, *attn_in, om_ref, ol_ref, oa_ref, h_scr = refs
    else:
        x_ref, g_ref, wm_ref, wl_ref, om_ref, ol_ref, h_scr = refs

    @pl.when(pl.program_id(1) == 0)
    def _():
        x = x_ref[...]
        ms = jnp.mean(x * x, axis=-1, keepdims=True)
        h = (x * lax.rsqrt(ms + NORM_EPS) * g_ref[...]).astype(BF16)
        h_scr[...] = h
        ol_ref[...] = jnp.dot(h, wl_ref[...], preferred_element_type=F32)

    om_ref[...] = jnp.dot(h_scr[...], wm_ref[...], preferred_element_type=F32)
    if with_sample_attn:
        unit = pl.program_id(0) * pl.num_programs(1) + pl.program_id(1)
        _attn_sample_unit(unit % (N_HEADS // SAMPLE_HEADS_PER_STEP), *attn_in, oa_ref, win=win)


def _in_proj(x2d, g, w_main, w_lora, tm, tn, sample_attn=None):
    m = x2d.shape[0]
    ni, nj = m // tm, C_MAIN // tn
    single = dict(pipeline_mode=pl.Buffered(1)) if sample_attn is not None else {}
    in_specs = [
        pl.BlockSpec((tm, D_MODEL), lambda i, j: (i, 0), **single),
        pl.BlockSpec((1, D_MODEL), lambda i, j: (0, 0)),
        pl.BlockSpec((D_MODEL, tn), lambda i, j: (0, j)),
        pl.BlockSpec((D_MODEL, C_LORA), lambda i, j: (0, 0), **single),
    ]
    out_specs = [pl.BlockSpec((tm, tn), lambda i, j: (i, j)), pl.BlockSpec((tm, C_LORA), lambda i, j: (i, 0))]
    out_shape = [jax.ShapeDtypeStruct((m, C_MAIN), F32), jax.ShapeDtypeStruct((m, C_LORA), F32)]
    args = [x2d, g, w_main, w_lora]
    win = None
    if sample_attn is not None:
        cache_kt = sample_attn[3]
        nb_s, win = cache_kt.shape[0], cache_kt.shape[-1]
        hs = SAMPLE_HEADS_PER_STEP
        ngrp = N_HEADS // hs
        assert ni * nj == nb_s * ngrp, "one sample-attention unit per grid step"
        assert win % (N_BACK * max(DIL_RATES)) == 0 and win % LANES == 0
        seq_of = lambda i, j: (i * nj + j) // ngrp
        tok = pl.BlockSpec((1, HEAD_DIM, N_HEADS), lambda i, j: (seq_of(i, j), 0, 0))
        cache = pl.BlockSpec((1, hs, HEAD_DIM, win), lambda i, j: (seq_of(i, j), (i * nj + j) % ngrp, 0, 0))
        in_specs += [tok, tok, tok, cache, cache, pl.BlockSpec((HEAD_DIM, N_HEADS), lambda i, j: (0, 0))]
        out_specs.append(tok)
        out_shape.append(jax.ShapeDtypeStruct((nb_s, HEAD_DIM, N_HEADS), F32))
        args += list(sample_attn)
    return pl.pallas_call(
        functools.partial(_in_proj_kernel, with_sample_attn=sample_attn is not None, win=win),
        grid=(ni, nj),
        in_specs=in_specs,
        out_specs=out_specs,
        out_shape=out_shape,
        scratch_shapes=[pltpu.VMEM((tm, D_MODEL), BF16)],
        compiler_params=pltpu.CompilerParams(dimension_semantics=("arbitrary",) * 2,
                                             vmem_limit_bytes=IN_PROJ_VMEM_LIMIT),
        name="in_proj",
    )(*args)


def _attn_prompt_kernel(q_ref, k_ref, v_ref, g_ref, o_ref, kt_out_ref, vt_out_ref, ob_scr, lb_scr,
                        *item_scr, seq):
    blk = N_BACK
    nitem = seq // blk
    lane = lax.broadcasted_iota(jnp.int32, (blk, PAIR), 1)
    is_a = lane < HEAD_DIM

    row2 = lax.broadcasted_iota(jnp.int32, (blk, 2 * blk), 0)
    col2 = lax.broadcasted_iota(jnp.int32, (blk, 2 * blk), 1)
    band_bias = jnp.where((col2 >= row2) & (col2 <= row2 + N_BACK), 0.0, NEG_BIG)
    causal_bias = band_bias[:, blk:]

    def item_rows(j, rate):
        nblk = seq // rate // blk
        return pl.ds((j // nblk) + (j % nblk) * (blk * rate), blk, stride=rate)

    def prep(j, rate, dst):
        qa_scr, qb_scr, kt_scr, va_scr, vb_scr = dst
        rows = item_rows(j, rate)
        qs = q_ref[rows, :] * ATT_SCALE
        v = v_ref[rows, :]
        kt = k_ref[rows, :].T
        qa_scr[j] = jnp.where(is_a, qs, 0.0).astype(BF16)
        qb_scr[j] = jnp.where(is_a, 0.0, qs).astype(BF16)
        kt_scr[j] = kt.astype(BF16)
        va_scr[j] = jnp.where(is_a, v, 0.0).astype(BF16)
        vb_scr[j] = jnp.where(is_a, 0.0, v).astype(BF16)
        if rate == 1:
            kt_out_ref[:, j * blk:(j + 1) * blk] = kt
            vt_out_ref[:, j * blk:(j + 1) * blk] = v.T

    def attend(i, br, src, next_rate, dst):
        qa_scr, qb_scr, kt_scr, va_scr, vb_scr = src
        rate = DIL_RATES[br]
        nblk = seq // rate // blk
        js = [i * ATTN_ITEMS_PER_ITER + u for u in range(ATTN_ITEMS_PER_ITER)]
        nk = blk if nblk == 1 else 2 * blk
        ones = jnp.ones((nk, PAIR), BF16)
        kts, vms, biases = [], [], []
        for j in js:
            if nblk == 1:
                biases.append(causal_bias)
                kts.append(kt_scr[j])
                vs = (va_scr[j], vb_scr[j])
            else:
                jp = jnp.maximum(j - 1, 0)
                lo = jnp.where(j % nblk == 0, blk, 0)
                biases.append(jnp.where(col2 >= lo, band_bias, NEG_BIG))
                kts.append(jnp.concatenate([kt_scr[jp], kt_scr[j]], axis=1))
                vs = (jnp.concatenate([va_scr[jp], va_scr[j]], axis=0),
                      jnp.concatenate([vb_scr[jp], vb_scr[j]], axis=0))
            vms.append([jnp.concatenate([v, ones], axis=1) for v in vs])
        qs = [(qa_scr[j], qb_scr[j]) for j in js]
        s = [[jnp.dot(q, kt, preferred_element_type=F32) + bias for q in qp]
             for qp, kt, bias in zip(qs, kts, biases)]
        if next_rate is not None:
            for j in js:
                prep(j, next_rate, dst)
        m = [[jnp.max(x, axis=-1, keepdims=True) for x in xs] for xs in s]
        e = [[jnp.exp(x - mx).astype(BF16) for x, mx in zip(xs, ms)] for xs, ms in zip(s, m)]
        of = [[jnp.dot(x, vm, preferred_element_type=F32) for x, vm in zip(xs, vp)] for xs, vp in zip(e, vms)]
        for j, (ofa, ofb), (ma, mb) in zip(js, of, m):
            den = jnp.where(is_a, ofa[:, PAIR:], ofb[:, PAIR:])
            rows = item_rows(j, rate)
            ob_scr[br, rows, :] = (ofa[:, :PAIR] + ofb[:, :PAIR]) * (1.0 / den)
            lb_scr[br, rows, :] = jnp.where(is_a, ma, mb) + jnp.log(den)

    sets = (item_scr[:5], item_scr[5:])
    for j in range(nitem):
        prep(j, DIL_RATES[0], sets[0])
    for br in range(len(DIL_RATES)):
        next_rate = DIL_RATES[br + 1] if br + 1 < len(DIL_RATES) else None

        def body(i, carry, br=br, next_rate=next_rate):
            attend(i, br, sets[br % 2], next_rate, sets[(br + 1) % 2])
            return carry
        lax.fori_loop(0, nitem // ATTN_ITEMS_PER_ITER, body, 0)

    ones_bd = _head_ones(PAIR)
    gain = g_ref[...]
    tile = 256

    def merge(i, carry):
        rows = pl.ds(pl.multiple_of(i * tile, tile), tile)
        l0, l1, l2 = lb_scr[0, rows, :], lb_scr[1, rows, :], lb_scr[2, rows, :]
        m = jnp.maximum(jnp.maximum(l0, l1), l2)
        w0, w1, w2 = jnp.exp(l0 - m), jnp.exp(l1 - m), jnp.exp(l2 - m)
        o = (w0 * ob_scr[0, rows, :] + w1 * ob_scr[1, rows, :] + w2 * ob_scr[2, rows, :]) / (w0 + w1 + w2)
        ms = _split_dot_r(o * o, ones_bd, 2) * (1.0 / HEAD_DIM)
        o_ref[rows, :] = (o * lax.rsqrt(ms + NORM_EPS) * gain).astype(o_ref.dtype)
        return carry
    lax.fori_loop(0, seq // tile, merge, 0)


def _attn_prompt(p_main3, att_g):
    nb, seq, _ = p_main3.shape
    npair = C_GRP // PAIR
    col = lambda off: pl.BlockSpec((None, seq, PAIR), lambda b, p, off=off: (b, 0, off + p))
    return pl.pallas_call(
        functools.partial(_attn_prompt_kernel, seq=seq),
        grid=(nb, npair),
        in_specs=[col(0), col(npair), col(2 * npair), pl.BlockSpec((1, PAIR), lambda b, p: (0, p))],
        out_specs=[pl.BlockSpec((None, seq, PAIR), lambda b, p: (b, 0, p)),
                   pl.BlockSpec((None, PAIR, seq), lambda b, p: (b, p, 0)),
                   pl.BlockSpec((None, PAIR, seq), lambda b, p: (b, p, 0))],
        out_shape=[jax.ShapeDtypeStruct((nb, seq, C_GRP), BF16),
                   jax.ShapeDtypeStruct((nb, C_GRP, seq), F32), jax.ShapeDtypeStruct((nb, C_GRP, seq), F32)],
        scratch_shapes=[pltpu.VMEM((3, seq, PAIR), F32), pltpu.VMEM((3, seq, PAIR), F32)]
        + [pltpu.VMEM((seq // N_BACK, N_BACK, PAIR), BF16)] * 10,
        compiler_params=_cparams(2),
        name="attn_prompt",
    )(p_main3, p_main3, p_main3, att_g)


SAMPLE_HEADS_PER_STEP = 8


def _attn_sample_unit(grp, q_ref, kn_ref, vn_ref, kt_ref, vt_ref, g_ref, o_ref, *, win):
    base = grp * SAMPLE_HEADS_PER_STEP
    head_lane = lax.broadcasted_iota(jnp.int32, (1, N_HEADS), 1)
    pick = lambda x, hl: jnp.sum(jnp.where(head_lane == hl, x, 0.0), axis=-1, keepdims=True)
    q_all, kn_all, vn_all = q_ref[0], kn_ref[0], vn_ref[0]
    hs = range(SAMPLE_HEADS_PER_STEP)
    qc = [pick(q_all, base + h) * ATT_SCALE for h in hs]
    kn = [pick(kn_all, base + h) for h in hs]
    vn = [pick(vn_all, base + h) for h in hs]
    s_all = [jnp.sum(kt_ref[0, h] * qc[h], axis=0, keepdims=True) for h in hs]
    s_new = [jnp.sum(kn[h] * qc[h], axis=0, keepdims=True) for h in hs]
    outs, lses = [], []
    for rate in DIL_RATES:
        lo = win - N_BACK * rate
        s = [x[:, lo:] for x in s_all]
        if rate > 1:
            pos = lax.broadcasted_iota(jnp.int32, s[0].shape, 1)
            s = [jnp.where((pos & (rate - 1)) == 0, x, NEG_BIG) for x in s]
        m = [jnp.maximum(jnp.max(s[h], axis=-1, keepdims=True), s_new[h]) for h in hs]
        e = [jnp.exp(s[h] - m[h]) for h in hs]
        e_new = [jnp.exp(s_new[h] - m[h]) for h in hs]
        den = [jnp.sum(e[h], axis=-1, keepdims=True) + e_new[h] for h in hs]
        o = [(jnp.sum(vt_ref[0, h, :, lo:] * e[h], axis=-1, keepdims=True) + e_new[h] * vn[h]) / den[h] for h in hs]
        o_b, l_b = jnp.zeros((HEAD_DIM, N_HEADS), F32), jnp.zeros((1, N_HEADS), F32)
        for h in hs:
            o_b = o_b + jnp.where(head_lane == base + h, o[h], 0.0)
            l_b = l_b + jnp.where(head_lane == base + h, m[h] + jnp.log(den[h]), 0.0)
        outs.append(o_b)
        lses.append(l_b)
    m = jnp.maximum(jnp.maximum(lses[0], lses[1]), lses[2])
    ws = [jnp.exp(l - m) for l in lses]
    o = (ws[0] * outs[0] + ws[1] * outs[1] + ws[2] * outs[2]) / (ws[0] + ws[1] + ws[2])
    ms = jnp.mean(o * o, axis=0, keepdims=True)
    mine = (head_lane >= base) & (head_lane < base + SAMPLE_HEADS_PER_STEP)
    out = jnp.where(mine, o * lax.rsqrt(ms + NORM_EPS) * g_ref[...], 0.0)

    @pl.when(grp == 0)
    def _():
        o_ref[0] = out

    @pl.when(grp != 0)
    def _():
        o_ref[0] += out


def _split_pieces(x, terms):
    pieces, rem = [], x
    for _ in range(terms):
        piece = rem.astype(BF16)
        pieces.append(piece)
        rem = rem - piece.astype(F32)
    return pieces


def _dot_pieces(pieces, mat01):
    return jnp.dot(jnp.concatenate(pieces, axis=1), jnp.concatenate([mat01] * len(pieces), axis=0),
                   preferred_element_type=F32)


def _rwkv_token_stages(inputs, prm, ones_bd):
    xr, xk, xv, xl = inputs()
    act_w, act_a = jnp.tanh(xl[:, :2 * W_LORA]).astype(BF16), xl[:, :2 * W_LORA].astype(BF16)
    act_g = _sigmoid(xl[:, 2 * W_LORA:]).astype(BF16)
    kk = xk * prm["k_k"]
    kk_sq = _split_pieces(kk * kk, 2)
    yield None
    w_raw = prm["w0"] + jnp.dot(act_w, prm["w_up"], preferred_element_type=F32)
    a_pre = prm["a0"] + jnp.dot(act_a, prm["a_up"], preferred_element_type=F32)
    gate = jnp.dot(act_g, prm["g_up"], preferred_element_type=F32)
    nrm_sq = _dot_pieces(kk_sq, ones_bd)
    yield None
    logw = -jnp.exp(-_softplus(-w_raw) - 0.5)
    a_sig = _sigmoid(a_pre)
    kk = kk / jnp.maximum(jnp.sqrt(nrm_sq), 1e-12)
    k_eff = xk * (1.0 + (a_sig - 1.0) * prm["k_a"])
    rk = _split_pieces(xr * k_eff * prm["r_k"], 2)
    yield None
    bonus = _dot_pieces(rk, ones_bd) * xv
    yield dict(r=xr, k=k_eff, v=xv, a=-kk, b=kk * a_sig, logw=logw, gate=gate, bonus=bonus)


def _rwkv_token_math(xr, xk, xv, xl, prm, ones_bd):
    *_, tok = _rwkv_token_stages(lambda: (xr, xk, xv, xl), prm, ones_bd)
    return tok


def _group_norm_gate(y, bonus, gate, lnx_w, lnx_b, ones_bd):
    mean = _split_dot_r(y, ones_bd, 2) * (1.0 / HEAD_DIM)
    d = y - mean
    var = _split_dot_r(d * d, ones_bd, 2) * (1.0 / HEAD_DIM)
    yn = d * lax.rsqrt(var + LNX_EPS) * lnx_w + lnx_b
    return (yn + bonus) * gate


_RW_VEC_NAMES = ("w0", "a0", "k_k", "k_a", "r_k", "lnx_w", "lnx_b")


def _rwkv_prompt_kernel(r_ref, k_ref, v_ref, l_ref, pm_ref, pl_ref, mu_m_ref, mu_l_ref, vec_ref,
                        wup_ref, aup_ref, gup_ref, o_ref, h_ref,
                        tok_scr, y_scr, rw_scr, y0_scr, g_scr, ha_scr, pc_scr, *, seq):
    n = PAIR
    grows = PHASE_A_UNROLL * CHUNK
    pad = 8

    def token_group(g):
        first = isinstance(g, int) and g == 0
        prm = {name: vec_ref[i:i + 1, :] for i, name in enumerate(_RW_VEC_NAMES)}
        prm.update(w_up=wup_ref[...], a_up=aup_ref[...], g_up=gup_ref[...])
        if first:
            rows = pl.ds(0, grows)
            row0 = lax.broadcasted_iota(jnp.int32, (grows, 1), 0) == 0

            def lerp(ref, prev, mu):
                x = ref[rows, :]
                return x + mu * (jnp.where(row0, prev, pltpu.roll(x, 1, axis=0)) - x)
        else:
            rows = pl.ds(pl.multiple_of(g * grows, grows), grows)
            ext_rows = pl.ds(pl.multiple_of(g * grows - pad, pad), grows + pad)

            def lerp(ref, prev, mu):
                xe = ref[ext_rows, :]
                return xe[pad:] + mu * (pltpu.roll(xe, 1, axis=0)[pad:] - xe[pad:])
        def inputs():
            pm, mu_m = pm_ref[...], mu_m_ref[...]
            return (lerp(r_ref, pm[0:1], mu_m[0:1]), lerp(k_ref, pm[1:2], mu_m[1:2]),
                    lerp(v_ref, pm[2:3], mu_m[2:3]), lerp(l_ref, pl_ref[...], mu_l_ref[...]))

        tok = None
        for tok in _rwkv_token_stages(inputs, prm, _head_ones(n)):
            if tok is None:
                yield
        for i, name in enumerate(("r", "k", "v", "a", "b", "logw", "gate", "bonus")):
            tok_scr[i, rows, :] = tok[name]

    nch = seq // CHUNK
    lane = lax.broadcasted_iota(jnp.int32, (CHUNK, n), 1)
    is_a = lane < HEAD_DIM
    ri = lax.broadcasted_iota(jnp.int32, (n, n), 0)
    ci = lax.broadcasted_iota(jnp.int32, (n, n), 1)
    strict = ri > ci
    incl = ri >= ci
    eye = ri == ci
    rc = lax.broadcasted_iota(jnp.int32, (CHUNK, CHUNK), 0)
    cc = lax.broadcasted_iota(jnp.int32, (CHUNK, CHUNK), 1)

    def stack(x):
        return jnp.concatenate([jnp.where(is_a, x, 0.0), jnp.where(is_a, 0.0, x)], axis=0)

    def phase_b_step(c):
        rows = pl.ds(pl.multiple_of(c * CHUNK, CHUNK), CHUNK)
        ht = h_ref[...]
        hb = ht.astype(BF16)
        ys = _bdot_nt(rw_scr[c], hb) + y0_scr[c]
        y_scr[rows, :] = ys[:CHUNK, :] + ys[CHUNK:, :]
        h_ref[...] = ht * pc_scr[c] + _bdot(hb, g_scr[c]) + ha_scr[c]

    def group(ga, gb, gt=None):
        b_todo = [] if gb is None else [gb * PHASE_A_UNROLL + u for u in range(PHASE_A_UNROLL)]

        def fill():
            if b_todo:
                phase_b_step(b_todo.pop(0))

        if ga is None:
            while b_todo:
                fill()
            return
        cs = [ga * PHASE_A_UNROLL + u for u in range(PHASE_A_UNROLL)]
        each = lambda fn, *lists: [fn(*xs) for xs in zip(*lists)]
        tril_ones = jnp.where(rc >= cc, 1.0, 0.0).astype(BF16)
        tok = [[tok_scr[i, pl.ds(pl.multiple_of(c * CHUNK, CHUNK), CHUNK), :] for i in range(6)] for c in cs]
        r_c, k_c, v_c, a_c, b_c, lw_c = [list(x) for x in zip(*tok)]
        token_gen = iter(()) if gt is None else token_group(gt)
        token_step = lambda: next(token_gen, None)
        token_step()
        pairs = [_split_dot(tril_ones, jnp.concatenate(lw_c[u:u + 2], axis=1), 3)
                 for u in range(0, PHASE_A_UNROLL, 2)]
        lcum = [p[:, h * n:(h + 1) * n] for p in pairs for h in range(2)]
        lend = each(lambda l: l[CHUNK - 1:CHUNK, :], lcum)
        fill()
        sb = lambda x: stack(x).astype(BF16)
        a_s = each(lambda a, l, lw: sb(a * jnp.exp(l - lw)), a_c, lcum, lw_c)
        r_s = each(lambda r, l: sb(r * jnp.exp(l)), r_c, lcum)
        b_s = each(lambda b, l: sb(b * jnp.exp(-l)), b_c, lcum)
        k_s = each(lambda k, l: sb(k * jnp.exp(-l)), k_c, lcum)
        v_s = each(sb, v_c)
        be_s = each(lambda b, l, le: sb(b * jnp.exp(le - l)), b_c, lcum, lend)
        ke_s = each(lambda k, l, le: sb(k * jnp.exp(le - l)), k_c, lcum, lend)

        sc = each(lambda a, r, b, k: _bdot_nt(jnp.concatenate([a, r], axis=0), jnp.concatenate([b, k], axis=0)),
                  a_s, r_s, b_s, k_s)
        fill()
        token_step()
        s_ab = each(lambda s: jnp.where(strict, s[:n, :n], 0.0), sc)
        s_ak = each(lambda s: jnp.where(strict, s[:n, n:], 0.0).astype(BF16), sc)
        s_rb = each(lambda s: jnp.where(incl, s[n:, :n], 0.0).astype(BF16), sc)
        s_rk = each(lambda s: jnp.where(incl, s[n:, n:], 0.0).astype(BF16), sc)

        tinv = each(lambda s: jnp.where(eye, 1.0, 0.0) + s, s_ab)
        apow = each(lambda s: _bdot(s, s).astype(BF16), s_ab)
        fill()
        m = 2
        while 2 * m < CHUNK:
            res = each(lambda p, t: _bdot(p, jnp.concatenate([t.astype(BF16), p], axis=1)), apow, tinv)
            tinv = each(lambda t, x: t + x[:, :n], tinv, res)
            apow = each(lambda x: x[:, n:].astype(BF16), res)
            m *= 2
            fill()
            if m in (4, 16):
                token_step()
        tinv = each(lambda t, p: t + _bdot(p, t), tinv, apow)

        x1 = each(_bdot, s_ak, v_s)
        wu = each(lambda t, a, x: _bdot(t, jnp.concatenate([a, x.astype(BF16)], axis=1)), tinv, a_s, x1)
        fill()
        yk = each(_bdot, s_rk, v_s)
        ry = each(lambda s, w: _bdot(s, w), s_rb, wu)
        gh = each(lambda w, be: _bdot(w.T, be), wu, be_s)
        hk = each(lambda v, ke: _bdot_tn(v, ke), v_s, ke_s)
        while b_todo:
            fill()
        for _ in token_gen:
            pass
        for u, c in enumerate(cs):
            rw_scr[c] = ry[u][:, :n] + r_s[u].astype(F32)
            y0_scr[c] = ry[u][:, n:] + yk[u]
            g_scr[c] = gh[u][:n, :]
            ha_scr[c] = gh[u][n:, :] + hk[u]
            pc_scr[c] = jnp.exp(lend[u])

    h_ref[...] = jnp.zeros((n, n), F32)
    ngroup = nch // PHASE_A_UNROLL
    assert ngroup >= 3
    for _ in token_group(0):
        pass
    group(0, None, 1)

    def pipelined(g, carry):
        group(g, g - 1, g + 1)
        return carry
    lax.fori_loop(1, ngroup - 1, pipelined, 0)
    group(ngroup - 1, ngroup - 2)
    group(None, ngroup - 1)

    iw, ib = _RW_VEC_NAMES.index("lnx_w"), _RW_VEC_NAMES.index("lnx_b")
    o_ref[...] = _group_norm_gate(y_scr[...], tok_scr[7], tok_scr[6], vec_ref[iw:iw + 1, :], vec_ref[ib:ib + 1, :],
                                  _head_ones(n)).astype(o_ref.dtype)


def _rwkv_prompt(p_main3, p_lora3, prev_main, prev_lora, mu_main, mu_lora, vecs, w_up, a_up, g_up):
    nb, seq, _ = p_main3.shape
    npair = C_GRP // PAIR
    col = lambda off: pl.BlockSpec((None, seq, PAIR), lambda b, p, off=off: (b, 0, off + p))
    pcol = lambda rows: pl.BlockSpec((rows, PAIR), lambda b, p: (0, p))
    return pl.pallas_call(
        functools.partial(_rwkv_prompt_kernel, seq=seq),
        grid=(nb, npair),
        in_specs=[
            col(3 * npair), col(4 * npair), col(5 * npair),
            pl.BlockSpec((None, seq, C_LORA), lambda b, p: (b, 0, 0)),
            pl.BlockSpec((None, 3, PAIR), lambda b, p: (b, 0, p)),
            pl.BlockSpec((None, 1, C_LORA), lambda b, p: (b, 0, 0)),
            pcol(3), pl.BlockSpec((1, C_LORA), lambda b, p: (0, 0)), pcol(len(_RW_VEC_NAMES)),
            pcol(2 * W_LORA), pcol(2 * A_LORA), pcol(G_LORA),
        ],
        out_specs=[
            pl.BlockSpec((None, seq, PAIR), lambda b, p: (b, 0, p)),
            pl.BlockSpec((None, None, PAIR, PAIR), lambda b, p: (b, p, 0, 0)),
        ],
        out_shape=[jax.ShapeDtypeStruct((nb, seq, C_GRP), BF16),
                   jax.ShapeDtypeStruct((nb, npair, PAIR, PAIR), F32)],
        scratch_shapes=[
            pltpu.VMEM((8, seq, PAIR), F32), pltpu.VMEM((seq, PAIR), F32),
            pltpu.VMEM((seq // CHUNK, PAIR, PAIR), F32), pltpu.VMEM((seq // CHUNK, PAIR, PAIR), F32),
            pltpu.VMEM((seq // CHUNK, PAIR, PAIR), F32), pltpu.VMEM((seq // CHUNK, PAIR, PAIR), F32),
            pltpu.VMEM((seq // CHUNK, 1, PAIR), F32),
        ],
        compiler_params=_cparams(2),
        name="rwkv_prompt",
    )(p_main3, p_main3, p_main3, p_lora3, prev_main, prev_lora, mu_main, mu_lora, vecs, w_up, a_up, g_up)


def _rwkv_sample_tok_kernel(pm_ref, l_ref, pvm_ref, pvl_ref, mu_m_ref, mu_l_ref, vec_ref,
                            wup_ref, aup_ref, gup_ref, out_ref):
    ones_bd = _head_ones(C_GRP)
    prm = {name: vec_ref[i:i + 1, :] for i, name in enumerate(_RW_VEC_NAMES)}
    prm.update(w_up=wup_ref[...], a_up=aup_ref[...], g_up=gup_ref[...])
    lerp = lambda x, prev, mu: x + mu * (prev - x)
    xs = [lerp(pm_ref[:, i * C_GRP:(i + 1) * C_GRP], pvm_ref[:, i * C_GRP:(i + 1) * C_GRP],
               mu_m_ref[:, i * C_GRP:(i + 1) * C_GRP]) for i in range(3)]
    tok = _rwkv_token_math(xs[0], xs[1], xs[2], lerp(l_ref[...], pvl_ref[...], mu_l_ref[...]), prm, ones_bd)
    for i, name in enumerate(("r", "k", "v", "a", "b", "logw", "gate", "bonus")):
        out_ref[i] = tok[name]


def _rwkv_sample_step_kernel(s_ref, row_ref, col_ref, s_out_ref, o_ref):
    head_lane = lax.broadcasted_iota(jnp.int32, (1, N_HEADS), 1)
    heads = range(N_HEADS)
    row = lambda i: [row_ref[0, i, h:h + 1, :] for h in heads]
    r, k, a, b, logw = [row(i) for i in range(5)]
    v = [col_ref[0, 0, :, h:h + 1] for h in heads]
    s = [s_ref[0, h] for h in heads]
    sa = [jnp.sum(s[h] * a[h], axis=-1, keepdims=True) for h in heads]
    s_new = [s[h] * jnp.exp(logw[h]) + sa[h] * b[h] + v[h] * k[h] for h in heads]
    for h in heads:
        s_out_ref[0, h] = s_new[h]
    y = jnp.zeros((HEAD_DIM, N_HEADS), F32)
    for h in heads:
        y = y + jnp.where(head_lane == h, jnp.sum(s_new[h] * r[h], axis=-1, keepdims=True), 0.0)
    gate, bonus, lnx_w, lnx_b = [col_ref[0, i] for i in range(1, 5)]
    mean = jnp.mean(y, axis=0, keepdims=True)
    d = y - mean
    var = jnp.mean(d * d, axis=0, keepdims=True)
    o_ref[0] = (d * lax.rsqrt(var + LNX_EPS) * lnx_w + lnx_b + bonus) * gate


def _rwkv_sample(pm_s, pl_s, prev_main, prev_lora, mu_main, mu_lora, vecs, w_up, a_up, g_up, wkv0):
    nb = pm_s.shape[0]
    full = lambda shape: pl.BlockSpec(shape, lambda i: (0,) * len(shape))
    tok = pl.pallas_call(
        _rwkv_sample_tok_kernel,
        grid=(1,),
        in_specs=[pl.BlockSpec((nb, 3 * C_GRP), lambda i: (0, 1)), full((nb, C_LORA)),
                  full((nb, 3 * C_GRP)), full((nb, C_LORA)), full((1, 3 * C_GRP)), full((1, C_LORA)),
                  full(vecs.shape), full(w_up.shape), full(a_up.shape), full(g_up.shape)],
        out_specs=full((8, nb, C_GRP)),
        out_shape=jax.ShapeDtypeStruct((8, nb, C_GRP), F32),
        compiler_params=_cparams(1),
        name="rwkv_sample_tok",
    )(pm_s, pl_s, prev_main, prev_lora, mu_main, mu_lora, vecs, w_up, a_up, g_up)
    heads = lambda x: x.reshape(x.shape[0], nb, N_HEADS, HEAD_DIM).transpose(1, 0, 2, 3)
    rows = heads(jnp.concatenate([tok[0:2], tok[3:6]], axis=0))
    lnx = jnp.broadcast_to(vecs[5:7, None, :], (2, nb, C_GRP))
    cols = jnp.swapaxes(heads(jnp.concatenate([tok[2:3], tok[6:8], lnx], axis=0)), -1, -2)
    st = (1, N_HEADS, HEAD_DIM, HEAD_DIM)
    s_new, o = pl.pallas_call(
        _rwkv_sample_step_kernel,
        grid=(nb,),
        in_specs=[pl.BlockSpec(st, lambda b: (b, 0, 0, 0)),
                  pl.BlockSpec((1, 5, N_HEADS, HEAD_DIM), lambda b: (b, 0, 0, 0)),
                  pl.BlockSpec((1, 5, HEAD_DIM, N_HEADS), lambda b: (b, 0, 0, 0))],
        out_specs=[pl.BlockSpec(st, lambda b: (b, 0, 0, 0)),
                   pl.BlockSpec((1, HEAD_DIM, N_HEADS), lambda b: (b, 0, 0))],
        out_shape=[jax.ShapeDtypeStruct((nb,) + st[1:], F32),
                   jax.ShapeDtypeStruct((nb, HEAD_DIM, N_HEADS), F32)],
        compiler_params=_cparams(1),
        name="rwkv_sample_step",
    )(wkv0, rows, cols)
    return jnp.swapaxes(o, -1, -2).reshape(nb, C_GRP), s_new


def _out_proj_kernel(x_ref, oa_ref, orw_ref, wa_ref, wb_ref, o_ref):
    o_ref[...] = (x_ref[...] + _bdot(oa_ref[...], wa_ref[...]) + _bdot(orw_ref[...], wb_ref[...]))


def _out_proj(x2d, o_att, o_rw, w_o, tm, tn):
    m = x2d.shape[0]
    return pl.pallas_call(
        _out_proj_kernel,
        grid=(m // tm, D_MODEL // tn),
        in_specs=[
            pl.BlockSpec((tm, tn), lambda i, j: (i, j)),
            pl.BlockSpec((tm, C_GRP), lambda i, j: (i, 0)),
            pl.BlockSpec((tm, C_GRP), lambda i, j: (i, 0)),
            pl.BlockSpec((C_GRP, tn), lambda i, j: (0, j)),
            pl.BlockSpec((C_GRP, tn), lambda i, j: (1, j)),
        ],
        out_specs=pl.BlockSpec((tm, tn), lambda i, j: (i, j)),
        out_shape=jax.ShapeDtypeStruct((m, D_MODEL), F32),
        compiler_params=_cparams(2),
        name="out_proj",
    )(x2d, o_att, o_rw, w_o, w_o)


HALO = 16


def _ffn_kernel(*refs, tm, tiles_per_seq, seq_mode):
    if seq_mode:
        (x_ref, xh_ref, g_ref, wg_ref, wv_ref, cwg_ref, cwv_ref, wd_ref, gf_ref,
         y_ref, ug_ref, uv_ref, h_scr, acc_scr) = refs
    else:
        (x_ref, pg_ref, pv_ref, g_ref, wg_ref, wv_ref, cwg_ref, cwv_ref, wd_ref, gf_ref,
         y_ref, ug_ref, uv_ref, wgq_ref, wvq_ref, wdq_ref, h_scr, acc_scr) = refs
    i, j = pl.program_id(0), pl.program_id(1)

    def weight(w_ref, wq_ref):
        if seq_mode:
            return w_ref[...]
        w = w_ref[...].astype(BF16)
        wq_ref[...] = w
        return w

    def norm(x):
        ms = jnp.mean(x * x, axis=-1, keepdims=True)
        return x * lax.rsqrt(ms + NORM_EPS) * g_ref[...]

    @pl.when(j == 0)
    def _():
        x = x_ref[...]
        acc_scr[...] = x
        if seq_mode:
            keep = jnp.where(i % tiles_per_seq == 0, 0.0, 1.0).astype(F32)
            h_scr[:HALO, :] = (norm(xh_ref[...]) * keep).astype(BF16)
            h_scr[HALO:, :] = norm(x).astype(BF16)
        else:
            h_scr[...] = norm(x).astype(BF16)

    h = h_scr[...]
    halves = []
    for w_ref, wq_ref, cw_ref, u_ref, p_ref in (
            (wg_ref, None if seq_mode else wgq_ref, cwg_ref, ug_ref, None if seq_mode else pg_ref),
            (wv_ref, None if seq_mode else wvq_ref, cwv_ref, uv_ref, None if seq_mode else pv_ref)):
        u = jnp.dot(h, weight(w_ref, wq_ref), preferred_element_type=F32)
        cw = cw_ref[...]
        if seq_mode:
            u_ref[0] = u[HALO + tm - 2:HALO + tm, :]
            c = (cw[3:4] + cw[0:1] * u[HALO - 2:HALO - 2 + tm] + cw[1:2] * u[HALO - 1:HALO - 1 + tm]
                 + cw[2:3] * u[HALO:HALO + tm])
        else:
            u_ref[...] = u
            c = cw[3:4] + cw[0:1] * p_ref[0] + cw[1:2] * p_ref[1] + cw[2:3] * u
        halves.append(c)
    gate, val = halves
    act = gate * _sigmoid(gate) * val
    acc_scr[...] += jnp.dot(act.astype(BF16), weight(wd_ref, None if seq_mode else wdq_ref),
                            preferred_element_type=F32)

    @pl.when(j == pl.num_programs(1) - 1)
    def _():
        x2 = acc_scr[...]
        ms = jnp.mean(x2 * x2, axis=-1, keepdims=True)
        y_ref[...] = x2 * lax.rsqrt(ms + NORM_EPS) * gf_ref[...]


def _ffn(x1, norm_g, w_gate, w_val, conv_wb, w_down, final_g, tm, tf, seq_len=None, prev=None):
    m = x1.shape[0]
    nf = D_FF // tf
    seq_mode = prev is None
    assert seq_mode or m == tm, "state mode emits each bf16 weight tile once: it needs a single row tile"
    val_off = 0 if seq_mode else nf
    tiles_per_seq = seq_len // tm if seq_mode else 1
    vec = lambda width: pl.BlockSpec((1, width), lambda i, j: (0, 0))
    in_specs = [pl.BlockSpec((tm, D_MODEL), lambda i, j: (i, 0))]
    args = [x1]
    if seq_mode:
        in_specs.append(pl.BlockSpec((HALO, D_MODEL), lambda i, j: (jnp.maximum(i * (tm // HALO) - 1, 0), 0)))
        args.append(x1)
    else:
        in_specs += [pl.BlockSpec((2, tm, tf), lambda i, j: (0, i, j)),
                     pl.BlockSpec((2, tm, tf), lambda i, j: (0, i, nf + j))]
        args += [prev, prev]
    in_specs += [
        vec(D_MODEL),
        pl.BlockSpec((D_MODEL, tf), lambda i, j: (0, j)),
        pl.BlockSpec((D_MODEL, tf), lambda i, j: (0, val_off + j)),
        pl.BlockSpec((4, tf), lambda i, j: (0, j)),
        pl.BlockSpec((4, tf), lambda i, j: (0, nf + j)),
        pl.BlockSpec((tf, D_MODEL), lambda i, j: (j, 0)),
        vec(D_MODEL),
    ]
    args += [norm_g, w_gate, w_val, conv_wb, conv_wb, w_down, final_g]
    out_specs = [pl.BlockSpec((tm, D_MODEL), lambda i, j: (i, 0))]
    out_shape = [jax.ShapeDtypeStruct((m, D_MODEL), F32)]
    if seq_mode:
        out_specs += [pl.BlockSpec((1, 2, tf), lambda i, j: (i, 0, j))] * 2
        out_shape += [jax.ShapeDtypeStruct((m // tm, 2, D_FF), F32)] * 2
    else:
        out_specs += [pl.BlockSpec((tm, tf), lambda i, j: (i, j))] * 2
        out_shape += [jax.ShapeDtypeStruct((m, D_FF), F32)] * 2
        out_specs += [pl.BlockSpec((D_MODEL, tf), lambda i, j: (0, j))] * 2 + [pl.BlockSpec((tf, D_MODEL), lambda i, j: (j, 0))]
        out_shape += [jax.ShapeDtypeStruct((D_MODEL, D_FF), BF16)] * 2 + [jax.ShapeDtypeStruct((D_FF, D_MODEL), BF16)]
    rows = tm + HALO if seq_mode else tm
    return pl.pallas_call(
        functools.partial(_ffn_kernel, tm=tm, tiles_per_seq=tiles_per_seq, seq_mode=seq_mode),
        grid=(m // tm, nf),
        in_specs=in_specs,
        out_specs=out_specs,
        out_shape=out_shape,
        scratch_shapes=[pltpu.VMEM((rows, D_MODEL), BF16), pltpu.VMEM((tm, D_MODEL), F32)],
        compiler_params=_cparams(2),
        name="ffn_seq" if seq_mode else "ffn_state",
    )(*args)


def _layer_params(l, norm_mix_g, w_in, att_out_g, rw_mu, rw_w0, rw_w_up, rw_a0, rw_a_up, rw_g_up, rw_k_k,
                  rw_k_a, rw_r_k, rw_lnx_w, rw_lnx_b, w_o, norm_ffn_g, ffn_w_up, ffn_conv_w, ffn_conv_b,
                  ffn_w_down):
    zeros = jnp.zeros((W_LORA, C_GRP), F32)
    return dict(
        norm_mix_g=norm_mix_g[l][None],
        w_main=w_in[l].astype(BF16),
        w_lora=w_in[l][:, C_MAIN:].astype(BF16),
        att_g=att_out_g[l][None],
        mu_main=rw_mu[l][:3 * C_GRP].reshape(3, C_GRP),
        mu_lora=rw_mu[l][None, 3 * C_GRP:],
        vecs=jnp.stack([rw_w0[l], rw_a0[l], rw_k_k[l], rw_k_a[l], rw_r_k[l], rw_lnx_w[l], rw_lnx_b[l]]),
        w_up=jnp.concatenate([rw_w_up[l], zeros]).astype(BF16),
        a_up=jnp.concatenate([zeros, rw_a_up[l]]).astype(BF16),
        g_up=rw_g_up[l].astype(BF16),
        w_o=w_o[l].astype(BF16),
        norm_ffn_g=norm_ffn_g[l][None],
        ffn_w_up=ffn_w_up[l],
        conv_wb=jnp.concatenate([ffn_conv_w[l], ffn_conv_b[l][None]]),
        ffn_w_down=ffn_w_down[l],
    )


def _prompt_layer(x2d, p_main, p_lora, nb, seq, lp, ffn_w_bf16, final_g):
    p_main3 = p_main.reshape(nb, seq, C_MAIN)
    p_lora3 = p_lora.reshape(nb, seq, C_LORA)
    o_att, kt_new, vt_new = _attn_prompt(p_main3, lp["att_g"])
    o_rw, h_fin = _rwkv_prompt(p_main3, p_lora3, jnp.zeros((nb, 3, C_GRP), F32), jnp.zeros((nb, 1, C_LORA), F32),
                               lp["mu_main"], lp["mu_lora"], lp["vecs"], lp["w_up"], lp["a_up"], lp["g_up"])
    x1 = _out_proj(x2d, o_att.reshape(nb * seq, C_GRP), o_rw.reshape(nb * seq, C_GRP), lp["w_o"], tm=512, tn=D_MODEL)
    w_gate, w_val, w_down = ffn_w_bf16
    y, u_g, u_v = _ffn(x1, lp["norm_ffn_g"], w_gate, w_val, lp["conv_wb"], w_down, final_g,
                       tm=512, tf=512, seq_len=seq)
    hd = (N_HEADS, HEAD_DIM)
    k_new = kt_new.reshape((nb,) + hd + (seq,)).transpose(0, 3, 1, 2)
    v_new = vt_new.reshape((nb,) + hd + (seq,)).transpose(0, 3, 1, 2)
    rw_last = jnp.concatenate([p_main3[:, -1:, 3 * C_GRP:], p_lora3[:, -1:, :]], axis=-1)
    wkv = jnp.stack([h_fin[:, :, :HEAD_DIM, :HEAD_DIM], h_fin[:, :, HEAD_DIM:, HEAD_DIM:]], axis=2)
    wkv = wkv.reshape(nb, N_HEADS, HEAD_DIM, HEAD_DIM)
    tiles_per_seq = u_g.shape[0] // nb
    ffn_last = jnp.concatenate([u_g, u_v], axis=-1)[tiles_per_seq - 1::tiles_per_seq]
    return y.reshape(nb, seq, D_MODEL), k_new, v_new, rw_last, wkv, ffn_last


def _layer(x_prompt, x_sample, cache_k, cache_v, rw_prev, wkv0, ffn_prev, lp, final_g):
    nb, hd = x_sample.shape[0], (N_HEADS, HEAD_DIM)
    x2d = x_sample.reshape(nb, D_MODEL)
    p_main, p_lora = _in_proj(x2d, lp["norm_mix_g"], lp["w_main"], lp["w_lora"], tm=nb, tn=1024)
    q = p_main[:, :C_GRP].reshape((nb,) + hd)
    k_new = p_main[:, C_GRP:2 * C_GRP].reshape((nb,) + hd)
    v_new = p_main[:, 2 * C_GRP:3 * C_GRP].reshape((nb,) + hd)
    cache_kt, cache_vt = cache_k.transpose(0, 2, 3, 1), cache_v.transpose(0, 2, 3, 1)
    head_minor = lambda t: jnp.swapaxes(t, -1, -2)
    nbp, seq, _ = x_prompt.shape
    xp2d = x_prompt.reshape(nbp * seq, D_MODEL)
    pp_main, pp_lora, o_att = _in_proj(
        xp2d, lp["norm_mix_g"], lp["w_main"], lp["w_lora"], tm=1024, tn=768,
        sample_attn=(head_minor(q), head_minor(k_new), head_minor(v_new), cache_kt, cache_vt,
                     head_minor(lp["att_g"].reshape(hd))))
    o_att = head_minor(o_att).reshape(nb, C_GRP)
    prev = rw_prev.reshape(nb, C_SHIFT)
    o_rw, wkv = _rwkv_sample(p_main, p_lora, prev[:, :3 * C_GRP], prev[:, 3 * C_GRP:],
                             lp["mu_main"].reshape(1, 3 * C_GRP), lp["mu_lora"], lp["vecs"],
                             lp["w_up"], lp["a_up"], lp["g_up"], wkv0)
    x1 = _out_proj(x2d, o_att, o_rw, lp["w_o"], tm=nb, tn=1024)
    prev_rows = ffn_prev.transpose(1, 0, 2)
    y, u_g, u_v, *ffn_w_bf16 = _ffn(x1, lp["norm_ffn_g"], lp["ffn_w_up"], lp["ffn_w_up"], lp["conv_wb"],
                                    lp["ffn_w_down"], final_g, tm=nb, tf=512, prev=prev_rows)
    rw_last = jnp.concatenate([p_main[:, 3 * C_GRP:], p_lora], axis=-1)[:, None, :]
    ffn_last = jnp.stack([ffn_prev[:, 1, :], jnp.concatenate([u_g, u_v], axis=-1)], axis=1)
    sample_out = (y.reshape(nb, 1, D_MODEL), k_new[:, None], v_new[:, None], rw_last, wkv, ffn_last)
    return _prompt_layer(xp2d, pp_main, pp_lora, nbp, seq, lp, ffn_w_bf16, final_g), sample_out


def kernel(x_prompt, x_sample, cache_att_k, cache_att_v, state_rwkv_shift, state_rwkv_wkv, state_ffn_conv, norm_mix_g, w_in, att_out_g, rw_mu, rw_w0, rw_w_up, rw_a0, rw_a_up, rw_g_up, rw_k_k, rw_k_a, rw_r_k, rw_lnx_w, rw_lnx_b, w_o, norm_ffn_g, ffn_w_up, ffn_conv_w, ffn_conv_b, ffn_w_down, norm_final_g):
    depth = w_in.shape[0]
    assert depth == 1, "the fused FFN + final-norm kernel assumes a single trunk layer"
    assert x_sample.shape[1] == 1, "the sample path handles one new token per sequence"
    final_g = norm_final_g[None]
    lp = _layer_params(0, norm_mix_g, w_in, att_out_g, rw_mu, rw_w0, rw_w_up, rw_a0, rw_a_up, rw_g_up, rw_k_k,
                       rw_k_a, rw_r_k, rw_lnx_w, rw_lnx_b, w_o, norm_ffn_g, ffn_w_up, ffn_conv_w, ffn_conv_b,
                       ffn_w_down)
    (yp, pk, pv, prw, pwkv, pffn), (ys, sk, sv, srw, swkv, sffn) = _layer(
        x_prompt, x_sample, cache_att_k[0], cache_att_v[0], state_rwkv_shift[0], state_rwkv_wkv[0],
        state_ffn_conv[0], lp, final_g)
    lead = lambda t: t[None]
    return (yp, ys, lead(pk), lead(pv), lead(prw), lead(pwkv), lead(pffn),
            lead(sk), lead(sv), lead(srw), lead(swkv), lead(sffn))
```

```python
import functools

import jax
import jax.numpy as jnp
from jax import lax
from jax.experimental import pallas as pl
from jax.experimental.pallas import tpu as pltpu

F32 = jnp.float32
BF16 = jnp.bfloat16

D_MODEL = 2048
HEAD_DIM = 64
N_HEADS = 16
C_GRP = N_HEADS * HEAD_DIM
W_LORA, A_LORA, G_LORA = 64, 64, 160
C_LORA = W_LORA + A_LORA + G_LORA
C_SHIFT = 3 * C_GRP + C_LORA
C_MAIN = 6 * C_GRP
D_FF = 5632
DIL_RATES = (1, 4, 16)
N_BACK = 128
ATT_SCALE = HEAD_DIM ** -0.5
NORM_EPS = 1e-6
LNX_EPS = HEAD_DIM * 1e-5
NEG_BIG = -1e30

LANES = 128
VMEM_LIMIT = 48 * 1024 * 1024
IN_PROJ_VMEM_LIMIT = 52 * 1024 * 1024

CHUNK = 64
PHASE_A_UNROLL = 8
ATTN_ITEMS_PER_ITER = (4, 4, 8)
PAIR = 2 * HEAD_DIM


def _cparams(n_grid):
    return pltpu.CompilerParams(dimension_semantics=("arbitrary",) * n_grid,
                                vmem_limit_bytes=VMEM_LIMIT)


def _bdot(a, b):
    return jnp.dot(a.astype(BF16), b.astype(BF16), preferred_element_type=F32)


def _bdot_nt(a, b):
    return lax.dot_general(a.astype(BF16), b.astype(BF16), (((1,), (1,)), ((), ())),
                           preferred_element_type=F32)


def _bdot_tn(a, b):
    return jnp.dot(a.astype(F32).T.astype(BF16), b.astype(BF16), preferred_element_type=F32)


def _split_dot(mat01, x, terms):
    acc = None
    rem = x
    for _ in range(terms):
        piece = rem.astype(BF16)
        part = jnp.dot(mat01, piece, preferred_element_type=F32)
        acc = part if acc is None else acc + part
        rem = rem - piece.astype(F32)
    return acc


def _split_dot_r(x, mat01, terms):
    return _dot_pieces(_split_pieces(x, terms), mat01)


def _head_ones(n):
    r = lax.broadcasted_iota(jnp.int32, (n, n), 0) // HEAD_DIM
    c = lax.broadcasted_iota(jnp.int32, (n, n), 1) // HEAD_DIM
    return jnp.where(r == c, 1.0, 0.0).astype(BF16)


def _sigmoid(x):
    return 1.0 / (1.0 + jnp.exp(-x))


def _softplus(x):
    return jnp.maximum(x, 0.0) + jnp.log(1.0 + jnp.exp(-jnp.abs(x)))


def _in_proj_kernel(*refs, with_sample_attn, win):
    if with_sample_attn:
        x_ref, g_ref, wm_ref, wl_ref, *attn_in, om_ref, ol_ref, oa_ref, h_scr = refs
    else:
        x_ref, g_ref, wm_ref, wl_ref, om_ref, ol_ref, h_scr = refs

    @pl.when(pl.program_id(1) == 0)
    def _():
        x = x_ref[...]
        ms = jnp.mean(x * x, axis=-1, keepdims=True)
        h = (x * lax.rsqrt(ms + NORM_EPS) * g_ref[...]).astype(BF16)
        h_scr[...] = h
        ol_ref[...] = jnp.dot(h, wl_ref[...], preferred_element_type=F32)

    om_ref[...] = jnp.dot(h_scr[...], wm_ref[...], preferred_element_type=F32)
    if with_sample_attn:
        unit = pl.program_id(0) * pl.num_programs(1) + pl.program_id(1)
        _attn_sample_unit(unit % (N_HEADS // SAMPLE_HEADS_PER_STEP), *attn_in, oa_ref, win=win)


def _in_proj(x2d, g, w_main, w_lora, tm, tn, sample_attn=None):
    m = x2d.shape[0]
    ni, nj = m // tm, C_MAIN // tn
    single = dict(pipeline_mode=pl.Buffered(1)) if sample_attn is not None else {}
    in_specs = [
        pl.BlockSpec((tm, D_MODEL), lambda i, j: (i, 0), **single),
        pl.BlockSpec((1, D_MODEL), lambda i, j: (0, 0)),
        pl.BlockSpec((D_MODEL, tn), lambda i, j: (0, j)),
        pl.BlockSpec((D_MODEL, C_LORA), lambda i, j: (0, 0), **single),
    ]
    out_specs = [pl.BlockSpec((tm, tn), lambda i, j: (i, j)), pl.BlockSpec((tm, C_LORA), lambda i, j: (i, 0))]
    out_shape = [jax.ShapeDtypeStruct((m, C_MAIN), F32), jax.ShapeDtypeStruct((m, C_LORA), F32)]
    args = [x2d, g, w_main, w_lora]
    win = None
    if sample_attn is not None:
        cache_kt = sample_attn[3]
        nb_s, win = cache_kt.shape[0], cache_kt.shape[-1]
        hs = SAMPLE_HEADS_PER_STEP
        ngrp = N_HEADS // hs
        assert ni * nj == nb_s * ngrp, "one sample-attention unit per grid step"
        assert win % (N_BACK * max(DIL_RATES)) == 0 and win % LANES == 0
        seq_of = lambda i, j: (i * nj + j) // ngrp
        tok = pl.BlockSpec((1, HEAD_DIM, N_HEADS), lambda i, j: (seq_of(i, j), 0, 0))
        cache = pl.BlockSpec((1, hs, HEAD_DIM, win), lambda i, j: (seq_of(i, j), (i * nj + j) % ngrp, 0, 0))
        in_specs += [tok, tok, tok, cache, cache, pl.BlockSpec((HEAD_DIM, N_HEADS), lambda i, j: (0, 0))]
        out_specs.append(tok)
        out_shape.append(jax.ShapeDtypeStruct((nb_s, HEAD_DIM, N_HEADS), F32))
        args += list(sample_attn)
    return pl.pallas_call(
        functools.partial(_in_proj_kernel, with_sample_attn=sample_attn is not None, win=win),
        grid=(ni, nj),
        in_specs=in_specs,
        out_specs=out_specs,
        out_shape=out_shape,
        scratch_shapes=[pltpu.VMEM((tm, D_MODEL), BF16)],
        compiler_params=pltpu.CompilerParams(dimension_semantics=("arbitrary",) * 2,
                                             vmem_limit_bytes=IN_PROJ_VMEM_LIMIT),
        name="in_proj",
    )(*args)


def _attn_prompt_kernel(q_ref, k_ref, v_ref, g_ref, o_ref, kt_out_ref, vt_out_ref, ob_scr, lb_scr,
                        *item_scr, seq):
    blk = N_BACK
    nitem = seq // blk
    lane = lax.broadcasted_iota(jnp.int32, (blk, PAIR), 1)
    is_a = lane < HEAD_DIM

    row2 = lax.broadcasted_iota(jnp.int32, (blk, 2 * blk), 0)
    col2 = lax.broadcasted_iota(jnp.int32, (blk, 2 * blk), 1)
    band_bias = jnp.where((col2 >= row2) & (col2 <= row2 + N_BACK), 0.0, NEG_BIG)
    causal_bias = band_bias[:, blk:]

    def item_rows(j, rate):
        nblk = seq // rate // blk
        return pl.ds((j // nblk) + (j % nblk) * (blk * rate), blk, stride=rate)

    def prep(j, rate, dst):
        qa_scr, qb_scr, kt_scr, va_scr, vb_scr = dst
        rows = item_rows(j, rate)
        qs = q_ref[rows, :] * ATT_SCALE
        v = v_ref[rows, :]
        kt = k_ref[rows, :].T
        qa_scr[j] = jnp.where(is_a, qs, 0.0).astype(BF16)
        qb_scr[j] = jnp.where(is_a, 0.0, qs).astype(BF16)
        kt_scr[j] = kt.astype(BF16)
        va_scr[j] = jnp.where(is_a, v, 0.0).astype(BF16)
        vb_scr[j] = jnp.where(is_a, 0.0, v).astype(BF16)
        if rate == 1:
            kt_out_ref[:, j * blk:(j + 1) * blk] = kt
            vt_out_ref[:, j * blk:(j + 1) * blk] = v.T

    def attend(i, br, src, next_rate, dst):
        qa_scr, qb_scr, kt_scr, va_scr, vb_scr = src
        rate = DIL_RATES[br]
        nblk = seq // rate // blk
        js = [i * ATTN_ITEMS_PER_ITER[br] + u for u in range(ATTN_ITEMS_PER_ITER[br])]
        nk = blk if nblk == 1 else 2 * blk
        ones = jnp.ones((nk, PAIR), BF16)
        kts, vms, biases = [], [], []
        for j in js:
            if nblk == 1:
                biases.append(causal_bias)
                kts.append(kt_scr[j])
                vs = (va_scr[j], vb_scr[j])
            else:
                jp = jnp.maximum(j - 1, 0)
                lo = jnp.where(j % nblk == 0, blk, 0)
                biases.append(jnp.where(col2 >= lo, band_bias, NEG_BIG))
                kts.append(jnp.concatenate([kt_scr[jp], kt_scr[j]], axis=1))
                vs = (jnp.concatenate([va_scr[jp], va_scr[j]], axis=0),
                      jnp.concatenate([vb_scr[jp], vb_scr[j]], axis=0))
            vms.append([jnp.concatenate([v, ones], axis=1) for v in vs])
        qs = [(qa_scr[j], qb_scr[j]) for j in js]
        s = [[jnp.dot(q, kt, preferred_element_type=F32) + bias for q in qp]
             for qp, kt, bias in zip(qs, kts, biases)]
        if next_rate is not None:
            for j in js:
                prep(j, next_rate, dst)
        m = [[jnp.max(x, axis=-1, keepdims=True) for x in xs] for xs in s]
        e = [[jnp.exp(x - mx).astype(BF16) for x, mx in zip(xs, ms)] for xs, ms in zip(s, m)]
        of = [[jnp.dot(x, vm, preferred_element_type=F32) for x, vm in zip(xs, vp)] for xs, vp in zip(e, vms)]
        for j, (ofa, ofb), (ma, mb) in zip(js, of, m):
            den = jnp.where(is_a, ofa[:, PAIR:], ofb[:, PAIR:])
            rows = item_rows(j, rate)
            ob_scr[br, rows, :] = (ofa[:, :PAIR] + ofb[:, :PAIR]) * (1.0 / den)
            lb_scr[br, rows, :] = jnp.where(is_a, ma, mb) + jnp.log(den)

    sets = (item_scr[:5], item_scr[5:])
    for j in range(nitem):
        prep(j, DIL_RATES[0], sets[0])
    for br in range(len(DIL_RATES)):
        next_rate = DIL_RATES[br + 1] if br + 1 < len(DIL_RATES) else None

        def body(i, carry, br=br, next_rate=next_rate):
            attend(i, br, sets[br % 2], next_rate, sets[(br + 1) % 2])
            return carry
        lax.fori_loop(0, nitem // ATTN_ITEMS_PER_ITER[br], body, 0)

    ones_bd = _head_ones(PAIR)
    gain = g_ref[...]
    tile = 256

    def merge(i, carry):
        rows = pl.ds(pl.multiple_of(i * tile, tile), tile)
        l0, l1, l2 = lb_scr[0, rows, :], lb_scr[1, rows, :], lb_scr[2, rows, :]
        m = jnp.maximum(jnp.maximum(l0, l1), l2)
        w0, w1, w2 = jnp.exp(l0 - m), jnp.exp(l1 - m), jnp.exp(l2 - m)
        o = (w0 * ob_scr[0, rows, :] + w1 * ob_scr[1, rows, :] + w2 * ob_scr[2, rows, :]) / (w0 + w1 + w2)
        ms = _split_dot_r(o * o, ones_bd, 2) * (1.0 / HEAD_DIM)
        o_ref[rows, :] = (o * lax.rsqrt(ms + NORM_EPS) * gain).astype(o_ref.dtype)
        return carry
    lax.fori_loop(0, seq // tile, merge, 0, unroll=2)


def _attn_prompt(p_main3, att_g):
    nb, seq, _ = p_main3.shape
    npair = C_GRP // PAIR
    col = lambda off: pl.BlockSpec((None, seq, PAIR), lambda b, p, off=off: (b, 0, off + p))
    return pl.pallas_call(
        functools.partial(_attn_prompt_kernel, seq=seq),
        grid=(nb, npair),
        in_specs=[col(0), col(npair), col(2 * npair), pl.BlockSpec((1, PAIR), lambda b, p: (0, p))],
        out_specs=[pl.BlockSpec((None, seq, PAIR), lambda b, p: (b, 0, p)),
                   pl.BlockSpec((None, PAIR, seq), lambda b, p: (b, p, 0)),
                   pl.BlockSpec((None, PAIR, seq), lambda b, p: (b, p, 0))],
        out_shape=[jax.ShapeDtypeStruct((nb, seq, C_GRP), BF16),
                   jax.ShapeDtypeStruct((nb, C_GRP, seq), F32), jax.ShapeDtypeStruct((nb, C_GRP, seq), F32)],
        scratch_shapes=[pltpu.VMEM((3, seq, PAIR), F32), pltpu.VMEM((3, seq, PAIR), F32)]
        + [pltpu.VMEM((seq // N_BACK, N_BACK, PAIR), BF16)] * 10,
        compiler_params=_cparams(2),
        name="attn_prompt",
    )(p_main3, p_main3, p_main3, att_g)


SAMPLE_HEADS_PER_STEP = 8


def _attn_sample_unit(grp, q_ref, kn_ref, vn_ref, kt_ref, vt_ref, g_ref, o_ref, *, win):
    base = grp * SAMPLE_HEADS_PER_STEP
    head_lane = lax.broadcasted_iota(jnp.int32, (1, N_HEADS), 1)
    pick = lambda x, hl: jnp.sum(jnp.where(head_lane == hl, x, 0.0), axis=-1, keepdims=True)
    q_all, kn_all, vn_all = q_ref[0], kn_ref[0], vn_ref[0]
    hs = range(SAMPLE_HEADS_PER_STEP)
    qc = [pick(q_all, base + h) * ATT_SCALE for h in hs]
    kn = [pick(kn_all, base + h) for h in hs]
    vn = [pick(vn_all, base + h) for h in hs]
    s_all = [jnp.sum(kt_ref[0, h] * qc[h], axis=0, keepdims=True) for h in hs]
    s_new = [jnp.sum(kn[h] * qc[h], axis=0, keepdims=True) for h in hs]
    outs, lses = [], []
    for rate in DIL_RATES:
        lo = win - N_BACK * rate
        s = [x[:, lo:] for x in s_all]
        if rate > 1:
            pos = lax.broadcasted_iota(jnp.int32, s[0].shape, 1)
            s = [jnp.where((pos & (rate - 1)) == 0, x, NEG_BIG) for x in s]
        m = [jnp.maximum(jnp.max(s[h], axis=-1, keepdims=True), s_new[h]) for h in hs]
        e = [jnp.exp(s[h] - m[h]) for h in hs]
        e_new = [jnp.exp(s_new[h] - m[h]) for h in hs]
        den = [jnp.sum(e[h], axis=-1, keepdims=True) + e_new[h] for h in hs]
        o = [(jnp.sum(vt_ref[0, h, :, lo:] * e[h], axis=-1, keepdims=True) + e_new[h] * vn[h]) / den[h] for h in hs]
        o_b, l_b = jnp.zeros((HEAD_DIM, N_HEADS), F32), jnp.zeros((1, N_HEADS), F32)
        for h in hs:
            o_b = o_b + jnp.where(head_lane == base + h, o[h], 0.0)
            l_b = l_b + jnp.where(head_lane == base + h, m[h] + jnp.log(den[h]), 0.0)
        outs.append(o_b)
        lses.append(l_b)
    m = jnp.maximum(jnp.maximum(lses[0], lses[1]), lses[2])
    ws = [jnp.exp(l - m) for l in lses]
    o = (ws[0] * outs[0] + ws[1] * outs[1] + ws[2] * outs[2]) / (ws[0] + ws[1] + ws[2])
    ms = jnp.mean(o * o, axis=0, keepdims=True)
    mine = (head_lane >= base) & (head_lane < base + SAMPLE_HEADS_PER_STEP)
    out = jnp.where(mine, o * lax.rsqrt(ms + NORM_EPS) * g_ref[...], 0.0)

    @pl.when(grp == 0)
    def _():
        o_ref[0] = out

    @pl.when(grp != 0)
    def _():
        o_ref[0] += out


def _split_pieces(x, terms):
    pieces, rem = [], x
    for _ in range(terms):
        piece = rem.astype(BF16)
        pieces.append(piece)
        rem = rem - piece.astype(F32)
    return pieces


def _dot_pieces(pieces, mat01):
    return jnp.dot(jnp.concatenate(pieces, axis=1), jnp.concatenate([mat01] * len(pieces), axis=0),
                   preferred_element_type=F32)


def _rwkv_token_stages(inputs, prm, ones_bd):
    xr, xk, xv, xl = inputs()
    act_w, act_a = jnp.tanh(xl[:, :2 * W_LORA]).astype(BF16), xl[:, :2 * W_LORA].astype(BF16)
    act_g = _sigmoid(xl[:, 2 * W_LORA:]).astype(BF16)
    kk = xk * prm["k_k"]
    kk_sq = _split_pieces(kk * kk, 2)
    yield None
    w_raw = prm["w0"] + jnp.dot(act_w, prm["w_up"], preferred_element_type=F32)
    a_pre = prm["a0"] + jnp.dot(act_a, prm["a_up"], preferred_element_type=F32)
    gate = jnp.dot(act_g, prm["g_up"], preferred_element_type=F32)
    nrm_sq = _dot_pieces(kk_sq, ones_bd)
    yield None
    logw = -jnp.exp(-_softplus(-w_raw) - 0.5)
    a_sig = _sigmoid(a_pre)
    kk = kk / jnp.maximum(jnp.sqrt(nrm_sq), 1e-12)
    k_eff = xk * (1.0 + (a_sig - 1.0) * prm["k_a"])
    rk = _split_pieces(xr * k_eff * prm["r_k"], 2)
    yield None
    bonus = _dot_pieces(rk, ones_bd) * xv
    yield dict(r=xr, k=k_eff, v=xv, a=-kk, b=kk * a_sig, logw=logw, gate=gate, bonus=bonus)


def _rwkv_token_math(xr, xk, xv, xl, prm, ones_bd):
    *_, tok = _rwkv_token_stages(lambda: (xr, xk, xv, xl), prm, ones_bd)
    return tok


def _group_norm_gate(y, bonus, gate, lnx_w, lnx_b, ones_bd):
    mean = _split_dot_r(y, ones_bd, 2) * (1.0 / HEAD_DIM)
    d = y - mean
    var = _split_dot_r(d * d, ones_bd, 2) * (1.0 / HEAD_DIM)
    yn = d * lax.rsqrt(var + LNX_EPS) * lnx_w + lnx_b
    return (yn + bonus) * gate


_RW_VEC_NAMES = ("w0", "a0", "k_k", "k_a", "r_k", "lnx_w", "lnx_b")


def _rwkv_prompt_kernel(r_ref, k_ref, v_ref, l_ref, pm_ref, pl_ref, mu_m_ref, mu_l_ref, vec_ref,
                        wup_ref, aup_ref, gup_ref, o_ref, h_ref,
                        tok_scr, y_scr, rw_scr, y0_scr, g_scr, ha_scr, pc_scr, *, seq):
    n = PAIR
    grows = PHASE_A_UNROLL * CHUNK
    pad = 8

    def token_group(g):
        first = isinstance(g, int) and g == 0
        prm = {name: vec_ref[i:i + 1, :] for i, name in enumerate(_RW_VEC_NAMES)}
        prm.update(w_up=wup_ref[...], a_up=aup_ref[...], g_up=gup_ref[...])
        if first:
            rows = pl.ds(0, grows)
            row0 = lax.broadcasted_iota(jnp.int32, (grows, 1), 0) == 0

            def lerp(ref, prev, mu):
                x = ref[rows, :]
                return x + mu * (jnp.where(row0, prev, pltpu.roll(x, 1, axis=0)) - x)
        else:
            rows = pl.ds(pl.multiple_of(g * grows, grows), grows)
            ext_rows = pl.ds(pl.multiple_of(g * grows - pad, pad), grows + pad)

            def lerp(ref, prev, mu):
                xe = ref[ext_rows, :]
                return xe[pad:] + mu * (pltpu.roll(xe, 1, axis=0)[pad:] - xe[pad:])
        def inputs():
            pm, mu_m = pm_ref[...], mu_m_ref[...]
            return (lerp(r_ref, pm[0:1], mu_m[0:1]), lerp(k_ref, pm[1:2], mu_m[1:2]),
                    lerp(v_ref, pm[2:3], mu_m[2:3]), lerp(l_ref, pl_ref[...], mu_l_ref[...]))

        tok = None
        for tok in _rwkv_token_stages(inputs, prm, _head_ones(n)):
            if tok is None:
                yield
        for i, name in enumerate(("r", "k", "v", "a", "b", "logw", "gate", "bonus")):
            tok_scr[i, rows, :] = tok[name]

    nch = seq // CHUNK
    lane = lax.broadcasted_iota(jnp.int32, (CHUNK, n), 1)
    is_a = lane < HEAD_DIM
    ri = lax.broadcasted_iota(jnp.int32, (n, n), 0)
    ci = lax.broadcasted_iota(jnp.int32, (n, n), 1)
    strict = ri > ci
    incl = ri >= ci
    eye = ri == ci
    rc = lax.broadcasted_iota(jnp.int32, (CHUNK, CHUNK), 0)
    cc = lax.broadcasted_iota(jnp.int32, (CHUNK, CHUNK), 1)

    def stack(x):
        return jnp.concatenate([jnp.where(is_a, x, 0.0), jnp.where(is_a, 0.0, x)], axis=0)

    def phase_b_step(c):
        rows = pl.ds(pl.multiple_of(c * CHUNK, CHUNK), CHUNK)
        ht = h_ref[...]
        hb = ht.astype(BF16)
        ys = _bdot_nt(rw_scr[c], hb) + y0_scr[c]
        y_scr[rows, :] = ys[:CHUNK, :] + ys[CHUNK:, :]
        h_ref[...] = ht * pc_scr[c] + _bdot(hb, g_scr[c]) + ha_scr[c]

    def group(ga, gb, gt=None):
        b_todo = [] if gb is None else [gb * PHASE_A_UNROLL + u for u in range(PHASE_A_UNROLL)]

        def fill():
            if b_todo:
                phase_b_step(b_todo.pop(0))

        if ga is None:
            while b_todo:
                fill()
            return
        cs = [ga * PHASE_A_UNROLL + u for u in range(PHASE_A_UNROLL)]
        each = lambda fn, *lists: [fn(*xs) for xs in zip(*lists)]
        tril_ones = jnp.where(rc >= cc, 1.0, 0.0).astype(BF16)
        tok = [[tok_scr[i, pl.ds(pl.multiple_of(c * CHUNK, CHUNK), CHUNK), :] for i in range(6)] for c in cs]
        r_c, k_c, v_c, a_c, b_c, lw_c = [list(x) for x in zip(*tok)]
        token_gen = iter(()) if gt is None else token_group(gt)
        token_step = lambda: next(token_gen, None)
        token_step()
        pairs = [_split_dot(tril_ones, jnp.concatenate(lw_c[u:u + 2], axis=1), 3)
                 for u in range(0, PHASE_A_UNROLL, 2)]
        lcum = [p[:, h * n:(h + 1) * n] for p in pairs for h in range(2)]
        lend = each(lambda l: l[CHUNK - 1:CHUNK, :], lcum)
        fill()
        sb = lambda x: stack(x).astype(BF16)
        a_s = each(lambda a, l, lw: sb(a * jnp.exp(l - lw)), a_c, lcum, lw_c)
        r_s = each(lambda r, l: sb(r * jnp.exp(l)), r_c, lcum)
        b_s = each(lambda b, l: sb(b * jnp.exp(-l)), b_c, lcum)
        k_s = each(lambda k, l: sb(k * jnp.exp(-l)), k_c, lcum)
        v_s = each(sb, v_c)
        be_s = each(lambda b, l, le: sb(b * jnp.exp(le - l)), b_c, lcum, lend)
        ke_s = each(lambda k, l, le: sb(k * jnp.exp(le - l)), k_c, lcum, lend)

        sc = each(lambda a, r, b, k: _bdot_nt(jnp.concatenate([a, r], axis=0), jnp.concatenate([b, k], axis=0)),
                  a_s, r_s, b_s, k_s)
        fill()
        token_step()
        s_ab = each(lambda s: jnp.where(strict, s[:n, :n], 0.0), sc)
        s_ak = each(lambda s: jnp.where(strict, s[:n, n:], 0.0).astype(BF16), sc)
        s_rb = each(lambda s: jnp.where(incl, s[n:, :n], 0.0).astype(BF16), sc)
        s_rk = each(lambda s: jnp.where(incl, s[n:, n:], 0.0).astype(BF16), sc)

        tinv = each(lambda s: jnp.where(eye, 1.0, 0.0) + s, s_ab)
        apow = each(lambda s: _bdot(s, s).astype(BF16), s_ab)
        fill()
        m = 2
        while 2 * m < CHUNK:
            res = each(lambda p, t: _bdot(p, jnp.concatenate([t.astype(BF16), p], axis=1)), apow, tinv)
            tinv = each(lambda t, x: t + x[:, :n], tinv, res)
            apow = each(lambda x: x[:, n:].astype(BF16), res)
            m *= 2
            fill()
            if m in (4, 16):
                token_step()
        tinv = each(lambda t, p: t + _bdot(p, t), tinv, apow)

        x1 = each(_bdot, s_ak, v_s)
        wu = each(lambda t, a, x: _bdot(t, jnp.concatenate([a, x.astype(BF16)], axis=1)), tinv, a_s, x1)
        fill()
        yk = each(_bdot, s_rk, v_s)
        ry = each(lambda s, w: _bdot(s, w), s_rb, wu)
        gh = each(lambda w, be: _bdot(w.T, be), wu, be_s)
        hk = each(lambda v, ke: _bdot_tn(v, ke), v_s, ke_s)
        while b_todo:
            fill()
        for _ in token_gen:
            pass
        for u, c in enumerate(cs):
            rw_scr[c] = ry[u][:, :n] + r_s[u].astype(F32)
            y0_scr[c] = ry[u][:, n:] + yk[u]
            g_scr[c] = gh[u][:n, :]
            ha_scr[c] = gh[u][n:, :] + hk[u]
            pc_scr[c] = jnp.exp(lend[u])

    h_ref[...] = jnp.zeros((n, n), F32)
    ngroup = nch // PHASE_A_UNROLL
    assert ngroup >= 3
    for _ in token_group(0):
        pass
    group(0, None, 1)

    def pipelined(g, carry):
        group(g, g - 1, g + 1)
        return carry
    lax.fori_loop(1, ngroup - 1, pipelined, 0)
    group(ngroup - 1, ngroup - 2)
    group(None, ngroup - 1)

    iw, ib = _RW_VEC_NAMES.index("lnx_w"), _RW_VEC_NAMES.index("lnx_b")
    o_ref[...] = _group_norm_gate(y_scr[...], tok_scr[7], tok_scr[6], vec_ref[iw:iw + 1, :], vec_ref[ib:ib + 1, :],
                                  _head_ones(n)).astype(o_ref.dtype)


def _rwkv_prompt(p_main3, p_lora3, prev_main, prev_lora, mu_main, mu_lora, vecs, w_up, a_up, g_up):
    nb, seq, _ = p_main3.shape
    npair = C_GRP // PAIR
    col = lambda off: pl.BlockSpec((None, seq, PAIR), lambda b, p, off=off: (b, 0, off + p))
    pcol = lambda rows: pl.BlockSpec((rows, PAIR), lambda b, p: (0, p))
    return pl.pallas_call(
        functools.partial(_rwkv_prompt_kernel, seq=seq),
        grid=(nb, npair),
        in_specs=[
            col(3 * npair), col(4 * npair), col(5 * npair),
            pl.BlockSpec((None, seq, C_LORA), lambda b, p: (b, 0, 0)),
            pl.BlockSpec((None, 3, PAIR), lambda b, p: (b, 0, p)),
            pl.BlockSpec((None, 1, C_LORA), lambda b, p: (b, 0, 0)),
            pcol(3), pl.BlockSpec((1, C_LORA), lambda b, p: (0, 0)), pcol(len(_RW_VEC_NAMES)),
            pcol(2 * W_LORA), pcol(2 * A_LORA), pcol(G_LORA),
        ],
        out_specs=[
            pl.BlockSpec((None, seq, PAIR), lambda b, p: (b, 0, p)),
            pl.BlockSpec((None, None, PAIR, PAIR), lambda b, p: (b, p, 0, 0)),
        ],
        out_shape=[jax.ShapeDtypeStruct((nb, seq, C_GRP), BF16),
                   jax.ShapeDtypeStruct((nb, npair, PAIR, PAIR), F32)],
        scratch_shapes=[
            pltpu.VMEM((8, seq, PAIR), F32), pltpu.VMEM((seq, PAIR), F32),
            pltpu.VMEM((seq // CHUNK, PAIR, PAIR), F32), pltpu.VMEM((seq // CHUNK, PAIR, PAIR), F32),
            pltpu.VMEM((seq // CHUNK, PAIR, PAIR), F32), pltpu.VMEM((seq // CHUNK, PAIR, PAIR), F32),
            pltpu.VMEM((seq // CHUNK, 1, PAIR), F32),
        ],
        compiler_params=_cparams(2),
        name="rwkv_prompt",
    )(p_main3, p_main3, p_main3, p_lora3, prev_main, prev_lora, mu_main, mu_lora, vecs, w_up, a_up, g_up)


def _rwkv_sample_tok_kernel(pm_ref, l_ref, pvm_ref, pvl_ref, mu_m_ref, mu_l_ref, vec_ref,
                            wup_ref, aup_ref, gup_ref, out_ref):
    ones_bd = _head_ones(C_GRP)
    prm = {name: vec_ref[i:i + 1, :] for i, name in enumerate(_RW_VEC_NAMES)}
    prm.update(w_up=wup_ref[...], a_up=aup_ref[...], g_up=gup_ref[...])
    lerp = lambda x, prev, mu: x + mu * (prev - x)
    xs = [lerp(pm_ref[:, i * C_GRP:(i + 1) * C_GRP], pvm_ref[:, i * C_GRP:(i + 1) * C_GRP],
               mu_m_ref[:, i * C_GRP:(i + 1) * C_GRP]) for i in range(3)]
    tok = _rwkv_token_math(xs[0], xs[1], xs[2], lerp(l_ref[...], pvl_ref[...], mu_l_ref[...]), prm, ones_bd)
    for i, name in enumerate(("r", "k", "v", "a", "b", "logw", "gate", "bonus")):
        out_ref[i] = tok[name]


def _rwkv_sample_step_kernel(s_ref, row_ref, col_ref, s_out_ref, o_ref):
    head_lane = lax.broadcasted_iota(jnp.int32, (1, N_HEADS), 1)
    heads = range(N_HEADS)
    row = lambda i: [row_ref[0, i, h:h + 1, :] for h in heads]
    r, k, a, b, logw = [row(i) for i in range(5)]
    v = [col_ref[0, 0, :, h:h + 1] for h in heads]
    s = [s_ref[0, h] for h in heads]
    sa = [jnp.sum(s[h] * a[h], axis=-1, keepdims=True) for h in heads]
    s_new = [s[h] * jnp.exp(logw[h]) + sa[h] * b[h] + v[h] * k[h] for h in heads]
    for h in heads:
        s_out_ref[0, h] = s_new[h]
    y = jnp.zeros((HEAD_DIM, N_HEADS), F32)
    for h in heads:
        y = y + jnp.where(head_lane == h, jnp.sum(s_new[h] * r[h], axis=-1, keepdims=True), 0.0)
    gate, bonus, lnx_w, lnx_b = [col_ref[0, i] for i in range(1, 5)]
    mean = jnp.mean(y, axis=0, keepdims=True)
    d = y - mean
    var = jnp.mean(d * d, axis=0, keepdims=True)
    o_ref[0] = (d * lax.rsqrt(var + LNX_EPS) * lnx_w + lnx_b + bonus) * gate


def _rwkv_sample(pm_s, pl_s, prev_main, prev_lora, mu_main, mu_lora, vecs, w_up, a_up, g_up, wkv0):
    nb = pm_s.shape[0]
    full = lambda shape: pl.BlockSpec(shape, lambda i: (0,) * len(shape))
    tok = pl.pallas_call(
        _rwkv_sample_tok_kernel,
        grid=(1,),
        in_specs=[pl.BlockSpec((nb, 3 * C_GRP), lambda i: (0, 1)), full((nb, C_LORA)),
                  full((nb, 3 * C_GRP)), full((nb, C_LORA)), full((1, 3 * C_GRP)), full((1, C_LORA)),
                  full(vecs.shape), full(w_up.shape), full(a_up.shape), full(g_up.shape)],
        out_specs=full((8, nb, C_GRP)),
        out_shape=jax.ShapeDtypeStruct((8, nb, C_GRP), F32),
        compiler_params=_cparams(1),
        name="rwkv_sample_tok",
    )(pm_s, pl_s, prev_main, prev_lora, mu_main, mu_lora, vecs, w_up, a_up, g_up)
    heads = lambda x: x.reshape(x.shape[0], nb, N_HEADS, HEAD_DIM).transpose(1, 0, 2, 3)
    rows = heads(jnp.concatenate([tok[0:2], tok[3:6]], axis=0))
    lnx = jnp.broadcast_to(vecs[5:7, None, :], (2, nb, C_GRP))
    cols = jnp.swapaxes(heads(jnp.concatenate([tok[2:3], tok[6:8], lnx], axis=0)), -1, -2)
    st = (1, N_HEADS, HEAD_DIM, HEAD_DIM)
    s_new, o = pl.pallas_call(
        _rwkv_sample_step_kernel,
        grid=(nb,),
        in_specs=[pl.BlockSpec(st, lambda b: (b, 0, 0, 0)),
                  pl.BlockSpec((1, 5, N_HEADS, HEAD_DIM), lambda b: (b, 0, 0, 0)),
                  pl.BlockSpec((1, 5, HEAD_DIM, N_HEADS), lambda b: (b, 0, 0, 0))],
        out_specs=[pl.BlockSpec(st, lambda b: (b, 0, 0, 0)),
                   pl.BlockSpec((1, HEAD_DIM, N_HEADS), lambda b: (b, 0, 0))],
        out_shape=[jax.ShapeDtypeStruct((nb,) + st[1:], F32),
                   jax.ShapeDtypeStruct((nb, HEAD_DIM, N_HEADS), F32)],
        compiler_params=_cparams(1),
        name="rwkv_sample_step",
    )(wkv0, rows, cols)
    return jnp.swapaxes(o, -1, -2).reshape(nb, C_GRP), s_new


def _out_proj_kernel(x_ref, oa_ref, orw_ref, wa_ref, wb_ref, o_ref):
    o_ref[...] = (x_ref[...] + _bdot(oa_ref[...], wa_ref[...]) + _bdot(orw_ref[...], wb_ref[...]))


def _out_proj(x2d, o_att, o_rw, w_o, tm, tn):
    m = x2d.shape[0]
    return pl.pallas_call(
        _out_proj_kernel,
        grid=(m // tm, D_MODEL // tn),
        in_specs=[
            pl.BlockSpec((tm, tn), lambda i, j: (i, j)),
            pl.BlockSpec((tm, C_GRP), lambda i, j: (i, 0)),
            pl.BlockSpec((tm, C_GRP), lambda i, j: (i, 0)),
            pl.BlockSpec((C_GRP, tn), lambda i, j: (0, j)),
            pl.BlockSpec((C_GRP, tn), lambda i, j: (1, j)),
        ],
        out_specs=pl.BlockSpec((tm, tn), lambda i, j: (i, j)),
        out_shape=jax.ShapeDtypeStruct((m, D_MODEL), F32),
        compiler_params=_cparams(2),
        name="out_proj",
    )(x2d, o_att, o_rw, w_o, w_o)


HALO = 16


def _ffn_kernel(*refs, tm, tiles_per_seq, seq_mode):
    if seq_mode:
        (x_ref, xh_ref, g_ref, wg_ref, wv_ref, cwg_ref, cwv_ref, wd_ref, gf_ref,
         y_ref, ug_ref, uv_ref, h_scr, acc_scr) = refs
    else:
        (x_ref, pg_ref, pv_ref, g_ref, wg_ref, wv_ref, cwg_ref, cwv_ref, wd_ref, gf_ref,
         y_ref, ug_ref, uv_ref, wgq_ref, wvq_ref, wdq_ref, h_scr, acc_scr) = refs
    i, j = pl.program_id(0), pl.program_id(1)

    def weight(w_ref, wq_ref):
        if seq_mode:
            return w_ref[...]
        w = w_ref[...].astype(BF16)
        wq_ref[...] = w
        return w

    def norm(x):
        ms = jnp.mean(x * x, axis=-1, keepdims=True)
        return x * lax.rsqrt(ms + NORM_EPS) * g_ref[...]

    @pl.when(j == 0)
    def _():
        x = x_ref[...]
        acc_scr[...] = x
        if seq_mode:
            keep = jnp.where(i % tiles_per_seq == 0, 0.0, 1.0).astype(F32)
            h_scr[:HALO, :] = (norm(xh_ref[...]) * keep).astype(BF16)
            h_scr[HALO:, :] = norm(x).astype(BF16)
        else:
            h_scr[...] = norm(x).astype(BF16)

    h = h_scr[...]
    halves = []
    for w_ref, wq_ref, cw_ref, u_ref, p_ref in (
            (wg_ref, None if seq_mode else wgq_ref, cwg_ref, ug_ref, None if seq_mode else pg_ref),
            (wv_ref, None if seq_mode else wvq_ref, cwv_ref, uv_ref, None if seq_mode else pv_ref)):
        u = jnp.dot(h, weight(w_ref, wq_ref), preferred_element_type=F32)
        cw = cw_ref[...]
        if seq_mode:
            u_ref[0] = u[HALO + tm - 2:HALO + tm, :]
            c = (cw[3:4] + cw[0:1] * u[HALO - 2:HALO - 2 + tm] + cw[1:2] * u[HALO - 1:HALO - 1 + tm]
                 + cw[2:3] * u[HALO:HALO + tm])
        else:
            u_ref[...] = u
            c = cw[3:4] + cw[0:1] * p_ref[0] + cw[1:2] * p_ref[1] + cw[2:3] * u
        halves.append(c)
    gate, val = halves
    act = gate * _sigmoid(gate) * val
    acc_scr[...] += jnp.dot(act.astype(BF16), weight(wd_ref, None if seq_mode else wdq_ref),
                            preferred_element_type=F32)

    @pl.when(j == pl.num_programs(1) - 1)
    def _():
        x2 = acc_scr[...]
        ms = jnp.mean(x2 * x2, axis=-1, keepdims=True)
        y_ref[...] = x2 * lax.rsqrt(ms + NORM_EPS) * gf_ref[...]


def _ffn(x1, norm_g, w_gate, w_val, conv_wb, w_down, final_g, tm, tf, seq_len=None, prev=None):
    m = x1.shape[0]
    nf = D_FF // tf
    seq_mode = prev is None
    assert seq_mode or m == tm, "state mode emits each bf16 weight tile once: it needs a single row tile"
    val_off = 0 if seq_mode else nf
    tiles_per_seq = seq_len // tm if seq_mode else 1
    vec = lambda width: pl.BlockSpec((1, width), lambda i, j: (0, 0))
    in_specs = [pl.BlockSpec((tm, D_MODEL), lambda i, j: (i, 0))]
    args = [x1]
    if seq_mode:
        in_specs.append(pl.BlockSpec((HALO, D_MODEL), lambda i, j: (jnp.maximum(i * (tm // HALO) - 1, 0), 0)))
        args.append(x1)
    else:
        in_specs += [pl.BlockSpec((2, tm, tf), lambda i, j: (0, i, j)),
                     pl.BlockSpec((2, tm, tf), lambda i, j: (0, i, nf + j))]
        args += [prev, prev]
    in_specs += [
        vec(D_MODEL),
        pl.BlockSpec((D_MODEL, tf), lambda i, j: (0, j)),
        pl.BlockSpec((D_MODEL, tf), lambda i, j: (0, val_off + j)),
        pl.BlockSpec((4, tf), lambda i, j: (0, j)),
        pl.BlockSpec((4, tf), lambda i, j: (0, nf + j)),
        pl.BlockSpec((tf, D_MODEL), lambda i, j: (j, 0)),
        vec(D_MODEL),
    ]
    args += [norm_g, w_gate, w_val, conv_wb, conv_wb, w_down, final_g]
    out_specs = [pl.BlockSpec((tm, D_MODEL), lambda i, j: (i, 0))]
    out_shape = [jax.ShapeDtypeStruct((m, D_MODEL), F32)]
    if seq_mode:
        out_specs += [pl.BlockSpec((1, 2, tf), lambda i, j: (i, 0, j))] * 2
        out_shape += [jax.ShapeDtypeStruct((m // tm, 2, D_FF), F32)] * 2
    else:
        out_specs += [pl.BlockSpec((tm, tf), lambda i, j: (i, j))] * 2
        out_shape += [jax.ShapeDtypeStruct((m, D_FF), F32)] * 2
        out_specs += [pl.BlockSpec((D_MODEL, tf), lambda i, j: (0, j))] * 2 + [pl.BlockSpec((tf, D_MODEL), lambda i, j: (j, 0))]
        out_shape += [jax.ShapeDtypeStruct((D_MODEL, D_FF), BF16)] * 2 + [jax.ShapeDtypeStruct((D_FF, D_MODEL), BF16)]
    rows = tm + HALO if seq_mode else tm
    return pl.pallas_call(
        functools.partial(_ffn_kernel, tm=tm, tiles_per_seq=tiles_per_seq, seq_mode=seq_mode),
        grid=(m // tm, nf),
        in_specs=in_specs,
        out_specs=out_specs,
        out_shape=out_shape,
        scratch_shapes=[pltpu.VMEM((rows, D_MODEL), BF16), pltpu.VMEM((tm, D_MODEL), F32)],
        compiler_params=_cparams(2),
        name="ffn_seq" if seq_mode else "ffn_state",
    )(*args)


def _layer_params(l, norm_mix_g, w_in, att_out_g, rw_mu, rw_w0, rw_w_up, rw_a0, rw_a_up, rw_g_up, rw_k_k,
                  rw_k_a, rw_r_k, rw_lnx_w, rw_lnx_b, w_o, norm_ffn_g, ffn_w_up, ffn_conv_w, ffn_conv_b,
                  ffn_w_down):
    zeros = jnp.zeros((W_LORA, C_GRP), F32)
    return dict(
        norm_mix_g=norm_mix_g[l][None],
        w_main=w_in[l].astype(BF16),
        w_lora=w_in[l][:, C_MAIN:].astype(BF16),
        att_g=att_out_g[l][None],
        mu_main=rw_mu[l][:3 * C_GRP].reshape(3, C_GRP),
        mu_lora=rw_mu[l][None, 3 * C_GRP:],
        vecs=jnp.stack([rw_w0[l], rw_a0[l], rw_k_k[l], rw_k_a[l], rw_r_k[l], rw_lnx_w[l], rw_lnx_b[l]]),
        w_up=jnp.concatenate([rw_w_up[l], zeros]).astype(BF16),
        a_up=jnp.concatenate([zeros, rw_a_up[l]]).astype(BF16),
        g_up=rw_g_up[l].astype(BF16),
        w_o=w_o[l].astype(BF16),
        norm_ffn_g=norm_ffn_g[l][None],
        ffn_w_up=ffn_w_up[l],
        conv_wb=jnp.concatenate([ffn_conv_w[l], ffn_conv_b[l][None]]),
        ffn_w_down=ffn_w_down[l],
    )


def _prompt_layer(x2d, p_main, p_lora, nb, seq, lp, ffn_w_bf16, final_g):
    p_main3 = p_main.reshape(nb, seq, C_MAIN)
    p_lora3 = p_lora.reshape(nb, seq, C_LORA)
    o_att, kt_new, vt_new = _attn_prompt(p_main3, lp["att_g"])
    o_rw, h_fin = _rwkv_prompt(p_main3, p_lora3, jnp.zeros((nb, 3, C_GRP), F32), jnp.zeros((nb, 1, C_LORA), F32),
                               lp["mu_main"], lp["mu_lora"], lp["vecs"], lp["w_up"], lp["a_up"], lp["g_up"])
    x1 = _out_proj(x2d, o_att.reshape(nb * seq, C_GRP), o_rw.reshape(nb * seq, C_GRP), lp["w_o"], tm=512, tn=D_MODEL)
    w_gate, w_val, w_down = ffn_w_bf16
    y, u_g, u_v = _ffn(x1, lp["norm_ffn_g"], w_gate, w_val, lp["conv_wb"], w_down, final_g,
                       tm=512, tf=512, seq_len=seq)
    hd = (N_HEADS, HEAD_DIM)
    k_new = kt_new.reshape((nb,) + hd + (seq,)).transpose(0, 3, 1, 2)
    v_new = vt_new.reshape((nb,) + hd + (seq,)).transpose(0, 3, 1, 2)
    rw_last = jnp.concatenate([p_main3[:, -1:, 3 * C_GRP:], p_lora3[:, -1:, :]], axis=-1)
    wkv = jnp.stack([h_fin[:, :, :HEAD_DIM, :HEAD_DIM], h_fin[:, :, HEAD_DIM:, HEAD_DIM:]], axis=2)
    wkv = wkv.reshape(nb, N_HEADS, HEAD_DIM, HEAD_DIM)
    tiles_per_seq = u_g.shape[0] // nb
    ffn_last = jnp.concatenate([u_g, u_v], axis=-1)[tiles_per_seq - 1::tiles_per_seq]
    return y.reshape(nb, seq, D_MODEL), k_new, v_new, rw_last, wkv, ffn_last


def _layer(x_prompt, x_sample, cache_k, cache_v, rw_prev, wkv0, ffn_prev, lp, final_g):
    nb, hd = x_sample.shape[0], (N_HEADS, HEAD_DIM)
    x2d = x_sample.reshape(nb, D_MODEL)
    p_main, p_lora = _in_proj(x2d, lp["norm_mix_g"], lp["w_main"], lp["w_lora"], tm=nb, tn=1024)
    q = p_main[:, :C_GRP].reshape((nb,) + hd)
    k_new = p_main[:, C_GRP:2 * C_GRP].reshape((nb,) + hd)
    v_new = p_main[:, 2 * C_GRP:3 * C_GRP].reshape((nb,) + hd)
    cache_kt, cache_vt = cache_k.transpose(0, 2, 3, 1), cache_v.transpose(0, 2, 3, 1)
    head_minor = lambda t: jnp.swapaxes(t, -1, -2)
    nbp, seq, _ = x_prompt.shape
    xp2d = x_prompt.reshape(nbp * seq, D_MODEL)
    pp_main, pp_lora, o_att = _in_proj(
        xp2d, lp["norm_mix_g"], lp["w_main"], lp["w_lora"], tm=1024, tn=768,
        sample_attn=(head_minor(q), head_minor(k_new), head_minor(v_new), cache_kt, cache_vt,
                     head_minor(lp["att_g"].reshape(hd))))
    o_att = head_minor(o_att).reshape(nb, C_GRP)
    prev = rw_prev.reshape(nb, C_SHIFT)
    o_rw, wkv = _rwkv_sample(p_main, p_lora, prev[:, :3 * C_GRP], prev[:, 3 * C_GRP:],
                             lp["mu_main"].reshape(1, 3 * C_GRP), lp["mu_lora"], lp["vecs"],
                             lp["w_up"], lp["a_up"], lp["g_up"], wkv0)
    x1 = _out_proj(x2d, o_att, o_rw, lp["w_o"], tm=nb, tn=1024)
    prev_rows = ffn_prev.transpose(1, 0, 2)
    y, u_g, u_v, *ffn_w_bf16 = _ffn(x1, lp["norm_ffn_g"], lp["ffn_w_up"], lp["ffn_w_up"], lp["conv_wb"],
                                    lp["ffn_w_down"], final_g, tm=nb, tf=512, prev=prev_rows)
    rw_last = jnp.concatenate([p_main[:, 3 * C_GRP:], p_lora], axis=-1)[:, None, :]
    ffn_last = jnp.stack([ffn_prev[:, 1, :], jnp.concatenate([u_g, u_v], axis=-1)], axis=1)
    sample_out = (y.reshape(nb, 1, D_MODEL), k_new[:, None], v_new[:, None], rw_last, wkv, ffn_last)
    return _prompt_layer(xp2d, pp_main, pp_lora, nbp, seq, lp, ffn_w_bf16, final_g), sample_out


def kernel(x_prompt, x_sample, cache_att_k, cache_att_v, state_rwkv_shift, state_rwkv_wkv, state_ffn_conv, norm_mix_g, w_in, att_out_g, rw_mu, rw_w0, rw_w_up, rw_a0, rw_a_up, rw_g_up, rw_k_k, rw_k_a, rw_r_k, rw_lnx_w, rw_lnx_b, w_o, norm_ffn_g, ffn_w_up, ffn_conv_w, ffn_conv_b, ffn_w_down, norm_final_g):
    depth = w_in.shape[0]
    assert depth == 1, "the fused FFN + final-norm kernel assumes a single trunk layer"
    assert x_sample.shape[1] == 1, "the sample path handles one new token per sequence"
    final_g = norm_final_g[None]
    lp = _layer_params(0, norm_mix_g, w_in, att_out_g, rw_mu, rw_w0, rw_w_up, rw_a0, rw_a_up, rw_g_up, rw_k_k,
                       rw_k_a, rw_r_k, rw_lnx_w, rw_lnx_b, w_o, norm_ffn_g, ffn_w_up, ffn_conv_w, ffn_conv_b,
                       ffn_w_down)
    (yp, pk, pv, prw, pwkv, pffn), (ys, sk, sv, srw, swkv, sffn) = _layer(
        x_prompt, x_sample, cache_att_k[0], cache_att_v[0], state_rwkv_shift[0], state_rwkv_wkv[0],
        state_ffn_conv[0], lp, final_g)
    lead = lambda t: t[None]
    return (yp, ys, lead(pk), lead(pv), lead(prw), lead(pwkv), lead(pffn),
            lead(sk), lead(sv), lead(srw), lead(swkv), lead(sffn))
```

```python
import functools

import jax
import jax.numpy as jnp
from jax import lax
from jax.experimental import pallas as pl
from jax.experimental.pallas import tpu as pltpu

F32 = jnp.float32
BF16 = jnp.bfloat16

D_MODEL = 2048
HEAD_DIM = 64
N_HEADS = 16
C_GRP = N_HEADS * HEAD_DIM
W_LORA, A_LORA, G_LORA = 64, 64, 160
C_LORA = W_LORA + A_LORA + G_LORA
C_SHIFT = 3 * C_GRP + C_LORA
C_MAIN = 6 * C_GRP
D_FF = 5632
DIL_RATES = (1, 4, 16)
N_BACK = 128
ATT_SCALE = HEAD_DIM ** -0.5
NORM_EPS = 1e-6
LNX_EPS = HEAD_DIM * 1e-5
NEG_BIG = -1e30

LANES = 128
VMEM_LIMIT = 48 * 1024 * 1024
FFN_VMEM_LIMIT = 52 * 1024 * 1024

CHUNK = 64
PHASE_A_UNROLL = 8
ATTN_ITEMS_PER_ITER = (4, 4, 8)
PAIR = 2 * HEAD_DIM


def _cparams(n_grid):
    return pltpu.CompilerParams(dimension_semantics=("arbitrary",) * n_grid,
                                vmem_limit_bytes=VMEM_LIMIT)


def _bdot(a, b):
    return jnp.dot(a.astype(BF16), b.astype(BF16), preferred_element_type=F32)


def _bdot_nt(a, b):
    return lax.dot_general(a.astype(BF16), b.astype(BF16), (((1,), (1,)), ((), ())),
                           preferred_element_type=F32)


def _bdot_tn(a, b):
    return jnp.dot(a.astype(F32).T.astype(BF16), b.astype(BF16), preferred_element_type=F32)


def _split_dot(mat01, x, terms):
    acc = None
    rem = x
    for _ in range(terms):
        piece = rem.astype(BF16)
        part = jnp.dot(mat01, piece, preferred_element_type=F32)
        acc = part if acc is None else acc + part
        rem = rem - piece.astype(F32)
    return acc


def _split_dot_r(x, mat01, terms):
    return _dot_pieces(_split_pieces(x, terms), mat01)


def _head_ones(n):
    r = lax.broadcasted_iota(jnp.int32, (n, n), 0) // HEAD_DIM
    c = lax.broadcasted_iota(jnp.int32, (n, n), 1) // HEAD_DIM
    return jnp.where(r == c, 1.0, 0.0).astype(BF16)


def _sigmoid(x):
    return 1.0 / (1.0 + jnp.exp(-x))


def _softplus(x):
    return jnp.maximum(x, 0.0) + jnp.log(1.0 + jnp.exp(-jnp.abs(x)))


def _in_proj_kernel(x_ref, g_ref, wm_ref, wl_ref, om_ref, ol_ref, h_scr):
    @pl.when(pl.program_id(1) == 0)
    def _():
        x = x_ref[...]
        ms = jnp.mean(x * x, axis=-1, keepdims=True)
        h = (x * lax.rsqrt(ms + NORM_EPS) * g_ref[...]).astype(BF16)
        h_scr[...] = h
        ol_ref[...] = jnp.dot(h, wl_ref[...], preferred_element_type=F32)

    om_ref[...] = jnp.dot(h_scr[...], wm_ref[...], preferred_element_type=F32)


def _in_proj(x2d, g, w_main, w_lora, tm, tn):
    m = x2d.shape[0]
    return pl.pallas_call(
        _in_proj_kernel,
        grid=(m // tm, C_MAIN // tn),
        in_specs=[
            pl.BlockSpec((tm, D_MODEL), lambda i, j: (i, 0)),
            pl.BlockSpec((1, D_MODEL), lambda i, j: (0, 0)),
            pl.BlockSpec((D_MODEL, tn), lambda i, j: (0, j)),
            pl.BlockSpec((D_MODEL, C_LORA), lambda i, j: (0, 0)),
        ],
        out_specs=[
            pl.BlockSpec((tm, tn), lambda i, j: (i, j)),
            pl.BlockSpec((tm, C_LORA), lambda i, j: (i, 0)),
        ],
        out_shape=[jax.ShapeDtypeStruct((m, C_MAIN), F32), jax.ShapeDtypeStruct((m, C_LORA), F32)],
        scratch_shapes=[pltpu.VMEM((tm, D_MODEL), BF16)],
        compiler_params=_cparams(2),
        name="in_proj",
    )(x2d, g, w_main, w_lora)


def _sample_attn_side_specs(sample_attn, unit_of):
    cache_kt = sample_attn[3]
    nb_s, win = cache_kt.shape[0], cache_kt.shape[-1]
    assert win % (N_BACK * max(DIL_RATES)) == 0 and win % LANES == 0
    hs = SAMPLE_HEADS_PER_STEP
    ngrp = N_HEADS // hs
    tok = pl.BlockSpec((1, HEAD_DIM, N_HEADS), lambda i, j: (unit_of(i, j) // ngrp, 0, 0))
    cache = pl.BlockSpec((1, hs, HEAD_DIM, win), lambda i, j: (unit_of(i, j) // ngrp, unit_of(i, j) % ngrp, 0, 0))
    in_specs = [tok, tok, tok, cache, cache, pl.BlockSpec((HEAD_DIM, N_HEADS), lambda i, j: (0, 0))]
    out_spec = pl.BlockSpec((1, 1, HEAD_DIM, N_HEADS), lambda i, j: (unit_of(i, j) // ngrp, unit_of(i, j) % ngrp, 0, 0))
    out_shape = jax.ShapeDtypeStruct((nb_s, ngrp, HEAD_DIM, N_HEADS), F32)
    return in_specs, out_spec, out_shape, win


def _attn_prompt_kernel(*refs, seq, ncast):
    q_ref, k_ref, v_ref, g_ref, *rest = refs
    cast_in, rest = rest[:ncast], rest[ncast:]
    o_ref, kt_out_ref, vt_out_ref, *rest = rest
    cast_out, (ob_scr, lb_scr, *item_scr) = rest[:ncast], rest[ncast:]
    for src, dst in zip(cast_in, cast_out):
        dst[...] = src[...].astype(BF16)
    blk = N_BACK
    nitem = seq // blk
    lane = lax.broadcasted_iota(jnp.int32, (blk, PAIR), 1)
    is_a = lane < HEAD_DIM

    row2 = lax.broadcasted_iota(jnp.int32, (blk, 2 * blk), 0)
    col2 = lax.broadcasted_iota(jnp.int32, (blk, 2 * blk), 1)
    band_bias = jnp.where((col2 >= row2) & (col2 <= row2 + N_BACK), 0.0, NEG_BIG)
    causal_bias = band_bias[:, blk:]

    def item_rows(j, rate):
        nblk = seq // rate // blk
        return pl.ds((j // nblk) + (j % nblk) * (blk * rate), blk, stride=rate)

    def prep(j, rate, dst):
        qa_scr, qb_scr, kt_scr, va_scr, vb_scr = dst
        rows = item_rows(j, rate)
        qs = q_ref[rows, :] * ATT_SCALE
        v = v_ref[rows, :]
        kt = k_ref[rows, :].T
        qa_scr[j] = jnp.where(is_a, qs, 0.0).astype(BF16)
        qb_scr[j] = jnp.where(is_a, 0.0, qs).astype(BF16)
        kt_scr[j] = kt.astype(BF16)
        va_scr[j] = jnp.where(is_a, v, 0.0).astype(BF16)
        vb_scr[j] = jnp.where(is_a, 0.0, v).astype(BF16)
        if rate == 1:
            kt_out_ref[:, j * blk:(j + 1) * blk] = kt
            vt_out_ref[:, j * blk:(j + 1) * blk] = v.T

    def attend(i, br, src, next_rate, dst):
        qa_scr, qb_scr, kt_scr, va_scr, vb_scr = src
        rate = DIL_RATES[br]
        nblk = seq // rate // blk
        js = [i * ATTN_ITEMS_PER_ITER[br] + u for u in range(ATTN_ITEMS_PER_ITER[br])]
        nk = blk if nblk == 1 else 2 * blk
        ones = jnp.ones((nk, PAIR), BF16)
        kts, vms, biases = [], [], []
        for j in js:
            if nblk == 1:
                biases.append(causal_bias)
                kts.append(kt_scr[j])
                vs = (va_scr[j], vb_scr[j])
            else:
                jp = jnp.maximum(j - 1, 0)
                lo = jnp.where(j % nblk == 0, blk, 0)
                biases.append(jnp.where(col2 >= lo, band_bias, NEG_BIG))
                kts.append(jnp.concatenate([kt_scr[jp], kt_scr[j]], axis=1))
                vs = (jnp.concatenate([va_scr[jp], va_scr[j]], axis=0),
                      jnp.concatenate([vb_scr[jp], vb_scr[j]], axis=0))
            vms.append([jnp.concatenate([v, ones], axis=1) for v in vs])
        qs = [(qa_scr[j], qb_scr[j]) for j in js]
        s = [[jnp.dot(q, kt, preferred_element_type=F32) + bias for q in qp]
             for qp, kt, bias in zip(qs, kts, biases)]
        if next_rate is not None:
            for j in js:
                prep(j, next_rate, dst)
        m = [[jnp.max(x, axis=-1, keepdims=True) for x in xs] for xs in s]
        e = [[jnp.exp(x - mx).astype(BF16) for x, mx in zip(xs, ms)] for xs, ms in zip(s, m)]
        of = [[jnp.dot(x, vm, preferred_element_type=F32) for x, vm in zip(xs, vp)] for xs, vp in zip(e, vms)]
        for j, (ofa, ofb), (ma, mb) in zip(js, of, m):
            den = jnp.where(is_a, ofa[:, PAIR:], ofb[:, PAIR:])
            rows = item_rows(j, rate)
            ob_scr[br, rows, :] = (ofa[:, :PAIR] + ofb[:, :PAIR]) * (1.0 / den)
            lb_scr[br, rows, :] = jnp.where(is_a, ma, mb) + jnp.log(den)

    sets = (item_scr[:5], item_scr[5:])
    for j in range(nitem):
        prep(j, DIL_RATES[0], sets[0])
    for br in range(len(DIL_RATES)):
        next_rate = DIL_RATES[br + 1] if br + 1 < len(DIL_RATES) else None

        def body(i, carry, br=br, next_rate=next_rate):
            attend(i, br, sets[br % 2], next_rate, sets[(br + 1) % 2])
            return carry
        lax.fori_loop(0, nitem // ATTN_ITEMS_PER_ITER[br], body, 0)

    ones_bd = _head_ones(PAIR)
    gain = g_ref[...]
    tile = 256

    def merge(i, carry):
        rows = pl.ds(pl.multiple_of(i * tile, tile), tile)
        l0, l1, l2 = lb_scr[0, rows, :], lb_scr[1, rows, :], lb_scr[2, rows, :]
        m = jnp.maximum(jnp.maximum(l0, l1), l2)
        w0, w1, w2 = jnp.exp(l0 - m), jnp.exp(l1 - m), jnp.exp(l2 - m)
        o = (w0 * ob_scr[0, rows, :] + w1 * ob_scr[1, rows, :] + w2 * ob_scr[2, rows, :]) / (w0 + w1 + w2)
        ms = _split_dot_r(o * o, ones_bd, 2) * (1.0 / HEAD_DIM)
        o_ref[rows, :] = (o * lax.rsqrt(ms + NORM_EPS) * gain).astype(o_ref.dtype)
        return carry
    lax.fori_loop(0, seq // tile, merge, 0, unroll=2)


def _attn_prompt(p_main3, att_g, cast_f32=()):
    nb, seq, _ = p_main3.shape
    npair = C_GRP // PAIR
    col = lambda off: pl.BlockSpec((None, seq, PAIR), lambda b, p, off=off: (b, 0, off + p))
    nstep = nb * npair
    cast_specs = []
    for w in cast_f32:
        assert w.shape[0] % (nstep * 16) == 0, "bf16 row slabs must be whole (16, 128) tiles"
        cast_specs.append(pl.BlockSpec((w.shape[0] // nstep, w.shape[1]), lambda b, p: (b * npair + p, 0)))
    return pl.pallas_call(
        functools.partial(_attn_prompt_kernel, seq=seq, ncast=len(cast_f32)),
        grid=(nb, npair),
        in_specs=[col(0), col(npair), col(2 * npair), pl.BlockSpec((1, PAIR), lambda b, p: (0, p))] + cast_specs,
        out_specs=[pl.BlockSpec((None, seq, PAIR), lambda b, p: (b, 0, p)),
                   pl.BlockSpec((None, PAIR, seq), lambda b, p: (b, p, 0)),
                   pl.BlockSpec((None, PAIR, seq), lambda b, p: (b, p, 0))] + cast_specs,
        out_shape=[jax.ShapeDtypeStruct((nb, seq, C_GRP), BF16),
                   jax.ShapeDtypeStruct((nb, C_GRP, seq), F32), jax.ShapeDtypeStruct((nb, C_GRP, seq), F32)]
        + [jax.ShapeDtypeStruct(w.shape, BF16) for w in cast_f32],
        scratch_shapes=[pltpu.VMEM((3, seq, PAIR), F32), pltpu.VMEM((3, seq, PAIR), F32)]
        + [pltpu.VMEM((seq // N_BACK, N_BACK, PAIR), BF16)] * 10,
        compiler_params=_cparams(2),
        name="attn_prompt",
    )(p_main3, p_main3, p_main3, att_g, *cast_f32)


SAMPLE_HEADS_PER_STEP = 4


def _attn_sample_unit(grp, q_ref, kn_ref, vn_ref, kt_ref, vt_ref, g_ref, o_ref, *, win):
    base = grp * SAMPLE_HEADS_PER_STEP
    head_lane = lax.broadcasted_iota(jnp.int32, (1, N_HEADS), 1)
    pick = lambda x, hl: jnp.sum(jnp.where(head_lane == hl, x, 0.0), axis=-1, keepdims=True)
    q_all, kn_all, vn_all = q_ref[0], kn_ref[0], vn_ref[0]
    hs = range(SAMPLE_HEADS_PER_STEP)
    qc = [pick(q_all, base + h) * ATT_SCALE for h in hs]
    kn = [pick(kn_all, base + h) for h in hs]
    vn = [pick(vn_all, base + h) for h in hs]
    s_all = [jnp.sum(kt_ref[0, h] * qc[h], axis=0, keepdims=True) for h in hs]
    s_new = [jnp.sum(kn[h] * qc[h], axis=0, keepdims=True) for h in hs]
    outs, lses = [], []
    for rate in DIL_RATES:
        lo = win - N_BACK * rate
        s = [x[:, lo:] for x in s_all]
        if rate > 1:
            pos = lax.broadcasted_iota(jnp.int32, s[0].shape, 1)
            s = [jnp.where((pos & (rate - 1)) == 0, x, NEG_BIG) for x in s]
        m = [jnp.maximum(jnp.max(s[h], axis=-1, keepdims=True), s_new[h]) for h in hs]
        e = [jnp.exp(s[h] - m[h]) for h in hs]
        e_new = [jnp.exp(s_new[h] - m[h]) for h in hs]
        den = [jnp.sum(e[h], axis=-1, keepdims=True) + e_new[h] for h in hs]
        o = [(jnp.sum(vt_ref[0, h, :, lo:] * e[h], axis=-1, keepdims=True) + e_new[h] * vn[h]) / den[h] for h in hs]
        o_b, l_b = jnp.zeros((HEAD_DIM, N_HEADS), F32), jnp.zeros((1, N_HEADS), F32)
        for h in hs:
            o_b = o_b + jnp.where(head_lane == base + h, o[h], 0.0)
            l_b = l_b + jnp.where(head_lane == base + h, m[h] + jnp.log(den[h]), 0.0)
        outs.append(o_b)
        lses.append(l_b)
    m = jnp.maximum(jnp.maximum(lses[0], lses[1]), lses[2])
    ws = [jnp.exp(l - m) for l in lses]
    o = (ws[0] * outs[0] + ws[1] * outs[1] + ws[2] * outs[2]) / (ws[0] + ws[1] + ws[2])
    ms = jnp.mean(o * o, axis=0, keepdims=True)
    mine = (head_lane >= base) & (head_lane < base + SAMPLE_HEADS_PER_STEP)
    o_ref[0, 0] = jnp.where(mine, o * lax.rsqrt(ms + NORM_EPS) * g_ref[...], 0.0)


def _split_pieces(x, terms):
    pieces, rem = [], x
    for _ in range(terms):
        piece = rem.astype(BF16)
        pieces.append(piece)
        rem = rem - piece.astype(F32)
    return pieces


def _dot_pieces(pieces, mat01):
    return jnp.dot(jnp.concatenate(pieces, axis=1), jnp.concatenate([mat01] * len(pieces), axis=0),
                   preferred_element_type=F32)


def _rwkv_token_stages(inputs, prm, ones_bd):
    xr, xk, xv, xl = inputs()
    act_w, act_a = jnp.tanh(xl[:, :2 * W_LORA]).astype(BF16), xl[:, :2 * W_LORA].astype(BF16)
    act_g = _sigmoid(xl[:, 2 * W_LORA:]).astype(BF16)
    kk = xk * prm["k_k"]
    kk_sq = _split_pieces(kk * kk, 2)
    yield None
    w_raw = prm["w0"] + jnp.dot(act_w, prm["w_up"], preferred_element_type=F32)
    a_pre = prm["a0"] + jnp.dot(act_a, prm["a_up"], preferred_element_type=F32)
    gate = jnp.dot(act_g, prm["g_up"], preferred_element_type=F32)
    nrm_sq = _dot_pieces(kk_sq, ones_bd)
    yield None
    logw = -jnp.exp(-_softplus(-w_raw) - 0.5)
    a_sig = _sigmoid(a_pre)
    kk = kk / jnp.maximum(jnp.sqrt(nrm_sq), 1e-12)
    k_eff = xk * (1.0 + (a_sig - 1.0) * prm["k_a"])
    rk = _split_pieces(xr * k_eff * prm["r_k"], 2)
    yield None
    bonus = _dot_pieces(rk, ones_bd) * xv
    yield dict(r=xr, k=k_eff, v=xv, a=-kk, b=kk * a_sig, logw=logw, gate=gate, bonus=bonus)


def _rwkv_token_math(xr, xk, xv, xl, prm, ones_bd):
    *_, tok = _rwkv_token_stages(lambda: (xr, xk, xv, xl), prm, ones_bd)
    return tok


def _group_norm_gate(y, bonus, gate, lnx_w, lnx_b, ones_bd):
    mean = _split_dot_r(y, ones_bd, 2) * (1.0 / HEAD_DIM)
    d = y - mean
    var = _split_dot_r(d * d, ones_bd, 2) * (1.0 / HEAD_DIM)
    yn = d * lax.rsqrt(var + LNX_EPS) * lnx_w + lnx_b
    return (yn + bonus) * gate


_RW_VEC_NAMES = ("w0", "a0", "k_k", "k_a", "r_k", "lnx_w", "lnx_b")


def _rwkv_prompt_kernel(r_ref, k_ref, v_ref, l_ref, pm_ref, pl_ref, mu_m_ref, mu_l_ref, vec_ref,
                        wup_ref, aup_ref, gup_ref, o_ref, h_ref,
                        tok_scr, y_scr, rw_scr, y0_scr, g_scr, ha_scr, pc_scr, *, seq):
    n = PAIR
    grows = PHASE_A_UNROLL * CHUNK
    pad = 8

    def token_group(g):
        first = isinstance(g, int) and g == 0
        prm = {name: vec_ref[i:i + 1, :] for i, name in enumerate(_RW_VEC_NAMES)}
        prm.update(w_up=wup_ref[...], a_up=aup_ref[...], g_up=gup_ref[...])
        if first:
            rows = pl.ds(0, grows)
            row0 = lax.broadcasted_iota(jnp.int32, (grows, 1), 0) == 0

            def lerp(ref, prev, mu):
                x = ref[rows, :]
                return x + mu * (jnp.where(row0, prev, pltpu.roll(x, 1, axis=0)) - x)
        else:
            rows = pl.ds(pl.multiple_of(g * grows, grows), grows)
            ext_rows = pl.ds(pl.multiple_of(g * grows - pad, pad), grows + pad)

            def lerp(ref, prev, mu):
                xe = ref[ext_rows, :]
                return xe[pad:] + mu * (pltpu.roll(xe, 1, axis=0)[pad:] - xe[pad:])
        def inputs():
            pm, mu_m = pm_ref[...], mu_m_ref[...]
            return (lerp(r_ref, pm[0:1], mu_m[0:1]), lerp(k_ref, pm[1:2], mu_m[1:2]),
                    lerp(v_ref, pm[2:3], mu_m[2:3]), lerp(l_ref, pl_ref[...], mu_l_ref[...]))

        tok = None
        for tok in _rwkv_token_stages(inputs, prm, _head_ones(n)):
            if tok is None:
                yield
        for i, name in enumerate(("r", "k", "v", "a", "b", "logw", "gate", "bonus")):
            tok_scr[i, rows, :] = tok[name]

    nch = seq // CHUNK
    lane = lax.broadcasted_iota(jnp.int32, (CHUNK, n), 1)
    is_a = lane < HEAD_DIM
    ri = lax.broadcasted_iota(jnp.int32, (n, n), 0)
    ci = lax.broadcasted_iota(jnp.int32, (n, n), 1)
    strict = ri > ci
    incl = ri >= ci
    eye = ri == ci
    rc = lax.broadcasted_iota(jnp.int32, (CHUNK, CHUNK), 0)
    cc = lax.broadcasted_iota(jnp.int32, (CHUNK, CHUNK), 1)

    def stack(x):
        return jnp.concatenate([jnp.where(is_a, x, 0.0), jnp.where(is_a, 0.0, x)], axis=0)

    def phase_b_step(c):
        rows = pl.ds(pl.multiple_of(c * CHUNK, CHUNK), CHUNK)
        ht = h_ref[...]
        hb = ht.astype(BF16)
        ys = _bdot_nt(rw_scr[c], hb) + y0_scr[c]
        y_scr[rows, :] = ys[:CHUNK, :] + ys[CHUNK:, :]
        h_ref[...] = ht * pc_scr[c] + _bdot(hb, g_scr[c]) + ha_scr[c]

    def group(ga, gb, gt=None):
        b_todo = [] if gb is None else [gb * PHASE_A_UNROLL + u for u in range(PHASE_A_UNROLL)]

        def fill():
            if b_todo:
                phase_b_step(b_todo.pop(0))

        if ga is None:
            while b_todo:
                fill()
            return
        cs = [ga * PHASE_A_UNROLL + u for u in range(PHASE_A_UNROLL)]
        each = lambda fn, *lists: [fn(*xs) for xs in zip(*lists)]
        tril_ones = jnp.where(rc >= cc, 1.0, 0.0).astype(BF16)
        tok = [[tok_scr[i, pl.ds(pl.multiple_of(c * CHUNK, CHUNK), CHUNK), :] for i in range(6)] for c in cs]
        r_c, k_c, v_c, a_c, b_c, lw_c = [list(x) for x in zip(*tok)]
        token_gen = iter(()) if gt is None else token_group(gt)
        token_step = lambda: next(token_gen, None)
        token_step()
        pairs = [_split_dot(tril_ones, jnp.concatenate(lw_c[u:u + 2], axis=1), 3)
                 for u in range(0, PHASE_A_UNROLL, 2)]
        lcum = [p[:, h * n:(h + 1) * n] for p in pairs for h in range(2)]
        lend = each(lambda l: l[CHUNK - 1:CHUNK, :], lcum)
        fill()
        sb = lambda x: stack(x).astype(BF16)
        a_s = each(lambda a, l, lw: sb(a * jnp.exp(l - lw)), a_c, lcum, lw_c)
        r_s = each(lambda r, l: sb(r * jnp.exp(l)), r_c, lcum)
        b_s = each(lambda b, l: sb(b * jnp.exp(-l)), b_c, lcum)
        k_s = each(lambda k, l: sb(k * jnp.exp(-l)), k_c, lcum)
        v_s = each(sb, v_c)
        be_s = each(lambda b, l, le: sb(b * jnp.exp(le - l)), b_c, lcum, lend)
        ke_s = each(lambda k, l, le: sb(k * jnp.exp(le - l)), k_c, lcum, lend)

        sc = each(lambda a, r, b, k: _bdot_nt(jnp.concatenate([a, r], axis=0), jnp.concatenate([b, k], axis=0)),
                  a_s, r_s, b_s, k_s)
        fill()
        token_step()
        s_ab = each(lambda s: jnp.where(strict, s[:n, :n], 0.0), sc)
        s_ak = each(lambda s: jnp.where(strict, s[:n, n:], 0.0).astype(BF16), sc)
        s_rb = each(lambda s: jnp.where(incl, s[n:, :n], 0.0).astype(BF16), sc)
        s_rk = each(lambda s: jnp.where(incl, s[n:, n:], 0.0).astype(BF16), sc)

        tinv = each(lambda s: jnp.where(eye, 1.0, 0.0) + s, s_ab)
        apow = each(lambda s: _bdot(s, s).astype(BF16), s_ab)
        fill()
        m = 2
        while 2 * m < CHUNK:
            res = each(lambda p, t: _bdot(p, jnp.concatenate([t.astype(BF16), p], axis=1)), apow, tinv)
            tinv = each(lambda t, x: t + x[:, :n], tinv, res)
            apow = each(lambda x: x[:, n:].astype(BF16), res)
            m *= 2
            fill()
            if m in (4, 16):
                token_step()
        tinv = each(lambda t, p: t + _bdot(p, t), tinv, apow)

        x1 = each(_bdot, s_ak, v_s)
        wu = each(lambda t, a, x: _bdot(t, jnp.concatenate([a, x.astype(BF16)], axis=1)), tinv, a_s, x1)
        fill()
        yk = each(_bdot, s_rk, v_s)
        ry = each(lambda s, w: _bdot(s, w), s_rb, wu)
        gh = each(lambda w, be: _bdot(w.T, be), wu, be_s)
        hk = each(lambda v, ke: _bdot_tn(v, ke), v_s, ke_s)
        while b_todo:
            fill()
        for _ in token_gen:
            pass
        for u, c in enumerate(cs):
            rw_scr[c] = ry[u][:, :n] + r_s[u].astype(F32)
            y0_scr[c] = ry[u][:, n:] + yk[u]
            g_scr[c] = gh[u][:n, :]
            ha_scr[c] = gh[u][n:, :] + hk[u]
            pc_scr[c] = jnp.exp(lend[u])

    h_ref[...] = jnp.zeros((n, n), F32)
    ngroup = nch // PHASE_A_UNROLL
    assert ngroup >= 3
    for _ in token_group(0):
        pass
    group(0, None, 1)

    def pipelined(g, carry):
        group(g, g - 1, g + 1)
        return carry
    lax.fori_loop(1, ngroup - 1, pipelined, 0)
    group(ngroup - 1, ngroup - 2)
    group(None, ngroup - 1)

    iw, ib = _RW_VEC_NAMES.index("lnx_w"), _RW_VEC_NAMES.index("lnx_b")
    o_ref[...] = _group_norm_gate(y_scr[...], tok_scr[7], tok_scr[6], vec_ref[iw:iw + 1, :], vec_ref[ib:ib + 1, :],
                                  _head_ones(n)).astype(o_ref.dtype)


def _rwkv_prompt(p_main3, p_lora3, prev_main, prev_lora, mu_main, mu_lora, vecs, w_up, a_up, g_up):
    nb, seq, _ = p_main3.shape
    npair = C_GRP // PAIR
    col = lambda off: pl.BlockSpec((None, seq, PAIR), lambda b, p, off=off: (b, 0, off + p))
    pcol = lambda rows: pl.BlockSpec((rows, PAIR), lambda b, p: (0, p))
    return pl.pallas_call(
        functools.partial(_rwkv_prompt_kernel, seq=seq),
        grid=(nb, npair),
        in_specs=[
            col(3 * npair), col(4 * npair), col(5 * npair),
            pl.BlockSpec((None, seq, C_LORA), lambda b, p: (b, 0, 0)),
            pl.BlockSpec((None, 3, PAIR), lambda b, p: (b, 0, p)),
            pl.BlockSpec((None, 1, C_LORA), lambda b, p: (b, 0, 0)),
            pcol(3), pl.BlockSpec((1, C_LORA), lambda b, p: (0, 0)), pcol(len(_RW_VEC_NAMES)),
            pcol(2 * W_LORA), pcol(2 * A_LORA), pcol(G_LORA),
        ],
        out_specs=[
            pl.BlockSpec((None, seq, PAIR), lambda b, p: (b, 0, p)),
            pl.BlockSpec((None, None, PAIR, PAIR), lambda b, p: (b, p, 0, 0)),
        ],
        out_shape=[jax.ShapeDtypeStruct((nb, seq, C_GRP), BF16),
                   jax.ShapeDtypeStruct((nb, npair, PAIR, PAIR), F32)],
        scratch_shapes=[
            pltpu.VMEM((8, seq, PAIR), F32), pltpu.VMEM((seq, PAIR), F32),
            pltpu.VMEM((seq // CHUNK, PAIR, PAIR), F32), pltpu.VMEM((seq // CHUNK, PAIR, PAIR), F32),
            pltpu.VMEM((seq // CHUNK, PAIR, PAIR), F32), pltpu.VMEM((seq // CHUNK, PAIR, PAIR), F32),
            pltpu.VMEM((seq // CHUNK, 1, PAIR), F32),
        ],
        compiler_params=_cparams(2),
        name="rwkv_prompt",
    )(p_main3, p_main3, p_main3, p_lora3, prev_main, prev_lora, mu_main, mu_lora, vecs, w_up, a_up, g_up)


def _rwkv_sample_tok_kernel(pm_ref, l_ref, pvm_ref, pvl_ref, mu_m_ref, mu_l_ref, vec_ref,
                            wup_ref, aup_ref, gup_ref, out_ref):
    ones_bd = _head_ones(C_GRP)
    prm = {name: vec_ref[i:i + 1, :] for i, name in enumerate(_RW_VEC_NAMES)}
    prm.update(w_up=wup_ref[...], a_up=aup_ref[...], g_up=gup_ref[...])
    lerp = lambda x, prev, mu: x + mu * (prev - x)
    xs = [lerp(pm_ref[:, i * C_GRP:(i + 1) * C_GRP], pvm_ref[:, i * C_GRP:(i + 1) * C_GRP],
               mu_m_ref[:, i * C_GRP:(i + 1) * C_GRP]) for i in range(3)]
    tok = _rwkv_token_math(xs[0], xs[1], xs[2], lerp(l_ref[...], pvl_ref[...], mu_l_ref[...]), prm, ones_bd)
    for i, name in enumerate(("r", "k", "v", "a", "b", "logw", "gate", "bonus")):
        out_ref[i] = tok[name]


def _rwkv_sample_step_kernel(s_ref, row_ref, col_ref, s_out_ref, o_ref):
    head_lane = lax.broadcasted_iota(jnp.int32, (1, N_HEADS), 1)
    heads = range(N_HEADS)
    row = lambda i: [row_ref[0, i, h:h + 1, :] for h in heads]
    r, k, a, b, logw = [row(i) for i in range(5)]
    v = [col_ref[0, 0, :, h:h + 1] for h in heads]
    s = [s_ref[0, h] for h in heads]
    sa = [jnp.sum(s[h] * a[h], axis=-1, keepdims=True) for h in heads]
    s_new = [s[h] * jnp.exp(logw[h]) + sa[h] * b[h] + v[h] * k[h] for h in heads]
    for h in heads:
        s_out_ref[0, h] = s_new[h]
    y = jnp.zeros((HEAD_DIM, N_HEADS), F32)
    for h in heads:
        y = y + jnp.where(head_lane == h, jnp.sum(s_new[h] * r[h], axis=-1, keepdims=True), 0.0)
    gate, bonus, lnx_w, lnx_b = [col_ref[0, i] for i in range(1, 5)]
    mean = jnp.mean(y, axis=0, keepdims=True)
    d = y - mean
    var = jnp.mean(d * d, axis=0, keepdims=True)
    o_ref[0] = (d * lax.rsqrt(var + LNX_EPS) * lnx_w + lnx_b + bonus) * gate


def _rwkv_sample(pm_s, pl_s, prev_main, prev_lora, mu_main, mu_lora, vecs, w_up, a_up, g_up, wkv0):
    nb = pm_s.shape[0]
    full = lambda shape: pl.BlockSpec(shape, lambda i: (0,) * len(shape))
    tok = pl.pallas_call(
        _rwkv_sample_tok_kernel,
        grid=(1,),
        in_specs=[pl.BlockSpec((nb, 3 * C_GRP), lambda i: (0, 1)), full((nb, C_LORA)),
                  full((nb, 3 * C_GRP)), full((nb, C_LORA)), full((1, 3 * C_GRP)), full((1, C_LORA)),
                  full(vecs.shape), full(w_up.shape), full(a_up.shape), full(g_up.shape)],
        out_specs=full((8, nb, C_GRP)),
        out_shape=jax.ShapeDtypeStruct((8, nb, C_GRP), F32),
        compiler_params=_cparams(1),
        name="rwkv_sample_tok",
    )(pm_s, pl_s, prev_main, prev_lora, mu_main, mu_lora, vecs, w_up, a_up, g_up)
    heads = lambda x: x.reshape(x.shape[0], nb, N_HEADS, HEAD_DIM).transpose(1, 0, 2, 3)
    rows = heads(jnp.concatenate([tok[0:2], tok[3:6]], axis=0))
    lnx = jnp.broadcast_to(vecs[5:7, None, :], (2, nb, C_GRP))
    cols = jnp.swapaxes(heads(jnp.concatenate([tok[2:3], tok[6:8], lnx], axis=0)), -1, -2)
    st = (1, N_HEADS, HEAD_DIM, HEAD_DIM)
    s_new, o = pl.pallas_call(
        _rwkv_sample_step_kernel,
        grid=(nb,),
        in_specs=[pl.BlockSpec(st, lambda b: (b, 0, 0, 0)),
                  pl.BlockSpec((1, 5, N_HEADS, HEAD_DIM), lambda b: (b, 0, 0, 0)),
                  pl.BlockSpec((1, 5, HEAD_DIM, N_HEADS), lambda b: (b, 0, 0, 0))],
        out_specs=[pl.BlockSpec(st, lambda b: (b, 0, 0, 0)),
                   pl.BlockSpec((1, HEAD_DIM, N_HEADS), lambda b: (b, 0, 0))],
        out_shape=[jax.ShapeDtypeStruct((nb,) + st[1:], F32),
                   jax.ShapeDtypeStruct((nb, HEAD_DIM, N_HEADS), F32)],
        compiler_params=_cparams(1),
        name="rwkv_sample_step",
    )(wkv0, rows, cols)
    return jnp.swapaxes(o, -1, -2).reshape(nb, C_GRP), s_new


def _out_proj_kernel(x_ref, oa_ref, orw_ref, wa_ref, wb_ref, o_ref):
    o_ref[...] = (x_ref[...] + _bdot(oa_ref[...], wa_ref[...]) + _bdot(orw_ref[...], wb_ref[...]))


def _out_proj(x2d, o_att, o_rw, w_o, tm, tn):
    m = x2d.shape[0]
    return pl.pallas_call(
        _out_proj_kernel,
        grid=(m // tm, D_MODEL // tn),
        in_specs=[
            pl.BlockSpec((tm, tn), lambda i, j: (i, j)),
            pl.BlockSpec((tm, C_GRP), lambda i, j: (i, 0)),
            pl.BlockSpec((tm, C_GRP), lambda i, j: (i, 0)),
            pl.BlockSpec((C_GRP, tn), lambda i, j: (0, j)),
            pl.BlockSpec((C_GRP, tn), lambda i, j: (1, j)),
        ],
        out_specs=pl.BlockSpec((tm, tn), lambda i, j: (i, j)),
        out_shape=jax.ShapeDtypeStruct((m, D_MODEL), F32),
        compiler_params=_cparams(2),
        name="out_proj",
    )(x2d, o_att, o_rw, w_o, w_o)


HALO = 16


def _ffn_kernel(*refs, tm, tiles_per_seq, seq_mode, side_units, win):
    attn_in = oa_ref = None
    if seq_mode and side_units:
        (x_ref, xh_ref, g_ref, wg_ref, wv_ref, cwg_ref, cwv_ref, wd_ref, gf_ref, *attn_in,
         y_ref, ug_ref, uv_ref, oa_ref, h_scr, acc_scr) = refs
    elif seq_mode:
        (x_ref, xh_ref, g_ref, wg_ref, wv_ref, cwg_ref, cwv_ref, wd_ref, gf_ref,
         y_ref, ug_ref, uv_ref, h_scr, acc_scr) = refs
    else:
        (x_ref, pg_ref, pv_ref, g_ref, wg_ref, wv_ref, cwg_ref, cwv_ref, wd_ref, gf_ref,
         y_ref, ug_ref, uv_ref, h_scr, acc_scr) = refs
    i, j = pl.program_id(0), pl.program_id(1)

    def norm(x):
        ms = jnp.mean(x * x, axis=-1, keepdims=True)
        return x * lax.rsqrt(ms + NORM_EPS) * g_ref[...]

    @pl.when(j == 0)
    def _():
        x = x_ref[...]
        acc_scr[...] = x
        if seq_mode:
            keep = jnp.where(i % tiles_per_seq == 0, 0.0, 1.0).astype(F32)
            h_scr[:HALO, :] = (norm(xh_ref[...]) * keep).astype(BF16)
            h_scr[HALO:, :] = norm(x).astype(BF16)
        else:
            h_scr[...] = norm(x).astype(BF16)

    h = h_scr[...]
    halves = []
    for w_ref, cw_ref, u_ref, p_ref in ((wg_ref, cwg_ref, ug_ref, None if seq_mode else pg_ref),
                                        (wv_ref, cwv_ref, uv_ref, None if seq_mode else pv_ref)):
        u = jnp.dot(h, w_ref[...], preferred_element_type=F32)
        cw = cw_ref[...]
        if seq_mode:
            u_ref[0] = u[HALO + tm - 2:HALO + tm, :]
            c = (cw[3:4] + cw[0:1] * u[HALO - 2:HALO - 2 + tm] + cw[1:2] * u[HALO - 1:HALO - 1 + tm]
                 + cw[2:3] * u[HALO:HALO + tm])
        else:
            u_ref[...] = u
            c = cw[3:4] + cw[0:1] * p_ref[0] + cw[1:2] * p_ref[1] + cw[2:3] * u
        halves.append(c)
    gate, val = halves
    act = gate * _sigmoid(gate) * val
    acc_scr[...] += jnp.dot(act.astype(BF16), wd_ref[...], preferred_element_type=F32)
    if oa_ref is not None:
        unit = i * side_units + jnp.minimum(j, side_units - 1)
        _attn_sample_unit(unit % (N_HEADS // SAMPLE_HEADS_PER_STEP), *attn_in, oa_ref, win=win)

    @pl.when(j == pl.num_programs(1) - 1)
    def _():
        x2 = acc_scr[...]
        ms = jnp.mean(x2 * x2, axis=-1, keepdims=True)
        y_ref[...] = x2 * lax.rsqrt(ms + NORM_EPS) * gf_ref[...]


def _ffn(x1, norm_g, w_up, conv_wb, w_down, final_g, tm, tf, seq_len=None, prev=None, sample_attn=None):
    m = x1.shape[0]
    nf = D_FF // tf
    seq_mode = prev is None
    tiles_per_seq = seq_len // tm if seq_mode else 1
    side_units, win, side = 0, None, None
    if sample_attn is not None:
        n_units = sample_attn[3].shape[0] * (N_HEADS // SAMPLE_HEADS_PER_STEP)
        side_units = n_units // (m // tm)
        assert seq_mode and side_units * (m // tm) == n_units and 0 < side_units <= nf
        side = _sample_attn_side_specs(sample_attn, lambda i, j: i * side_units + jnp.minimum(j, side_units - 1))
        win = side[3]
    vec = lambda width: pl.BlockSpec((1, width), lambda i, j: (0, 0))
    in_specs = [pl.BlockSpec((tm, D_MODEL), lambda i, j: (i, 0))]
    args = [x1]
    if seq_mode:
        in_specs.append(pl.BlockSpec((HALO, D_MODEL), lambda i, j: (jnp.maximum(i * (tm // HALO) - 1, 0), 0)))
        args.append(x1)
    else:
        in_specs += [pl.BlockSpec((2, tm, tf), lambda i, j: (0, i, j)),
                     pl.BlockSpec((2, tm, tf), lambda i, j: (0, i, nf + j))]
        args += [prev, prev]
    in_specs += [
        vec(D_MODEL),
        pl.BlockSpec((D_MODEL, tf), lambda i, j: (0, j)),
        pl.BlockSpec((D_MODEL, tf), lambda i, j: (0, nf + j)),
        pl.BlockSpec((4, tf), lambda i, j: (0, j)),
        pl.BlockSpec((4, tf), lambda i, j: (0, nf + j)),
        pl.BlockSpec((tf, D_MODEL), lambda i, j: (j, 0)),
        vec(D_MODEL),
    ]
    args += [norm_g, w_up, w_up, conv_wb, conv_wb, w_down, final_g]
    out_specs = [pl.BlockSpec((tm, D_MODEL), lambda i, j: (i, 0))]
    out_shape = [jax.ShapeDtypeStruct((m, D_MODEL), F32)]
    if seq_mode:
        out_specs += [pl.BlockSpec((1, 2, tf), lambda i, j: (i, 0, j))] * 2
        out_shape += [jax.ShapeDtypeStruct((m // tm, 2, D_FF), F32)] * 2
    else:
        out_specs += [pl.BlockSpec((tm, tf), lambda i, j: (i, j))] * 2
        out_shape += [jax.ShapeDtypeStruct((m, D_FF), F32)] * 2
    if side is not None:
        in_specs += side[0]
        args += list(sample_attn)
        out_specs.append(side[1])
        out_shape.append(side[2])
    rows = tm + HALO if seq_mode else tm
    return pl.pallas_call(
        functools.partial(_ffn_kernel, tm=tm, tiles_per_seq=tiles_per_seq, seq_mode=seq_mode,
                          side_units=side_units, win=win),
        grid=(m // tm, nf),
        in_specs=in_specs,
        out_specs=out_specs,
        out_shape=out_shape,
        scratch_shapes=[pltpu.VMEM((rows, D_MODEL), BF16), pltpu.VMEM((tm, D_MODEL), F32)],
        compiler_params=pltpu.CompilerParams(dimension_semantics=("arbitrary",) * 2,
                                             vmem_limit_bytes=FFN_VMEM_LIMIT if side else VMEM_LIMIT),
        name="ffn_seq" if seq_mode else "ffn_state",
    )(*args)


def _layer_params(l, norm_mix_g, w_in, att_out_g, rw_mu, rw_w0, rw_w_up, rw_a0, rw_a_up, rw_g_up, rw_k_k,
                  rw_k_a, rw_r_k, rw_lnx_w, rw_lnx_b, w_o, norm_ffn_g, ffn_w_up, ffn_conv_w, ffn_conv_b,
                  ffn_w_down):
    zeros = jnp.zeros((W_LORA, C_GRP), F32)
    return dict(
        norm_mix_g=norm_mix_g[l][None],
        w_main=w_in[l].astype(BF16),
        w_lora=w_in[l][:, C_MAIN:].astype(BF16),
        att_g=att_out_g[l][None],
        mu_main=rw_mu[l][:3 * C_GRP].reshape(3, C_GRP),
        mu_lora=rw_mu[l][None, 3 * C_GRP:],
        vecs=jnp.stack([rw_w0[l], rw_a0[l], rw_k_k[l], rw_k_a[l], rw_r_k[l], rw_lnx_w[l], rw_lnx_b[l]]),
        w_up=jnp.concatenate([rw_w_up[l], zeros]).astype(BF16),
        a_up=jnp.concatenate([zeros, rw_a_up[l]]).astype(BF16),
        g_up=rw_g_up[l].astype(BF16),
        w_o=w_o[l].astype(BF16),
        norm_ffn_g=norm_ffn_g[l][None],
        ffn_w_up=ffn_w_up[l],
        conv_wb=jnp.concatenate([ffn_conv_w[l], ffn_conv_b[l][None]]),
        ffn_w_down=ffn_w_down[l],
    )


def _prompt_layer(x, lp, final_g, sample_attn):
    nb, seq, _ = x.shape
    x2d = x.reshape(nb * seq, D_MODEL)
    p_main, p_lora = _in_proj(x2d, lp["norm_mix_g"], lp["w_main"], lp["w_lora"], tm=1024, tn=1024)
    p_main3 = p_main.reshape(nb, seq, C_MAIN)
    p_lora3 = p_lora.reshape(nb, seq, C_LORA)
    o_att, kt_new, vt_new, w_up, w_down = _attn_prompt(p_main3, lp["att_g"],
                                                       cast_f32=(lp["ffn_w_up"], lp["ffn_w_down"]))
    o_rw, h_fin = _rwkv_prompt(p_main3, p_lora3, jnp.zeros((nb, 3, C_GRP), F32), jnp.zeros((nb, 1, C_LORA), F32),
                               lp["mu_main"], lp["mu_lora"], lp["vecs"], lp["w_up"], lp["a_up"], lp["g_up"])
    x1 = _out_proj(x2d, o_att.reshape(nb * seq, C_GRP), o_rw.reshape(nb * seq, C_GRP), lp["w_o"], tm=512, tn=D_MODEL)
    y, u_g, u_v, o_att_sample = _ffn(x1, lp["norm_ffn_g"], w_up, lp["conv_wb"], w_down, final_g,
                                     tm=512, tf=512, seq_len=seq, sample_attn=sample_attn)
    hd = (N_HEADS, HEAD_DIM)
    k_new = kt_new.reshape((nb,) + hd + (seq,)).transpose(0, 3, 1, 2)
    v_new = vt_new.reshape((nb,) + hd + (seq,)).transpose(0, 3, 1, 2)
    rw_last = jnp.concatenate([p_main3[:, -1:, 3 * C_GRP:], p_lora3[:, -1:, :]], axis=-1)
    wkv = jnp.stack([h_fin[:, :, :HEAD_DIM, :HEAD_DIM], h_fin[:, :, HEAD_DIM:, HEAD_DIM:]], axis=2)
    wkv = wkv.reshape(nb, N_HEADS, HEAD_DIM, HEAD_DIM)
    tiles_per_seq = u_g.shape[0] // nb
    ffn_last = jnp.concatenate([u_g, u_v], axis=-1)[tiles_per_seq - 1::tiles_per_seq]
    return (y.reshape(nb, seq, D_MODEL), k_new, v_new, rw_last, wkv, ffn_last), o_att_sample, (w_up, w_down)


def _layer(x_prompt, x_sample, cache_k, cache_v, rw_prev, wkv0, ffn_prev, lp, final_g):
    nb, hd = x_sample.shape[0], (N_HEADS, HEAD_DIM)
    x2d = x_sample.reshape(nb, D_MODEL)
    p_main, p_lora = _in_proj(x2d, lp["norm_mix_g"], lp["w_main"], lp["w_lora"], tm=nb, tn=1024)
    q = p_main[:, :C_GRP].reshape((nb,) + hd)
    k_new = p_main[:, C_GRP:2 * C_GRP].reshape((nb,) + hd)
    v_new = p_main[:, 2 * C_GRP:3 * C_GRP].reshape((nb,) + hd)
    cache_kt, cache_vt = cache_k.transpose(0, 2, 3, 1), cache_v.transpose(0, 2, 3, 1)
    head_minor = lambda t: jnp.swapaxes(t, -1, -2)
    prompt_out, o_att, (ffn_w_up, ffn_w_down) = _prompt_layer(
        x_prompt, lp, final_g,
        sample_attn=(head_minor(q), head_minor(k_new), head_minor(v_new), cache_kt, cache_vt,
                     head_minor(lp["att_g"].reshape(hd))))
    o_att = head_minor(o_att.sum(axis=1)).reshape(nb, C_GRP)
    prev = rw_prev.reshape(nb, C_SHIFT)
    o_rw, wkv = _rwkv_sample(p_main, p_lora, prev[:, :3 * C_GRP], prev[:, 3 * C_GRP:],
                             lp["mu_main"].reshape(1, 3 * C_GRP), lp["mu_lora"], lp["vecs"],
                             lp["w_up"], lp["a_up"], lp["g_up"], wkv0)
    x1 = _out_proj(x2d, o_att, o_rw, lp["w_o"], tm=nb, tn=1024)
    prev_rows = ffn_prev.transpose(1, 0, 2)
    y, u_g, u_v = _ffn(x1, lp["norm_ffn_g"], ffn_w_up, lp["conv_wb"], ffn_w_down, final_g,
                       tm=nb, tf=512, prev=prev_rows)
    rw_last = jnp.concatenate([p_main[:, 3 * C_GRP:], p_lora], axis=-1)[:, None, :]
    ffn_last = jnp.stack([ffn_prev[:, 1, :], jnp.concatenate([u_g, u_v], axis=-1)], axis=1)
    sample_out = (y.reshape(nb, 1, D_MODEL), k_new[:, None], v_new[:, None], rw_last, wkv, ffn_last)
    return prompt_out, sample_out


def kernel(x_prompt, x_sample, cache_att_k, cache_att_v, state_rwkv_shift, state_rwkv_wkv, state_ffn_conv, norm_mix_g, w_in, att_out_g, rw_mu, rw_w0, rw_w_up, rw_a0, rw_a_up, rw_g_up, rw_k_k, rw_k_a, rw_r_k, rw_lnx_w, rw_lnx_b, w_o, norm_ffn_g, ffn_w_up, ffn_conv_w, ffn_conv_b, ffn_w_down, norm_final_g):
    depth = w_in.shape[0]
    assert depth == 1, "the fused FFN + final-norm kernel assumes a single trunk layer"
    assert x_sample.shape[1] == 1, "the sample path handles one new token per sequence"
    final_g = norm_final_g[None]
    lp = _layer_params(0, norm_mix_g, w_in, att_out_g, rw_mu, rw_w0, rw_w_up, rw_a0, rw_a_up, rw_g_up, rw_k_k,
                       rw_k_a, rw_r_k, rw_lnx_w, rw_lnx_b, w_o, norm_ffn_g, ffn_w_up, ffn_conv_w, ffn_conv_b,
                       ffn_w_down)
    (yp, pk, pv, prw, pwkv, pffn), (ys, sk, sv, srw, swkv, sffn) = _layer(
        x_prompt, x_sample, cache_att_k[0], cache_att_v[0], state_rwkv_shift[0], state_rwkv_wkv[0],
        state_ffn_conv[0], lp, final_g)
    lead = lambda t: t[None]
    return (yp, ys, lead(pk), lead(pv), lead(prw), lead(pwkv), lead(pffn),
            lead(sk), lead(sv), lead(srw), lead(swkv), lead(sffn))
```

```python
import functools

import jax
import jax.numpy as jnp
from jax import lax
from jax.experimental import pallas as pl
from jax.experimental.pallas import tpu as pltpu

F32 = jnp.float32
BF16 = jnp.bfloat16

D_MODEL = 2048
HEAD_DIM = 64
N_HEADS = 16
C_GRP = N_HEADS * HEAD_DIM
W_LORA, A_LORA, G_LORA = 64, 64, 160
C_LORA = W_LORA + A_LORA + G_LORA
C_SHIFT = 3 * C_GRP + C_LORA
C_MAIN = 6 * C_GRP
D_FF = 5632
DIL_RATES = (1, 4, 16)
N_BACK = 128
ATT_SCALE = HEAD_DIM ** -0.5
NORM_EPS = 1e-6
LNX_EPS = HEAD_DIM * 1e-5
NEG_BIG = -1e30

LANES = 128
VMEM_LIMIT = 48 * 1024 * 1024
SIDE_WORK_VMEM_LIMIT = 52 * 1024 * 1024

CHUNK = 64
PHASE_A_UNROLL = 8
ATTN_ITEMS_PER_ITER = (4, 4, 8)
PAIR = 2 * HEAD_DIM


def _cparams(n_grid):
    return pltpu.CompilerParams(dimension_semantics=("arbitrary",) * n_grid,
                                vmem_limit_bytes=VMEM_LIMIT)


def _bdot(a, b):
    return jnp.dot(a.astype(BF16), b.astype(BF16), preferred_element_type=F32)


def _bdot_nt(a, b):
    return lax.dot_general(a.astype(BF16), b.astype(BF16), (((1,), (1,)), ((), ())),
                           preferred_element_type=F32)


def _bdot_tn(a, b):
    return jnp.dot(a.astype(F32).T.astype(BF16), b.astype(BF16), preferred_element_type=F32)


def _split_dot(mat01, x, terms):
    acc = None
    rem = x
    for _ in range(terms):
        piece = rem.astype(BF16)
        part = jnp.dot(mat01, piece, preferred_element_type=F32)
        acc = part if acc is None else acc + part
        rem = rem - piece.astype(F32)
    return acc


def _split_dot_r(x, mat01, terms):
    return _dot_pieces(_split_pieces(x, terms), mat01)


def _head_ones(n):
    r = lax.broadcasted_iota(jnp.int32, (n, n), 0) // HEAD_DIM
    c = lax.broadcasted_iota(jnp.int32, (n, n), 1) // HEAD_DIM
    return jnp.where(r == c, 1.0, 0.0).astype(BF16)


def _sigmoid(x):
    return 1.0 / (1.0 + jnp.exp(-x))


def _softplus(x):
    return jnp.maximum(x, 0.0) + jnp.log(1.0 + jnp.exp(-jnp.abs(x)))


def _in_proj_kernel(*refs, nj, win):
    attn_in = oa_ref = None
    if win is not None:
        x_ref, g_ref, wm_ref, wl_ref, *attn_in, om_ref, ol_ref, oa_ref, h_scr = refs
    else:
        x_ref, g_ref, wm_ref, wl_ref, om_ref, ol_ref, h_scr = refs

    @pl.when(pl.program_id(1) == 0)
    def _():
        x = x_ref[...]
        ms = jnp.mean(x * x, axis=-1, keepdims=True)
        h = (x * lax.rsqrt(ms + NORM_EPS) * g_ref[...]).astype(BF16)
        h_scr[...] = h
        ol_ref[...] = jnp.dot(h, wl_ref[...], preferred_element_type=F32)

    om_ref[...] = jnp.dot(h_scr[...], wm_ref[...], preferred_element_type=F32)
    if oa_ref is not None:
        unit = pl.program_id(0) * nj + pl.program_id(1)
        _attn_sample_unit(unit % (N_HEADS // SAMPLE_HEADS_PER_STEP), *attn_in, oa_ref, win=win)


def _in_proj(x2d, g, w_main, w_lora, tm, tn, sample_attn=None):
    m = x2d.shape[0]
    ni, nj = m // tm, C_MAIN // tn
    single = dict(pipeline_mode=pl.Buffered(1)) if sample_attn is not None else {}
    in_specs = [
        pl.BlockSpec((tm, D_MODEL), lambda i, j: (i, 0), **single),
        pl.BlockSpec((1, D_MODEL), lambda i, j: (0, 0)),
        pl.BlockSpec((D_MODEL, tn), lambda i, j: (0, j)),
        pl.BlockSpec((D_MODEL, C_LORA), lambda i, j: (0, 0), **single),
    ]
    out_specs = [pl.BlockSpec((tm, tn), lambda i, j: (i, j)), pl.BlockSpec((tm, C_LORA), lambda i, j: (i, 0))]
    out_shape = [jax.ShapeDtypeStruct((m, C_MAIN), F32), jax.ShapeDtypeStruct((m, C_LORA), F32)]
    args = [x2d, g, w_main, w_lora]
    win = None
    if sample_attn is not None:
        assert ni * nj == sample_attn[3].shape[0] * (N_HEADS // SAMPLE_HEADS_PER_STEP), "one unit per grid step"
        side_in, side_out, side_shape, win = _sample_attn_side_specs(sample_attn, lambda i, j: i * nj + j)
        in_specs += side_in
        args += list(sample_attn)
        out_specs.append(side_out)
        out_shape.append(side_shape)
    return pl.pallas_call(
        functools.partial(_in_proj_kernel, nj=nj, win=win),
        grid=(ni, nj),
        in_specs=in_specs,
        out_specs=out_specs,
        out_shape=out_shape,
        scratch_shapes=[pltpu.VMEM((tm, D_MODEL), BF16)],
        compiler_params=pltpu.CompilerParams(dimension_semantics=("arbitrary",) * 2,
                                             vmem_limit_bytes=SIDE_WORK_VMEM_LIMIT if sample_attn else VMEM_LIMIT),
        name="in_proj",
    )(*args)


def _sample_attn_side_specs(sample_attn, unit_of):
    cache_kt = sample_attn[3]
    nb_s, win = cache_kt.shape[0], cache_kt.shape[-1]
    assert win % (N_BACK * max(DIL_RATES)) == 0 and win % LANES == 0
    hs = SAMPLE_HEADS_PER_STEP
    ngrp = N_HEADS // hs
    tok = pl.BlockSpec((1, HEAD_DIM, N_HEADS), lambda i, j: (unit_of(i, j) // ngrp, 0, 0))
    cache = pl.BlockSpec((1, hs, HEAD_DIM, win), lambda i, j: (unit_of(i, j) // ngrp, unit_of(i, j) % ngrp, 0, 0))
    in_specs = [tok, tok, tok, cache, cache, pl.BlockSpec((HEAD_DIM, N_HEADS), lambda i, j: (0, 0))]
    out_spec = pl.BlockSpec((1, 1, HEAD_DIM, N_HEADS), lambda i, j: (unit_of(i, j) // ngrp, unit_of(i, j) % ngrp, 0, 0))
    out_shape = jax.ShapeDtypeStruct((nb_s, ngrp, HEAD_DIM, N_HEADS), F32)
    return in_specs, out_spec, out_shape, win


def _attn_prompt_kernel(*refs, seq, ncast):
    q_ref, k_ref, v_ref, g_ref, *rest = refs
    cast_in, rest = rest[:ncast], rest[ncast:]
    o_ref, kt_out_ref, vt_out_ref, *rest = rest
    cast_out, (ob_scr, lb_scr, *item_scr) = rest[:ncast], rest[ncast:]
    for src, dst in zip(cast_in, cast_out):
        dst[...] = src[...].astype(BF16)
    blk = N_BACK
    nitem = seq // blk
    lane = lax.broadcasted_iota(jnp.int32, (blk, PAIR), 1)
    is_a = lane < HEAD_DIM

    row2 = lax.broadcasted_iota(jnp.int32, (blk, 2 * blk), 0)
    col2 = lax.broadcasted_iota(jnp.int32, (blk, 2 * blk), 1)
    band_bias = jnp.where((col2 >= row2) & (col2 <= row2 + N_BACK), 0.0, NEG_BIG)
    causal_bias = band_bias[:, blk:]

    def item_rows(j, rate):
        nblk = seq // rate // blk
        return pl.ds((j // nblk) + (j % nblk) * (blk * rate), blk, stride=rate)

    def prep(j, rate, dst):
        qa_scr, qb_scr, kt_scr, va_scr, vb_scr = dst
        rows = item_rows(j, rate)
        qs = q_ref[rows, :] * ATT_SCALE
        v = v_ref[rows, :]
        kt = k_ref[rows, :].T
        qa_scr[j] = jnp.where(is_a, qs, 0.0).astype(BF16)
        qb_scr[j] = jnp.where(is_a, 0.0, qs).astype(BF16)
        kt_scr[j] = kt.astype(BF16)
        va_scr[j] = jnp.where(is_a, v, 0.0).astype(BF16)
        vb_scr[j] = jnp.where(is_a, 0.0, v).astype(BF16)
        if rate == 1:
            kt_out_ref[:, j * blk:(j + 1) * blk] = kt
            vt_out_ref[:, j * blk:(j + 1) * blk] = v.T

    def attend(i, br, src, next_rate, dst):
        qa_scr, qb_scr, kt_scr, va_scr, vb_scr = src
        rate = DIL_RATES[br]
        nblk = seq // rate // blk
        js = [i * ATTN_ITEMS_PER_ITER[br] + u for u in range(ATTN_ITEMS_PER_ITER[br])]
        nk = blk if nblk == 1 else 2 * blk
        ones = jnp.ones((nk, PAIR), BF16)
        kts, vms, biases = [], [], []
        for j in js:
            if nblk == 1:
                biases.append(causal_bias)
                kts.append(kt_scr[j])
                vs = (va_scr[j], vb_scr[j])
            else:
                jp = jnp.maximum(j - 1, 0)
                lo = jnp.where(j % nblk == 0, blk, 0)
                biases.append(jnp.where(col2 >= lo, band_bias, NEG_BIG))
                kts.append(jnp.concatenate([kt_scr[jp], kt_scr[j]], axis=1))
                vs = (jnp.concatenate([va_scr[jp], va_scr[j]], axis=0),
                      jnp.concatenate([vb_scr[jp], vb_scr[j]], axis=0))
            vms.append([jnp.concatenate([v, ones], axis=1) for v in vs])
        qs = [(qa_scr[j], qb_scr[j]) for j in js]
        s = [[jnp.dot(q, kt, preferred_element_type=F32) + bias for q in qp]
             for qp, kt, bias in zip(qs, kts, biases)]
        if next_rate is not None:
            for j in js:
                prep(j, next_rate, dst)
        m = [[jnp.max(x, axis=-1, keepdims=True) for x in xs] for xs in s]
        e = [[jnp.exp(x - mx).astype(BF16) for x, mx in zip(xs, ms)] for xs, ms in zip(s, m)]
        of = [[jnp.dot(x, vm, preferred_element_type=F32) for x, vm in zip(xs, vp)] for xs, vp in zip(e, vms)]
        for j, (ofa, ofb), (ma, mb) in zip(js, of, m):
            den = jnp.where(is_a, ofa[:, PAIR:], ofb[:, PAIR:])
            rows = item_rows(j, rate)
            ob_scr[br, rows, :] = (ofa[:, :PAIR] + ofb[:, :PAIR]) * (1.0 / den)
            lb_scr[br, rows, :] = jnp.where(is_a, ma, mb) + jnp.log(den)

    sets = (item_scr[:5], item_scr[5:])
    for j in range(nitem):
        prep(j, DIL_RATES[0], sets[0])
    for br in range(len(DIL_RATES)):
        next_rate = DIL_RATES[br + 1] if br + 1 < len(DIL_RATES) else None

        def body(i, carry, br=br, next_rate=next_rate):
            attend(i, br, sets[br % 2], next_rate, sets[(br + 1) % 2])
            return carry
        lax.fori_loop(0, nitem // ATTN_ITEMS_PER_ITER[br], body, 0)

    ones_bd = _head_ones(PAIR)
    gain = g_ref[...]
    tile = 256

    def merge(i, carry):
        rows = pl.ds(pl.multiple_of(i * tile, tile), tile)
        l0, l1, l2 = lb_scr[0, rows, :], lb_scr[1, rows, :], lb_scr[2, rows, :]
        m = jnp.maximum(jnp.maximum(l0, l1), l2)
        w0, w1, w2 = jnp.exp(l0 - m), jnp.exp(l1 - m), jnp.exp(l2 - m)
        o = (w0 * ob_scr[0, rows, :] + w1 * ob_scr[1, rows, :] + w2 * ob_scr[2, rows, :]) / (w0 + w1 + w2)
        ms = _split_dot_r(o * o, ones_bd, 2) * (1.0 / HEAD_DIM)
        o_ref[rows, :] = (o * lax.rsqrt(ms + NORM_EPS) * gain).astype(o_ref.dtype)
        return carry
    lax.fori_loop(0, seq // tile, merge, 0, unroll=2)


def _attn_prompt(p_main3, att_g, cast_f32=()):
    nb, seq, _ = p_main3.shape
    npair = C_GRP // PAIR
    col = lambda off: pl.BlockSpec((None, seq, PAIR), lambda b, p, off=off: (b, 0, off + p))
    nstep = nb * npair
    cast_specs = []
    for w in cast_f32:
        assert w.shape[0] % (nstep * 16) == 0, "bf16 row slabs must be whole (16, 128) tiles"
        cast_specs.append(pl.BlockSpec((w.shape[0] // nstep, w.shape[1]), lambda b, p: (b * npair + p, 0)))
    return pl.pallas_call(
        functools.partial(_attn_prompt_kernel, seq=seq, ncast=len(cast_f32)),
        grid=(nb, npair),
        in_specs=[col(0), col(npair), col(2 * npair), pl.BlockSpec((1, PAIR), lambda b, p: (0, p))] + cast_specs,
        out_specs=[pl.BlockSpec((None, seq, PAIR), lambda b, p: (b, 0, p)),
                   pl.BlockSpec((None, PAIR, seq), lambda b, p: (b, p, 0)),
                   pl.BlockSpec((None, PAIR, seq), lambda b, p: (b, p, 0))] + cast_specs,
        out_shape=[jax.ShapeDtypeStruct((nb, seq, C_GRP), BF16),
                   jax.ShapeDtypeStruct((nb, C_GRP, seq), F32), jax.ShapeDtypeStruct((nb, C_GRP, seq), F32)]
        + [jax.ShapeDtypeStruct(w.shape, BF16) for w in cast_f32],
        scratch_shapes=[pltpu.VMEM((3, seq, PAIR), F32), pltpu.VMEM((3, seq, PAIR), F32)]
        + [pltpu.VMEM((seq // N_BACK, N_BACK, PAIR), BF16)] * 10,
        compiler_params=_cparams(2),
        name="attn_prompt",
    )(p_main3, p_main3, p_main3, att_g, *cast_f32)


SAMPLE_HEADS_PER_STEP = 8


def _attn_sample_unit(grp, q_ref, kn_ref, vn_ref, kt_ref, vt_ref, g_ref, o_ref, *, win):
    base = grp * SAMPLE_HEADS_PER_STEP
    head_lane = lax.broadcasted_iota(jnp.int32, (1, N_HEADS), 1)
    pick = lambda x, hl: jnp.sum(jnp.where(head_lane == hl, x, 0.0), axis=-1, keepdims=True)
    q_all, kn_all, vn_all = q_ref[0], kn_ref[0], vn_ref[0]
    hs = range(SAMPLE_HEADS_PER_STEP)
    qc = [pick(q_all, base + h) * ATT_SCALE for h in hs]
    kn = [pick(kn_all, base + h) for h in hs]
    vn = [pick(vn_all, base + h) for h in hs]
    s_all = [jnp.sum(kt_ref[0, h] * qc[h], axis=0, keepdims=True) for h in hs]
    s_new = [jnp.sum(kn[h] * qc[h], axis=0, keepdims=True) for h in hs]
    outs, lses = [], []
    for rate in DIL_RATES:
        lo = win - N_BACK * rate
        s = [x[:, lo:] for x in s_all]
        if rate > 1:
            pos = lax.broadcasted_iota(jnp.int32, s[0].shape, 1)
            s = [jnp.where((pos & (rate - 1)) == 0, x, NEG_BIG) for x in s]
        m = [jnp.maximum(jnp.max(s[h], axis=-1, keepdims=True), s_new[h]) for h in hs]
        e = [jnp.exp(s[h] - m[h]) for h in hs]
        e_new = [jnp.exp(s_new[h] - m[h]) for h in hs]
        den = [jnp.sum(e[h], axis=-1, keepdims=True) + e_new[h] for h in hs]
        o = [(jnp.sum(vt_ref[0, h, :, lo:] * e[h], axis=-1, keepdims=True) + e_new[h] * vn[h]) / den[h] for h in hs]
        o_b, l_b = jnp.zeros((HEAD_DIM, N_HEADS), F32), jnp.zeros((1, N_HEADS), F32)
        for h in hs:
            o_b = o_b + jnp.where(head_lane == base + h, o[h], 0.0)
            l_b = l_b + jnp.where(head_lane == base + h, m[h] + jnp.log(den[h]), 0.0)
        outs.append(o_b)
        lses.append(l_b)
    m = jnp.maximum(jnp.maximum(lses[0], lses[1]), lses[2])
    ws = [jnp.exp(l - m) for l in lses]
    o = (ws[0] * outs[0] + ws[1] * outs[1] + ws[2] * outs[2]) / (ws[0] + ws[1] + ws[2])
    ms = jnp.mean(o * o, axis=0, keepdims=True)
    mine = (head_lane >= base) & (head_lane < base + SAMPLE_HEADS_PER_STEP)
    o_ref[0, 0] = jnp.where(mine, o * lax.rsqrt(ms + NORM_EPS) * g_ref[...], 0.0)


def _split_pieces(x, terms):
    pieces, rem = [], x
    for _ in range(terms):
        piece = rem.astype(BF16)
        pieces.append(piece)
        rem = rem - piece.astype(F32)
    return pieces


def _dot_pieces(pieces, mat01):
    return jnp.dot(jnp.concatenate(pieces, axis=1), jnp.concatenate([mat01] * len(pieces), axis=0),
                   preferred_element_type=F32)


def _rwkv_token_stages(inputs, prm, ones_bd):
    xr, xk, xv, xl = inputs()
    act_w, act_a = jnp.tanh(xl[:, :2 * W_LORA]).astype(BF16), xl[:, :2 * W_LORA].astype(BF16)
    act_g = _sigmoid(xl[:, 2 * W_LORA:]).astype(BF16)
    kk = xk * prm["k_k"]
    kk_sq = _split_pieces(kk * kk, 2)
    yield None
    w_raw = prm["w0"] + jnp.dot(act_w, prm["w_up"], preferred_element_type=F32)
    a_pre = prm["a0"] + jnp.dot(act_a, prm["a_up"], preferred_element_type=F32)
    gate = jnp.dot(act_g, prm["g_up"], preferred_element_type=F32)
    nrm_sq = _dot_pieces(kk_sq, ones_bd)
    yield None
    logw = -jnp.exp(-_softplus(-w_raw) - 0.5)
    a_sig = _sigmoid(a_pre)
    kk = kk / jnp.maximum(jnp.sqrt(nrm_sq), 1e-12)
    k_eff = xk * (1.0 + (a_sig - 1.0) * prm["k_a"])
    rk = _split_pieces(xr * k_eff * prm["r_k"], 2)
    yield None
    bonus = _dot_pieces(rk, ones_bd) * xv
    yield dict(r=xr, k=k_eff, v=xv, a=-kk, b=kk * a_sig, logw=logw, gate=gate, bonus=bonus)


def _rwkv_token_math(xr, xk, xv, xl, prm, ones_bd):
    *_, tok = _rwkv_token_stages(lambda: (xr, xk, xv, xl), prm, ones_bd)
    return tok


def _group_norm_gate(y, bonus, gate, lnx_w, lnx_b, ones_bd):
    mean = _split_dot_r(y, ones_bd, 2) * (1.0 / HEAD_DIM)
    d = y - mean
    var = _split_dot_r(d * d, ones_bd, 2) * (1.0 / HEAD_DIM)
    yn = d * lax.rsqrt(var + LNX_EPS) * lnx_w + lnx_b
    return (yn + bonus) * gate


_RW_VEC_NAMES = ("w0", "a0", "k_k", "k_a", "r_k", "lnx_w", "lnx_b")


def _rwkv_prompt_kernel(r_ref, k_ref, v_ref, l_ref, pm_ref, pl_ref, mu_m_ref, mu_l_ref, vec_ref,
                        wup_ref, aup_ref, gup_ref, o_ref, h_ref,
                        tok_scr, y_scr, rw_scr, y0_scr, g_scr, ha_scr, pc_scr, *, seq):
    n = PAIR
    grows = PHASE_A_UNROLL * CHUNK
    pad = 8

    def token_group(g):
        first = isinstance(g, int) and g == 0
        prm = {name: vec_ref[i:i + 1, :] for i, name in enumerate(_RW_VEC_NAMES)}
        prm.update(w_up=wup_ref[...], a_up=aup_ref[...], g_up=gup_ref[...])
        if first:
            rows = pl.ds(0, grows)
            row0 = lax.broadcasted_iota(jnp.int32, (grows, 1), 0) == 0

            def lerp(ref, prev, mu):
                x = ref[rows, :]
                return x + mu * (jnp.where(row0, prev, pltpu.roll(x, 1, axis=0)) - x)
        else:
            rows = pl.ds(pl.multiple_of(g * grows, grows), grows)
            ext_rows = pl.ds(pl.multiple_of(g * grows - pad, pad), grows + pad)

            def lerp(ref, prev, mu):
                xe = ref[ext_rows, :]
                return xe[pad:] + mu * (pltpu.roll(xe, 1, axis=0)[pad:] - xe[pad:])
        def inputs():
            pm, mu_m = pm_ref[...], mu_m_ref[...]
            return (lerp(r_ref, pm[0:1], mu_m[0:1]), lerp(k_ref, pm[1:2], mu_m[1:2]),
                    lerp(v_ref, pm[2:3], mu_m[2:3]), lerp(l_ref, pl_ref[...], mu_l_ref[...]))

        tok = None
        for tok in _rwkv_token_stages(inputs, prm, _head_ones(n)):
            if tok is None:
                yield
        for i, name in enumerate(("r", "k", "v", "a", "b", "logw", "gate", "bonus")):
            tok_scr[i, rows, :] = tok[name]

    nch = seq // CHUNK
    lane = lax.broadcasted_iota(jnp.int32, (CHUNK, n), 1)
    is_a = lane < HEAD_DIM
    ri = lax.broadcasted_iota(jnp.int32, (n, n), 0)
    ci = lax.broadcasted_iota(jnp.int32, (n, n), 1)
    strict = ri > ci
    incl = ri >= ci
    eye = ri == ci
    rc = lax.broadcasted_iota(jnp.int32, (CHUNK, CHUNK), 0)
    cc = lax.broadcasted_iota(jnp.int32, (CHUNK, CHUNK), 1)

    def stack(x):
        return jnp.concatenate([jnp.where(is_a, x, 0.0), jnp.where(is_a, 0.0, x)], axis=0)

    def phase_b_step(c):
        rows = pl.ds(pl.multiple_of(c * CHUNK, CHUNK), CHUNK)
        ht = h_ref[...]
        hb = ht.astype(BF16)
        ys = _bdot_nt(rw_scr[c], hb) + y0_scr[c]
        y_scr[rows, :] = ys[:CHUNK, :] + ys[CHUNK:, :]
        h_ref[...] = ht * pc_scr[c] + _bdot(hb, g_scr[c]) + ha_scr[c]

    def group(ga, gb, gt=None):
        b_todo = [] if gb is None else [gb * PHASE_A_UNROLL + u for u in range(PHASE_A_UNROLL)]

        def fill():
            if b_todo:
                phase_b_step(b_todo.pop(0))

        if ga is None:
            while b_todo:
                fill()
            return
        cs = [ga * PHASE_A_UNROLL + u for u in range(PHASE_A_UNROLL)]
        each = lambda fn, *lists: [fn(*xs) for xs in zip(*lists)]
        tril_ones = jnp.where(rc >= cc, 1.0, 0.0).astype(BF16)
        tok = [[tok_scr[i, pl.ds(pl.multiple_of(c * CHUNK, CHUNK), CHUNK), :] for i in range(6)] for c in cs]
        r_c, k_c, v_c, a_c, b_c, lw_c = [list(x) for x in zip(*tok)]
        token_gen = iter(()) if gt is None else token_group(gt)
        token_step = lambda: next(token_gen, None)
        token_step()
        pairs = [_split_dot(tril_ones, jnp.concatenate(lw_c[u:u + 2], axis=1), 3)
                 for u in range(0, PHASE_A_UNROLL, 2)]
        lcum = [p[:, h * n:(h + 1) * n] for p in pairs for h in range(2)]
        lend = each(lambda l: l[CHUNK - 1:CHUNK, :], lcum)
        fill()
        sb = lambda x: stack(x).astype(BF16)
        a_s = each(lambda a, l, lw: sb(a * jnp.exp(l - lw)), a_c, lcum, lw_c)
        r_s = each(lambda r, l: sb(r * jnp.exp(l)), r_c, lcum)
        b_s = each(lambda b, l: sb(b * jnp.exp(-l)), b_c, lcum)
        k_s = each(lambda k, l: sb(k * jnp.exp(-l)), k_c, lcum)
        v_s = each(sb, v_c)
        be_s = each(lambda b, l, le: sb(b * jnp.exp(le - l)), b_c, lcum, lend)
        ke_s = each(lambda k, l, le: sb(k * jnp.exp(le - l)), k_c, lcum, lend)

        sc = each(lambda a, r, b, k: _bdot_nt(jnp.concatenate([a, r], axis=0), jnp.concatenate([b, k], axis=0)),
                  a_s, r_s, b_s, k_s)
        fill()
        token_step()
        s_ab = each(lambda s: jnp.where(strict, s[:n, :n], 0.0), sc)
        s_ak = each(lambda s: jnp.where(strict, s[:n, n:], 0.0).astype(BF16), sc)
        s_rb = each(lambda s: jnp.where(incl, s[n:, :n], 0.0).astype(BF16), sc)
        s_rk = each(lambda s: jnp.where(incl, s[n:, n:], 0.0).astype(BF16), sc)

        tinv = each(lambda s: jnp.where(eye, 1.0, 0.0) + s, s_ab)
        apow = each(lambda s: _bdot(s, s).astype(BF16), s_ab)
        fill()
        m = 2
        while 2 * m < CHUNK:
            res = each(lambda p, t: _bdot(p, jnp.concatenate([t.astype(BF16), p], axis=1)), apow, tinv)
            tinv = each(lambda t, x: t + x[:, :n], tinv, res)
            apow = each(lambda x: x[:, n:].astype(BF16), res)
            m *= 2
            fill()
            if m in (4, 16):
                token_step()
        tinv = each(lambda t, p: t + _bdot(p, t), tinv, apow)

        x1 = each(_bdot, s_ak, v_s)
        wu = each(lambda t, a, x: _bdot(t, jnp.concatenate([a, x.astype(BF16)], axis=1)), tinv, a_s, x1)
        fill()
        yk = each(_bdot, s_rk, v_s)
        ry = each(lambda s, w: _bdot(s, w), s_rb, wu)
        gh = each(lambda w, be: _bdot(w.T, be), wu, be_s)
        hk = each(lambda v, ke: _bdot_tn(v, ke), v_s, ke_s)
        while b_todo:
            fill()
        for _ in token_gen:
            pass
        for u, c in enumerate(cs):
            rw_scr[c] = ry[u][:, :n] + r_s[u].astype(F32)
            y0_scr[c] = ry[u][:, n:] + yk[u]
            g_scr[c] = gh[u][:n, :]
            ha_scr[c] = gh[u][n:, :] + hk[u]
            pc_scr[c] = jnp.exp(lend[u])

    h_ref[...] = jnp.zeros((n, n), F32)
    ngroup = nch // PHASE_A_UNROLL
    assert ngroup >= 3
    for _ in token_group(0):
        pass
    group(0, None, 1)

    def pipelined(g, carry):
        group(g, g - 1, g + 1)
        return carry
    lax.fori_loop(1, ngroup - 1, pipelined, 0)
    group(ngroup - 1, ngroup - 2)
    group(None, ngroup - 1)

    iw, ib = _RW_VEC_NAMES.index("lnx_w"), _RW_VEC_NAMES.index("lnx_b")
    o_ref[...] = _group_norm_gate(y_scr[...], tok_scr[7], tok_scr[6], vec_ref[iw:iw + 1, :], vec_ref[ib:ib + 1, :],
                                  _head_ones(n)).astype(o_ref.dtype)


def _rwkv_prompt(p_main3, p_lora3, prev_main, prev_lora, mu_main, mu_lora, vecs, w_up, a_up, g_up):
    nb, seq, _ = p_main3.shape
    npair = C_GRP // PAIR
    col = lambda off: pl.BlockSpec((None, seq, PAIR), lambda b, p, off=off: (b, 0, off + p))
    pcol = lambda rows: pl.BlockSpec((rows, PAIR), lambda b, p: (0, p))
    return pl.pallas_call(
        functools.partial(_rwkv_prompt_kernel, seq=seq),
        grid=(nb, npair),
        in_specs=[
            col(3 * npair), col(4 * npair), col(5 * npair),
            pl.BlockSpec((None, seq, C_LORA), lambda b, p: (b, 0, 0)),
            pl.BlockSpec((None, 3, PAIR), lambda b, p: (b, 0, p)),
            pl.BlockSpec((None, 1, C_LORA), lambda b, p: (b, 0, 0)),
            pcol(3), pl.BlockSpec((1, C_LORA), lambda b, p: (0, 0)), pcol(len(_RW_VEC_NAMES)),
            pcol(2 * W_LORA), pcol(2 * A_LORA), pcol(G_LORA),
        ],
        out_specs=[
            pl.BlockSpec((None, seq, PAIR), lambda b, p: (b, 0, p)),
            pl.BlockSpec((None, None, PAIR, PAIR), lambda b, p: (b, p, 0, 0)),
        ],
        out_shape=[jax.ShapeDtypeStruct((nb, seq, C_GRP), BF16),
                   jax.ShapeDtypeStruct((nb, npair, PAIR, PAIR), F32)],
        scratch_shapes=[
            pltpu.VMEM((8, seq, PAIR), F32), pltpu.VMEM((seq, PAIR), F32),
            pltpu.VMEM((seq // CHUNK, PAIR, PAIR), F32), pltpu.VMEM((seq // CHUNK, PAIR, PAIR), F32),
            pltpu.VMEM((seq // CHUNK, PAIR, PAIR), F32), pltpu.VMEM((seq // CHUNK, PAIR, PAIR), F32),
            pltpu.VMEM((seq // CHUNK, 1, PAIR), F32),
        ],
        compiler_params=_cparams(2),
        name="rwkv_prompt",
    )(p_main3, p_main3, p_main3, p_lora3, prev_main, prev_lora, mu_main, mu_lora, vecs, w_up, a_up, g_up)


def _rwkv_sample_tok_kernel(pm_ref, l_ref, pvm_ref, pvl_ref, mu_m_ref, mu_l_ref, vec_ref,
                            wup_ref, aup_ref, gup_ref, out_ref):
    ones_bd = _head_ones(C_GRP)
    prm = {name: vec_ref[i:i + 1, :] for i, name in enumerate(_RW_VEC_NAMES)}
    prm.update(w_up=wup_ref[...], a_up=aup_ref[...], g_up=gup_ref[...])
    lerp = lambda x, prev, mu: x + mu * (prev - x)
    xs = [lerp(pm_ref[:, i * C_GRP:(i + 1) * C_GRP], pvm_ref[:, i * C_GRP:(i + 1) * C_GRP],
               mu_m_ref[:, i * C_GRP:(i + 1) * C_GRP]) for i in range(3)]
    tok = _rwkv_token_math(xs[0], xs[1], xs[2], lerp(l_ref[...], pvl_ref[...], mu_l_ref[...]), prm, ones_bd)
    for i, name in enumerate(("r", "k", "v", "a", "b", "logw", "gate", "bonus")):
        out_ref[i] = tok[name]


def _rwkv_sample_step_kernel(s_ref, row_ref, col_ref, s_out_ref, o_ref):
    head_lane = lax.broadcasted_iota(jnp.int32, (1, N_HEADS), 1)
    heads = range(N_HEADS)
    row = lambda i: [row_ref[0, i, h:h + 1, :] for h in heads]
    r, k, a, b, logw = [row(i) for i in range(5)]
    v = [col_ref[0, 0, :, h:h + 1] for h in heads]
    s = [s_ref[0, h] for h in heads]
    sa = [jnp.sum(s[h] * a[h], axis=-1, keepdims=True) for h in heads]
    s_new = [s[h] * jnp.exp(logw[h]) + sa[h] * b[h] + v[h] * k[h] for h in heads]
    for h in heads:
        s_out_ref[0, h] = s_new[h]
    y = jnp.zeros((HEAD_DIM, N_HEADS), F32)
    for h in heads:
        y = y + jnp.where(head_lane == h, jnp.sum(s_new[h] * r[h], axis=-1, keepdims=True), 0.0)
    gate, bonus, lnx_w, lnx_b = [col_ref[0, i] for i in range(1, 5)]
    mean = jnp.mean(y, axis=0, keepdims=True)
    d = y - mean
    var = jnp.mean(d * d, axis=0, keepdims=True)
    o_ref[0] = (d * lax.rsqrt(var + LNX_EPS) * lnx_w + lnx_b + bonus) * gate


def _rwkv_sample(pm_s, pl_s, prev_main, prev_lora, mu_main, mu_lora, vecs, w_up, a_up, g_up, wkv0):
    nb = pm_s.shape[0]
    full = lambda shape: pl.BlockSpec(shape, lambda i: (0,) * len(shape))
    tok = pl.pallas_call(
        _rwkv_sample_tok_kernel,
        grid=(1,),
        in_specs=[pl.BlockSpec((nb, 3 * C_GRP), lambda i: (0, 1)), full((nb, C_LORA)),
                  full((nb, 3 * C_GRP)), full((nb, C_LORA)), full((1, 3 * C_GRP)), full((1, C_LORA)),
                  full(vecs.shape), full(w_up.shape), full(a_up.shape), full(g_up.shape)],
        out_specs=full((8, nb, C_GRP)),
        out_shape=jax.ShapeDtypeStruct((8, nb, C_GRP), F32),
        compiler_params=_cparams(1),
        name="rwkv_sample_tok",
    )(pm_s, pl_s, prev_main, prev_lora, mu_main, mu_lora, vecs, w_up, a_up, g_up)
    heads = lambda x: x.reshape(x.shape[0], nb, N_HEADS, HEAD_DIM).transpose(1, 0, 2, 3)
    rows = heads(jnp.concatenate([tok[0:2], tok[3:6]], axis=0))
    lnx = jnp.broadcast_to(vecs[5:7, None, :], (2, nb, C_GRP))
    cols = jnp.swapaxes(heads(jnp.concatenate([tok[2:3], tok[6:8], lnx], axis=0)), -1, -2)
    st = (1, N_HEADS, HEAD_DIM, HEAD_DIM)
    s_new, o = pl.pallas_call(
        _rwkv_sample_step_kernel,
        grid=(nb,),
        in_specs=[pl.BlockSpec(st, lambda b: (b, 0, 0, 0)),
                  pl.BlockSpec((1, 5, N_HEADS, HEAD_DIM), lambda b: (b, 0, 0, 0)),
                  pl.BlockSpec((1, 5, HEAD_DIM, N_HEADS), lambda b: (b, 0, 0, 0))],
        out_specs=[pl.BlockSpec(st, lambda b: (b, 0, 0, 0)),
                   pl.BlockSpec((1, HEAD_DIM, N_HEADS), lambda b: (b, 0, 0))],
        out_shape=[jax.ShapeDtypeStruct((nb,) + st[1:], F32),
                   jax.ShapeDtypeStruct((nb, HEAD_DIM, N_HEADS), F32)],
        compiler_params=_cparams(1),
        name="rwkv_sample_step",
    )(wkv0, rows, cols)
    return jnp.swapaxes(o, -1, -2).reshape(nb, C_GRP), s_new


def _out_proj_kernel(x_ref, oa_ref, orw_ref, wa_ref, wb_ref, o_ref):
    o_ref[...] = (x_ref[...] + _bdot(oa_ref[...], wa_ref[...]) + _bdot(orw_ref[...], wb_ref[...]))


def _out_proj(x2d, o_att, o_rw, w_o, tm, tn):
    m = x2d.shape[0]
    return pl.pallas_call(
        _out_proj_kernel,
        grid=(m // tm, D_MODEL // tn),
        in_specs=[
            pl.BlockSpec((tm, tn), lambda i, j: (i, j)),
            pl.BlockSpec((tm, C_GRP), lambda i, j: (i, 0)),
            pl.BlockSpec((tm, C_GRP), lambda i, j: (i, 0)),
            pl.BlockSpec((C_GRP, tn), lambda i, j: (0, j)),
            pl.BlockSpec((C_GRP, tn), lambda i, j: (1, j)),
        ],
        out_specs=pl.BlockSpec((tm, tn), lambda i, j: (i, j)),
        out_shape=jax.ShapeDtypeStruct((m, D_MODEL), F32),
        compiler_params=_cparams(2),
        name="out_proj",
    )(x2d, o_att, o_rw, w_o, w_o)


HALO = 16


def _ffn_kernel(*refs, tm, tiles_per_seq, seq_mode):
    if seq_mode:
        (x_ref, xh_ref, g_ref, wg_ref, wv_ref, cwg_ref, cwv_ref, wd_ref, gf_ref,
         y_ref, ug_ref, uv_ref, h_scr, acc_scr) = refs
    else:
        (x_ref, pg_ref, pv_ref, g_ref, wg_ref, wv_ref, cwg_ref, cwv_ref, wd_ref, gf_ref,
         y_ref, ug_ref, uv_ref, h_scr, acc_scr) = refs
    i, j = pl.program_id(0), pl.program_id(1)

    def norm(x):
        ms = jnp.mean(x * x, axis=-1, keepdims=True)
        return x * lax.rsqrt(ms + NORM_EPS) * g_ref[...]

    @pl.when(j == 0)
    def _():
        x = x_ref[...]
        acc_scr[...] = x
        if seq_mode:
            keep = jnp.where(i % tiles_per_seq == 0, 0.0, 1.0).astype(F32)
            h_scr[:HALO, :] = (norm(xh_ref[...]) * keep).astype(BF16)
            h_scr[HALO:, :] = norm(x).astype(BF16)
        else:
            h_scr[...] = norm(x).astype(BF16)

    h = h_scr[...]
    halves = []
    for w_ref, cw_ref, u_ref, p_ref in ((wg_ref, cwg_ref, ug_ref, None if seq_mode else pg_ref),
                                        (wv_ref, cwv_ref, uv_ref, None if seq_mode else pv_ref)):
        u = jnp.dot(h, w_ref[...], preferred_element_type=F32)
        cw = cw_ref[...]
        if seq_mode:
            u_ref[0] = u[HALO + tm - 2:HALO + tm, :]
            c = (cw[3:4] + cw[0:1] * u[HALO - 2:HALO - 2 + tm] + cw[1:2] * u[HALO - 1:HALO - 1 + tm]
                 + cw[2:3] * u[HALO:HALO + tm])
        else:
            u_ref[...] = u
            c = cw[3:4] + cw[0:1] * p_ref[0] + cw[1:2] * p_ref[1] + cw[2:3] * u
        halves.append(c)
    gate, val = halves
    act = gate * _sigmoid(gate) * val
    acc_scr[...] += jnp.dot(act.astype(BF16), wd_ref[...], preferred_element_type=F32)

    @pl.when(j == pl.num_programs(1) - 1)
    def _():
        x2 = acc_scr[...]
        ms = jnp.mean(x2 * x2, axis=-1, keepdims=True)
        y_ref[...] = x2 * lax.rsqrt(ms + NORM_EPS) * gf_ref[...]


def _ffn(x1, norm_g, w_up, conv_wb, w_down, final_g, tm, tf, seq_len=None, prev=None):
    m = x1.shape[0]
    nf = D_FF // tf
    seq_mode = prev is None
    tiles_per_seq = seq_len // tm if seq_mode else 1
    vec = lambda width: pl.BlockSpec((1, width), lambda i, j: (0, 0))
    in_specs = [pl.BlockSpec((tm, D_MODEL), lambda i, j: (i, 0))]
    args = [x1]
    if seq_mode:
        in_specs.append(pl.BlockSpec((HALO, D_MODEL), lambda i, j: (jnp.maximum(i * (tm // HALO) - 1, 0), 0)))
        args.append(x1)
    else:
        in_specs += [pl.BlockSpec((2, tm, tf), lambda i, j: (0, i, j)),
                     pl.BlockSpec((2, tm, tf), lambda i, j: (0, i, nf + j))]
        args += [prev, prev]
    in_specs += [
        vec(D_MODEL),
        pl.BlockSpec((D_MODEL, tf), lambda i, j: (0, j)),
        pl.BlockSpec((D_MODEL, tf), lambda i, j: (0, nf + j)),
        pl.BlockSpec((4, tf), lambda i, j: (0, j)),
        pl.BlockSpec((4, tf), lambda i, j: (0, nf + j)),
        pl.BlockSpec((tf, D_MODEL), lambda i, j: (j, 0)),
        vec(D_MODEL),
    ]
    args += [norm_g, w_up, w_up, conv_wb, conv_wb, w_down, final_g]
    out_specs = [pl.BlockSpec((tm, D_MODEL), lambda i, j: (i, 0))]
    out_shape = [jax.ShapeDtypeStruct((m, D_MODEL), F32)]
    if seq_mode:
        out_specs += [pl.BlockSpec((1, 2, tf), lambda i, j: (i, 0, j))] * 2
        out_shape += [jax.ShapeDtypeStruct((m // tm, 2, D_FF), F32)] * 2
    else:
        out_specs += [pl.BlockSpec((tm, tf), lambda i, j: (i, j))] * 2
        out_shape += [jax.ShapeDtypeStruct((m, D_FF), F32)] * 2
    rows = tm + HALO if seq_mode else tm
    return pl.pallas_call(
        functools.partial(_ffn_kernel, tm=tm, tiles_per_seq=tiles_per_seq, seq_mode=seq_mode),
        grid=(m // tm, nf),
        in_specs=in_specs,
        out_specs=out_specs,
        out_shape=out_shape,
        scratch_shapes=[pltpu.VMEM((rows, D_MODEL), BF16), pltpu.VMEM((tm, D_MODEL), F32)],
        compiler_params=_cparams(2),
        name="ffn_seq" if seq_mode else "ffn_state",
    )(*args)


def _layer_params(l, norm_mix_g, w_in, att_out_g, rw_mu, rw_w0, rw_w_up, rw_a0, rw_a_up, rw_g_up, rw_k_k,
                  rw_k_a, rw_r_k, rw_lnx_w, rw_lnx_b, w_o, norm_ffn_g, ffn_w_up, ffn_conv_w, ffn_conv_b,
                  ffn_w_down):
    zeros = jnp.zeros((W_LORA, C_GRP), F32)
    return dict(
        norm_mix_g=norm_mix_g[l][None],
        w_main=w_in[l].astype(BF16),
        w_lora=w_in[l][:, C_MAIN:].astype(BF16),
        att_g=att_out_g[l][None],
        mu_main=rw_mu[l][:3 * C_GRP].reshape(3, C_GRP),
        mu_lora=rw_mu[l][None, 3 * C_GRP:],
        vecs=jnp.stack([rw_w0[l], rw_a0[l], rw_k_k[l], rw_k_a[l], rw_r_k[l], rw_lnx_w[l], rw_lnx_b[l]]),
        w_up=jnp.concatenate([rw_w_up[l], zeros]).astype(BF16),
        a_up=jnp.concatenate([zeros, rw_a_up[l]]).astype(BF16),
        g_up=rw_g_up[l].astype(BF16),
        w_o=w_o[l].astype(BF16),
        norm_ffn_g=norm_ffn_g[l][None],
        ffn_w_up=ffn_w_up[l],
        conv_wb=jnp.concatenate([ffn_conv_w[l], ffn_conv_b[l][None]]),
        ffn_w_down=ffn_w_down[l],
    )


def _prompt_layer(x, lp, final_g, sample_attn):
    nb, seq, _ = x.shape
    x2d = x.reshape(nb * seq, D_MODEL)
    p_main, p_lora, o_att_sample = _in_proj(x2d, lp["norm_mix_g"], lp["w_main"], lp["w_lora"], tm=1024, tn=768,
                                            sample_attn=sample_attn)
    p_main3 = p_main.reshape(nb, seq, C_MAIN)
    p_lora3 = p_lora.reshape(nb, seq, C_LORA)
    o_att, kt_new, vt_new, w_up, w_down = _attn_prompt(p_main3, lp["att_g"],
                                                       cast_f32=(lp["ffn_w_up"], lp["ffn_w_down"]))
    o_rw, h_fin = _rwkv_prompt(p_main3, p_lora3, jnp.zeros((nb, 3, C_GRP), F32), jnp.zeros((nb, 1, C_LORA), F32),
                               lp["mu_main"], lp["mu_lora"], lp["vecs"], lp["w_up"], lp["a_up"], lp["g_up"])
    x1 = _out_proj(x2d, o_att.reshape(nb * seq, C_GRP), o_rw.reshape(nb * seq, C_GRP), lp["w_o"], tm=512, tn=D_MODEL)
    y, u_g, u_v = _ffn(x1, lp["norm_ffn_g"], w_up, lp["conv_wb"], w_down, final_g, tm=512, tf=512, seq_len=seq)
    hd = (N_HEADS, HEAD_DIM)
    k_new = kt_new.reshape((nb,) + hd + (seq,)).transpose(0, 3, 1, 2)
    v_new = vt_new.reshape((nb,) + hd + (seq,)).transpose(0, 3, 1, 2)
    rw_last = jnp.concatenate([p_main3[:, -1:, 3 * C_GRP:], p_lora3[:, -1:, :]], axis=-1)
    wkv = jnp.stack([h_fin[:, :, :HEAD_DIM, :HEAD_DIM], h_fin[:, :, HEAD_DIM:, HEAD_DIM:]], axis=2)
    wkv = wkv.reshape(nb, N_HEADS, HEAD_DIM, HEAD_DIM)
    tiles_per_seq = u_g.shape[0] // nb
    ffn_last = jnp.concatenate([u_g, u_v], axis=-1)[tiles_per_seq - 1::tiles_per_seq]
    return (y.reshape(nb, seq, D_MODEL), k_new, v_new, rw_last, wkv, ffn_last), o_att_sample, (w_up, w_down)


def _layer(x_prompt, x_sample, cache_k, cache_v, rw_prev, wkv0, ffn_prev, lp, final_g):
    nb, hd = x_sample.shape[0], (N_HEADS, HEAD_DIM)
    x2d = x_sample.reshape(nb, D_MODEL)
    p_main, p_lora = _in_proj(x2d, lp["norm_mix_g"], lp["w_main"], lp["w_lora"], tm=nb, tn=1024)
    q = p_main[:, :C_GRP].reshape((nb,) + hd)
    k_new = p_main[:, C_GRP:2 * C_GRP].reshape((nb,) + hd)
    v_new = p_main[:, 2 * C_GRP:3 * C_GRP].reshape((nb,) + hd)
    cache_kt, cache_vt = cache_k.transpose(0, 2, 3, 1), cache_v.transpose(0, 2, 3, 1)
    head_minor = lambda t: jnp.swapaxes(t, -1, -2)
    prompt_out, o_att, (ffn_w_up, ffn_w_down) = _prompt_layer(
        x_prompt, lp, final_g,
        sample_attn=(head_minor(q), head_minor(k_new), head_minor(v_new), cache_kt, cache_vt,
                     head_minor(lp["att_g"].reshape(hd))))
    o_att = head_minor(o_att.sum(axis=1)).reshape(nb, C_GRP)
    prev = rw_prev.reshape(nb, C_SHIFT)
    o_rw, wkv = _rwkv_sample(p_main, p_lora, prev[:, :3 * C_GRP], prev[:, 3 * C_GRP:],
                             lp["mu_main"].reshape(1, 3 * C_GRP), lp["mu_lora"], lp["vecs"],
                             lp["w_up"], lp["a_up"], lp["g_up"], wkv0)
    x1 = _out_proj(x2d, o_att, o_rw, lp["w_o"], tm=nb, tn=1024)
    prev_rows = ffn_prev.transpose(1, 0, 2)
    y, u_g, u_v = _ffn(x1, lp["norm_ffn_g"], ffn_w_up, lp["conv_wb"], ffn_w_down, final_g,
                       tm=nb, tf=512, prev=prev_rows)
    rw_last = jnp.concatenate([p_main[:, 3 * C_GRP:], p_lora], axis=-1)[:, None, :]
    ffn_last = jnp.stack([ffn_prev[:, 1, :], jnp.concatenate([u_g, u_v], axis=-1)], axis=1)
    sample_out = (y.reshape(nb, 1, D_MODEL), k_new[:, None], v_new[:, None], rw_last, wkv, ffn_last)
    return prompt_out, sample_out


def kernel(x_prompt, x_sample, cache_att_k, cache_att_v, state_rwkv_shift, state_rwkv_wkv, state_ffn_conv, norm_mix_g, w_in, att_out_g, rw_mu, rw_w0, rw_w_up, rw_a0, rw_a_up, rw_g_up, rw_k_k, rw_k_a, rw_r_k, rw_lnx_w, rw_lnx_b, w_o, norm_ffn_g, ffn_w_up, ffn_conv_w, ffn_conv_b, ffn_w_down, norm_final_g):
    depth = w_in.shape[0]
    assert depth == 1, "the fused FFN + final-norm kernel assumes a single trunk layer"
    assert x_sample.shape[1] == 1, "the sample path handles one new token per sequence"
    final_g = norm_final_g[None]
    lp = _layer_params(0, norm_mix_g, w_in, att_out_g, rw_mu, rw_w0, rw_w_up, rw_a0, rw_a_up, rw_g_up, rw_k_k,
                       rw_k_a, rw_r_k, rw_lnx_w, rw_lnx_b, w_o, norm_ffn_g, ffn_w_up, ffn_conv_w, ffn_conv_b,
                       ffn_w_down)
    (yp, pk, pv, prw, pwkv, pffn), (ys, sk, sv, srw, swkv, sffn) = _layer(
        x_prompt, x_sample, cache_att_k[0], cache_att_v[0], state_rwkv_shift[0], state_rwkv_wkv[0],
        state_ffn_conv[0], lp, final_g)
    lead = lambda t: t[None]
    return (yp, ys, lead(pk), lead(pv), lead(prw), lead(pwkv), lead(pffn),
            lead(sk), lead(sv), lead(srw), lead(swkv), lead(sffn))
```

```python
import functools

import jax
import jax.numpy as jnp
from jax import lax
from jax.experimental import pallas as pl
from jax.experimental.pallas import tpu as pltpu

F32 = jnp.float32
BF16 = jnp.bfloat16

D_MODEL = 2048
HEAD_DIM = 64
N_HEADS = 16
C_GRP = N_HEADS * HEAD_DIM
W_LORA, A_LORA, G_LORA = 64, 64, 160
C_LORA = W_LORA + A_LORA + G_LORA
C_SHIFT = 3 * C_GRP + C_LORA
C_MAIN = 6 * C_GRP
D_FF = 5632
DIL_RATES = (1, 4, 16)
N_BACK = 128
ATT_SCALE = HEAD_DIM ** -0.5
NORM_EPS = 1e-6
LNX_EPS = HEAD_DIM * 1e-5
NEG_BIG = -1e30

LANES = 128
VMEM_LIMIT = 48 * 1024 * 1024
SIDE_WORK_VMEM_LIMIT = 52 * 1024 * 1024

IN_PROJ_TILE = (1024, 768)
IN_PROJ_TN_DECODE = 1024
OUT_PROJ_TILE = (512, D_MODEL)
OUT_PROJ_TN_DECODE = 1024
FFN_TILE = (512, 512)

CHUNK = 64
PHASE_A_UNROLL = 8
ATTN_ITEMS_PER_ITER = (4, 4, 8)
PAIR = 2 * HEAD_DIM


def _cparams(n_grid):
    return pltpu.CompilerParams(dimension_semantics=("arbitrary",) * n_grid,
                                vmem_limit_bytes=VMEM_LIMIT)


def _bdot(a, b):
    return jnp.dot(a.astype(BF16), b.astype(BF16), preferred_element_type=F32)


def _bdot_nt(a, b):
    return lax.dot_general(a.astype(BF16), b.astype(BF16), (((1,), (1,)), ((), ())),
                           preferred_element_type=F32)


def _bdot_tn(a, b):
    return jnp.dot(a.astype(F32).T.astype(BF16), b.astype(BF16), preferred_element_type=F32)


def _split_dot(mat01, x, terms):
    acc = None
    rem = x
    for _ in range(terms):
        piece = rem.astype(BF16)
        part = jnp.dot(mat01, piece, preferred_element_type=F32)
        acc = part if acc is None else acc + part
        rem = rem - piece.astype(F32)
    return acc


def _split_dot_r(x, mat01, terms):
    return _dot_pieces(_split_pieces(x, terms), mat01)


def _head_ones(n):
    r = lax.broadcasted_iota(jnp.int32, (n, n), 0) // HEAD_DIM
    c = lax.broadcasted_iota(jnp.int32, (n, n), 1) // HEAD_DIM
    return jnp.where(r == c, 1.0, 0.0).astype(BF16)


def _sigmoid(x):
    return 1.0 / (1.0 + jnp.exp(-x))


def _softplus(x):
    return jnp.maximum(x, 0.0) + jnp.log(1.0 + jnp.exp(-jnp.abs(x)))


def _in_proj_kernel(*refs, nj, win):
    attn_in = oa_ref = None
    if win is not None:
        x_ref, g_ref, wm_ref, wl_ref, *attn_in, om_ref, ol_ref, oa_ref, h_scr = refs
    else:
        x_ref, g_ref, wm_ref, wl_ref, om_ref, ol_ref, h_scr = refs

    @pl.when(pl.program_id(1) == 0)
    def _():
        x = x_ref[...]
        ms = jnp.mean(x * x, axis=-1, keepdims=True)
        h = (x * lax.rsqrt(ms + NORM_EPS) * g_ref[...]).astype(BF16)
        h_scr[...] = h
        ol_ref[...] = jnp.dot(h, wl_ref[...], preferred_element_type=F32)

    om_ref[...] = jnp.dot(h_scr[...], wm_ref[...], preferred_element_type=F32)
    if oa_ref is not None:
        unit = pl.program_id(0) * nj + pl.program_id(1)
        _attn_sample_unit(unit % (N_HEADS // SAMPLE_HEADS_PER_STEP), *attn_in, oa_ref, win=win)


def _in_proj(x2d, g, w_main, w_lora, tm, tn, sample_attn=None):
    m = x2d.shape[0]
    ni, nj = m // tm, C_MAIN // tn
    single = dict(pipeline_mode=pl.Buffered(1)) if sample_attn is not None else {}
    in_specs = [
        pl.BlockSpec((tm, D_MODEL), lambda i, j: (i, 0), **single),
        pl.BlockSpec((1, D_MODEL), lambda i, j: (0, 0)),
        pl.BlockSpec((D_MODEL, tn), lambda i, j: (0, j)),
        pl.BlockSpec((D_MODEL, C_LORA), lambda i, j: (0, 0), **single),
    ]
    out_specs = [pl.BlockSpec((tm, tn), lambda i, j: (i, j)), pl.BlockSpec((tm, C_LORA), lambda i, j: (i, 0))]
    out_shape = [jax.ShapeDtypeStruct((m, C_MAIN), F32), jax.ShapeDtypeStruct((m, C_LORA), F32)]
    args = [x2d, g, w_main, w_lora]
    win = None
    if sample_attn is not None:
        assert ni * nj == sample_attn[3].shape[0] * (N_HEADS // SAMPLE_HEADS_PER_STEP), "one unit per grid step"
        side_in, side_out, side_shape, win = _sample_attn_side_specs(sample_attn, lambda i, j: i * nj + j)
        in_specs += side_in
        args += list(sample_attn)
        out_specs.append(side_out)
        out_shape.append(side_shape)
    return pl.pallas_call(
        functools.partial(_in_proj_kernel, nj=nj, win=win),
        grid=(ni, nj),
        in_specs=in_specs,
        out_specs=out_specs,
        out_shape=out_shape,
        scratch_shapes=[pltpu.VMEM((tm, D_MODEL), BF16)],
        compiler_params=pltpu.CompilerParams(dimension_semantics=("arbitrary",) * 2,
                                             vmem_limit_bytes=SIDE_WORK_VMEM_LIMIT if sample_attn else VMEM_LIMIT),
        name="in_proj",
    )(*args)


def _sample_attn_side_specs(sample_attn, unit_of):
    cache_kt = sample_attn[3]
    nb_s, win = cache_kt.shape[0], cache_kt.shape[-1]
    assert win % (N_BACK * max(DIL_RATES)) == 0 and win % LANES == 0
    hs = SAMPLE_HEADS_PER_STEP
    ngrp = N_HEADS // hs
    tok = pl.BlockSpec((1, HEAD_DIM, N_HEADS), lambda i, j: (unit_of(i, j) // ngrp, 0, 0))
    cache = pl.BlockSpec((1, hs, HEAD_DIM, win), lambda i, j: (unit_of(i, j) // ngrp, unit_of(i, j) % ngrp, 0, 0))
    in_specs = [tok, tok, tok, cache, cache, pl.BlockSpec((HEAD_DIM, N_HEADS), lambda i, j: (0, 0))]
    out_spec = pl.BlockSpec((1, 1, HEAD_DIM, N_HEADS), lambda i, j: (unit_of(i, j) // ngrp, unit_of(i, j) % ngrp, 0, 0))
    out_shape = jax.ShapeDtypeStruct((nb_s, ngrp, HEAD_DIM, N_HEADS), F32)
    return in_specs, out_spec, out_shape, win


def _attn_prompt_kernel(*refs, seq, ncast):
    q_ref, k_ref, v_ref, g_ref, *rest = refs
    cast_in, rest = rest[:ncast], rest[ncast:]
    o_ref, kt_out_ref, vt_out_ref, *rest = rest
    cast_out, (ob_scr, lb_scr, *item_scr) = rest[:ncast], rest[ncast:]
    for src, dst in zip(cast_in, cast_out):
        dst[...] = src[...].astype(BF16)
    blk = N_BACK
    nitem = seq // blk
    lane = lax.broadcasted_iota(jnp.int32, (blk, PAIR), 1)
    is_a = lane < HEAD_DIM

    row2 = lax.broadcasted_iota(jnp.int32, (blk, 2 * blk), 0)
    col2 = lax.broadcasted_iota(jnp.int32, (blk, 2 * blk), 1)
    band_bias = jnp.where((col2 >= row2) & (col2 <= row2 + N_BACK), 0.0, NEG_BIG)
    causal_bias = band_bias[:, blk:]

    def item_rows(j, rate):
        nblk = seq // rate // blk
        return pl.ds((j // nblk) + (j % nblk) * (blk * rate), blk, stride=rate)

    def prep(j, rate, dst):
        qa_scr, qb_scr, kt_scr, va_scr, vb_scr = dst
        rows = item_rows(j, rate)
        qs = q_ref[rows, :] * ATT_SCALE
        v = v_ref[rows, :]
        kt = k_ref[rows, :].T
        qa_scr[j] = jnp.where(is_a, qs, 0.0).astype(BF16)
        qb_scr[j] = jnp.where(is_a, 0.0, qs).astype(BF16)
        kt_scr[j] = kt.astype(BF16)
        va_scr[j] = jnp.where(is_a, v, 0.0).astype(BF16)
        vb_scr[j] = jnp.where(is_a, 0.0, v).astype(BF16)
        if rate == 1:
            kt_out_ref[:, j * blk:(j + 1) * blk] = kt
            vt_out_ref[:, j * blk:(j + 1) * blk] = v.T

    def attend(i, br, src, next_rate, dst):
        qa_scr, qb_scr, kt_scr, va_scr, vb_scr = src
        rate = DIL_RATES[br]
        nblk = seq // rate // blk
        js = [i * ATTN_ITEMS_PER_ITER[br] + u for u in range(ATTN_ITEMS_PER_ITER[br])]
        nk = blk if nblk == 1 else 2 * blk
        ones = jnp.ones((nk, PAIR), BF16)
        kts, vms, biases = [], [], []
        for j in js:
            if nblk == 1:
                biases.append(causal_bias)
                kts.append(kt_scr[j])
                vs = (va_scr[j], vb_scr[j])
            else:
                jp = jnp.maximum(j - 1, 0)
                lo = jnp.where(j % nblk == 0, blk, 0)
                biases.append(jnp.where(col2 >= lo, band_bias, NEG_BIG))
                kts.append(jnp.concatenate([kt_scr[jp], kt_scr[j]], axis=1))
                vs = (jnp.concatenate([va_scr[jp], va_scr[j]], axis=0),
                      jnp.concatenate([vb_scr[jp], vb_scr[j]], axis=0))
            vms.append([jnp.concatenate([v, ones], axis=1) for v in vs])
        qs = [(qa_scr[j], qb_scr[j]) for j in js]
        s = [[jnp.dot(q, kt, preferred_element_type=F32) + bias for q in qp]
             for qp, kt, bias in zip(qs, kts, biases)]
        if next_rate is not None:
            for j in js:
                prep(j, next_rate, dst)
        m = [[jnp.max(x, axis=-1, keepdims=True) for x in xs] for xs in s]
        e = [[jnp.exp(x - mx).astype(BF16) for x, mx in zip(xs, ms)] for xs, ms in zip(s, m)]
        of = [[jnp.dot(x, vm, preferred_element_type=F32) for x, vm in zip(xs, vp)] for xs, vp in zip(e, vms)]
        for j, (ofa, ofb), (ma, mb) in zip(js, of, m):
            den = jnp.where(is_a, ofa[:, PAIR:], ofb[:, PAIR:])
            rows = item_rows(j, rate)
            ob_scr[br, rows, :] = (ofa[:, :PAIR] + ofb[:, :PAIR]) * (1.0 / den)
            lb_scr[br, rows, :] = jnp.where(is_a, ma, mb) + jnp.log(den)

    sets = (item_scr[:5], item_scr[5:])
    for j in range(nitem):
        prep(j, DIL_RATES[0], sets[0])
    for br in range(len(DIL_RATES)):
        next_rate = DIL_RATES[br + 1] if br + 1 < len(DIL_RATES) else None

        def body(i, carry, br=br, next_rate=next_rate):
            attend(i, br, sets[br % 2], next_rate, sets[(br + 1) % 2])
            return carry
        lax.fori_loop(0, nitem // ATTN_ITEMS_PER_ITER[br], body, 0)

    ones_bd = _head_ones(PAIR)
    gain = g_ref[...]
    tile = 256

    def merge(i, carry):
        rows = pl.ds(pl.multiple_of(i * tile, tile), tile)
        l0, l1, l2 = lb_scr[0, rows, :], lb_scr[1, rows, :], lb_scr[2, rows, :]
        m = jnp.maximum(jnp.maximum(l0, l1), l2)
        w0, w1, w2 = jnp.exp(l0 - m), jnp.exp(l1 - m), jnp.exp(l2 - m)
        o = (w0 * ob_scr[0, rows, :] + w1 * ob_scr[1, rows, :] + w2 * ob_scr[2, rows, :]) / (w0 + w1 + w2)
        ms = _split_dot_r(o * o, ones_bd, 2) * (1.0 / HEAD_DIM)
        o_ref[rows, :] = (o * lax.rsqrt(ms + NORM_EPS) * gain).astype(o_ref.dtype)
        return carry
    lax.fori_loop(0, seq // tile, merge, 0, unroll=4)


def _attn_prompt(p_main3, att_g, cast_f32=()):
    nb, seq, _ = p_main3.shape
    npair = C_GRP // PAIR
    col = lambda off: pl.BlockSpec((None, seq, PAIR), lambda b, p, off=off: (b, 0, off + p))
    nstep = nb * npair
    cast_specs = []
    for w in cast_f32:
        assert w.shape[0] % (nstep * 16) == 0, "bf16 row slabs must be whole (16, 128) tiles"
        cast_specs.append(pl.BlockSpec((w.shape[0] // nstep, w.shape[1]), lambda b, p: (b * npair + p, 0)))
    return pl.pallas_call(
        functools.partial(_attn_prompt_kernel, seq=seq, ncast=len(cast_f32)),
        grid=(nb, npair),
        in_specs=[col(0), col(npair), col(2 * npair), pl.BlockSpec((1, PAIR), lambda b, p: (0, p))] + cast_specs,
        out_specs=[pl.BlockSpec((None, seq, PAIR), lambda b, p: (b, 0, p)),
                   pl.BlockSpec((None, PAIR, seq), lambda b, p: (b, p, 0)),
                   pl.BlockSpec((None, PAIR, seq), lambda b, p: (b, p, 0))] + cast_specs,
        out_shape=[jax.ShapeDtypeStruct((nb, seq, C_GRP), BF16),
                   jax.ShapeDtypeStruct((nb, C_GRP, seq), F32), jax.ShapeDtypeStruct((nb, C_GRP, seq), F32)]
        + [jax.ShapeDtypeStruct(w.shape, BF16) for w in cast_f32],
        scratch_shapes=[pltpu.VMEM((3, seq, PAIR), F32), pltpu.VMEM((3, seq, PAIR), F32)]
        + [pltpu.VMEM((seq // N_BACK, N_BACK, PAIR), BF16)] * 10,
        compiler_params=_cparams(2),
        name="attn_prompt",
    )(p_main3, p_main3, p_main3, att_g, *cast_f32)


SAMPLE_HEADS_PER_STEP = 8


def _attn_sample_unit(grp, q_ref, kn_ref, vn_ref, kt_ref, vt_ref, g_ref, o_ref, *, win):
    base = grp * SAMPLE_HEADS_PER_STEP
    head_lane = lax.broadcasted_iota(jnp.int32, (1, N_HEADS), 1)
    pick = lambda x, hl: jnp.sum(jnp.where(head_lane == hl, x, 0.0), axis=-1, keepdims=True)
    q_all, kn_all, vn_all = q_ref[0], kn_ref[0], vn_ref[0]
    hs = range(SAMPLE_HEADS_PER_STEP)
    qc = [pick(q_all, base + h) * ATT_SCALE for h in hs]
    kn = [pick(kn_all, base + h) for h in hs]
    vn = [pick(vn_all, base + h) for h in hs]
    s_all = [jnp.sum(kt_ref[0, h] * qc[h], axis=0, keepdims=True) for h in hs]
    s_new = [jnp.sum(kn[h] * qc[h], axis=0, keepdims=True) for h in hs]
    outs, lses = [], []
    for rate in DIL_RATES:
        lo = win - N_BACK * rate
        s = [x[:, lo:] for x in s_all]
        if rate > 1:
            pos = lax.broadcasted_iota(jnp.int32, s[0].shape, 1)
            s = [jnp.where((pos & (rate - 1)) == 0, x, NEG_BIG) for x in s]
        m = [jnp.maximum(jnp.max(s[h], axis=-1, keepdims=True), s_new[h]) for h in hs]
        e = [jnp.exp(s[h] - m[h]) for h in hs]
        e_new = [jnp.exp(s_new[h] - m[h]) for h in hs]
        den = [jnp.sum(e[h], axis=-1, keepdims=True) + e_new[h] for h in hs]
        o = [(jnp.sum(vt_ref[0, h, :, lo:] * e[h], axis=-1, keepdims=True) + e_new[h] * vn[h]) / den[h] for h in hs]
        o_b, l_b = jnp.zeros((HEAD_DIM, N_HEADS), F32), jnp.zeros((1, N_HEADS), F32)
        for h in hs:
            o_b = o_b + jnp.where(head_lane == base + h, o[h], 0.0)
            l_b = l_b + jnp.where(head_lane == base + h, m[h] + jnp.log(den[h]), 0.0)
        outs.append(o_b)
        lses.append(l_b)
    m = jnp.maximum(jnp.maximum(lses[0], lses[1]), lses[2])
    ws = [jnp.exp(l - m) for l in lses]
    o = (ws[0] * outs[0] + ws[1] * outs[1] + ws[2] * outs[2]) / (ws[0] + ws[1] + ws[2])
    ms = jnp.mean(o * o, axis=0, keepdims=True)
    mine = (head_lane >= base) & (head_lane < base + SAMPLE_HEADS_PER_STEP)
    o_ref[0, 0] = jnp.where(mine, o * lax.rsqrt(ms + NORM_EPS) * g_ref[...], 0.0)


def _split_pieces(x, terms):
    pieces, rem = [], x
    for _ in range(terms):
        piece = rem.astype(BF16)
        pieces.append(piece)
        rem = rem - piece.astype(F32)
    return pieces


def _dot_pieces(pieces, mat01):
    return jnp.dot(jnp.concatenate(pieces, axis=1), jnp.concatenate([mat01] * len(pieces), axis=0),
                   preferred_element_type=F32)


def _rwkv_token_stages(inputs, prm, ones_bd):
    xr, xk, xv, xl = inputs()
    act_w, act_a = jnp.tanh(xl[:, :2 * W_LORA]).astype(BF16), xl[:, :2 * W_LORA].astype(BF16)
    act_g = _sigmoid(xl[:, 2 * W_LORA:]).astype(BF16)
    kk = xk * prm["k_k"]
    kk_sq = _split_pieces(kk * kk, 2)
    yield None
    w_raw = prm["w0"] + jnp.dot(act_w, prm["w_up"], preferred_element_type=F32)
    a_pre = prm["a0"] + jnp.dot(act_a, prm["a_up"], preferred_element_type=F32)
    gate = jnp.dot(act_g, prm["g_up"], preferred_element_type=F32)
    nrm_sq = _dot_pieces(kk_sq, ones_bd)
    yield None
    logw = -jnp.exp(-_softplus(-w_raw) - 0.5)
    a_sig = _sigmoid(a_pre)
    kk = kk / jnp.maximum(jnp.sqrt(nrm_sq), 1e-12)
    k_eff = xk * (1.0 + (a_sig - 1.0) * prm["k_a"])
    rk = _split_pieces(xr * k_eff * prm["r_k"], 2)
    yield None
    bonus = _dot_pieces(rk, ones_bd) * xv
    yield dict(r=xr, k=k_eff, v=xv, a=-kk, b=kk * a_sig, logw=logw, gate=gate, bonus=bonus)


def _rwkv_token_math(xr, xk, xv, xl, prm, ones_bd):
    *_, tok = _rwkv_token_stages(lambda: (xr, xk, xv, xl), prm, ones_bd)
    return tok


def _group_norm_gate(y, bonus, gate, lnx_w, lnx_b, ones_bd):
    mean = _split_dot_r(y, ones_bd, 2) * (1.0 / HEAD_DIM)
    d = y - mean
    var = _split_dot_r(d * d, ones_bd, 2) * (1.0 / HEAD_DIM)
    yn = d * lax.rsqrt(var + LNX_EPS) * lnx_w + lnx_b
    return (yn + bonus) * gate


_RW_VEC_NAMES = ("w0", "a0", "k_k", "k_a", "r_k", "lnx_w", "lnx_b")


def _rwkv_prompt_kernel(r_ref, k_ref, v_ref, l_ref, pm_ref, pl_ref, mu_m_ref, mu_l_ref, vec_ref,
                        wup_ref, aup_ref, gup_ref, o_ref, h_ref,
                        tok_scr, y_scr, rw_scr, y0_scr, g_scr, ha_scr, pc_scr, *, seq):
    n = PAIR
    grows = PHASE_A_UNROLL * CHUNK
    pad = 8

    def token_group(g):
        first = isinstance(g, int) and g == 0
        prm = {name: vec_ref[i:i + 1, :] for i, name in enumerate(_RW_VEC_NAMES)}
        prm.update(w_up=wup_ref[...], a_up=aup_ref[...], g_up=gup_ref[...])
        if first:
            rows = pl.ds(0, grows)
            row0 = lax.broadcasted_iota(jnp.int32, (grows, 1), 0) == 0

            def lerp(ref, prev, mu):
                x = ref[rows, :]
                return x + mu * (jnp.where(row0, prev, pltpu.roll(x, 1, axis=0)) - x)
        else:
            rows = pl.ds(pl.multiple_of(g * grows, grows), grows)
            ext_rows = pl.ds(pl.multiple_of(g * grows - pad, pad), grows + pad)

            def lerp(ref, prev, mu):
                xe = ref[ext_rows, :]
                return xe[pad:] + mu * (pltpu.roll(xe, 1, axis=0)[pad:] - xe[pad:])
        def inputs():
            pm, mu_m = pm_ref[...], mu_m_ref[...]
            return (lerp(r_ref, pm[0:1], mu_m[0:1]), lerp(k_ref, pm[1:2], mu_m[1:2]),
                    lerp(v_ref, pm[2:3], mu_m[2:3]), lerp(l_ref, pl_ref[...], mu_l_ref[...]))

        tok = None
        for tok in _rwkv_token_stages(inputs, prm, _head_ones(n)):
            if tok is None:
                yield
        for i, name in enumerate(("r", "k", "v", "a", "b", "logw", "gate", "bonus")):
            tok_scr[i, rows, :] = tok[name]

    nch = seq // CHUNK
    lane = lax.broadcasted_iota(jnp.int32, (CHUNK, n), 1)
    is_a = lane < HEAD_DIM
    ri = lax.broadcasted_iota(jnp.int32, (n, n), 0)
    ci = lax.broadcasted_iota(jnp.int32, (n, n), 1)
    strict = ri > ci
    incl = ri >= ci
    eye = ri == ci
    rc = lax.broadcasted_iota(jnp.int32, (CHUNK, CHUNK), 0)
    cc = lax.broadcasted_iota(jnp.int32, (CHUNK, CHUNK), 1)

    def stack(x):
        return jnp.concatenate([jnp.where(is_a, x, 0.0), jnp.where(is_a, 0.0, x)], axis=0)

    def phase_b_step(c):
        rows = pl.ds(pl.multiple_of(c * CHUNK, CHUNK), CHUNK)
        ht = h_ref[...]
        hb = ht.astype(BF16)
        ys = _bdot_nt(rw_scr[c], hb) + y0_scr[c]
        y_scr[rows, :] = ys[:CHUNK, :] + ys[CHUNK:, :]
        h_ref[...] = ht * pc_scr[c] + _bdot(hb, g_scr[c]) + ha_scr[c]

    def group(ga, gb, gt=None):
        b_todo = [] if gb is None else [gb * PHASE_A_UNROLL + u for u in range(PHASE_A_UNROLL)]

        def fill():
            if b_todo:
                phase_b_step(b_todo.pop(0))

        if ga is None:
            while b_todo:
                fill()
            return
        cs = [ga * PHASE_A_UNROLL + u for u in range(PHASE_A_UNROLL)]
        each = lambda fn, *lists: [fn(*xs) for xs in zip(*lists)]
        tril_ones = jnp.where(rc >= cc, 1.0, 0.0).astype(BF16)
        tok = [[tok_scr[i, pl.ds(pl.multiple_of(c * CHUNK, CHUNK), CHUNK), :] for i in range(6)] for c in cs]
        r_c, k_c, v_c, a_c, b_c, lw_c = [list(x) for x in zip(*tok)]
        token_gen = iter(()) if gt is None else token_group(gt)
        token_step = lambda: next(token_gen, None)
        token_step()
        pairs = [_split_dot(tril_ones, jnp.concatenate(lw_c[u:u + 2], axis=1), 3)
                 for u in range(0, PHASE_A_UNROLL, 2)]
        lcum = [p[:, h * n:(h + 1) * n] for p in pairs for h in range(2)]
        lend = each(lambda l: l[CHUNK - 1:CHUNK, :], lcum)
        fill()
        sb = lambda x: stack(x).astype(BF16)
        a_s = each(lambda a, l, lw: sb(a * jnp.exp(l - lw)), a_c, lcum, lw_c)
        r_s = each(lambda r, l: sb(r * jnp.exp(l)), r_c, lcum)
        b_s = each(lambda b, l: sb(b * jnp.exp(-l)), b_c, lcum)
        k_s = each(lambda k, l: sb(k * jnp.exp(-l)), k_c, lcum)
        v_s = each(sb, v_c)
        be_s = each(lambda b, l, le: sb(b * jnp.exp(le - l)), b_c, lcum, lend)
        ke_s = each(lambda k, l, le: sb(k * jnp.exp(le - l)), k_c, lcum, lend)

        sc = each(lambda a, r, b, k: _bdot_nt(jnp.concatenate([a, r], axis=0), jnp.concatenate([b, k], axis=0)),
                  a_s, r_s, b_s, k_s)
        fill()
        token_step()
        s_ab = each(lambda s: jnp.where(strict, s[:n, :n], 0.0), sc)
        s_ak = each(lambda s: jnp.where(strict, s[:n, n:], 0.0).astype(BF16), sc)
        s_rb = each(lambda s: jnp.where(incl, s[n:, :n], 0.0).astype(BF16), sc)
        s_rk = each(lambda s: jnp.where(incl, s[n:, n:], 0.0).astype(BF16), sc)

        tinv = each(lambda s: jnp.where(eye, 1.0, 0.0) + s, s_ab)
        apow = each(lambda s: _bdot(s, s).astype(BF16), s_ab)
        fill()
        m = 2
        while 2 * m < CHUNK:
            res = each(lambda p, t: _bdot(p, jnp.concatenate([t.astype(BF16), p], axis=1)), apow, tinv)
            tinv = each(lambda t, x: t + x[:, :n], tinv, res)
            apow = each(lambda x: x[:, n:].astype(BF16), res)
            m *= 2
            fill()
            if m in (4, 16):
                token_step()
        tinv = each(lambda t, p: t + _bdot(p, t), tinv, apow)

        x1 = each(_bdot, s_ak, v_s)
        wu = each(lambda t, a, x: _bdot(t, jnp.concatenate([a, x.astype(BF16)], axis=1)), tinv, a_s, x1)
        fill()
        yk = each(_bdot, s_rk, v_s)
        ry = each(lambda s, w: _bdot(s, w), s_rb, wu)
        gh = each(lambda w, be: _bdot(w.T, be), wu, be_s)
        hk = each(lambda v, ke: _bdot_tn(v, ke), v_s, ke_s)
        while b_todo:
            fill()
        for _ in token_gen:
            pass
        for u, c in enumerate(cs):
            rw_scr[c] = ry[u][:, :n] + r_s[u].astype(F32)
            y0_scr[c] = ry[u][:, n:] + yk[u]
            g_scr[c] = gh[u][:n, :]
            ha_scr[c] = gh[u][n:, :] + hk[u]
            pc_scr[c] = jnp.exp(lend[u])

    h_ref[...] = jnp.zeros((n, n), F32)
    ngroup = nch // PHASE_A_UNROLL
    assert ngroup >= 3
    for _ in token_group(0):
        pass
    group(0, None, 1)

    def pipelined(g, carry):
        group(g, g - 1, g + 1)
        return carry
    lax.fori_loop(1, ngroup - 1, pipelined, 0)
    group(ngroup - 1, ngroup - 2)
    group(None, ngroup - 1)

    iw, ib = _RW_VEC_NAMES.index("lnx_w"), _RW_VEC_NAMES.index("lnx_b")
    o_ref[...] = _group_norm_gate(y_scr[...], tok_scr[7], tok_scr[6], vec_ref[iw:iw + 1, :], vec_ref[ib:ib + 1, :],
                                  _head_ones(n)).astype(o_ref.dtype)


def _rwkv_prompt(p_main3, p_lora3, prev_main, prev_lora, mu_main, mu_lora, vecs, w_up, a_up, g_up):
    nb, seq, _ = p_main3.shape
    npair = C_GRP // PAIR
    col = lambda off: pl.BlockSpec((None, seq, PAIR), lambda b, p, off=off: (b, 0, off + p))
    pcol = lambda rows: pl.BlockSpec((rows, PAIR), lambda b, p: (0, p))
    return pl.pallas_call(
        functools.partial(_rwkv_prompt_kernel, seq=seq),
        grid=(nb, npair),
        in_specs=[
            col(3 * npair), col(4 * npair), col(5 * npair),
            pl.BlockSpec((None, seq, C_LORA), lambda b, p: (b, 0, 0)),
            pl.BlockSpec((None, 3, PAIR), lambda b, p: (b, 0, p)),
            pl.BlockSpec((None, 1, C_LORA), lambda b, p: (b, 0, 0)),
            pcol(3), pl.BlockSpec((1, C_LORA), lambda b, p: (0, 0)), pcol(len(_RW_VEC_NAMES)),
            pcol(2 * W_LORA), pcol(2 * A_LORA), pcol(G_LORA),
        ],
        out_specs=[
            pl.BlockSpec((None, seq, PAIR), lambda b, p: (b, 0, p)),
            pl.BlockSpec((None, None, PAIR, PAIR), lambda b, p: (b, p, 0, 0)),
        ],
        out_shape=[jax.ShapeDtypeStruct((nb, seq, C_GRP), BF16),
                   jax.ShapeDtypeStruct((nb, npair, PAIR, PAIR), F32)],
        scratch_shapes=[
            pltpu.VMEM((8, seq, PAIR), F32), pltpu.VMEM((seq, PAIR), F32),
            pltpu.VMEM((seq // CHUNK, PAIR, PAIR), F32), pltpu.VMEM((seq // CHUNK, PAIR, PAIR), F32),
            pltpu.VMEM((seq // CHUNK, PAIR, PAIR), F32), pltpu.VMEM((seq // CHUNK, PAIR, PAIR), F32),
            pltpu.VMEM((seq // CHUNK, 1, PAIR), F32),
        ],
        compiler_params=_cparams(2),
        name="rwkv_prompt",
    )(p_main3, p_main3, p_main3, p_lora3, prev_main, prev_lora, mu_main, mu_lora, vecs, w_up, a_up, g_up)


def _rwkv_sample_tok_kernel(pm_ref, l_ref, pvm_ref, pvl_ref, mu_m_ref, mu_l_ref, vec_ref,
                            wup_ref, aup_ref, gup_ref, out_ref):
    ones_bd = _head_ones(C_GRP)
    prm = {name: vec_ref[i:i + 1, :] for i, name in enumerate(_RW_VEC_NAMES)}
    prm.update(w_up=wup_ref[...], a_up=aup_ref[...], g_up=gup_ref[...])
    lerp = lambda x, prev, mu: x + mu * (prev - x)
    xs = [lerp(pm_ref[:, i * C_GRP:(i + 1) * C_GRP], pvm_ref[:, i * C_GRP:(i + 1) * C_GRP],
               mu_m_ref[:, i * C_GRP:(i + 1) * C_GRP]) for i in range(3)]
    tok = _rwkv_token_math(xs[0], xs[1], xs[2], lerp(l_ref[...], pvl_ref[...], mu_l_ref[...]), prm, ones_bd)
    for i, name in enumerate(("r", "k", "v", "a", "b", "logw", "gate", "bonus")):
        out_ref[i] = tok[name]


def _rwkv_sample_step_kernel(s_ref, row_ref, col_ref, s_out_ref, o_ref):
    head_lane = lax.broadcasted_iota(jnp.int32, (1, N_HEADS), 1)
    heads = range(N_HEADS)
    row = lambda i: [row_ref[0, i, h:h + 1, :] for h in heads]
    r, k, a, b, logw = [row(i) for i in range(5)]
    v = [col_ref[0, 0, :, h:h + 1] for h in heads]
    s = [s_ref[0, h] for h in heads]
    sa = [jnp.sum(s[h] * a[h], axis=-1, keepdims=True) for h in heads]
    s_new = [s[h] * jnp.exp(logw[h]) + sa[h] * b[h] + v[h] * k[h] for h in heads]
    for h in heads:
        s_out_ref[0, h] = s_new[h]
    y = jnp.zeros((HEAD_DIM, N_HEADS), F32)
    for h in heads:
        y = y + jnp.where(head_lane == h, jnp.sum(s_new[h] * r[h], axis=-1, keepdims=True), 0.0)
    gate, bonus, lnx_w, lnx_b = [col_ref[0, i] for i in range(1, 5)]
    mean = jnp.mean(y, axis=0, keepdims=True)
    d = y - mean
    var = jnp.mean(d * d, axis=0, keepdims=True)
    o_ref[0] = (d * lax.rsqrt(var + LNX_EPS) * lnx_w + lnx_b + bonus) * gate


def _rwkv_sample(pm_s, pl_s, prev_main, prev_lora, mu_main, mu_lora, vecs, w_up, a_up, g_up, wkv0):
    nb = pm_s.shape[0]
    full = lambda shape: pl.BlockSpec(shape, lambda i: (0,) * len(shape))
    tok = pl.pallas_call(
        _rwkv_sample_tok_kernel,
        grid=(1,),
        in_specs=[pl.BlockSpec((nb, 3 * C_GRP), lambda i: (0, 1)), full((nb, C_LORA)),
                  full((nb, 3 * C_GRP)), full((nb, C_LORA)), full((1, 3 * C_GRP)), full((1, C_LORA)),
                  full(vecs.shape), full(w_up.shape), full(a_up.shape), full(g_up.shape)],
        out_specs=full((8, nb, C_GRP)),
        out_shape=jax.ShapeDtypeStruct((8, nb, C_GRP), F32),
        compiler_params=_cparams(1),
        name="rwkv_sample_tok",
    )(pm_s, pl_s, prev_main, prev_lora, mu_main, mu_lora, vecs, w_up, a_up, g_up)
    heads = lambda x: x.reshape(x.shape[0], nb, N_HEADS, HEAD_DIM).transpose(1, 0, 2, 3)
    rows = heads(jnp.concatenate([tok[0:2], tok[3:6]], axis=0))
    lnx = jnp.broadcast_to(vecs[5:7, None, :], (2, nb, C_GRP))
    cols = jnp.swapaxes(heads(jnp.concatenate([tok[2:3], tok[6:8], lnx], axis=0)), -1, -2)
    st = (1, N_HEADS, HEAD_DIM, HEAD_DIM)
    s_new, o = pl.pallas_call(
        _rwkv_sample_step_kernel,
        grid=(nb,),
        in_specs=[pl.BlockSpec(st, lambda b: (b, 0, 0, 0)),
                  pl.BlockSpec((1, 5, N_HEADS, HEAD_DIM), lambda b: (b, 0, 0, 0)),
                  pl.BlockSpec((1, 5, HEAD_DIM, N_HEADS), lambda b: (b, 0, 0, 0))],
        out_specs=[pl.BlockSpec(st, lambda b: (b, 0, 0, 0)),
                   pl.BlockSpec((1, HEAD_DIM, N_HEADS), lambda b: (b, 0, 0))],
        out_shape=[jax.ShapeDtypeStruct((nb,) + st[1:], F32),
                   jax.ShapeDtypeStruct((nb, HEAD_DIM, N_HEADS), F32)],
        compiler_params=_cparams(1),
        name="rwkv_sample_step",
    )(wkv0, rows, cols)
    return jnp.swapaxes(o, -1, -2).reshape(nb, C_GRP), s_new


def _out_proj_kernel(x_ref, oa_ref, orw_ref, wa_ref, wb_ref, o_ref):
    o_ref[...] = (x_ref[...] + _bdot(oa_ref[...], wa_ref[...]) + _bdot(orw_ref[...], wb_ref[...]))


def _out_proj(x2d, o_att, o_rw, w_o, tm, tn):
    m = x2d.shape[0]
    return pl.pallas_call(
        _out_proj_kernel,
        grid=(m // tm, D_MODEL // tn),
        in_specs=[
            pl.BlockSpec((tm, tn), lambda i, j: (i, j)),
            pl.BlockSpec((tm, C_GRP), lambda i, j: (i, 0)),
            pl.BlockSpec((tm, C_GRP), lambda i, j: (i, 0)),
            pl.BlockSpec((C_GRP, tn), lambda i, j: (0, j)),
            pl.BlockSpec((C_GRP, tn), lambda i, j: (1, j)),
        ],
        out_specs=pl.BlockSpec((tm, tn), lambda i, j: (i, j)),
        out_shape=jax.ShapeDtypeStruct((m, D_MODEL), F32),
        compiler_params=_cparams(2),
        name="out_proj",
    )(x2d, o_att, o_rw, w_o, w_o)


HALO = 16


def _ffn_kernel(*refs, tm, tiles_per_seq, seq_mode):
    if seq_mode:
        (x_ref, xh_ref, g_ref, wg_ref, wv_ref, cwg_ref, cwv_ref, wd_ref, gf_ref,
         y_ref, ug_ref, uv_ref, h_scr, acc_scr) = refs
    else:
        (x_ref, pg_ref, pv_ref, g_ref, wg_ref, wv_ref, cwg_ref, cwv_ref, wd_ref, gf_ref,
         y_ref, ug_ref, uv_ref, h_scr, acc_scr) = refs
    i, j = pl.program_id(0), pl.program_id(1)

    def norm(x):
        ms = jnp.mean(x * x, axis=-1, keepdims=True)
        return x * lax.rsqrt(ms + NORM_EPS) * g_ref[...]

    @pl.when(j == 0)
    def _():
        x = x_ref[...]
        acc_scr[...] = x
        if seq_mode:
            keep = jnp.where(i % tiles_per_seq == 0, 0.0, 1.0).astype(F32)
            h_scr[:HALO, :] = (norm(xh_ref[...]) * keep).astype(BF16)
            h_scr[HALO:, :] = norm(x).astype(BF16)
        else:
            h_scr[...] = norm(x).astype(BF16)

    h = h_scr[...]
    halves = []
    for w_ref, cw_ref, u_ref, p_ref in ((wg_ref, cwg_ref, ug_ref, None if seq_mode else pg_ref),
                                        (wv_ref, cwv_ref, uv_ref, None if seq_mode else pv_ref)):
        u = jnp.dot(h, w_ref[...], preferred_element_type=F32)
        cw = cw_ref[...]
        if seq_mode:
            u_ref[0] = u[HALO + tm - 2:HALO + tm, :]
            c = (cw[3:4] + cw[0:1] * u[HALO - 2:HALO - 2 + tm] + cw[1:2] * u[HALO - 1:HALO - 1 + tm]
                 + cw[2:3] * u[HALO:HALO + tm])
        else:
            u_ref[...] = u
            c = cw[3:4] + cw[0:1] * p_ref[0] + cw[1:2] * p_ref[1] + cw[2:3] * u
        halves.append(c)
    gate, val = halves
    act = gate * _sigmoid(gate) * val
    acc_scr[...] += jnp.dot(act.astype(BF16), wd_ref[...], preferred_element_type=F32)

    @pl.when(j == pl.num_programs(1) - 1)
    def _():
        x2 = acc_scr[...]
        ms = jnp.mean(x2 * x2, axis=-1, keepdims=True)
        y_ref[...] = x2 * lax.rsqrt(ms + NORM_EPS) * gf_ref[...]


def _ffn(x1, norm_g, w_up, conv_wb, w_down, final_g, tm, tf, seq_len=None, prev=None):
    m = x1.shape[0]
    nf = D_FF // tf
    seq_mode = prev is None
    tiles_per_seq = seq_len // tm if seq_mode else 1
    vec = lambda width: pl.BlockSpec((1, width), lambda i, j: (0, 0))
    in_specs = [pl.BlockSpec((tm, D_MODEL), lambda i, j: (i, 0))]
    args = [x1]
    if seq_mode:
        in_specs.append(pl.BlockSpec((HALO, D_MODEL), lambda i, j: (jnp.maximum(i * (tm // HALO) - 1, 0), 0)))
        args.append(x1)
    else:
        in_specs += [pl.BlockSpec((2, tm, tf), lambda i, j: (0, i, j)),
                     pl.BlockSpec((2, tm, tf), lambda i, j: (0, i, nf + j))]
        args += [prev, prev]
    in_specs += [
        vec(D_MODEL),
        pl.BlockSpec((D_MODEL, tf), lambda i, j: (0, j)),
        pl.BlockSpec((D_MODEL, tf), lambda i, j: (0, nf + j)),
        pl.BlockSpec((4, tf), lambda i, j: (0, j)),
        pl.BlockSpec((4, tf), lambda i, j: (0, nf + j)),
        pl.BlockSpec((tf, D_MODEL), lambda i, j: (j, 0)),
        vec(D_MODEL),
    ]
    args += [norm_g, w_up, w_up, conv_wb, conv_wb, w_down, final_g]
    out_specs = [pl.BlockSpec((tm, D_MODEL), lambda i, j: (i, 0))]
    out_shape = [jax.ShapeDtypeStruct((m, D_MODEL), F32)]
    if seq_mode:
        out_specs += [pl.BlockSpec((1, 2, tf), lambda i, j: (i, 0, j))] * 2
        out_shape += [jax.ShapeDtypeStruct((m // tm, 2, D_FF), F32)] * 2
    else:
        out_specs += [pl.BlockSpec((tm, tf), lambda i, j: (i, j))] * 2
        out_shape += [jax.ShapeDtypeStruct((m, D_FF), F32)] * 2
    rows = tm + HALO if seq_mode else tm
    return pl.pallas_call(
        functools.partial(_ffn_kernel, tm=tm, tiles_per_seq=tiles_per_seq, seq_mode=seq_mode),
        grid=(m // tm, nf),
        in_specs=in_specs,
        out_specs=out_specs,
        out_shape=out_shape,
        scratch_shapes=[pltpu.VMEM((rows, D_MODEL), BF16), pltpu.VMEM((tm, D_MODEL), F32)],
        compiler_params=_cparams(2),
        name="ffn_seq" if seq_mode else "ffn_state",
    )(*args)


def _layer_params(l, norm_mix_g, w_in, att_out_g, rw_mu, rw_w0, rw_w_up, rw_a0, rw_a_up, rw_g_up, rw_k_k,
                  rw_k_a, rw_r_k, rw_lnx_w, rw_lnx_b, w_o, norm_ffn_g, ffn_w_up, ffn_conv_w, ffn_conv_b,
                  ffn_w_down):
    zeros = jnp.zeros((W_LORA, C_GRP), F32)
    return dict(
        norm_mix_g=norm_mix_g[l][None],
        w_main=w_in[l].astype(BF16),
        w_lora=w_in[l][:, C_MAIN:].astype(BF16),
        att_g=att_out_g[l][None],
        mu_main=rw_mu[l][:3 * C_GRP].reshape(3, C_GRP),
        mu_lora=rw_mu[l][None, 3 * C_GRP:],
        vecs=jnp.stack([rw_w0[l], rw_a0[l], rw_k_k[l], rw_k_a[l], rw_r_k[l], rw_lnx_w[l], rw_lnx_b[l]]),
        w_up=jnp.concatenate([rw_w_up[l], zeros]).astype(BF16),
        a_up=jnp.concatenate([zeros, rw_a_up[l]]).astype(BF16),
        g_up=rw_g_up[l].astype(BF16),
        w_o=w_o[l].astype(BF16),
        norm_ffn_g=norm_ffn_g[l][None],
        ffn_w_up=ffn_w_up[l],
        conv_wb=jnp.concatenate([ffn_conv_w[l], ffn_conv_b[l][None]]),
        ffn_w_down=ffn_w_down[l],
    )


def _prompt_layer(x, lp, final_g, sample_attn):
    nb, seq, _ = x.shape
    x2d = x.reshape(nb * seq, D_MODEL)
    p_main, p_lora, o_att_sample = _in_proj(x2d, lp["norm_mix_g"], lp["w_main"], lp["w_lora"], *IN_PROJ_TILE,
                                            sample_attn=sample_attn)
    p_main3 = p_main.reshape(nb, seq, C_MAIN)
    p_lora3 = p_lora.reshape(nb, seq, C_LORA)
    o_att, kt_new, vt_new, w_up, w_down = _attn_prompt(p_main3, lp["att_g"],
                                                       cast_f32=(lp["ffn_w_up"], lp["ffn_w_down"]))
    o_rw, h_fin = _rwkv_prompt(p_main3, p_lora3, jnp.zeros((nb, 3, C_GRP), F32), jnp.zeros((nb, 1, C_LORA), F32),
                               lp["mu_main"], lp["mu_lora"], lp["vecs"], lp["w_up"], lp["a_up"], lp["g_up"])
    x1 = _out_proj(x2d, o_att.reshape(nb * seq, C_GRP), o_rw.reshape(nb * seq, C_GRP), lp["w_o"], *OUT_PROJ_TILE)
    y, u_g, u_v = _ffn(x1, lp["norm_ffn_g"], w_up, lp["conv_wb"], w_down, final_g, *FFN_TILE, seq_len=seq)
    hd = (N_HEADS, HEAD_DIM)
    k_new = kt_new.reshape((nb,) + hd + (seq,)).transpose(0, 3, 1, 2)
    v_new = vt_new.reshape((nb,) + hd + (seq,)).transpose(0, 3, 1, 2)
    rw_last = jnp.concatenate([p_main3[:, -1:, 3 * C_GRP:], p_lora3[:, -1:, :]], axis=-1)
    wkv = jnp.stack([h_fin[:, :, :HEAD_DIM, :HEAD_DIM], h_fin[:, :, HEAD_DIM:, HEAD_DIM:]], axis=2)
    wkv = wkv.reshape(nb, N_HEADS, HEAD_DIM, HEAD_DIM)
    tiles_per_seq = u_g.shape[0] // nb
    ffn_last = jnp.concatenate([u_g, u_v], axis=-1)[tiles_per_seq - 1::tiles_per_seq]
    return (y.reshape(nb, seq, D_MODEL), k_new, v_new, rw_last, wkv, ffn_last), o_att_sample, (w_up, w_down)


def _layer(x_prompt, x_sample, cache_k, cache_v, rw_prev, wkv0, ffn_prev, lp, final_g):
    nb, hd = x_sample.shape[0], (N_HEADS, HEAD_DIM)
    x2d = x_sample.reshape(nb, D_MODEL)
    p_main, p_lora = _in_proj(x2d, lp["norm_mix_g"], lp["w_main"], lp["w_lora"], nb, IN_PROJ_TN_DECODE)
    q = p_main[:, :C_GRP].reshape((nb,) + hd)
    k_new = p_main[:, C_GRP:2 * C_GRP].reshape((nb,) + hd)
    v_new = p_main[:, 2 * C_GRP:3 * C_GRP].reshape((nb,) + hd)
    cache_kt, cache_vt = cache_k.transpose(0, 2, 3, 1), cache_v.transpose(0, 2, 3, 1)
    head_minor = lambda t: jnp.swapaxes(t, -1, -2)
    prompt_out, o_att, (ffn_w_up, ffn_w_down) = _prompt_layer(
        x_prompt, lp, final_g,
        sample_attn=(head_minor(q), head_minor(k_new), head_minor(v_new), cache_kt, cache_vt,
                     head_minor(lp["att_g"].reshape(hd))))
    o_att = head_minor(o_att.sum(axis=1)).reshape(nb, C_GRP)
    prev = rw_prev.reshape(nb, C_SHIFT)
    o_rw, wkv = _rwkv_sample(p_main, p_lora, prev[:, :3 * C_GRP], prev[:, 3 * C_GRP:],
                             lp["mu_main"].reshape(1, 3 * C_GRP), lp["mu_lora"], lp["vecs"],
                             lp["w_up"], lp["a_up"], lp["g_up"], wkv0)
    x1 = _out_proj(x2d, o_att, o_rw, lp["w_o"], nb, OUT_PROJ_TN_DECODE)
    prev_rows = ffn_prev.transpose(1, 0, 2)
    y, u_g, u_v = _ffn(x1, lp["norm_ffn_g"], ffn_w_up, lp["conv_wb"], ffn_w_down, final_g,
                       nb, FFN_TILE[1], prev=prev_rows)
    rw_last = jnp.concatenate([p_main[:, 3 * C_GRP:], p_lora], axis=-1)[:, None, :]
    ffn_last = jnp.stack([ffn_prev[:, 1, :], jnp.concatenate([u_g, u_v], axis=-1)], axis=1)
    sample_out = (y.reshape(nb, 1, D_MODEL), k_new[:, None], v_new[:, None], rw_last, wkv, ffn_last)
    return prompt_out, sample_out


def kernel(x_prompt, x_sample, cache_att_k, cache_att_v, state_rwkv_shift, state_rwkv_wkv, state_ffn_conv, norm_mix_g, w_in, att_out_g, rw_mu, rw_w0, rw_w_up, rw_a0, rw_a_up, rw_g_up, rw_k_k, rw_k_a, rw_r_k, rw_lnx_w, rw_lnx_b, w_o, norm_ffn_g, ffn_w_up, ffn_conv_w, ffn_conv_b, ffn_w_down, norm_final_g):
    depth = w_in.shape[0]
    assert depth == 1, "the fused FFN + final-norm kernel assumes a single trunk layer"
    assert x_sample.shape[1] == 1, "the sample path handles one new token per sequence"
    final_g = norm_final_g[None]
    lp = _layer_params(0, norm_mix_g, w_in, att_out_g, rw_mu, rw_w0, rw_w_up, rw_a0, rw_a_up, rw_g_up, rw_k_k,
                       rw_k_a, rw_r_k, rw_lnx_w, rw_lnx_b, w_o, norm_ffn_g, ffn_w_up, ffn_conv_w, ffn_conv_b,
                       ffn_w_down)
    (yp, pk, pv, prw, pwkv, pffn), (ys, sk, sv, srw, swkv, sffn) = _layer(
        x_prompt, x_sample, cache_att_k[0], cache_att_v[0], state_rwkv_shift[0], state_rwkv_wkv[0],
        state_ffn_conv[0], lp, final_g)
    lead = lambda t: t[None]
    return (yp, ys, lead(pk), lead(pv), lead(prw), lead(pwkv), lead(pffn),
            lead(sk), lead(sv), lead(srw), lead(swkv), lead(sffn))
```
